```python
import math
import jax, jax.numpy as jnp
from jax import lax
import numpy as np

D_MODEL = 1024
BATCH = 4
SEQ = 8192
DEPTH = 1
DEC_BATCH = 16
DEC_SEQ = 16
PAST_LEN = 2048

CHUNK = 64
QBLOCK = 128
HEAD_DIM = 64
DIFF_HEADS = 8
DIFF_COMP = HEAD_DIM // 2
SB_HEADS = 8
DIFF_WIDTH = DIFF_HEADS * HEAD_DIM
SB_WIDTH = SB_HEADS * HEAD_DIM
MIX_WIDTH = DIFF_WIDTH + SB_WIDTH
IN_COLS = 3 * MIX_WIDTH
ROPE_THETA = 10000.0
NORM_EPS = 1e-6
SUBLN_EPS = 1e-5
PEER_HEADS = 8
PEER_KEYS = 128
N_EXPERTS = PEER_KEYS * PEER_KEYS
PEER_TOPK = 16
PEER_QDIM = 256
PEER_HALF = PEER_QDIM // 2
PEER_BLOCK = 128
PLE_DIM = 256

kernel_name = 'stream_diff_stickbreak_peer'


def _rmsnorm(x, g, eps=NORM_EPS):
    x32 = x.astype(jnp.float32)
    y = x32 * lax.rsqrt(jnp.mean(x32 * x32, axis=-1, keepdims=True) + eps)
    return (y * g.astype(jnp.float32)).astype(x.dtype)


def _rope(x, pos):
    d = x.shape[-1]
    inv = ROPE_THETA ** (-jnp.arange(0, d, 2, dtype=jnp.float32) / d)
    ang = pos.astype(jnp.float32)[:, None] * inv[None, :]
    cos = jnp.cos(ang)[None, :, None, :]
    sin = jnp.sin(ang)[None, :, None, :]
    x32 = x.astype(jnp.float32)
    x1, x2 = jnp.split(x32, 2, axis=-1)
    return jnp.concatenate([x1 * cos - x2 * sin, x1 * sin + x2 * cos], axis=-1).astype(x.dtype)


def _rope_diff(x, pos):
    b, s, h, _ = x.shape
    return _rope(x.reshape(b, s, h * 2, DIFF_COMP), pos).reshape(b, s, h, 2 * DIFF_COMP)


def _diff_attend(q, q_pos, k, v, k_pos, lam):
    b, nq, h, _ = q.shape
    nk = k.shape[1]
    qc = q.reshape(b, nq, h, 2, DIFF_COMP)
    kc = k.reshape(b, nk, h, 2, DIFF_COMP)
    s = jnp.einsum('bqhcd,bkhcd->bhcqk', qc, kc).astype(jnp.float32) * (DIFF_COMP ** -0.5)
    visible = (k_pos[None, :] // CHUNK) <= (q_pos[:, None] // CHUNK)
    s = jnp.where(visible, s, -jnp.inf)
    prob = jax.nn.softmax(s, axis=-1)
    w = prob[:, :, 0] - lam * prob[:, :, 1]
    return jnp.einsum('bhqk,bkhd->bqhd', w, v.astype(jnp.float32))


def _sb_attend(q, q_pos, k, v, k_pos):
    z = jnp.einsum('bqhd,bkhd->bhqk', q, k).astype(jnp.float32) * (HEAD_DIM ** -0.5)
    earlier = k_pos[None, :] < q_pos[:, None]
    log_keep = jnp.where(earlier, jax.nn.log_sigmoid(-z), 0.0)
    log_keep_after = lax.cumsum(log_keep, axis=3, reverse=True) - log_keep
    wts = jnp.where(earlier, jnp.exp(jax.nn.log_sigmoid(z) + log_keep_after), 0.0)
    return jnp.einsum('bhqk,bkhd->bqhd', wts, v.astype(jnp.float32))


def _sweep(fn, q, q_pos):
    b, nq = q.shape[:2]
    if nq <= QBLOCK:
        return fn(q, q_pos)
    nb = nq // QBLOCK
    qb = q.reshape((b, nb, QBLOCK) + q.shape[2:]).swapaxes(0, 1)
    pb = q_pos.reshape(nb, QBLOCK)
    ob = lax.map(lambda args: fn(args[0], args[1]), (qb, pb))
    return ob.swapaxes(0, 1).reshape((b, nq) + ob.shape[3:])


def _peer(c, w_query, sub_keys, expert_u, expert_v):
    t = c.shape[0]
    q = (c @ w_query).reshape(t, PEER_HEADS, 2, PEER_HALF)
    s = jnp.einsum('thpd,pnd->thpn', q, sub_keys).astype(jnp.float32)
    s1, i1 = lax.top_k(s[:, :, 0], PEER_TOPK)
    s2, i2 = lax.top_k(s[:, :, 1], PEER_TOPK)
    cand = (s1[..., :, None] + s2[..., None, :]).reshape(t, PEER_HEADS, PEER_TOPK * PEER_TOPK)
    cidx = (i1[..., :, None] * PEER_KEYS + i2[..., None, :]).reshape(t, PEER_HEADS, PEER_TOPK * PEER_TOPK)
    top, sel = lax.top_k(cand, PEER_TOPK)
    idx = jnp.take_along_axis(cidx, sel, axis=-1)
    g = jax.nn.softmax(top, axis=-1)
    u = expert_u[idx]
    act = jax.nn.gelu(jnp.einsum('td,thkd->thk', c, u).astype(jnp.float32), approximate=False)
    coef = (g * act).astype(c.dtype)
    return jnp.einsum('thk,thkd->td', coef, expert_v[idx])


def _channel(c, w_query, sub_keys, expert_u, expert_v):
    b, s, d = c.shape
    t = b * s
    flat = c.reshape(t, d)
    if t > PEER_BLOCK and t % PEER_BLOCK == 0:
        blocks = flat.reshape(t // PEER_BLOCK, PEER_BLOCK, d)
        out = lax.map(lambda blk: _peer(blk, w_query, sub_keys, expert_u, expert_v), blocks).reshape(t, d)
    else:
        out = _peer(flat, w_query, sub_keys, expert_u, expert_v)
    return out.reshape(b, s, d)


def _mixers(a, pos, w_in, w_out, g_subln, lam, lam_init, cache):
    b, s, _ = a.shape
    proj = (a @ w_in).reshape(b, s, 6, DIFF_HEADS, HEAD_DIM)
    qd = _rope_diff(proj[:, :, 0], pos)
    kd = _rope_diff(proj[:, :, 1], pos)
    vd = proj[:, :, 2]
    qs = proj[:, :, 3]
    ks = proj[:, :, 4]
    vs = proj[:, :, 5]
    if cache is None:
        kd_all, vd_all, ks_all, vs_all = kd, vd, ks, vs
        k_pos = pos
    else:
        ck_d, cv_d, ck_s, cv_s = cache
        past = ck_d.shape[1]
        kd_all = jnp.concatenate([ck_d.astype(kd.dtype), kd], axis=1)
        vd_all = jnp.concatenate([cv_d.astype(vd.dtype), vd], axis=1)
        ks_all = jnp.concatenate([ck_s.astype(ks.dtype), ks], axis=1)
        vs_all = jnp.concatenate([cv_s.astype(vs.dtype), vs], axis=1)
        k_pos = jnp.concatenate([jnp.arange(past, dtype=jnp.int32), pos])
    diff = _sweep(lambda qb, pb: _diff_attend(qb, pb, kd_all, vd_all, k_pos, lam), qd, pos)
    diff = _rmsnorm(diff, g_subln, SUBLN_EPS) * (1.0 - lam_init)
    sb = _sweep(lambda qb, pb: _sb_attend(qb, pb, ks_all, vs_all, k_pos), qs, pos)
    heads = jnp.concatenate([diff.reshape(b, s, DIFF_WIDTH), sb.reshape(b, s, SB_WIDTH)], axis=-1).astype(a.dtype)
    return heads @ w_out, (kd, vd, ks, vs)


def setup_inputs(seed: int = 0) -> dict:
    key = jax.random.key(seed)
    ks = jax.random.split(key, 26)
    f32 = jnp.float32

    def nrm(k, shape, scale=1.0):
        return jax.random.normal(k, shape, f32) * scale

    def gain(k, shape):
        return 1.0 + 0.05 * jax.random.normal(k, shape, f32)

    kv_d = (DEPTH, DEC_BATCH, PAST_LEN, DIFF_HEADS, HEAD_DIM)
    kv_s = (DEPTH, DEC_BATCH, PAST_LEN, SB_HEADS, HEAD_DIM)
    return {
        'x_prompt': nrm(ks[0], (BATCH, SEQ, D_MODEL)),
        'x_sample': nrm(ks[1], (DEC_BATCH, DEC_SEQ, D_MODEL)),
        'cache_diff_k': nrm(ks[2], kv_d),
        'cache_diff_v': nrm(ks[3], kv_d),
        'cache_sb_k': nrm(ks[4], kv_s),
        'cache_sb_v': nrm(ks[5], kv_s),
        'p_prompt': nrm(ks[6], (DEPTH, BATCH, SEQ, PLE_DIM)),
        'p_sample': nrm(ks[7], (DEPTH, DEC_BATCH, DEC_SEQ, PLE_DIM)),
        'g_mix': gain(ks[8], (DEPTH, D_MODEL)),
        'w_in': nrm(ks[9], (DEPTH, D_MODEL, IN_COLS), D_MODEL ** -0.5),
        'lambda_q1': nrm(ks[10], (DEPTH, DIFF_COMP), 0.1),
        'lambda_k1': nrm(ks[11], (DEPTH, DIFF_COMP), 0.1),
        'lambda_q2': nrm(ks[12], (DEPTH, DIFF_COMP), 0.1),
        'lambda_k2': nrm(ks[13], (DEPTH, DIFF_COMP), 0.1),
        'g_subln': gain(ks[14], (DEPTH, HEAD_DIM)),
        'w_out': nrm(ks[15], (DEPTH, MIX_WIDTH, D_MODEL), MIX_WIDTH ** -0.5),
        'g_ffn': gain(ks[16], (DEPTH, D_MODEL)),
        'w_query': nrm(ks[17], (DEPTH, D_MODEL, PEER_HEADS * PEER_QDIM), D_MODEL ** -0.5),
        'sub_keys': nrm(ks[18], (DEPTH, 2, PEER_KEYS, PEER_HALF), PEER_HALF ** -0.5),
        'expert_u': nrm(ks[19], (DEPTH, N_EXPERTS, D_MODEL), D_MODEL ** -0.5),
        'expert_v': nrm(ks[20], (DEPTH, N_EXPERTS, D_MODEL), PEER_HEADS ** -0.5),
        'g_ple': gain(ks[21], (DEPTH, D_MODEL)),
        'w_pgate': nrm(ks[22], (DEPTH, D_MODEL, D_MODEL), D_MODEL ** -0.5),
        'w_ple': nrm(ks[23], (DEPTH, PLE_DIM, D_MODEL), PLE_DIM ** -0.5),
        'g_final': gain(ks[24], (D_MODEL,)),
    }


def reference(x_prompt, x_sample, cache_diff_k, cache_diff_v, cache_sb_k, cache_sb_v, p_prompt, p_sample,
              g_mix, w_in, lambda_q1, lambda_k1, lambda_q2, lambda_k2, g_subln, w_out, g_ffn, w_query,
              sub_keys, expert_u, expert_v, g_ple, w_pgate, w_ple, g_final):
    def run(x, p, pos, caches):
        h = x
        rows = []
        for i in range(DEPTH):
            lam_init = 0.8 - 0.6 * math.exp(-0.3 * i)
            lam = (jnp.exp(jnp.sum(lambda_q1[i].astype(jnp.float32) * lambda_k1[i].astype(jnp.float32)))
                   - jnp.exp(jnp.sum(lambda_q2[i].astype(jnp.float32) * lambda_k2[i].astype(jnp.float32)))
                   + lam_init)
            cache_i = None if caches is None else tuple(c[i] for c in caches)
            a = _rmsnorm(h, g_mix[i])
            mixed, new_rows = _mixers(a, pos, w_in[i], w_out[i], g_subln[i], lam, lam_init, cache_i)
            h = h + mixed
            h = h + _channel(_rmsnorm(h, g_ffn[i]), w_query[i], sub_keys[i], expert_u[i], expert_v[i])
            gate = jax.nn.sigmoid((_rmsnorm(h, g_ple[i]) @ w_pgate[i]).astype(jnp.float32)).astype(h.dtype)
            h = h + (p[i] @ w_ple[i]) * gate
            rows.append(new_rows)
        y = _rmsnorm(h, g_final)
        stacked = tuple(jnp.stack([r[j] for r in rows]) for j in range(4))
        return y, stacked

    pos_prompt = jnp.arange(x_prompt.shape[1], dtype=jnp.int32)
    past = cache_diff_k.shape[2]
    pos_sample = past + jnp.arange(x_sample.shape[1], dtype=jnp.int32)
    y_prompt, (dk_p, dv_p, sk_p, sv_p) = run(x_prompt, p_prompt, pos_prompt, None)
    y_sample, (dk_s, dv_s, sk_s, sv_s) = run(x_sample, p_sample, pos_sample,
                                             (cache_diff_k, cache_diff_v, cache_sb_k, cache_sb_v))
    return (y_prompt, y_sample, dk_p, dv_p, sk_p, sv_p, dk_s, dv_s, sk_s, sv_s)
```

```python
import functools
import math

import jax
import jax.numpy as jnp
from jax import lax
from jax.experimental import pallas as pl
from jax.experimental.pallas import tpu as pltpu

F32 = jnp.float32
BF16 = jnp.bfloat16

D_MODEL = 1024
HEADS = 8
HEAD_DIM = 64
DIFF_COMP = 32
MIX = HEADS * HEAD_DIM
CHUNK = 64
ROPE_THETA = 10000.0
NORM_EPS = 1e-6
SUBLN_EPS = 1e-5
PEER_HEADS = 8
PEER_KEYS = 128
PEER_TOPK = 16
PEER_HALF = 128
PEER_SEL = PEER_HEADS * PEER_TOPK
PLE_DIM = 256
LAM_INIT = 0.8 - 0.6 * math.exp(-0.3 * 0)

LANES = 128
VMEM_LIMIT = 48 * 1024 * 1024

NT_DIMS = (((1,), (1,)), ((), ()))


def _nt(a, b):
    return lax.dot_general(a, b, NT_DIMS, preferred_element_type=F32)


def _mm(a, b):
    return jnp.dot(a, b, preferred_element_type=F32)


def _rms(x, g, eps):
    return x * lax.rsqrt(jnp.mean(x * x, axis=-1, keepdims=True) + eps) * g


def _params(*sem):
    return pltpu.CompilerParams(dimension_semantics=sem, vmem_limit_bytes=VMEM_LIMIT)


def _proj_body(x_ref, g_ref, w_ref, cos_ref, sin_ref,
               kd_ref, vd_ref, ks_ref, vs_ref,
               qd2_ref, kdb_ref, vdb_ref, qsb_ref, ksb_ref, vsb_ref):
    ts = x_ref.shape[0]
    a = _rms(x_ref[...], g_ref[...], NORM_EPS).astype(BF16)
    cos = jnp.tile(cos_ref[...], (1, MIX // LANES))
    sin = jnp.tile(sin_ref[...], (1, MIX // LANES))
    lane = lax.broadcasted_iota(jnp.int32, (ts, MIX), 1)
    first_half = (lane % DIFF_COMP) < (DIFF_COMP // 2)

    def group(i):
        return _mm(a, w_ref[:, i * MIX:(i + 1) * MIX])

    def rope(t):
        partner = jnp.where(first_half,
                            pltpu.roll(t, MIX - DIFF_COMP // 2, 1),
                            pltpu.roll(t, DIFF_COMP // 2, 1))
        return t * cos + partner * sin

    qd = rope(group(0)) * (DIFF_COMP ** -0.5)
    kd = rope(group(1))
    vd = group(2)
    qs = group(3) * (HEAD_DIM ** -0.5)
    ks = group(4)
    vs = group(5)
    kd_ref[...] = kd
    vd_ref[...] = vd
    ks_ref[...] = ks
    vs_ref[...] = vs
    comp0 = lax.broadcasted_iota(jnp.int32, (ts, HEAD_DIM), 1) < DIFF_COMP
    for h in range(HEADS):
        sl = slice(h * HEAD_DIM, (h + 1) * HEAD_DIM)
        qh = qd[:, sl]
        qd2_ref[0, h, 0] = jnp.where(comp0, qh, 0.0).astype(BF16)
        qd2_ref[0, h, 1] = jnp.where(comp0, 0.0, qh).astype(BF16)
        kdb_ref[0, h] = kd[:, sl].astype(BF16)
        vdb_ref[0, h] = vd[:, sl].astype(BF16)
        qsb_ref[0, h] = qs[:, sl].astype(BF16)
        ksb_ref[0, h] = ks[:, sl].astype(BF16)
        vsb_ref[0, h] = vs[:, sl].astype(BF16)


def _proj(x, g_mix, w_in_b, cos_t, sin_t, nb, seq):
    t = nb * seq
    ts = min(256, seq)
    nst = seq // ts
    row = pl.BlockSpec((ts, MIX), lambda i: (i, 0))
    hm = pl.BlockSpec((1, HEADS, ts, HEAD_DIM), lambda i: (i // nst, 0, i % nst, 0))
    hm2 = pl.BlockSpec((1, HEADS, 2, ts, HEAD_DIM), lambda i: (i // nst, 0, 0, i % nst, 0))
    rows = jax.ShapeDtypeStruct((t, MIX), F32)
    heads = jax.ShapeDtypeStruct((nb, HEADS, seq, HEAD_DIM), BF16)
    heads2 = jax.ShapeDtypeStruct((nb, HEADS, 2, seq, HEAD_DIM), BF16)
    return pl.pallas_call(
        _proj_body,
        grid=(t // ts,),
        in_specs=[
            pl.BlockSpec((ts, D_MODEL), lambda i: (i, 0)),
            pl.BlockSpec((1, D_MODEL), lambda i: (0, 0)),
            pl.BlockSpec((D_MODEL, 6 * MIX), lambda i: (0, 0)),
            pl.BlockSpec((ts, LANES), lambda i: (i % nst, 0)),
            pl.BlockSpec((ts, LANES), lambda i: (i % nst, 0)),
        ],
        out_specs=[row, row, row, row, hm2, hm, hm, hm, hm, hm],
        out_shape=[rows, rows, rows, rows, heads2, heads, heads, heads, heads, heads],
        compiler_params=_params("parallel"),
        name="proj",
    )(x, g_mix, w_in_b, cos_t, sin_t)


def _diff_update(q2, k, v, visible, carry):
    m, l, acc = carry
    s = _nt(q2, k)
    if visible is not None:
        s = jnp.where(visible, s, -jnp.inf)
    m_new = jnp.maximum(m, jnp.max(s, axis=-1, keepdims=True))
    p = jnp.exp(s - m_new)
    alpha = jnp.exp(m - m_new)
    l = alpha * l + jnp.sum(p, axis=-1, keepdims=True)
    acc = alpha * acc + _mm(p.astype(BF16), v)
    return m_new, l, acc


def _diff_finish(carry, lam, g_subln, tq):
    _, l, acc = carry
    o = acc / l
    d = o[:tq] - lam * o[tq:]
    return _rms(d, g_subln, SUBLN_EPS) * (1.0 - LAM_INIT)


def _suffix_sums(lk, tri):
    hi = lk.astype(BF16)
    lo = (lk - hi.astype(F32)).astype(BF16)
    return _mm(hi, tri) + _mm(lo, tri)


def _sb_update(q, k, v, tri, earlier, carry):
    run, acc = carry
    z = _nt(q, k)
    sp = jnp.maximum(z, 0.0) + jnp.log1p(jnp.exp(-jnp.abs(z)))
    lk = -sp if earlier is None else jnp.where(earlier, -sp, 0.0)
    after = _suffix_sums(lk, tri)
    w = jnp.exp((z - sp) + after + run)
    if earlier is not None:
        w = jnp.where(earlier, w, 0.0)
    acc = acc + _mm(w.astype(BF16), v)
    run = run + after[:, 0:1] + lk[:, 0:1]
    return run, acc


def _tri(n):
    j = lax.broadcasted_iota(jnp.int32, (n, n), 0)
    s = lax.broadcasted_iota(jnp.int32, (n, n), 1)
    return (j > s).astype(BF16)


def _diff_body(lam_ref, q_ref, k_ref, v_ref, g_ref, o_ref, *, tq):
    qi = pl.program_id(2)
    q2 = q_ref[0, 0].reshape(2 * tq, HEAD_DIM)

    def tile(j):
        start = pl.multiple_of(j * tq, tq)
        return k_ref[0, 0, pl.ds(start, tq), :], v_ref[0, 0, pl.ds(start, tq), :]

    carry = (jnp.full((2 * tq, 1), -jnp.inf, F32), jnp.zeros((2 * tq, 1), F32),
             jnp.zeros((2 * tq, HEAD_DIM), F32))
    carry = lax.fori_loop(0, qi, lambda j, c: _diff_update(q2, *tile(j), None, c), carry)
    r = lax.broadcasted_iota(jnp.int32, (2 * tq, tq), 0) % tq
    c = lax.broadcasted_iota(jnp.int32, (2 * tq, tq), 1)
    carry = _diff_update(q2, *tile(qi), (c // CHUNK) <= (r // CHUNK), carry)
    o_ref[0, 0] = _diff_finish(carry, lam_ref[0], g_ref[...], tq)


def _diff_attention(lam, qd2, kdb, vdb, g_subln):
    nb, _, _, seq, _ = qd2.shape
    tq = 256
    kv = pl.BlockSpec((1, 1, seq, HEAD_DIM), lambda b, h, i: (b, h, 0, 0))
    return pl.pallas_call(
        functools.partial(_diff_body, tq=tq),
        grid=(nb, HEADS, seq // tq),
        in_specs=[
            pl.BlockSpec(memory_space=pltpu.SMEM),
            pl.BlockSpec((1, 1, 2, tq, HEAD_DIM), lambda b, h, i: (b, h, 0, i, 0)),
            kv, kv,
            pl.BlockSpec((1, HEAD_DIM), lambda b, h, i: (0, 0)),
        ],
        out_specs=pl.BlockSpec((1, 1, tq, HEAD_DIM), lambda b, h, i: (b, h, i, 0)),
        out_shape=jax.ShapeDtypeStruct((nb, HEADS, seq, HEAD_DIM), F32),
        compiler_params=_params("parallel", "parallel", "arbitrary"),
        name="diff_attention",
    )(lam, qd2, kdb, vdb, g_subln)


def _sb_body(q_ref, k_ref, v_ref, o_ref, *, tq):
    qi = pl.program_id(2)
    q = q_ref[0, 0]
    tri = _tri(tq)

    def tile(j):
        start = pl.multiple_of(j * tq, tq)
        return k_ref[0, 0, pl.ds(start, tq), :], v_ref[0, 0, pl.ds(start, tq), :]

    r = lax.broadcasted_iota(jnp.int32, (tq, tq), 0)
    c = lax.broadcasted_iota(jnp.int32, (tq, tq), 1)
    carry = (jnp.zeros((tq, 1), F32), jnp.zeros((tq, HEAD_DIM), F32))
    carry = _sb_update(q, *tile(qi), tri, c < r, carry)
    carry = lax.fori_loop(0, qi, lambda i, cr: _sb_update(q, *tile(qi - 1 - i), tri, None, cr), carry)
    o_ref[0, 0] = carry[1]


def _sb_attention(qsb, ksb, vsb):
    nb, _, seq, _ = qsb.shape
    tq = 256
    kv = pl.BlockSpec((1, 1, seq, HEAD_DIM), lambda b, h, i: (b, h, 0, 0))
    return pl.pallas_call(
        functools.partial(_sb_body, tq=tq),
        grid=(nb, HEADS, seq // tq),
        in_specs=[pl.BlockSpec((1, 1, tq, HEAD_DIM), lambda b, h, i: (b, h, i, 0)), kv, kv],
        out_specs=pl.BlockSpec((1, 1, tq, HEAD_DIM), lambda b, h, i: (b, h, i, 0)),
        out_shape=jax.ShapeDtypeStruct((nb, HEADS, seq, HEAD_DIM), F32),
        compiler_params=_params("parallel", "parallel", "arbitrary"),
        name="sb_attention",
    )(qsb, ksb, vsb)


SAMPLE_HEADS = 4
SAMPLE_TILE = 256


def _sample_body(lam_ref, qd2_ref, kdn_ref, vdn_ref, qs_ref, ksn_ref, vsn_ref,
                 ckd_ref, cvd_ref, cks_ref, cvs_ref, g_ref, do_ref, so_ref, *, past, nq):
    lam = lam_ref[0]
    tri_c = _tri(SAMPLE_TILE)
    tri_n = _tri(nq)
    i2 = lax.broadcasted_iota(jnp.int32, (2 * nq, nq), 0) % nq
    j2 = lax.broadcasted_iota(jnp.int32, (2 * nq, nq), 1)
    visible_new = ((past + j2) // CHUNK) <= ((past + i2) // CHUNK)
    i1 = lax.broadcasted_iota(jnp.int32, (nq, nq), 0)
    j1 = lax.broadcasted_iota(jnp.int32, (nq, nq), 1)
    earlier_new = j1 < i1
    for h in range(SAMPLE_HEADS):
        sl = slice(h * HEAD_DIM, (h + 1) * HEAD_DIM)
        q2 = qd2_ref[0, h].reshape(2 * nq, HEAD_DIM)
        carry = (jnp.full((2 * nq, 1), -jnp.inf, F32), jnp.zeros((2 * nq, 1), F32),
                 jnp.zeros((2 * nq, HEAD_DIM), F32))
        carry = _diff_update(q2, ckd_ref[0, :, sl].astype(BF16), cvd_ref[0, :, sl].astype(BF16), None, carry)
        carry = _diff_update(q2, kdn_ref[0, h], vdn_ref[0, h], visible_new, carry)
        do_ref[0, h] = _diff_finish(carry, lam, g_ref[...], nq)
        q = qs_ref[0, h]
        carry = (jnp.zeros((nq, 1), F32), jnp.zeros((nq, HEAD_DIM), F32))
        carry = _sb_update(q, ksn_ref[0, h], vsn_ref[0, h], tri_n, earlier_new, carry)
        for t in reversed(range(past // SAMPLE_TILE)):
            rows = slice(t * SAMPLE_TILE, (t + 1) * SAMPLE_TILE)
            carry = _sb_update(q, cks_ref[0, rows, sl].astype(BF16), cvs_ref[0, rows, sl].astype(BF16),
                               tri_c, None, carry)
        so_ref[0, h] = carry[1]


def _sample_attention(lam, qd2, kdb, vdb, qsb, ksb, vsb, ckd, cvd, cks, cvs, g_subln, nb, nq):
    past = ckd.shape[1]
    nhg = HEADS // SAMPLE_HEADS
    hm = pl.BlockSpec((1, SAMPLE_HEADS, nq, HEAD_DIM), lambda b, g: (0, g, b, 0))
    hm2 = pl.BlockSpec((1, SAMPLE_HEADS, 2, nq, HEAD_DIM), lambda b, g: (0, g, 0, b, 0))
    cache = pl.BlockSpec((1, past, SAMPLE_HEADS * HEAD_DIM), lambda b, g: (b, 0, g))
    out = jax.ShapeDtypeStruct((1, HEADS, nb * nq, HEAD_DIM), F32)
    return pl.pallas_call(
        functools.partial(_sample_body, past=past, nq=nq),
        grid=(nb, nhg),
        in_specs=[pl.BlockSpec(memory_space=pltpu.SMEM), hm2, hm, hm, hm, hm, hm,
                  cache, cache, cache, cache,
                  pl.BlockSpec((1, HEAD_DIM), lambda b, g: (0, 0))],
        out_specs=[hm, hm],
        out_shape=[out, out],
        compiler_params=_params("parallel", "parallel"),
        name="sample_attention",
    )(lam, qd2, kdb, vdb, qsb, ksb, vsb, ckd, cvd, cks, cvs, g_subln)


def _topk_rows(s, k):
    n = s.shape[0]
    rows = lax.broadcasted_iota(jnp.int32, s.shape, 0)
    vals, ids = [], []
    for _ in range(k):
        m = jnp.max(s, axis=0, keepdims=True)
        i = jnp.min(jnp.where(s == m, rows, n), axis=0, keepdims=True)
        vals.append(m)
        ids.append(i)
        s = jnp.where(rows == i, -jnp.inf, s)
    return jnp.concatenate(vals, axis=0), jnp.concatenate(ids, axis=0)


_STAIR = [(i, j) for i in range(PEER_TOPK) for j in range(PEER_TOPK) if (i + 1) * (j + 1) <= PEER_TOPK]


def _post_body(x_ref, do_ref, so_ref, wo_ref, gf_ref, wq_ref, sk_ref,
               h1_ref, c_ref, idx_ref, gate_ref, q_scr, idx_scr, gate_scr):
    ts = x_ref.shape[0]
    mixed = jnp.zeros((ts, D_MODEL), F32)
    for h in range(HEADS):
        mixed += _mm(do_ref[0, h].astype(BF16), wo_ref[h * HEAD_DIM:(h + 1) * HEAD_DIM, :])
        mixed += _mm(so_ref[0, h].astype(BF16), wo_ref[MIX + h * HEAD_DIM:MIX + (h + 1) * HEAD_DIM, :])
    h1 = x_ref[...] + mixed
    h1_ref[...] = h1
    c = _rms(h1, gf_ref[...], NORM_EPS)
    c_ref[...] = c
    q = _mm(c.astype(BF16), wq_ref[...])
    for hp in range(2 * PEER_HEADS):
        q_scr[hp] = q[:, hp * PEER_HALF:(hp + 1) * PEER_HALF].astype(BF16)

    npad = -len(_STAIR) % 8

    def head(h, _):
        v1, i1 = _topk_rows(_nt(sk_ref[0], q_scr[2 * h]), PEER_TOPK)
        v2, i2 = _topk_rows(_nt(sk_ref[1], q_scr[2 * h + 1]), PEER_TOPK)
        cand = jnp.concatenate([v1[i:i + 1] + v2[j:j + 1] for i, j in _STAIR]
                               + [jnp.full((npad, ts), -jnp.inf, F32)], axis=0)
        eid = jnp.concatenate([i1[i:i + 1] * PEER_KEYS + i2[j:j + 1] for i, j in _STAIR]
                              + [jnp.zeros((npad, ts), jnp.int32)], axis=0)
        top, pos = _topk_rows(cand, PEER_TOPK)
        rows = lax.broadcasted_iota(jnp.int32, cand.shape, 0)
        sel = jnp.concatenate([jnp.sum(jnp.where(rows == pos[r:r + 1], eid, 0), axis=0, keepdims=True)
                               for r in range(PEER_TOPK)], axis=0)
        e = jnp.exp(top - top[0:1])
        gate_scr[h] = e / jnp.sum(e, axis=0, keepdims=True)
        idx_scr[h] = sel
        return 0

    lax.fori_loop(0, PEER_HEADS, head, 0)
    idx_ref[...] = idx_scr[...].reshape(PEER_SEL, ts).T
    gate_ref[...] = gate_scr[...].reshape(PEER_SEL, ts).T


def _post(x, dout, sout, w_out_b, g_ffn, w_query_b, sub_keys_b, nb, seq):
    t = nb * seq
    ts = min(256, seq)
    nst = seq // ts
    row = pl.BlockSpec((ts, D_MODEL), lambda i: (i, 0))
    hm = pl.BlockSpec((1, HEADS, ts, HEAD_DIM), lambda i: (i // nst, 0, i % nst, 0))
    sel = pl.BlockSpec((ts, PEER_SEL), lambda i: (i, 0))
    full = lambda *shape: pl.BlockSpec(shape, lambda i: (0,) * len(shape))
    return pl.pallas_call(
        _post_body,
        grid=(t // ts,),
        in_specs=[row, hm, hm, full(2 * MIX, D_MODEL), full(1, D_MODEL),
                  full(D_MODEL, 2 * PEER_HEADS * PEER_HALF), full(2, PEER_KEYS, PEER_HALF)],
        out_specs=[row, row, sel, sel],
        out_shape=[jax.ShapeDtypeStruct((t, D_MODEL), F32), jax.ShapeDtypeStruct((t, D_MODEL), F32),
                   jax.ShapeDtypeStruct((t, PEER_SEL), jnp.int32), jax.ShapeDtypeStruct((t, PEER_SEL), F32)],
        scratch_shapes=[pltpu.VMEM((2 * PEER_HEADS, ts, PEER_HALF), BF16),
                        pltpu.VMEM((PEER_HEADS, PEER_TOPK, ts), jnp.int32),
                        pltpu.VMEM((PEER_HEADS, PEER_TOPK, ts), F32)],
        compiler_params=_params("parallel"),
        name="post_peer_select",
    )(x, dout, sout, w_out_b, g_ffn, w_query_b, sub_keys_b)


PEER_TOKENS = 64


def _peer_body(idx_ref, c_ref, gate_ref, u_hbm, v_hbm, o_ref, ubuf, vbuf, sem):
    tb = c_ref.shape[0]

    def copies(t, slot):
        out = []
        for j in range(PEER_SEL):
            e = idx_ref[t, j]
            out.append(pltpu.make_async_copy(u_hbm.at[pl.ds(e, 1)], ubuf.at[slot, pl.ds(j, 1)], sem.at[0, slot]))
            out.append(pltpu.make_async_copy(v_hbm.at[pl.ds(e, 1)], vbuf.at[slot, pl.ds(j, 1)], sem.at[1, slot]))
        return out

    for cp in copies(0, 0):
        cp.start()
    eye = (lax.broadcasted_iota(jnp.int32, (PEER_SEL, PEER_SEL), 0)
           == lax.broadcasted_iota(jnp.int32, (PEER_SEL, PEER_SEL), 1))

    def token(t, _):
        slot = t % 2

        @pl.when(t + 1 < tb)
        def _():
            for cp in copies(t + 1, 1 - slot):
                cp.start()

        for cp in copies(t, slot):
            cp.wait()
        dots = jnp.sum(ubuf[slot] * c_ref[pl.ds(t, 1), :], axis=-1, keepdims=True)
        gate_col = jnp.sum(jnp.where(eye, gate_ref[pl.ds(t, 1), :], 0.0), axis=-1, keepdims=True)
        act = 0.5 * dots * (1.0 + lax.erf(dots * (2.0 ** -0.5)))
        o_ref[pl.ds(t, 1), :] = jnp.sum((gate_col * act) * vbuf[slot], axis=0, keepdims=True)
        return 0

    lax.fori_loop(0, tb, token, 0)


def _peer(idx, c, gate, expert_u, expert_v):
    t = c.shape[0]
    tb = min(PEER_TOKENS, t)
    return pl.pallas_call(
        _peer_body,
        grid=(t // tb,),
        in_specs=[pl.BlockSpec((tb, PEER_SEL), lambda i: (i, 0), memory_space=pltpu.SMEM),
                  pl.BlockSpec((tb, D_MODEL), lambda i: (i, 0)),
                  pl.BlockSpec((tb, PEER_SEL), lambda i: (i, 0)),
                  pl.BlockSpec(memory_space=pltpu.HBM),
                  pl.BlockSpec(memory_space=pltpu.HBM)],
        out_specs=pl.BlockSpec((tb, D_MODEL), lambda i: (i, 0)),
        out_shape=jax.ShapeDtypeStruct((t, D_MODEL), F32),
        scratch_shapes=[pltpu.VMEM((2, PEER_SEL, D_MODEL), F32), pltpu.VMEM((2, PEER_SEL, D_MODEL), F32),
                        pltpu.SemaphoreType.DMA((2, 2))],
        compiler_params=_params("arbitrary"),
        name="peer_experts",
    )(idx, c, gate, expert_u, expert_v)


def _ple_body(h1_ref, peer_ref, p_ref, gp_ref, wg_ref, we_ref, gfin_ref, y_ref):
    h = h1_ref[...] + peer_ref[...]
    a = _rms(h, gp_ref[...], NORM_EPS).astype(BF16)
    gate = jax.nn.sigmoid(_mm(a, wg_ref[...]))
    h = h + _mm(p_ref[...].astype(BF16), we_ref[...]) * gate
    y_ref[...] = _rms(h, gfin_ref[...], NORM_EPS)


def _ple(h1, peer, p, g_ple, w_pgate_b, w_ple_b, g_final):
    t = h1.shape[0]
    ts = min(512, t)
    row = pl.BlockSpec((ts, D_MODEL), lambda i: (i, 0))
    full = lambda *shape: pl.BlockSpec(shape, lambda i: (0,) * len(shape))
    return pl.pallas_call(
        _ple_body,
        grid=(t // ts,),
        in_specs=[row, row, pl.BlockSpec((ts, PLE_DIM), lambda i: (i, 0)), full(1, D_MODEL),
                  full(D_MODEL, D_MODEL), full(PLE_DIM, D_MODEL), full(1, D_MODEL)],
        out_specs=row,
        out_shape=jax.ShapeDtypeStruct((t, D_MODEL), F32),
        compiler_params=_params("parallel"),
        name="ple_final",
    )(h1, peer, p, g_ple, w_pgate_b, w_ple_b, g_final)


def _rope_tables(pos):
    half = DIFF_COMP // 2
    inv = ROPE_THETA ** (-jnp.arange(0, DIFF_COMP, 2, dtype=F32) / DIFF_COMP)
    ang = pos.astype(F32)[:, None] * inv[None, :]
    cos = jnp.cos(ang)
    sin = jnp.sin(ang)
    reps = LANES // DIFF_COMP
    del half
    return (jnp.tile(jnp.concatenate([cos, cos], axis=-1), (1, reps)),
            jnp.tile(jnp.concatenate([-sin, sin], axis=-1), (1, reps)))


def kernel(x_prompt, x_sample, cache_diff_k, cache_diff_v, cache_sb_k, cache_sb_v, p_prompt, p_sample, g_mix, w_in, lambda_q1, lambda_k1, lambda_q2, lambda_k2, g_subln, w_out, g_ffn, w_query, sub_keys, expert_u, expert_v, g_ple, w_pgate, w_ple, g_final):
    assert w_in.shape[0] == 1, "single-layer encoder"
    nb, seq, _ = x_prompt.shape
    db, dq, _ = x_sample.shape
    past = cache_diff_k.shape[2]

    lam = (jnp.exp(jnp.sum(lambda_q1[0].astype(F32) * lambda_k1[0].astype(F32)))
           - jnp.exp(jnp.sum(lambda_q2[0].astype(F32) * lambda_k2[0].astype(F32))) + LAM_INIT).reshape(1)
    w_in_b = w_in[0].astype(BF16)
    w_out_b = w_out[0].astype(BF16)
    w_query_b = w_query[0].astype(BF16)
    sub_keys_b = sub_keys[0].astype(BF16)
    w_pgate_b = w_pgate[0].astype(BF16)
    w_ple_b = w_ple[0].astype(BF16)
    g_sub = g_subln[0].reshape(1, HEAD_DIM)
    g_fin = g_final.reshape(1, D_MODEL)

    def tail(x, p, dout, sout, b, s):
        h1, c, idx, gate = _post(x, dout, sout, w_out_b, g_ffn, w_query_b, sub_keys_b, b, s)
        peer = _peer(idx, c, gate, expert_u[0], expert_v[0])
        return _ple(h1, peer, p, g_ple, w_pgate_b, w_ple_b, g_fin)

    xp = x_prompt.reshape(nb * seq, D_MODEL)
    cos_p, sin_p = _rope_tables(jnp.arange(seq, dtype=jnp.int32))
    kd, vd, ks, vs, qd2, kdb, vdb, qsb, ksb, vsb = _proj(xp, g_mix, w_in_b, cos_p, sin_p, nb, seq)
    dout = _diff_attention(lam, qd2, kdb, vdb, g_sub)
    sout = _sb_attention(qsb, ksb, vsb)
    y_prompt = tail(xp, p_prompt[0].reshape(nb * seq, PLE_DIM), dout, sout, nb, seq).reshape(nb, seq, D_MODEL)
    rows_p = tuple(r.reshape(1, nb, seq, HEADS, HEAD_DIM) for r in (kd, vd, ks, vs))

    ts = db * dq
    xs = x_sample.reshape(ts, D_MODEL)
    cos_s, sin_s = _rope_tables(jnp.tile(past + jnp.arange(dq, dtype=jnp.int32), db))
    kd, vd, ks, vs, qd2, kdb, vdb, qsb, ksb, vsb = _proj(xs, g_mix, w_in_b, cos_s, sin_s, 1, ts)
    caches = [c[0].reshape(db, past, MIX) for c in (cache_diff_k, cache_diff_v, cache_sb_k, cache_sb_v)]
    dout, sout = _sample_attention(lam, qd2, kdb, vdb, qsb, ksb, vsb, *caches, g_sub, db, dq)
    y_sample = tail(xs, p_sample[0].reshape(ts, PLE_DIM), dout, sout, 1, ts).reshape(db, dq, D_MODEL)
    rows_s = tuple(r.reshape(1, db, dq, HEADS, HEAD_DIM) for r in (kd, vd, ks, vs))

    return (y_prompt, y_sample) + rows_p + rows_s
```

```python
import functools
import math

import jax
import jax.numpy as jnp
from jax import lax
from jax.experimental import pallas as pl
from jax.experimental.pallas import tpu as pltpu
from jax.experimental.pallas import tpu_sc as plsc

F32 = jnp.float32
BF16 = jnp.bfloat16

D_MODEL = 1024
HEADS = 8
HEAD_DIM = 64
DIFF_COMP = 32
MIX = HEADS * HEAD_DIM
CHUNK = 64
ROPE_THETA = 10000.0
NORM_EPS = 1e-6
SUBLN_EPS = 1e-5
PEER_HEADS = 8
PEER_KEYS = 128
PEER_TOPK = 16
PEER_HALF = 128
PEER_SEL = PEER_HEADS * PEER_TOPK
PLE_DIM = 256
LAM_INIT = 0.8 - 0.6 * math.exp(-0.3 * 0)

LANES = 128
VMEM_LIMIT = 48 * 1024 * 1024

NT_DIMS = (((1,), (1,)), ((), ()))


def _nt(a, b):
    return lax.dot_general(a, b, NT_DIMS, preferred_element_type=F32)


def _mm(a, b):
    return jnp.dot(a, b, preferred_element_type=F32)


def _rms(x, g, eps):
    return x * lax.rsqrt(jnp.mean(x * x, axis=-1, keepdims=True) + eps) * g


def _params(*sem):
    return pltpu.CompilerParams(dimension_semantics=sem, vmem_limit_bytes=VMEM_LIMIT)


def _proj_body(x_ref, g_ref, w_ref, cos_ref, sin_ref,
               kd_ref, vd_ref, ks_ref, vs_ref,
               qd2_ref, kdb_ref, vdb_ref, qsb_ref, ksb_ref, vsb_ref):
    ts = x_ref.shape[0]
    a = _rms(x_ref[...], g_ref[...], NORM_EPS).astype(BF16)
    cos = jnp.tile(cos_ref[...], (1, MIX // LANES))
    sin = jnp.tile(sin_ref[...], (1, MIX // LANES))
    lane = lax.broadcasted_iota(jnp.int32, (ts, MIX), 1)
    first_half = (lane % DIFF_COMP) < (DIFF_COMP // 2)

    def group(i):
        return _mm(a, w_ref[:, i * MIX:(i + 1) * MIX])

    def rope(t):
        partner = jnp.where(first_half,
                            pltpu.roll(t, MIX - DIFF_COMP // 2, 1),
                            pltpu.roll(t, DIFF_COMP // 2, 1))
        return t * cos + partner * sin

    qd = rope(group(0)) * (DIFF_COMP ** -0.5)
    kd = rope(group(1))
    vd = group(2)
    qs = group(3) * (HEAD_DIM ** -0.5)
    ks = group(4)
    vs = group(5)
    kd_ref[...] = kd
    vd_ref[...] = vd
    ks_ref[...] = ks
    vs_ref[...] = vs
    comp0 = lax.broadcasted_iota(jnp.int32, (ts, HEAD_DIM), 1) < DIFF_COMP
    for h in range(HEADS):
        sl = slice(h * HEAD_DIM, (h + 1) * HEAD_DIM)
        qh = qd[:, sl]
        qd2_ref[0, h, 0] = jnp.where(comp0, qh, 0.0).astype(BF16)
        qd2_ref[0, h, 1] = jnp.where(comp0, 0.0, qh).astype(BF16)
        kdb_ref[0, h] = kd[:, sl].astype(BF16)
        vdb_ref[0, h] = vd[:, sl].astype(BF16)
        qsb_ref[0, h] = qs[:, sl].astype(BF16)
        ksb_ref[0, h] = ks[:, sl].astype(BF16)
        vsb_ref[0, h] = vs[:, sl].astype(BF16)


def _proj(x, g_mix, w_in_b, cos_t, sin_t, nb, seq):
    t = nb * seq
    ts = min(256, seq)
    nst = seq // ts
    row = pl.BlockSpec((ts, MIX), lambda i: (i, 0))
    hm = pl.BlockSpec((1, HEADS, ts, HEAD_DIM), lambda i: (i // nst, 0, i % nst, 0))
    hm2 = pl.BlockSpec((1, HEADS, 2, ts, HEAD_DIM), lambda i: (i // nst, 0, 0, i % nst, 0))
    rows = jax.ShapeDtypeStruct((t, MIX), F32)
    heads = jax.ShapeDtypeStruct((nb, HEADS, seq, HEAD_DIM), BF16)
    heads2 = jax.ShapeDtypeStruct((nb, HEADS, 2, seq, HEAD_DIM), BF16)
    return pl.pallas_call(
        _proj_body,
        grid=(t // ts,),
        in_specs=[
            pl.BlockSpec((ts, D_MODEL), lambda i: (i, 0)),
            pl.BlockSpec((1, D_MODEL), lambda i: (0, 0)),
            pl.BlockSpec((D_MODEL, 6 * MIX), lambda i: (0, 0)),
            pl.BlockSpec((ts, LANES), lambda i: (i % nst, 0)),
            pl.BlockSpec((ts, LANES), lambda i: (i % nst, 0)),
        ],
        out_specs=[row, row, row, row, hm2, hm, hm, hm, hm, hm],
        out_shape=[rows, rows, rows, rows, heads2, heads, heads, heads, heads, heads],
        compiler_params=_params("parallel"),
        name="proj",
    )(x, g_mix, w_in_b, cos_t, sin_t)


def _diff_update(q2, k, v, visible, carry):
    m, l, acc = carry
    s = _nt(q2, k)
    if visible is not None:
        s = jnp.where(visible, s, -jnp.inf)
    m_new = jnp.maximum(m, jnp.max(s, axis=-1, keepdims=True))
    p = jnp.exp(s - m_new)
    alpha = jnp.exp(m - m_new)
    l = alpha * l + jnp.sum(p, axis=-1, keepdims=True)
    acc = alpha * acc + _mm(p.astype(BF16), v)
    return m_new, l, acc


def _diff_finish(carry, lam, g_subln, tq):
    _, l, acc = carry
    o = acc / l
    d = o[:tq] - lam * o[tq:]
    return _rms(d, g_subln, SUBLN_EPS) * (1.0 - LAM_INIT)


def _suffix_sums(lk, tri):
    hi = lk.astype(BF16)
    lo = (lk - hi.astype(F32)).astype(BF16)
    return _mm(hi, tri) + _mm(lo, tri)


def _sb_update(q, k, v, tri, earlier, carry):
    run, acc = carry
    z = _nt(q, k)
    sp = jnp.maximum(z, 0.0) + jnp.log1p(jnp.exp(-jnp.abs(z)))
    lk = -sp if earlier is None else jnp.where(earlier, -sp, 0.0)
    after = _suffix_sums(lk, tri)
    w = jnp.exp((z - sp) + after + run)
    if earlier is not None:
        w = jnp.where(earlier, w, 0.0)
    acc = acc + _mm(w.astype(BF16), v)
    run = run + after[:, 0:1] + lk[:, 0:1]
    return run, acc


def _tri(n):
    j = lax.broadcasted_iota(jnp.int32, (n, n), 0)
    s = lax.broadcasted_iota(jnp.int32, (n, n), 1)
    return (j > s).astype(BF16)


def _diff_body(lam_ref, q_ref, k_ref, v_ref, g_ref, o_ref, *, tq):
    qi = pl.program_id(2)
    q2 = q_ref[0, 0].reshape(2 * tq, HEAD_DIM)

    def tile(j):
        start = pl.multiple_of(j * tq, tq)
        return k_ref[0, 0, pl.ds(start, tq), :], v_ref[0, 0, pl.ds(start, tq), :]

    carry = (jnp.full((2 * tq, 1), -jnp.inf, F32), jnp.zeros((2 * tq, 1), F32),
             jnp.zeros((2 * tq, HEAD_DIM), F32))
    carry = lax.fori_loop(0, qi, lambda j, c: _diff_update(q2, *tile(j), None, c), carry)
    r = lax.broadcasted_iota(jnp.int32, (2 * tq, tq), 0) % tq
    c = lax.broadcasted_iota(jnp.int32, (2 * tq, tq), 1)
    carry = _diff_update(q2, *tile(qi), (c // CHUNK) <= (r // CHUNK), carry)
    o_ref[0, 0] = _diff_finish(carry, lam_ref[0], g_ref[...], tq)


def _diff_attention(lam, qd2, kdb, vdb, g_subln):
    nb, _, _, seq, _ = qd2.shape
    tq = 256
    kv = pl.BlockSpec((1, 1, seq, HEAD_DIM), lambda b, h, i: (b, h, 0, 0))
    return pl.pallas_call(
        functools.partial(_diff_body, tq=tq),
        grid=(nb, HEADS, seq // tq),
        in_specs=[
            pl.BlockSpec(memory_space=pltpu.SMEM),
            pl.BlockSpec((1, 1, 2, tq, HEAD_DIM), lambda b, h, i: (b, h, 0, i, 0)),
            kv, kv,
            pl.BlockSpec((1, HEAD_DIM), lambda b, h, i: (0, 0)),
        ],
        out_specs=pl.BlockSpec((1, 1, tq, HEAD_DIM), lambda b, h, i: (b, h, i, 0)),
        out_shape=jax.ShapeDtypeStruct((nb, HEADS, seq, HEAD_DIM), F32),
        compiler_params=_params("parallel", "parallel", "arbitrary"),
        name="diff_attention",
    )(lam, qd2, kdb, vdb, g_subln)


def _sb_body(q_ref, k_ref, v_ref, o_ref, *, tq):
    qi = pl.program_id(2)
    q = q_ref[0, 0]
    tri = _tri(tq)

    def tile(j):
        start = pl.multiple_of(j * tq, tq)
        return k_ref[0, 0, pl.ds(start, tq), :], v_ref[0, 0, pl.ds(start, tq), :]

    r = lax.broadcasted_iota(jnp.int32, (tq, tq), 0)
    c = lax.broadcasted_iota(jnp.int32, (tq, tq), 1)
    carry = (jnp.zeros((tq, 1), F32), jnp.zeros((tq, HEAD_DIM), F32))
    carry = _sb_update(q, *tile(qi), tri, c < r, carry)
    carry = lax.fori_loop(0, qi, lambda i, cr: _sb_update(q, *tile(qi - 1 - i), tri, None, cr), carry)
    o_ref[0, 0] = carry[1]


def _sb_attention(qsb, ksb, vsb):
    nb, _, seq, _ = qsb.shape
    tq = 256
    kv = pl.BlockSpec((1, 1, seq, HEAD_DIM), lambda b, h, i: (b, h, 0, 0))
    return pl.pallas_call(
        functools.partial(_sb_body, tq=tq),
        grid=(nb, HEADS, seq // tq),
        in_specs=[pl.BlockSpec((1, 1, tq, HEAD_DIM), lambda b, h, i: (b, h, i, 0)), kv, kv],
        out_specs=pl.BlockSpec((1, 1, tq, HEAD_DIM), lambda b, h, i: (b, h, i, 0)),
        out_shape=jax.ShapeDtypeStruct((nb, HEADS, seq, HEAD_DIM), F32),
        compiler_params=_params("parallel", "parallel", "arbitrary"),
        name="sb_attention",
    )(qsb, ksb, vsb)


SAMPLE_HEADS = 4
SAMPLE_TILE = 256


def _sample_body(lam_ref, qd2_ref, kdn_ref, vdn_ref, qs_ref, ksn_ref, vsn_ref,
                 ckd_ref, cvd_ref, cks_ref, cvs_ref, g_ref, do_ref, so_ref, *, past, nq):
    lam = lam_ref[0]
    tri_c = _tri(SAMPLE_TILE)
    tri_n = _tri(nq)
    i2 = lax.broadcasted_iota(jnp.int32, (2 * nq, nq), 0) % nq
    j2 = lax.broadcasted_iota(jnp.int32, (2 * nq, nq), 1)
    visible_new = ((past + j2) // CHUNK) <= ((past + i2) // CHUNK)
    i1 = lax.broadcasted_iota(jnp.int32, (nq, nq), 0)
    j1 = lax.broadcasted_iota(jnp.int32, (nq, nq), 1)
    earlier_new = j1 < i1
    for h in range(SAMPLE_HEADS):
        sl = slice(h * HEAD_DIM, (h + 1) * HEAD_DIM)
        q2 = qd2_ref[0, h].reshape(2 * nq, HEAD_DIM)
        carry = (jnp.full((2 * nq, 1), -jnp.inf, F32), jnp.zeros((2 * nq, 1), F32),
                 jnp.zeros((2 * nq, HEAD_DIM), F32))
        carry = _diff_update(q2, ckd_ref[0, :, sl].astype(BF16), cvd_ref[0, :, sl].astype(BF16), None, carry)
        carry = _diff_update(q2, kdn_ref[0, h], vdn_ref[0, h], visible_new, carry)
        do_ref[0, h] = _diff_finish(carry, lam, g_ref[...], nq)
        q = qs_ref[0, h]
        carry = (jnp.zeros((nq, 1), F32), jnp.zeros((nq, HEAD_DIM), F32))
        carry = _sb_update(q, ksn_ref[0, h], vsn_ref[0, h], tri_n, earlier_new, carry)
        for t in reversed(range(past // SAMPLE_TILE)):
            rows = slice(t * SAMPLE_TILE, (t + 1) * SAMPLE_TILE)
            carry = _sb_update(q, cks_ref[0, rows, sl].astype(BF16), cvs_ref[0, rows, sl].astype(BF16),
                               tri_c, None, carry)
        so_ref[0, h] = carry[1]


def _sample_attention(lam, qd2, kdb, vdb, qsb, ksb, vsb, ckd, cvd, cks, cvs, g_subln, nb, nq):
    past = ckd.shape[1]
    nhg = HEADS // SAMPLE_HEADS
    hm = pl.BlockSpec((1, SAMPLE_HEADS, nq, HEAD_DIM), lambda b, g: (0, g, b, 0))
    hm2 = pl.BlockSpec((1, SAMPLE_HEADS, 2, nq, HEAD_DIM), lambda b, g: (0, g, 0, b, 0))
    cache = pl.BlockSpec((1, past, SAMPLE_HEADS * HEAD_DIM), lambda b, g: (b, 0, g))
    out = jax.ShapeDtypeStruct((1, HEADS, nb * nq, HEAD_DIM), F32)
    return pl.pallas_call(
        functools.partial(_sample_body, past=past, nq=nq),
        grid=(nb, nhg),
        in_specs=[pl.BlockSpec(memory_space=pltpu.SMEM), hm2, hm, hm, hm, hm, hm,
                  cache, cache, cache, cache,
                  pl.BlockSpec((1, HEAD_DIM), lambda b, g: (0, 0))],
        out_specs=[hm, hm],
        out_shape=[out, out],
        compiler_params=_params("parallel", "parallel"),
        name="sample_attention",
    )(lam, qd2, kdb, vdb, qsb, ksb, vsb, ckd, cvd, cks, cvs, g_subln)


def _topk_rows(s, k):
    n = s.shape[0]
    rows = lax.broadcasted_iota(jnp.int32, s.shape, 0)
    vals, ids = [], []
    for _ in range(k):
        m = jnp.max(s, axis=0, keepdims=True)
        i = jnp.min(jnp.where(s == m, rows, n), axis=0, keepdims=True)
        vals.append(m)
        ids.append(i)
        s = jnp.where(rows == i, -jnp.inf, s)
    return jnp.concatenate(vals, axis=0), jnp.concatenate(ids, axis=0)


_STAIR = [(i, j) for i in range(PEER_TOPK) for j in range(PEER_TOPK) if (i + 1) * (j + 1) <= PEER_TOPK]


def _post_body(x_ref, do_ref, so_ref, wo_ref, gf_ref, wq_ref, sk_ref,
               h1_ref, c_ref, idx_ref, gate_ref, q_scr, idx_scr, gate_scr):
    ts = x_ref.shape[0]
    mixed = jnp.zeros((ts, D_MODEL), F32)
    for h in range(HEADS):
        mixed += _mm(do_ref[0, h].astype(BF16), wo_ref[h * HEAD_DIM:(h + 1) * HEAD_DIM, :])
        mixed += _mm(so_ref[0, h].astype(BF16), wo_ref[MIX + h * HEAD_DIM:MIX + (h + 1) * HEAD_DIM, :])
    h1 = x_ref[...] + mixed
    h1_ref[...] = h1
    c = _rms(h1, gf_ref[...], NORM_EPS)
    c_ref[...] = c
    q = _mm(c.astype(BF16), wq_ref[...])
    for hp in range(2 * PEER_HEADS):
        q_scr[hp] = q[:, hp * PEER_HALF:(hp + 1) * PEER_HALF].astype(BF16)

    npad = -len(_STAIR) % 8

    def head(h, _):
        v1, i1 = _topk_rows(_nt(sk_ref[0], q_scr[2 * h]), PEER_TOPK)
        v2, i2 = _topk_rows(_nt(sk_ref[1], q_scr[2 * h + 1]), PEER_TOPK)
        cand = jnp.concatenate([v1[i:i + 1] + v2[j:j + 1] for i, j in _STAIR]
                               + [jnp.full((npad, ts), -jnp.inf, F32)], axis=0)
        eid = jnp.concatenate([i1[i:i + 1] * PEER_KEYS + i2[j:j + 1] for i, j in _STAIR]
                              + [jnp.zeros((npad, ts), jnp.int32)], axis=0)
        top, pos = _topk_rows(cand, PEER_TOPK)
        rows = lax.broadcasted_iota(jnp.int32, cand.shape, 0)
        sel = jnp.concatenate([jnp.sum(jnp.where(rows == pos[r:r + 1], eid, 0), axis=0, keepdims=True)
                               for r in range(PEER_TOPK)], axis=0)
        e = jnp.exp(top - top[0:1])
        gate_scr[h] = e / jnp.sum(e, axis=0, keepdims=True)
        idx_scr[h] = sel
        return 0

    lax.fori_loop(0, PEER_HEADS, head, 0)
    idx_ref[...] = idx_scr[...].reshape(PEER_SEL, ts).T
    gate_ref[...] = gate_scr[...].reshape(PEER_SEL, ts).T


def _post(x, dout, sout, w_out_b, g_ffn, w_query_b, sub_keys_b, nb, seq):
    t = nb * seq
    ts = min(256, seq)
    nst = seq // ts
    row = pl.BlockSpec((ts, D_MODEL), lambda i: (i, 0))
    hm = pl.BlockSpec((1, HEADS, ts, HEAD_DIM), lambda i: (i // nst, 0, i % nst, 0))
    sel = pl.BlockSpec((ts, PEER_SEL), lambda i: (i, 0))
    full = lambda *shape: pl.BlockSpec(shape, lambda i: (0,) * len(shape))
    return pl.pallas_call(
        _post_body,
        grid=(t // ts,),
        in_specs=[row, hm, hm, full(2 * MIX, D_MODEL), full(1, D_MODEL),
                  full(D_MODEL, 2 * PEER_HEADS * PEER_HALF), full(2, PEER_KEYS, PEER_HALF)],
        out_specs=[row, row, sel, sel],
        out_shape=[jax.ShapeDtypeStruct((t, D_MODEL), F32), jax.ShapeDtypeStruct((t, D_MODEL), F32),
                   jax.ShapeDtypeStruct((t, PEER_SEL), jnp.int32), jax.ShapeDtypeStruct((t, PEER_SEL), F32)],
        scratch_shapes=[pltpu.VMEM((2 * PEER_HEADS, ts, PEER_HALF), BF16),
                        pltpu.VMEM((PEER_HEADS, PEER_TOPK, ts), jnp.int32),
                        pltpu.VMEM((PEER_HEADS, PEER_TOPK, ts), F32)],
        compiler_params=_params("parallel"),
        name="post_peer_select",
    )(x, dout, sout, w_out_b, g_ffn, w_query_b, sub_keys_b)


def _coef_body(gate_ref, dots_ref, o_ref):
    d = dots_ref[...]
    o_ref[...] = gate_ref[...] * (0.5 * d * (1.0 + lax.erf(d * (2.0 ** -0.5))))


def _coef(gate, dots):
    t = gate.shape[0]
    ts = min(2048, t)
    blk = pl.BlockSpec((ts, PEER_SEL), lambda i: (i, 0))
    return pl.pallas_call(
        _coef_body, grid=(t // ts,), in_specs=[blk, blk], out_specs=blk,
        out_shape=jax.ShapeDtypeStruct((t, PEER_SEL), F32),
        compiler_params=_params("parallel"), name="peer_coef",
    )(gate, dots)


SC_CORES = 2
SC_SUBCORES = 16
SC_LANES = 16
SC_WORKERS = SC_CORES * SC_SUBCORES
SC_ROWS = 32
SC_GROUP = 8
SC_CHUNKS = PEER_SEL // SC_ROWS
SC_STEPS = SC_GROUP * SC_CHUNKS
SC_VECS = D_MODEL // SC_LANES


def _sc_mesh():
    return plsc.VectorSubcoreMesh(core_axis_name="c", subcore_axis_name="s",
                                  num_cores=SC_CORES, num_subcores=SC_SUBCORES)


def _sc_pipeline(table_hbm, idx_v, rows_v, sem, compute):
    def gather(step, buf):
        return pltpu.make_async_copy(table_hbm.at[idx_v.at[step]], rows_v.at[buf], sem.at[buf])

    gather(0, 0).start()

    def pair(i, _):
        for buf in range(2):
            step = 2 * i + buf

            @pl.when(step + 1 < SC_STEPS)
            def _():
                gather(step + 1, 1 - buf).start()

            gather(step, buf).wait()
            compute(rows_v.at[buf], step // SC_CHUNKS, step % SC_CHUNKS)
        return 0

    lax.fori_loop(0, SC_STEPS // 2, pair, 0)


def _sc_dots_body(u_hbm, idx_hbm, c_hbm, out_hbm, idx_v, c_v, rows_v, dots_v, sem, *, tpw):
    wid = lax.axis_index("s") * SC_CORES + lax.axis_index("c")
    lane = lax.broadcasted_iota(jnp.int32, (SC_LANES,), 0)
    zero = jnp.zeros((SC_LANES,), F32)

    def compute(rows, tt, ch):
        for half in range(SC_ROWS // SC_LANES):
            outv = zero
            for q in range(2):
                r0 = half * SC_LANES + q * 8

                def vec(kk, accs):
                    off = pl.multiple_of(kk * SC_LANES, SC_LANES)
                    cv = c_v[tt, pl.ds(off, SC_LANES)]
                    return tuple(a + rows[r0 + r, pl.ds(off, SC_LANES)] * cv for r, a in enumerate(accs))

                accs = lax.fori_loop(0, SC_VECS, vec, (zero,) * 8)
                for r in range(8):
                    outv = jnp.where(lane == q * 8 + r, jnp.sum(accs[r]), outv)
            dots_v[tt, pl.ds(pl.multiple_of(ch * SC_ROWS + half * SC_LANES, SC_LANES), SC_LANES)] = outv

    def group(g, _):
        tok0 = pl.multiple_of(wid * tpw + g * SC_GROUP, SC_GROUP)
        pltpu.sync_copy(idx_hbm.at[pl.ds(tok0 * SC_CHUNKS, SC_STEPS)], idx_v)
        pltpu.sync_copy(c_hbm.at[pl.ds(tok0, SC_GROUP)], c_v)
        _sc_pipeline(u_hbm, idx_v, rows_v, sem, compute)
        pltpu.sync_copy(dots_v, out_hbm.at[pl.ds(tok0, SC_GROUP)])
        return 0

    lax.fori_loop(0, tpw // SC_GROUP, group, 0)


def _sc_combine_body(v_hbm, idx_hbm, coef_hbm, out_hbm, idx_v, coef_v, rows_v, acc_v, sem, *, tpw):
    wid = lax.axis_index("s") * SC_CORES + lax.axis_index("c")
    lane = lax.broadcasted_iota(jnp.int32, (SC_LANES,), 0)
    zero = jnp.zeros((SC_LANES,), F32)

    def compute(rows, tt, ch):
        for half in range(SC_ROWS // SC_LANES):
            cf = coef_v[tt, pl.ds(pl.multiple_of(ch * SC_ROWS + half * SC_LANES, SC_LANES), SC_LANES)]
            splat = [jnp.full((SC_LANES,), jnp.sum(jnp.where(lane == r, cf, 0.0)), F32) for r in range(SC_LANES)]

            def vec(kk, _):
                off = pl.multiple_of(kk * SC_LANES, SC_LANES)
                a = acc_v[tt, pl.ds(off, SC_LANES)]
                for r in range(SC_LANES):
                    a = a + rows[half * SC_LANES + r, pl.ds(off, SC_LANES)] * splat[r]
                acc_v[tt, pl.ds(off, SC_LANES)] = a
                return 0

            lax.fori_loop(0, SC_VECS, vec, 0)

    def group(g, _):
        tok0 = pl.multiple_of(wid * tpw + g * SC_GROUP, SC_GROUP)
        pltpu.sync_copy(idx_hbm.at[pl.ds(tok0 * SC_CHUNKS, SC_STEPS)], idx_v)
        pltpu.sync_copy(coef_hbm.at[pl.ds(tok0, SC_GROUP)], coef_v)

        def clear(i, _):
            acc_v[i // SC_VECS, pl.ds(pl.multiple_of((i % SC_VECS) * SC_LANES, SC_LANES), SC_LANES)] = zero
            return 0

        lax.fori_loop(0, SC_GROUP * SC_VECS, clear, 0)
        _sc_pipeline(v_hbm, idx_v, rows_v, sem, compute)
        pltpu.sync_copy(acc_v, out_hbm.at[pl.ds(tok0, SC_GROUP)])
        return 0

    lax.fori_loop(0, tpw // SC_GROUP, group, 0)


def _sc_call(body, table, idx, per_token, out_width, name):
    t = per_token.shape[0]
    tpw = t // SC_WORKERS
    assert tpw % SC_GROUP == 0
    return pl.kernel(
        functools.partial(body, tpw=tpw),
        out_type=jax.ShapeDtypeStruct((t, out_width), F32),
        mesh=_sc_mesh(),
        scratch_types=[pltpu.VMEM((SC_STEPS, SC_ROWS), jnp.int32),
                       pltpu.VMEM((SC_GROUP, per_token.shape[1]), F32),
                       pltpu.VMEM((2, SC_ROWS, D_MODEL), F32),
                       pltpu.VMEM((SC_GROUP, out_width), F32),
                       pltpu.SemaphoreType.DMA((2,))],
        compiler_params=pltpu.CompilerParams(needs_layout_passes=False),
        name=name,
    )(table, idx.reshape(t * SC_CHUNKS, SC_ROWS), per_token)


def _sc_dots(expert_u, idx, c):
    return _sc_call(_sc_dots_body, expert_u, idx, c, PEER_SEL, "peer_dots")


def _sc_combine(expert_v, idx, coef):
    return _sc_call(_sc_combine_body, expert_v, idx, coef, D_MODEL, "peer_combine")


def _ple_body(h1_ref, peer_ref, p_ref, gp_ref, wg_ref, we_ref, gfin_ref, y_ref):
    h = h1_ref[...] + peer_ref[...]
    a = _rms(h, gp_ref[...], NORM_EPS).astype(BF16)
    gate = jax.nn.sigmoid(_mm(a, wg_ref[...]))
    h = h + _mm(p_ref[...].astype(BF16), we_ref[...]) * gate
    y_ref[...] = _rms(h, gfin_ref[...], NORM_EPS)


def _ple(h1, peer, p, g_ple, w_pgate_b, w_ple_b, g_final):
    t = h1.shape[0]
    ts = min(512, t)
    row = pl.BlockSpec((ts, D_MODEL), lambda i: (i, 0))
    full = lambda *shape: pl.BlockSpec(shape, lambda i: (0,) * len(shape))
    return pl.pallas_call(
        _ple_body,
        grid=(t // ts,),
        in_specs=[row, row, pl.BlockSpec((ts, PLE_DIM), lambda i: (i, 0)), full(1, D_MODEL),
                  full(D_MODEL, D_MODEL), full(PLE_DIM, D_MODEL), full(1, D_MODEL)],
        out_specs=row,
        out_shape=jax.ShapeDtypeStruct((t, D_MODEL), F32),
        compiler_params=_params("parallel"),
        name="ple_final",
    )(h1, peer, p, g_ple, w_pgate_b, w_ple_b, g_final)


def _rope_tables(pos):
    half = DIFF_COMP // 2
    inv = ROPE_THETA ** (-jnp.arange(0, DIFF_COMP, 2, dtype=F32) / DIFF_COMP)
    ang = pos.astype(F32)[:, None] * inv[None, :]
    cos = jnp.cos(ang)
    sin = jnp.sin(ang)
    reps = LANES // DIFF_COMP
    del half
    return (jnp.tile(jnp.concatenate([cos, cos], axis=-1), (1, reps)),
            jnp.tile(jnp.concatenate([-sin, sin], axis=-1), (1, reps)))


def kernel(x_prompt, x_sample, cache_diff_k, cache_diff_v, cache_sb_k, cache_sb_v, p_prompt, p_sample, g_mix, w_in, lambda_q1, lambda_k1, lambda_q2, lambda_k2, g_subln, w_out, g_ffn, w_query, sub_keys, expert_u, expert_v, g_ple, w_pgate, w_ple, g_final):
    assert w_in.shape[0] == 1, "single-layer encoder"
    nb, seq, _ = x_prompt.shape
    db, dq, _ = x_sample.shape
    past = cache_diff_k.shape[2]

    lam = (jnp.exp(jnp.sum(lambda_q1[0].astype(F32) * lambda_k1[0].astype(F32)))
           - jnp.exp(jnp.sum(lambda_q2[0].astype(F32) * lambda_k2[0].astype(F32))) + LAM_INIT).reshape(1)
    w_in_b = w_in[0].astype(BF16)
    w_out_b = w_out[0].astype(BF16)
    w_query_b = w_query[0].astype(BF16)
    sub_keys_b = sub_keys[0].astype(BF16)
    w_pgate_b = w_pgate[0].astype(BF16)
    w_ple_b = w_ple[0].astype(BF16)
    g_sub = g_subln[0].reshape(1, HEAD_DIM)
    g_fin = g_final.reshape(1, D_MODEL)

    def tail(x, p, dout, sout, b, s):
        h1, c, idx, gate = _post(x, dout, sout, w_out_b, g_ffn, w_query_b, sub_keys_b, b, s)
        coef = _coef(gate, _sc_dots(expert_u[0], idx, c))
        peer = _sc_combine(expert_v[0], idx, coef)
        return _ple(h1, peer, p, g_ple, w_pgate_b, w_ple_b, g_fin)

    xp = x_prompt.reshape(nb * seq, D_MODEL)
    cos_p, sin_p = _rope_tables(jnp.arange(seq, dtype=jnp.int32))
    kd, vd, ks, vs, qd2, kdb, vdb, qsb, ksb, vsb = _proj(xp, g_mix, w_in_b, cos_p, sin_p, nb, seq)
    dout = _diff_attention(lam, qd2, kdb, vdb, g_sub)
    sout = _sb_attention(qsb, ksb, vsb)
    y_prompt = tail(xp, p_prompt[0].reshape(nb * seq, PLE_DIM), dout, sout, nb, seq).reshape(nb, seq, D_MODEL)
    rows_p = tuple(r.reshape(1, nb, seq, HEADS, HEAD_DIM) for r in (kd, vd, ks, vs))

    ts = db * dq
    xs = x_sample.reshape(ts, D_MODEL)
    cos_s, sin_s = _rope_tables(jnp.tile(past + jnp.arange(dq, dtype=jnp.int32), db))
    kd, vd, ks, vs, qd2, kdb, vdb, qsb, ksb, vsb = _proj(xs, g_mix, w_in_b, cos_s, sin_s, 1, ts)
    caches = [c[0].reshape(db, past, MIX) for c in (cache_diff_k, cache_diff_v, cache_sb_k, cache_sb_v)]
    dout, sout = _sample_attention(lam, qd2, kdb, vdb, qsb, ksb, vsb, *caches, g_sub, db, dq)
    y_sample = tail(xs, p_sample[0].reshape(ts, PLE_DIM), dout, sout, 1, ts).reshape(db, dq, D_MODEL)
    rows_s = tuple(r.reshape(1, db, dq, HEADS, HEAD_DIM) for r in (kd, vd, ks, vs))

    return (y_prompt, y_sample) + rows_p + rows_s
```

```python
import functools
import math

import jax
import jax.numpy as jnp
from jax import lax
from jax.experimental import pallas as pl
from jax.experimental.pallas import tpu as pltpu
from jax.experimental.pallas import tpu_sc as plsc

F32 = jnp.float32
BF16 = jnp.bfloat16

D_MODEL = 1024
HEADS = 8
HEAD_DIM = 64
DIFF_COMP = 32
MIX = HEADS * HEAD_DIM
CHUNK = 64
ROPE_THETA = 10000.0
NORM_EPS = 1e-6
SUBLN_EPS = 1e-5
PEER_HEADS = 8
PEER_KEYS = 128
PEER_TOPK = 16
PEER_HALF = 128
PEER_SEL = PEER_HEADS * PEER_TOPK
PLE_DIM = 256
LAM_INIT = 0.8 - 0.6 * math.exp(-0.3 * 0)
SB_LOG_FLOOR = -104.0

LANES = 128
VMEM_LIMIT = 48 * 1024 * 1024

NT_DIMS = (((1,), (1,)), ((), ()))


def _nt(a, b):
    return lax.dot_general(a, b, NT_DIMS, preferred_element_type=F32)


def _mm(a, b):
    return jnp.dot(a, b, preferred_element_type=F32)


def _rms(x, g, eps):
    return x * lax.rsqrt(jnp.mean(x * x, axis=-1, keepdims=True) + eps) * g


def _params(*sem):
    return pltpu.CompilerParams(dimension_semantics=sem, vmem_limit_bytes=VMEM_LIMIT)


def _with_ones(v):
    n = v.shape[0]
    ones = (lax.broadcasted_iota(jnp.int32, (n, HEAD_DIM), 1) == 0).astype(v.dtype)
    return jnp.concatenate([v, ones], axis=1)


def _proj_body(x_ref, g_ref, w_ref, cos_ref, sin_ref,
               kd_ref, vd_ref, ks_ref, vs_ref,
               qd2_ref, kdb_ref, vdb_ref, qsb_ref, ksb_ref, vsb_ref):
    ts = x_ref.shape[0]
    a = _rms(x_ref[...], g_ref[...], NORM_EPS).astype(BF16)
    cos = jnp.tile(cos_ref[...], (1, MIX // LANES))
    sin = jnp.tile(sin_ref[...], (1, MIX // LANES))
    lane = lax.broadcasted_iota(jnp.int32, (ts, MIX), 1)
    first_half = (lane % DIFF_COMP) < (DIFF_COMP // 2)

    def group(i):
        return _mm(a, w_ref[:, i * MIX:(i + 1) * MIX])

    def rope(t):
        partner = jnp.where(first_half,
                            pltpu.roll(t, MIX - DIFF_COMP // 2, 1),
                            pltpu.roll(t, DIFF_COMP // 2, 1))
        return t * cos + partner * sin

    qd = rope(group(0)) * (DIFF_COMP ** -0.5)
    kd = rope(group(1))
    vd = group(2)
    qs = group(3) * (HEAD_DIM ** -0.5)
    ks = group(4)
    vs = group(5)
    kd_ref[...] = kd
    vd_ref[...] = vd
    ks_ref[...] = ks
    vs_ref[...] = vs
    comp0 = lax.broadcasted_iota(jnp.int32, (ts, HEAD_DIM), 1) < DIFF_COMP
    for h in range(HEADS):
        sl = slice(h * HEAD_DIM, (h + 1) * HEAD_DIM)
        qh = qd[:, sl]
        qd2_ref[0, h, 0] = jnp.where(comp0, qh, 0.0).astype(BF16)
        qd2_ref[0, h, 1] = jnp.where(comp0, 0.0, qh).astype(BF16)
        kdb_ref[0, h] = kd[:, sl].astype(BF16)
        vdb_ref[0, h] = _with_ones(vd[:, sl].astype(BF16))
        qsb_ref[0, h] = qs[:, sl].astype(BF16)
        ksb_ref[0, h] = ks[:, sl].astype(BF16)
        vsb_ref[0, h] = vs[:, sl].astype(BF16)


def _proj(x, g_mix, w_in_b, cos_t, sin_t, nb, seq):
    t = nb * seq
    ts = min(256, seq)
    nst = seq // ts
    row = pl.BlockSpec((ts, MIX), lambda i: (i, 0))
    hm = pl.BlockSpec((1, HEADS, ts, HEAD_DIM), lambda i: (i // nst, 0, i % nst, 0))
    hm2 = pl.BlockSpec((1, HEADS, 2, ts, HEAD_DIM), lambda i: (i // nst, 0, 0, i % nst, 0))
    rows = jax.ShapeDtypeStruct((t, MIX), F32)
    heads = jax.ShapeDtypeStruct((nb, HEADS, seq, HEAD_DIM), BF16)
    heads2 = jax.ShapeDtypeStruct((nb, HEADS, 2, seq, HEAD_DIM), BF16)
    hm_ext = pl.BlockSpec((1, HEADS, ts, 2 * HEAD_DIM), lambda i: (i // nst, 0, i % nst, 0))
    heads_ext = jax.ShapeDtypeStruct((nb, HEADS, seq, 2 * HEAD_DIM), BF16)
    return pl.pallas_call(
        _proj_body,
        grid=(t // ts,),
        in_specs=[
            pl.BlockSpec((ts, D_MODEL), lambda i: (i, 0)),
            pl.BlockSpec((1, D_MODEL), lambda i: (0, 0)),
            pl.BlockSpec((D_MODEL, 6 * MIX), lambda i: (0, 0)),
            pl.BlockSpec((ts, LANES), lambda i: (i % nst, 0)),
            pl.BlockSpec((ts, LANES), lambda i: (i % nst, 0)),
        ],
        out_specs=[row, row, row, row, hm2, hm, hm_ext, hm, hm, hm],
        out_shape=[rows, rows, rows, rows, heads2, heads, heads_ext, heads, heads, heads],
        compiler_params=_params("parallel"),
        name="proj",
    )(x, g_mix, w_in_b, cos_t, sin_t)


def _diff_init(rows):
    return jnp.full((rows, 1), -jnp.inf, F32), jnp.zeros((rows, 2 * HEAD_DIM), F32)


def _diff_update(s, v_ext, carry):
    m, acc = carry
    m_new = jnp.maximum(m, jnp.max(s, axis=-1, keepdims=True))
    p = jnp.exp(s - m_new)
    acc = jnp.exp(m - m_new) * acc + _mm(p.astype(BF16), v_ext)
    return m_new, acc


def _diff_finish(carry, lam, g_subln, tq):
    _, acc = carry
    o = acc[:, :HEAD_DIM] / acc[:, HEAD_DIM:HEAD_DIM + 1]
    d = o[:tq] - lam * o[tq:]
    return _rms(d, g_subln, SUBLN_EPS) * (1.0 - LAM_INIT)


def _suffix_sums(lk, tri):
    hi = lk.astype(BF16)
    lo = (lk - hi.astype(F32)).astype(BF16)
    return _mm(hi, tri) + _mm(lo, tri)


def _sb_update(q, k, v, tri, earlier, carry):
    run, acc = carry
    z = _nt(q, k)
    sp = jnp.maximum(z, 0.0) + jnp.log1p(jnp.exp(-jnp.abs(z)))
    lk = -sp if earlier is None else jnp.where(earlier, -sp, 0.0)
    after = _suffix_sums(lk, tri)
    w = jnp.exp((z - sp) + after + run)
    if earlier is not None:
        w = jnp.where(earlier, w, 0.0)
    acc = acc + _mm(w.astype(BF16), v)
    run = run + after[:, 0:1] + lk[:, 0:1]
    return run, acc


def _tri(n):
    j = lax.broadcasted_iota(jnp.int32, (n, n), 0)
    s = lax.broadcasted_iota(jnp.int32, (n, n), 1)
    return (j > s).astype(BF16)


def _diff_body(lam_ref, q_ref, k_ref, v_ref, g_ref, o_ref, *, tq):
    qi = pl.program_id(2)
    q2 = q_ref[0, 0].reshape(2 * tq, HEAD_DIM)

    def scores(j):
        return _nt(q2, k_ref[0, 0, pl.ds(pl.multiple_of(j * tq, tq), tq), :])

    def step(j, state):
        s, carry = state
        s_next = scores(j + 1)
        return s_next, _diff_update(s, v_ref[0, 0, pl.ds(pl.multiple_of(j * tq, tq), tq), :], carry)

    s, carry = lax.fori_loop(0, qi, step, (scores(0), _diff_init(2 * tq)))
    r = lax.broadcasted_iota(jnp.int32, (2 * tq, tq), 0) % tq
    c = lax.broadcasted_iota(jnp.int32, (2 * tq, tq), 1)
    s = jnp.where((c // CHUNK) <= (r // CHUNK), s, -jnp.inf)
    carry = _diff_update(s, v_ref[0, 0, pl.ds(pl.multiple_of(qi * tq, tq), tq), :], carry)
    o_ref[0, 0] = _diff_finish(carry, lam_ref[0], g_ref[...], tq)


def _diff_attention(lam, qd2, kdb, vdb, g_subln):
    nb, _, _, seq, _ = qd2.shape
    tq = min(512, seq)
    kv = pl.BlockSpec((1, 1, seq, HEAD_DIM), lambda b, h, i: (b, h, 0, 0))
    return pl.pallas_call(
        functools.partial(_diff_body, tq=tq),
        grid=(nb, HEADS, seq // tq),
        in_specs=[
            pl.BlockSpec(memory_space=pltpu.SMEM),
            pl.BlockSpec((1, 1, 2, tq, HEAD_DIM), lambda b, h, i: (b, h, 0, i, 0)),
            kv, pl.BlockSpec((1, 1, seq, 2 * HEAD_DIM), lambda b, h, i: (b, h, 0, 0)),
            pl.BlockSpec((1, HEAD_DIM), lambda b, h, i: (0, 0)),
        ],
        out_specs=pl.BlockSpec((1, 1, tq, HEAD_DIM), lambda b, h, i: (b, h, i, 0)),
        out_shape=jax.ShapeDtypeStruct((nb, HEADS, seq, HEAD_DIM), F32),
        compiler_params=_params("parallel", "parallel", "arbitrary"),
        name="diff_attention",
    )(lam, qd2, kdb, vdb, g_subln)


def _sb_body(q_ref, k_ref, v_ref, o_ref, *, tq):
    qi = pl.program_id(2)
    q = q_ref[0, 0]
    tri = _tri(tq)

    def tile(j):
        start = pl.multiple_of(j * tq, tq)
        return k_ref[0, 0, pl.ds(start, tq), :], v_ref[0, 0, pl.ds(start, tq), :]

    r = lax.broadcasted_iota(jnp.int32, (tq, tq), 0)
    c = lax.broadcasted_iota(jnp.int32, (tq, tq), 1)
    carry = (jnp.zeros((tq, 1), F32), jnp.zeros((tq, HEAD_DIM), F32))
    run, acc = _sb_update(q, *tile(qi), tri, c < r, carry)

    def live(state):
        j, run, _ = state
        return jnp.logical_and(j >= 0, jnp.max(run) > SB_LOG_FLOOR)

    def step(state):
        j, run, acc = state
        run, acc = _sb_update(q, *tile(j), tri, None, (run, acc))
        return j - 1, run, acc

    o_ref[0, 0] = lax.while_loop(live, step, (qi - 1, run, acc))[2]


def _sb_attention(qsb, ksb, vsb):
    nb, _, seq, _ = qsb.shape
    tq = 256
    kv = pl.BlockSpec((1, 1, seq, HEAD_DIM), lambda b, h, i: (b, h, 0, 0))
    return pl.pallas_call(
        functools.partial(_sb_body, tq=tq),
        grid=(nb, HEADS, seq // tq),
        in_specs=[pl.BlockSpec((1, 1, tq, HEAD_DIM), lambda b, h, i: (b, h, i, 0)), kv, kv],
        out_specs=pl.BlockSpec((1, 1, tq, HEAD_DIM), lambda b, h, i: (b, h, i, 0)),
        out_shape=jax.ShapeDtypeStruct((nb, HEADS, seq, HEAD_DIM), F32),
        compiler_params=_params("parallel", "parallel", "arbitrary"),
        name="sb_attention",
    )(qsb, ksb, vsb)


SAMPLE_HEADS = 4
SAMPLE_TILE = 256


def _sample_body(lam_ref, qd2_ref, kdn_ref, vdn_ref, qs_ref, ksn_ref, vsn_ref,
                 ckd_ref, cvd_ref, cks_ref, cvs_ref, g_ref, do_ref, so_ref, *, past, nq):
    lam = lam_ref[0]
    tri_c = _tri(SAMPLE_TILE)
    tri_n = _tri(nq)
    i2 = lax.broadcasted_iota(jnp.int32, (2 * nq, nq), 0) % nq
    j2 = lax.broadcasted_iota(jnp.int32, (2 * nq, nq), 1)
    visible_new = ((past + j2) // CHUNK) <= ((past + i2) // CHUNK)
    i1 = lax.broadcasted_iota(jnp.int32, (nq, nq), 0)
    j1 = lax.broadcasted_iota(jnp.int32, (nq, nq), 1)
    earlier_new = j1 < i1
    for h in range(SAMPLE_HEADS):
        sl = slice(h * HEAD_DIM, (h + 1) * HEAD_DIM)
        q2 = qd2_ref[0, h].reshape(2 * nq, HEAD_DIM)
        carry = _diff_update(_nt(q2, ckd_ref[0, :, sl].astype(BF16)),
                             _with_ones(cvd_ref[0, :, sl].astype(BF16)), _diff_init(2 * nq))
        s_new = jnp.where(visible_new, _nt(q2, kdn_ref[0, h]), -jnp.inf)
        carry = _diff_update(s_new, vdn_ref[0, h], carry)
        do_ref[0, h] = _diff_finish(carry, lam, g_ref[...], nq)
        q = qs_ref[0, h]
        carry = (jnp.zeros((nq, 1), F32), jnp.zeros((nq, HEAD_DIM), F32))
        carry = _sb_update(q, ksn_ref[0, h], vsn_ref[0, h], tri_n, earlier_new, carry)
        for t in reversed(range(past // SAMPLE_TILE)):
            rows = slice(t * SAMPLE_TILE, (t + 1) * SAMPLE_TILE)
            carry = _sb_update(q, cks_ref[0, rows, sl].astype(BF16), cvs_ref[0, rows, sl].astype(BF16),
                               tri_c, None, carry)
        so_ref[0, h] = carry[1]


def _sample_attention(lam, qd2, kdb, vdb, qsb, ksb, vsb, ckd, cvd, cks, cvs, g_subln, nb, nq):
    past = ckd.shape[1]
    nhg = HEADS // SAMPLE_HEADS
    hm = pl.BlockSpec((1, SAMPLE_HEADS, nq, HEAD_DIM), lambda b, g: (0, g, b, 0))
    hm2 = pl.BlockSpec((1, SAMPLE_HEADS, 2, nq, HEAD_DIM), lambda b, g: (0, g, 0, b, 0))
    cache = pl.BlockSpec((1, past, SAMPLE_HEADS * HEAD_DIM), lambda b, g: (b, 0, g))
    out = jax.ShapeDtypeStruct((1, HEADS, nb * nq, HEAD_DIM), F32)
    return pl.pallas_call(
        functools.partial(_sample_body, past=past, nq=nq),
        grid=(nb, nhg),
        in_specs=[pl.BlockSpec(memory_space=pltpu.SMEM), hm2, hm,
                  pl.BlockSpec((1, SAMPLE_HEADS, nq, 2 * HEAD_DIM), lambda b, g: (0, g, b, 0)), hm, hm, hm,
                  cache, cache, cache, cache,
                  pl.BlockSpec((1, HEAD_DIM), lambda b, g: (0, 0))],
        out_specs=[hm, hm],
        out_shape=[out, out],
        compiler_params=_params("parallel", "parallel"),
        name="sample_attention",
    )(lam, qd2, kdb, vdb, qsb, ksb, vsb, ckd, cvd, cks, cvs, g_subln)


def _topk_rows(s, k):
    n = s.shape[0]
    rows = lax.broadcasted_iota(jnp.int32, s.shape, 0)
    vals, ids = [], []
    for _ in range(k):
        m = jnp.max(s, axis=0, keepdims=True)
        i = jnp.min(jnp.where(s == m, rows, n), axis=0, keepdims=True)
        vals.append(m)
        ids.append(i)
        s = jnp.where(rows == i, -jnp.inf, s)
    return jnp.concatenate(vals, axis=0), jnp.concatenate(ids, axis=0)


_STAIR = [(i, j) for i in range(PEER_TOPK) for j in range(PEER_TOPK) if (i + 1) * (j + 1) <= PEER_TOPK]


def _post_body(x_ref, do_ref, so_ref, wo_ref, gf_ref, wq_ref, sk_ref,
               h1_ref, c_ref, idx_ref, gate_ref, q_scr, idx_scr, gate_scr):
    ts = x_ref.shape[0]
    mixed = jnp.zeros((ts, D_MODEL), F32)
    for h in range(HEADS):
        mixed += _mm(do_ref[0, h].astype(BF16), wo_ref[h * HEAD_DIM:(h + 1) * HEAD_DIM, :])
        mixed += _mm(so_ref[0, h].astype(BF16), wo_ref[MIX + h * HEAD_DIM:MIX + (h + 1) * HEAD_DIM, :])
    h1 = x_ref[...] + mixed
    h1_ref[...] = h1
    c = _rms(h1, gf_ref[...], NORM_EPS)
    c_ref[...] = c
    q = _mm(c.astype(BF16), wq_ref[...])
    for hp in range(2 * PEER_HEADS):
        q_scr[hp] = q[:, hp * PEER_HALF:(hp + 1) * PEER_HALF].astype(BF16)

    npad = -len(_STAIR) % 8

    def head(h, _):
        v1, i1 = _topk_rows(_nt(sk_ref[0], q_scr[2 * h]), PEER_TOPK)
        v2, i2 = _topk_rows(_nt(sk_ref[1], q_scr[2 * h + 1]), PEER_TOPK)
        cand = jnp.concatenate([v1[i:i + 1] + v2[j:j + 1] for i, j in _STAIR]
                               + [jnp.full((npad, ts), -jnp.inf, F32)], axis=0)
        eid = jnp.concatenate([i1[i:i + 1] * PEER_KEYS + i2[j:j + 1] for i, j in _STAIR]
                              + [jnp.zeros((npad, ts), jnp.int32)], axis=0)
        top, pos = _topk_rows(cand, PEER_TOPK)
        rows = lax.broadcasted_iota(jnp.int32, cand.shape, 0)
        sel = jnp.concatenate([jnp.sum(jnp.where(rows == pos[r:r + 1], eid, 0), axis=0, keepdims=True)
                               for r in range(PEER_TOPK)], axis=0)
        e = jnp.exp(top - top[0:1])
        gate_scr[h] = e / jnp.sum(e, axis=0, keepdims=True)
        idx_scr[h] = sel
        return 0

    lax.fori_loop(0, PEER_HEADS, head, 0)
    idx_ref[...] = idx_scr[...].reshape(PEER_SEL, ts).T
    gate_ref[...] = gate_scr[...].reshape(PEER_SEL, ts).T


def _post(x, dout, sout, w_out_b, g_ffn, w_query_b, sub_keys_b, nb, seq):
    t = nb * seq
    ts = min(256, seq)
    nst = seq // ts
    row = pl.BlockSpec((ts, D_MODEL), lambda i: (i, 0))
    hm = pl.BlockSpec((1, HEADS, ts, HEAD_DIM), lambda i: (i // nst, 0, i % nst, 0))
    sel = pl.BlockSpec((ts, PEER_SEL), lambda i: (i, 0))
    full = lambda *shape: pl.BlockSpec(shape, lambda i: (0,) * len(shape))
    return pl.pallas_call(
        _post_body,
        grid=(t // ts,),
        in_specs=[row, hm, hm, full(2 * MIX, D_MODEL), full(1, D_MODEL),
                  full(D_MODEL, 2 * PEER_HEADS * PEER_HALF), full(2, PEER_KEYS, PEER_HALF)],
        out_specs=[row, row, sel, sel],
        out_shape=[jax.ShapeDtypeStruct((t, D_MODEL), F32), jax.ShapeDtypeStruct((t, D_MODEL), F32),
                   jax.ShapeDtypeStruct((t, PEER_SEL), jnp.int32), jax.ShapeDtypeStruct((t, PEER_SEL), F32)],
        scratch_shapes=[pltpu.VMEM((2 * PEER_HEADS, ts, PEER_HALF), BF16),
                        pltpu.VMEM((PEER_HEADS, PEER_TOPK, ts), jnp.int32),
                        pltpu.VMEM((PEER_HEADS, PEER_TOPK, ts), F32)],
        compiler_params=_params("parallel"),
        name="post_peer_select",
    )(x, dout, sout, w_out_b, g_ffn, w_query_b, sub_keys_b)


def _coef_body(gate_ref, dots_ref, o_ref):
    d = dots_ref[...]
    o_ref[...] = gate_ref[...] * (0.5 * d * (1.0 + lax.erf(d * (2.0 ** -0.5))))


def _coef(gate, dots):
    t = gate.shape[0]
    ts = min(2048, t)
    blk = pl.BlockSpec((ts, PEER_SEL), lambda i: (i, 0))
    return pl.pallas_call(
        _coef_body, grid=(t // ts,), in_specs=[blk, blk], out_specs=blk,
        out_shape=jax.ShapeDtypeStruct((t, PEER_SEL), F32),
        compiler_params=_params("parallel"), name="peer_coef",
    )(gate, dots)


SC_CORES = 2
SC_SUBCORES = 16
SC_LANES = 16
SC_WORKERS = SC_CORES * SC_SUBCORES
SC_ROWS = 32
SC_GROUP = 8
SC_CHUNKS = PEER_SEL // SC_ROWS
SC_STEPS = SC_GROUP * SC_CHUNKS
SC_VECS = D_MODEL // SC_LANES


def _sc_mesh():
    return plsc.VectorSubcoreMesh(core_axis_name="c", subcore_axis_name="s",
                                  num_cores=SC_CORES, num_subcores=SC_SUBCORES)


def _sc_pipeline(table_hbm, idx_v, rows_v, sem, compute):
    def gather(step, buf):
        return pltpu.make_async_copy(table_hbm.at[idx_v.at[step]], rows_v.at[buf], sem.at[buf])

    gather(0, 0).start()

    def pair(i, _):
        for buf in range(2):
            step = 2 * i + buf

            @pl.when(step + 1 < SC_STEPS)
            def _():
                gather(step + 1, 1 - buf).start()

            gather(step, buf).wait()
            compute(rows_v.at[buf], step // SC_CHUNKS, step % SC_CHUNKS)
        return 0

    lax.fori_loop(0, SC_STEPS // 2, pair, 0)


def _sc_dots_body(u_hbm, idx_hbm, c_hbm, out_hbm, idx_v, c_v, rows_v, dots_v, sem, *, tpw):
    wid = lax.axis_index("s") * SC_CORES + lax.axis_index("c")
    lane = lax.broadcasted_iota(jnp.int32, (SC_LANES,), 0)
    zero = jnp.zeros((SC_LANES,), F32)

    def compute(rows, tt, ch):
        for half in range(SC_ROWS // SC_LANES):
            outv = zero
            for q in range(2):
                r0 = half * SC_LANES + q * 8

                def vec(kk, accs):
                    off = pl.multiple_of(kk * SC_LANES, SC_LANES)
                    cv = c_v[tt, pl.ds(off, SC_LANES)]
                    return tuple(a + rows[r0 + r, pl.ds(off, SC_LANES)] * cv for r, a in enumerate(accs))

                accs = lax.fori_loop(0, SC_VECS, vec, (zero,) * 8)
                for r in range(8):
                    outv = jnp.where(lane == q * 8 + r, jnp.sum(accs[r]), outv)
            dots_v[tt, pl.ds(pl.multiple_of(ch * SC_ROWS + half * SC_LANES, SC_LANES), SC_LANES)] = outv

    def group(g, _):
        tok0 = pl.multiple_of(wid * tpw + g * SC_GROUP, SC_GROUP)
        pltpu.sync_copy(idx_hbm.at[pl.ds(tok0 * SC_CHUNKS, SC_STEPS)], idx_v)
        pltpu.sync_copy(c_hbm.at[pl.ds(tok0, SC_GROUP)], c_v)
        _sc_pipeline(u_hbm, idx_v, rows_v, sem, compute)
        pltpu.sync_copy(dots_v, out_hbm.at[pl.ds(tok0, SC_GROUP)])
        return 0

    lax.fori_loop(0, tpw // SC_GROUP, group, 0)


def _sc_combine_body(v_hbm, idx_hbm, coef_hbm, out_hbm, idx_v, coef_v, rows_v, acc_v, sem, *, tpw):
    wid = lax.axis_index("s") * SC_CORES + lax.axis_index("c")
    lane = lax.broadcasted_iota(jnp.int32, (SC_LANES,), 0)
    zero = jnp.zeros((SC_LANES,), F32)

    def compute(rows, tt, ch):
        for half in range(SC_ROWS // SC_LANES):
            cf = coef_v[tt, pl.ds(pl.multiple_of(ch * SC_ROWS + half * SC_LANES, SC_LANES), SC_LANES)]
            splat = [jnp.full((SC_LANES,), jnp.sum(jnp.where(lane == r, cf, 0.0)), F32) for r in range(SC_LANES)]

            def vec(kk, _):
                off = pl.multiple_of(kk * SC_LANES, SC_LANES)
                terms = [rows[half * SC_LANES + r, pl.ds(off, SC_LANES)] * splat[r] for r in range(SC_LANES)]
                while len(terms) > 1:
                    terms = [a + b for a, b in zip(terms[0::2], terms[1::2])]
                acc_v[tt, pl.ds(off, SC_LANES)] = acc_v[tt, pl.ds(off, SC_LANES)] + terms[0]
                return 0

            lax.fori_loop(0, SC_VECS, vec, 0)

    def group(g, _):
        tok0 = pl.multiple_of(wid * tpw + g * SC_GROUP, SC_GROUP)
        pltpu.sync_copy(idx_hbm.at[pl.ds(tok0 * SC_CHUNKS, SC_STEPS)], idx_v)
        pltpu.sync_copy(coef_hbm.at[pl.ds(tok0, SC_GROUP)], coef_v)

        def clear(i, _):
            acc_v[i // SC_VECS, pl.ds(pl.multiple_of((i % SC_VECS) * SC_LANES, SC_LANES), SC_LANES)] = zero
            return 0

        lax.fori_loop(0, SC_GROUP * SC_VECS, clear, 0)
        _sc_pipeline(v_hbm, idx_v, rows_v, sem, compute)
        pltpu.sync_copy(acc_v, out_hbm.at[pl.ds(tok0, SC_GROUP)])
        return 0

    lax.fori_loop(0, tpw // SC_GROUP, group, 0)


def _sc_call(body, table, idx, per_token, out_width, name):
    t = per_token.shape[0]
    tpw = t // SC_WORKERS
    assert tpw % SC_GROUP == 0
    return pl.kernel(
        functools.partial(body, tpw=tpw),
        out_type=jax.ShapeDtypeStruct((t, out_width), F32),
        mesh=_sc_mesh(),
        scratch_types=[pltpu.VMEM((SC_STEPS, SC_ROWS), jnp.int32),
                       pltpu.VMEM((SC_GROUP, per_token.shape[1]), F32),
                       pltpu.VMEM((2, SC_ROWS, D_MODEL), F32),
                       pltpu.VMEM((SC_GROUP, out_width), F32),
                       pltpu.SemaphoreType.DMA((2,))],
        compiler_params=pltpu.CompilerParams(needs_layout_passes=False),
        name=name,
    )(table, idx.reshape(t * SC_CHUNKS, SC_ROWS), per_token)


def _sc_dots(expert_u, idx, c):
    return _sc_call(_sc_dots_body, expert_u, idx, c, PEER_SEL, "peer_dots")


def _sc_combine(expert_v, idx, coef):
    return _sc_call(_sc_combine_body, expert_v, idx, coef, D_MODEL, "peer_combine")


def _ple_body(h1_ref, peer_ref, p_ref, gp_ref, wg_ref, we_ref, gfin_ref, y_ref):
    h = h1_ref[...] + peer_ref[...]
    a = _rms(h, gp_ref[...], NORM_EPS).astype(BF16)
    gate = jax.nn.sigmoid(_mm(a, wg_ref[...]))
    h = h + _mm(p_ref[...].astype(BF16), we_ref[...]) * gate
    y_ref[...] = _rms(h, gfin_ref[...], NORM_EPS)


def _ple(h1, peer, p, g_ple, w_pgate_b, w_ple_b, g_final):
    t = h1.shape[0]
    ts = min(512, t)
    row = pl.BlockSpec((ts, D_MODEL), lambda i: (i, 0))
    full = lambda *shape: pl.BlockSpec(shape, lambda i: (0,) * len(shape))
    return pl.pallas_call(
        _ple_body,
        grid=(t // ts,),
        in_specs=[row, row, pl.BlockSpec((ts, PLE_DIM), lambda i: (i, 0)), full(1, D_MODEL),
                  full(D_MODEL, D_MODEL), full(PLE_DIM, D_MODEL), full(1, D_MODEL)],
        out_specs=row,
        out_shape=jax.ShapeDtypeStruct((t, D_MODEL), F32),
        compiler_params=_params("parallel"),
        name="ple_final",
    )(h1, peer, p, g_ple, w_pgate_b, w_ple_b, g_final)


def _rope_tables(pos):
    half = DIFF_COMP // 2
    inv = ROPE_THETA ** (-jnp.arange(0, DIFF_COMP, 2, dtype=F32) / DIFF_COMP)
    ang = pos.astype(F32)[:, None] * inv[None, :]
    cos = jnp.cos(ang)
    sin = jnp.sin(ang)
    reps = LANES // DIFF_COMP
    del half
    return (jnp.tile(jnp.concatenate([cos, cos], axis=-1), (1, reps)),
            jnp.tile(jnp.concatenate([-sin, sin], axis=-1), (1, reps)))


def kernel(x_prompt, x_sample, cache_diff_k, cache_diff_v, cache_sb_k, cache_sb_v, p_prompt, p_sample, g_mix, w_in, lambda_q1, lambda_k1, lambda_q2, lambda_k2, g_subln, w_out, g_ffn, w_query, sub_keys, expert_u, expert_v, g_ple, w_pgate, w_ple, g_final):
    assert w_in.shape[0] == 1, "single-layer encoder"
    nb, seq, _ = x_prompt.shape
    db, dq, _ = x_sample.shape
    past = cache_diff_k.shape[2]

    lam = (jnp.exp(jnp.sum(lambda_q1[0].astype(F32) * lambda_k1[0].astype(F32)))
           - jnp.exp(jnp.sum(lambda_q2[0].astype(F32) * lambda_k2[0].astype(F32))) + LAM_INIT).reshape(1)
    w_in_b = w_in[0].astype(BF16)
    w_out_b = w_out[0].astype(BF16)
    w_query_b = w_query[0].astype(BF16)
    sub_keys_b = sub_keys[0].astype(BF16)
    w_pgate_b = w_pgate[0].astype(BF16)
    w_ple_b = w_ple[0].astype(BF16)
    g_sub = g_subln[0].reshape(1, HEAD_DIM)
    g_fin = g_final.reshape(1, D_MODEL)

    def tail(x, p, dout, sout, b, s):
        h1, c, idx, gate = _post(x, dout, sout, w_out_b, g_ffn, w_query_b, sub_keys_b, b, s)
        coef = _coef(gate, _sc_dots(expert_u[0], idx, c))
        peer = _sc_combine(expert_v[0], idx, coef)
        return _ple(h1, peer, p, g_ple, w_pgate_b, w_ple_b, g_fin)

    xp = x_prompt.reshape(nb * seq, D_MODEL)
    cos_p, sin_p = _rope_tables(jnp.arange(seq, dtype=jnp.int32))
    kd, vd, ks, vs, qd2, kdb, vdb, qsb, ksb, vsb = _proj(xp, g_mix, w_in_b, cos_p, sin_p, nb, seq)
    dout = _diff_attention(lam, qd2, kdb, vdb, g_sub)
    sout = _sb_attention(qsb, ksb, vsb)
    y_prompt = tail(xp, p_prompt[0].reshape(nb * seq, PLE_DIM), dout, sout, nb, seq).reshape(nb, seq, D_MODEL)
    rows_p = tuple(r.reshape(1, nb, seq, HEADS, HEAD_DIM) for r in (kd, vd, ks, vs))

    ts = db * dq
    xs = x_sample.reshape(ts, D_MODEL)
    cos_s, sin_s = _rope_tables(jnp.tile(past + jnp.arange(dq, dtype=jnp.int32), db))
    kd, vd, ks, vs, qd2, kdb, vdb, qsb, ksb, vsb = _proj(xs, g_mix, w_in_b, cos_s, sin_s, 1, ts)
    caches = [c[0].reshape(db, past, MIX) for c in (cache_diff_k, cache_diff_v, cache_sb_k, cache_sb_v)]
    dout, sout = _sample_attention(lam, qd2, kdb, vdb, qsb, ksb, vsb, *caches, g_sub, db, dq)
    y_sample = tail(xs, p_sample[0].reshape(ts, PLE_DIM), dout, sout, 1, ts).reshape(db, dq, D_MODEL)
    rows_s = tuple(r.reshape(1, db, dq, HEADS, HEAD_DIM) for r in (kd, vd, ks, vs))

    return (y_prompt, y_sample) + rows_p + rows_s
```

```python
import functools
import math

import jax
import jax.numpy as jnp
from jax import lax
from jax.experimental import pallas as pl
from jax.experimental.pallas import tpu as pltpu
from jax.experimental.pallas import tpu_sc as plsc

F32 = jnp.float32
BF16 = jnp.bfloat16

D_MODEL = 1024
HEADS = 8
HEAD_DIM = 64
DIFF_COMP = 32
MIX = HEADS * HEAD_DIM
CHUNK = 64
ROPE_THETA = 10000.0
NORM_EPS = 1e-6
SUBLN_EPS = 1e-5
PEER_HEADS = 8
PEER_KEYS = 128
PEER_TOPK = 16
PEER_HALF = 128
PEER_SEL = PEER_HEADS * PEER_TOPK
PLE_DIM = 256
LAM_INIT = 0.8 - 0.6 * math.exp(-0.3 * 0)
SB_LOG_FLOOR = -104.0

LANES = 128
VMEM_LIMIT = 48 * 1024 * 1024

NT_DIMS = (((1,), (1,)), ((), ()))


def _nt(a, b):
    return lax.dot_general(a, b, NT_DIMS, preferred_element_type=F32)


def _mm(a, b):
    return jnp.dot(a, b, preferred_element_type=F32)


def _rms(x, g, eps):
    return x * lax.rsqrt(jnp.mean(x * x, axis=-1, keepdims=True) + eps) * g


def _params(*sem):
    return pltpu.CompilerParams(dimension_semantics=sem, vmem_limit_bytes=VMEM_LIMIT)


def _with_ones(v):
    n = v.shape[0]
    ones = (lax.broadcasted_iota(jnp.int32, (n, HEAD_DIM), 1) == 0).astype(v.dtype)
    return jnp.concatenate([v, ones], axis=1)


def _proj_body(x_ref, g_ref, w_ref, cos_ref, sin_ref,
               kd_ref, vd_ref, ks_ref, vs_ref,
               qd2_ref, kdb_ref, vdb_ref, qsb_ref, ksb_ref, vsb_ref):
    ts = x_ref.shape[0]
    a = _rms(x_ref[...], g_ref[...], NORM_EPS).astype(BF16)
    cos = jnp.tile(cos_ref[...], (1, MIX // LANES))
    sin = jnp.tile(sin_ref[...], (1, MIX // LANES))
    lane = lax.broadcasted_iota(jnp.int32, (ts, MIX), 1)
    first_half = (lane % DIFF_COMP) < (DIFF_COMP // 2)

    def group(i):
        return _mm(a, w_ref[:, i * MIX:(i + 1) * MIX])

    def rope(t):
        partner = jnp.where(first_half,
                            pltpu.roll(t, MIX - DIFF_COMP // 2, 1),
                            pltpu.roll(t, DIFF_COMP // 2, 1))
        return t * cos + partner * sin

    qd = rope(group(0)) * (DIFF_COMP ** -0.5)
    kd = rope(group(1))
    vd = group(2)
    qs = group(3) * (HEAD_DIM ** -0.5)
    ks = group(4)
    vs = group(5)
    kd_ref[...] = kd
    vd_ref[...] = vd
    ks_ref[...] = ks
    vs_ref[...] = vs
    comp0 = lax.broadcasted_iota(jnp.int32, (ts, HEAD_DIM), 1) < DIFF_COMP
    for h in range(HEADS):
        sl = slice(h * HEAD_DIM, (h + 1) * HEAD_DIM)
        qh = qd[:, sl]
        qd2_ref[0, h, 0] = jnp.where(comp0, qh, 0.0).astype(BF16)
        qd2_ref[0, h, 1] = jnp.where(comp0, 0.0, qh).astype(BF16)
        kdb_ref[0, h] = kd[:, sl].astype(BF16)
        vdb_ref[0, h] = _with_ones(vd[:, sl].astype(BF16))
        qsb_ref[0, h] = qs[:, sl].astype(BF16)
        ksb_ref[0, h] = ks[:, sl].astype(BF16)
        vsb_ref[0, h] = vs[:, sl].astype(BF16)


def _proj(x, g_mix, w_in_b, cos_t, sin_t, nb, seq):
    t = nb * seq
    ts = min(256, seq)
    nst = seq // ts
    row = pl.BlockSpec((ts, MIX), lambda i: (i, 0))
    hm = pl.BlockSpec((1, HEADS, ts, HEAD_DIM), lambda i: (i // nst, 0, i % nst, 0))
    hm2 = pl.BlockSpec((1, HEADS, 2, ts, HEAD_DIM), lambda i: (i // nst, 0, 0, i % nst, 0))
    rows = jax.ShapeDtypeStruct((t, MIX), F32)
    heads = jax.ShapeDtypeStruct((nb, HEADS, seq, HEAD_DIM), BF16)
    heads2 = jax.ShapeDtypeStruct((nb, HEADS, 2, seq, HEAD_DIM), BF16)
    hm_ext = pl.BlockSpec((1, HEADS, ts, 2 * HEAD_DIM), lambda i: (i // nst, 0, i % nst, 0))
    heads_ext = jax.ShapeDtypeStruct((nb, HEADS, seq, 2 * HEAD_DIM), BF16)
    return pl.pallas_call(
        _proj_body,
        grid=(t // ts,),
        in_specs=[
            pl.BlockSpec((ts, D_MODEL), lambda i: (i, 0)),
            pl.BlockSpec((1, D_MODEL), lambda i: (0, 0)),
            pl.BlockSpec((D_MODEL, 6 * MIX), lambda i: (0, 0)),
            pl.BlockSpec((ts, LANES), lambda i: (i % nst, 0)),
            pl.BlockSpec((ts, LANES), lambda i: (i % nst, 0)),
        ],
        out_specs=[row, row, row, row, hm2, hm, hm_ext, hm, hm, hm],
        out_shape=[rows, rows, rows, rows, heads2, heads, heads_ext, heads, heads, heads],
        compiler_params=_params("parallel"),
        name="proj",
    )(x, g_mix, w_in_b, cos_t, sin_t)


def _diff_init(rows):
    return jnp.full((rows, 1), -jnp.inf, F32), jnp.zeros((rows, 2 * HEAD_DIM), F32)


def _diff_update(s, v_ext, carry):
    m, acc = carry
    m_new = jnp.maximum(m, jnp.max(s, axis=-1, keepdims=True))
    p = jnp.exp(s - m_new)
    acc = jnp.exp(m - m_new) * acc + _mm(p.astype(BF16), v_ext)
    return m_new, acc


def _diff_finish(carry, lam, g_subln, tq):
    _, acc = carry
    o = acc[:, :HEAD_DIM] / acc[:, HEAD_DIM:HEAD_DIM + 1]
    d = o[:tq] - lam * o[tq:]
    return _rms(d, g_subln, SUBLN_EPS) * (1.0 - LAM_INIT)


def _suffix_sums(lk, tri):
    hi = lk.astype(BF16)
    lo = (lk - hi.astype(F32)).astype(BF16)
    return _mm(hi, tri) + _mm(lo, tri)


def _sb_update(q, k, v, tri, earlier, carry):
    run, acc = carry
    z = _nt(q, k)
    sp = jnp.maximum(z, 0.0) + jnp.log1p(jnp.exp(-jnp.abs(z)))
    lk = -sp if earlier is None else jnp.where(earlier, -sp, 0.0)
    after = _suffix_sums(lk, tri)
    w = jnp.exp((z - sp) + after + run)
    if earlier is not None:
        w = jnp.where(earlier, w, 0.0)
    acc = acc + _mm(w.astype(BF16), v)
    run = run + after[:, 0:1] + lk[:, 0:1]
    return run, acc


def _tri(n):
    j = lax.broadcasted_iota(jnp.int32, (n, n), 0)
    s = lax.broadcasted_iota(jnp.int32, (n, n), 1)
    return (j > s).astype(BF16)


def _diff_body(lam_ref, q_ref, k_ref, v_ref, g_ref, o_ref, *, tq):
    qi = pl.program_id(1)
    q2 = q_ref[0, 0].reshape(2 * tq, HEAD_DIM)

    def scores(j):
        return _nt(q2, k_ref[0, 0, pl.ds(pl.multiple_of(j * tq, tq), tq), :])

    def step(j, state):
        s, carry = state
        s_next = scores(j + 1)
        return s_next, _diff_update(s, v_ref[0, 0, pl.ds(pl.multiple_of(j * tq, tq), tq), :], carry)

    s, carry = lax.fori_loop(0, qi, step, (scores(0), _diff_init(2 * tq)))
    r = lax.broadcasted_iota(jnp.int32, (2 * tq, tq), 0) % tq
    c = lax.broadcasted_iota(jnp.int32, (2 * tq, tq), 1)
    s = jnp.where((c // CHUNK) <= (r // CHUNK), s, -jnp.inf)
    carry = _diff_update(s, v_ref[0, 0, pl.ds(pl.multiple_of(qi * tq, tq), tq), :], carry)
    o_ref[0, 0] = _diff_finish(carry, lam_ref[0], g_ref[...], tq)


def _diff_attention(lam, qd2, kdb, vdb, g_subln, b):
    seq = qd2.shape[3]
    tq = min(512, seq)
    return pl.pallas_call(
        functools.partial(_diff_body, tq=tq),
        grid=(HEADS, seq // tq),
        in_specs=[
            pl.BlockSpec(memory_space=pltpu.SMEM),
            pl.BlockSpec((1, 1, 2, tq, HEAD_DIM), lambda h, i: (b, h, 0, i, 0)),
            pl.BlockSpec((1, 1, seq, HEAD_DIM), lambda h, i: (b, h, 0, 0)),
            pl.BlockSpec((1, 1, seq, 2 * HEAD_DIM), lambda h, i: (b, h, 0, 0)),
            pl.BlockSpec((1, HEAD_DIM), lambda h, i: (0, 0)),
        ],
        out_specs=pl.BlockSpec((1, 1, tq, HEAD_DIM), lambda h, i: (0, h, i, 0)),
        out_shape=jax.ShapeDtypeStruct((1, HEADS, seq, HEAD_DIM), F32),
        compiler_params=_params("parallel", "arbitrary"),
        name="diff_attention",
    )(lam, qd2, kdb, vdb, g_subln)


def _sb_body(q_ref, k_ref, v_ref, o_ref, *, tq):
    qi = pl.program_id(1)
    q = q_ref[0, 0]
    tri = _tri(tq)

    def tile(j):
        start = pl.multiple_of(j * tq, tq)
        return k_ref[0, 0, pl.ds(start, tq), :], v_ref[0, 0, pl.ds(start, tq), :]

    r = lax.broadcasted_iota(jnp.int32, (tq, tq), 0)
    c = lax.broadcasted_iota(jnp.int32, (tq, tq), 1)
    carry = (jnp.zeros((tq, 1), F32), jnp.zeros((tq, HEAD_DIM), F32))
    run, acc = _sb_update(q, *tile(qi), tri, c < r, carry)

    def live(state):
        j, run, _ = state
        return jnp.logical_and(j >= 0, jnp.max(run) > SB_LOG_FLOOR)

    def step(state):
        j, run, acc = state
        run, acc = _sb_update(q, *tile(j), tri, None, (run, acc))
        return j - 1, run, acc

    o_ref[0, 0] = lax.while_loop(live, step, (qi - 1, run, acc))[2]


def _sb_attention(qsb, ksb, vsb, b):
    seq = qsb.shape[2]
    tq = 256
    kv = pl.BlockSpec((1, 1, seq, HEAD_DIM), lambda h, i: (b, h, 0, 0))
    return pl.pallas_call(
        functools.partial(_sb_body, tq=tq),
        grid=(HEADS, seq // tq),
        in_specs=[pl.BlockSpec((1, 1, tq, HEAD_DIM), lambda h, i: (b, h, i, 0)), kv, kv],
        out_specs=pl.BlockSpec((1, 1, tq, HEAD_DIM), lambda h, i: (0, h, i, 0)),
        out_shape=jax.ShapeDtypeStruct((1, HEADS, seq, HEAD_DIM), F32),
        compiler_params=_params("parallel", "arbitrary"),
        name="sb_attention",
    )(qsb, ksb, vsb)


SAMPLE_HEADS = 4
SAMPLE_TILE = 256


def _sample_body(lam_ref, qd2_ref, kdn_ref, vdn_ref, qs_ref, ksn_ref, vsn_ref,
                 ckd_ref, cvd_ref, cks_ref, cvs_ref, g_ref, do_ref, so_ref, *, past, nq):
    lam = lam_ref[0]
    tri_c = _tri(SAMPLE_TILE)
    tri_n = _tri(nq)
    i2 = lax.broadcasted_iota(jnp.int32, (2 * nq, nq), 0) % nq
    j2 = lax.broadcasted_iota(jnp.int32, (2 * nq, nq), 1)
    visible_new = ((past + j2) // CHUNK) <= ((past + i2) // CHUNK)
    i1 = lax.broadcasted_iota(jnp.int32, (nq, nq), 0)
    j1 = lax.broadcasted_iota(jnp.int32, (nq, nq), 1)
    earlier_new = j1 < i1
    for h in range(SAMPLE_HEADS):
        sl = slice(h * HEAD_DIM, (h + 1) * HEAD_DIM)
        q2 = qd2_ref[0, h].reshape(2 * nq, HEAD_DIM)
        carry = _diff_update(_nt(q2, ckd_ref[0, :, sl].astype(BF16)),
                             _with_ones(cvd_ref[0, :, sl].astype(BF16)), _diff_init(2 * nq))
        s_new = jnp.where(visible_new, _nt(q2, kdn_ref[0, h]), -jnp.inf)
        carry = _diff_update(s_new, vdn_ref[0, h], carry)
        do_ref[0, h] = _diff_finish(carry, lam, g_ref[...], nq)
        q = qs_ref[0, h]
        carry = (jnp.zeros((nq, 1), F32), jnp.zeros((nq, HEAD_DIM), F32))
        carry = _sb_update(q, ksn_ref[0, h], vsn_ref[0, h], tri_n, earlier_new, carry)
        for t in reversed(range(past // SAMPLE_TILE)):
            rows = slice(t * SAMPLE_TILE, (t + 1) * SAMPLE_TILE)
            carry = _sb_update(q, cks_ref[0, rows, sl].astype(BF16), cvs_ref[0, rows, sl].astype(BF16),
                               tri_c, None, carry)
        so_ref[0, h] = carry[1]


def _sample_attention(lam, qd2, kdb, vdb, qsb, ksb, vsb, ckd, cvd, cks, cvs, g_subln, nb, nq):
    past = ckd.shape[1]
    nhg = HEADS // SAMPLE_HEADS
    hm = pl.BlockSpec((1, SAMPLE_HEADS, nq, HEAD_DIM), lambda b, g: (0, g, b, 0))
    hm2 = pl.BlockSpec((1, SAMPLE_HEADS, 2, nq, HEAD_DIM), lambda b, g: (0, g, 0, b, 0))
    cache = pl.BlockSpec((1, past, SAMPLE_HEADS * HEAD_DIM), lambda b, g: (b, 0, g))
    out = jax.ShapeDtypeStruct((1, HEADS, nb * nq, HEAD_DIM), F32)
    return pl.pallas_call(
        functools.partial(_sample_body, past=past, nq=nq),
        grid=(nb, nhg),
        in_specs=[pl.BlockSpec(memory_space=pltpu.SMEM), hm2, hm,
                  pl.BlockSpec((1, SAMPLE_HEADS, nq, 2 * HEAD_DIM), lambda b, g: (0, g, b, 0)), hm, hm, hm,
                  cache, cache, cache, cache,
                  pl.BlockSpec((1, HEAD_DIM), lambda b, g: (0, 0))],
        out_specs=[hm, hm],
        out_shape=[out, out],
        compiler_params=_params("parallel", "parallel"),
        name="sample_attention",
    )(lam, qd2, kdb, vdb, qsb, ksb, vsb, ckd, cvd, cks, cvs, g_subln)


def _topk_rows(s, k):
    n = s.shape[0]
    rows = lax.broadcasted_iota(jnp.int32, s.shape, 0)
    vals, ids = [], []
    for _ in range(k):
        m = jnp.max(s, axis=0, keepdims=True)
        i = jnp.min(jnp.where(s == m, rows, n), axis=0, keepdims=True)
        vals.append(m)
        ids.append(i)
        s = jnp.where(rows == i, -jnp.inf, s)
    return jnp.concatenate(vals, axis=0), jnp.concatenate(ids, axis=0)


_STAIR = [(i, j) for i in range(PEER_TOPK) for j in range(PEER_TOPK) if (i + 1) * (j + 1) <= PEER_TOPK]


def _post_body(x_ref, do_ref, so_ref, wo_ref, gf_ref, wq_ref, sk_ref,
               h1_ref, c_ref, idx_ref, gate_ref, q_scr, idx_scr, gate_scr):
    ts = x_ref.shape[0]
    mixed = jnp.zeros((ts, D_MODEL), F32)
    for h in range(HEADS):
        mixed += _mm(do_ref[0, h].astype(BF16), wo_ref[h * HEAD_DIM:(h + 1) * HEAD_DIM, :])
        mixed += _mm(so_ref[0, h].astype(BF16), wo_ref[MIX + h * HEAD_DIM:MIX + (h + 1) * HEAD_DIM, :])
    h1 = x_ref[...] + mixed
    h1_ref[...] = h1
    c = _rms(h1, gf_ref[...], NORM_EPS)
    c_ref[...] = c
    q = _mm(c.astype(BF16), wq_ref[...])
    for hp in range(2 * PEER_HEADS):
        q_scr[hp] = q[:, hp * PEER_HALF:(hp + 1) * PEER_HALF].astype(BF16)

    npad = -len(_STAIR) % 8

    def head(h, _):
        v1, i1 = _topk_rows(_nt(sk_ref[0], q_scr[2 * h]), PEER_TOPK)
        v2, i2 = _topk_rows(_nt(sk_ref[1], q_scr[2 * h + 1]), PEER_TOPK)
        cand = jnp.concatenate([v1[i:i + 1] + v2[j:j + 1] for i, j in _STAIR]
                               + [jnp.full((npad, ts), -jnp.inf, F32)], axis=0)
        eid = jnp.concatenate([i1[i:i + 1] * PEER_KEYS + i2[j:j + 1] for i, j in _STAIR]
                              + [jnp.zeros((npad, ts), jnp.int32)], axis=0)
        top, pos = _topk_rows(cand, PEER_TOPK)
        rows = lax.broadcasted_iota(jnp.int32, cand.shape, 0)
        sel = jnp.concatenate([jnp.sum(jnp.where(rows == pos[r:r + 1], eid, 0), axis=0, keepdims=True)
                               for r in range(PEER_TOPK)], axis=0)
        e = jnp.exp(top - top[0:1])
        gate_scr[h] = e / jnp.sum(e, axis=0, keepdims=True)
        idx_scr[h] = sel
        return 0

    lax.fori_loop(0, PEER_HEADS, head, 0)
    idx_ref[...] = idx_scr[...].reshape(PEER_SEL, ts).T
    gate_ref[...] = gate_scr[...].reshape(PEER_SEL, ts).T


def _post(x, dout, sout, w_out_b, g_ffn, w_query_b, sub_keys_b, b, seq):
    t = seq
    ts = min(256, seq)
    nst = seq // ts
    row = pl.BlockSpec((ts, D_MODEL), lambda i: (i, 0))
    hm = pl.BlockSpec((1, HEADS, ts, HEAD_DIM), lambda i: (0, 0, i, 0))
    sel = pl.BlockSpec((ts, PEER_SEL), lambda i: (i, 0))
    full = lambda *shape: pl.BlockSpec(shape, lambda i: (0,) * len(shape))
    return pl.pallas_call(
        _post_body,
        grid=(t // ts,),
        in_specs=[pl.BlockSpec((ts, D_MODEL), lambda i: (b * nst + i, 0)),
                  hm, hm, full(2 * MIX, D_MODEL), full(1, D_MODEL),
                  full(D_MODEL, 2 * PEER_HEADS * PEER_HALF), full(2, PEER_KEYS, PEER_HALF)],
        out_specs=[row, row, sel, sel],
        out_shape=[jax.ShapeDtypeStruct((t, D_MODEL), F32), jax.ShapeDtypeStruct((t, D_MODEL), F32),
                   jax.ShapeDtypeStruct((t, PEER_SEL), jnp.int32), jax.ShapeDtypeStruct((t, PEER_SEL), F32)],
        scratch_shapes=[pltpu.VMEM((2 * PEER_HEADS, ts, PEER_HALF), BF16),
                        pltpu.VMEM((PEER_HEADS, PEER_TOPK, ts), jnp.int32),
                        pltpu.VMEM((PEER_HEADS, PEER_TOPK, ts), F32)],
        compiler_params=_params("parallel"),
        name="post_peer_select",
    )(x, dout, sout, w_out_b, g_ffn, w_query_b, sub_keys_b)


def _coef_body(gate_ref, dots_ref, o_ref):
    d = dots_ref[...]
    o_ref[...] = gate_ref[...] * (0.5 * d * (1.0 + lax.erf(d * (2.0 ** -0.5))))


def _coef(gate, dots):
    t = gate.shape[0]
    ts = min(2048, t)
    blk = pl.BlockSpec((ts, PEER_SEL), lambda i: (i, 0))
    return pl.pallas_call(
        _coef_body, grid=(t // ts,), in_specs=[blk, blk], out_specs=blk,
        out_shape=jax.ShapeDtypeStruct((t, PEER_SEL), F32),
        compiler_params=_params("parallel"), name="peer_coef",
    )(gate, dots)


SC_CORES = 2
SC_SUBCORES = 16
SC_LANES = 16
SC_WORKERS = SC_CORES * SC_SUBCORES
SC_ROWS = 32
SC_GROUP = 8
SC_CHUNKS = PEER_SEL // SC_ROWS
SC_STEPS = SC_GROUP * SC_CHUNKS
SC_VECS = D_MODEL // SC_LANES


def _sc_mesh():
    return plsc.VectorSubcoreMesh(core_axis_name="c", subcore_axis_name="s",
                                  num_cores=SC_CORES, num_subcores=SC_SUBCORES)


def _sc_pipeline(table_hbm, idx_v, rows_v, sem, compute):
    def gather(step, buf):
        return pltpu.make_async_copy(table_hbm.at[idx_v.at[step]], rows_v.at[buf], sem.at[buf])

    gather(0, 0).start()

    def pair(i, _):
        for buf in range(2):
            step = 2 * i + buf

            @pl.when(step + 1 < SC_STEPS)
            def _():
                gather(step + 1, 1 - buf).start()

            gather(step, buf).wait()
            compute(rows_v.at[buf], step // SC_CHUNKS, step % SC_CHUNKS)
        return 0

    lax.fori_loop(0, SC_STEPS // 2, pair, 0)


def _sc_dots_body(u_hbm, idx_hbm, c_hbm, out_hbm, idx_v, c_v, rows_v, dots_v, sem, *, tpw):
    wid = lax.axis_index("s") * SC_CORES + lax.axis_index("c")
    lane = lax.broadcasted_iota(jnp.int32, (SC_LANES,), 0)
    zero = jnp.zeros((SC_LANES,), F32)

    def compute(rows, tt, ch):
        for half in range(SC_ROWS // SC_LANES):
            outv = zero
            for q in range(2):
                r0 = half * SC_LANES + q * 8

                def vec(kk, accs):
                    off = pl.multiple_of(kk * SC_LANES, SC_LANES)
                    cv = c_v[tt, pl.ds(off, SC_LANES)]
                    return tuple(a + rows[r0 + r, pl.ds(off, SC_LANES)] * cv for r, a in enumerate(accs))

                accs = lax.fori_loop(0, SC_VECS, vec, (zero,) * 8)
                for r in range(8):
                    outv = jnp.where(lane == q * 8 + r, jnp.sum(accs[r]), outv)
            dots_v[tt, pl.ds(pl.multiple_of(ch * SC_ROWS + half * SC_LANES, SC_LANES), SC_LANES)] = outv

    def group(g, _):
        tok0 = pl.multiple_of(wid * tpw + g * SC_GROUP, SC_GROUP)
        pltpu.sync_copy(idx_hbm.at[pl.ds(tok0 * SC_CHUNKS, SC_STEPS)], idx_v)
        pltpu.sync_copy(c_hbm.at[pl.ds(tok0, SC_GROUP)], c_v)
        _sc_pipeline(u_hbm, idx_v, rows_v, sem, compute)
        pltpu.sync_copy(dots_v, out_hbm.at[pl.ds(tok0, SC_GROUP)])
        return 0

    lax.fori_loop(0, tpw // SC_GROUP, group, 0)


def _sc_combine_body(v_hbm, idx_hbm, coef_hbm, out_hbm, idx_v, coef_v, rows_v, acc_v, sem, *, tpw):
    wid = lax.axis_index("s") * SC_CORES + lax.axis_index("c")
    lane = lax.broadcasted_iota(jnp.int32, (SC_LANES,), 0)
    zero = jnp.zeros((SC_LANES,), F32)

    def compute(rows, tt, ch):
        for half in range(SC_ROWS // SC_LANES):
            cf = coef_v[tt, pl.ds(pl.multiple_of(ch * SC_ROWS + half * SC_LANES, SC_LANES), SC_LANES)]
            splat = [jnp.full((SC_LANES,), jnp.sum(jnp.where(lane == r, cf, 0.0)), F32) for r in range(SC_LANES)]

            @plsc.parallel_loop(0, SC_VECS, unroll=2)
            def _(kk):
                off = pl.multiple_of(kk * SC_LANES, SC_LANES)
                terms = [rows[half * SC_LANES + r, pl.ds(off, SC_LANES)] * splat[r] for r in range(SC_LANES)]
                while len(terms) > 1:
                    terms = [a + b for a, b in zip(terms[0::2], terms[1::2])]
                acc_v[tt, pl.ds(off, SC_LANES)] = acc_v[tt, pl.ds(off, SC_LANES)] + terms[0]

    def group(g, _):
        tok0 = pl.multiple_of(wid * tpw + g * SC_GROUP, SC_GROUP)
        pltpu.sync_copy(idx_hbm.at[pl.ds(tok0 * SC_CHUNKS, SC_STEPS)], idx_v)
        pltpu.sync_copy(coef_hbm.at[pl.ds(tok0, SC_GROUP)], coef_v)

        def clear(i, _):
            acc_v[i // SC_VECS, pl.ds(pl.multiple_of((i % SC_VECS) * SC_LANES, SC_LANES), SC_LANES)] = zero
            return 0

        lax.fori_loop(0, SC_GROUP * SC_VECS, clear, 0)
        _sc_pipeline(v_hbm, idx_v, rows_v, sem, compute)
        pltpu.sync_copy(acc_v, out_hbm.at[pl.ds(tok0, SC_GROUP)])
        return 0

    lax.fori_loop(0, tpw // SC_GROUP, group, 0)


def _sc_call(body, table, idx, per_token, out_width, name):
    t = per_token.shape[0]
    tpw = t // SC_WORKERS
    assert tpw % SC_GROUP == 0
    return pl.kernel(
        functools.partial(body, tpw=tpw),
        out_type=jax.ShapeDtypeStruct((t, out_width), F32),
        mesh=_sc_mesh(),
        scratch_types=[pltpu.VMEM((SC_STEPS, SC_ROWS), jnp.int32),
                       pltpu.VMEM((SC_GROUP, per_token.shape[1]), F32),
                       pltpu.VMEM((2, SC_ROWS, D_MODEL), F32),
                       pltpu.VMEM((SC_GROUP, out_width), F32),
                       pltpu.SemaphoreType.DMA((2,))],
        compiler_params=pltpu.CompilerParams(needs_layout_passes=False),
        name=name,
    )(table, idx.reshape(t * SC_CHUNKS, SC_ROWS), per_token)


def _sc_dots(expert_u, idx, c):
    return _sc_call(_sc_dots_body, expert_u, idx, c, PEER_SEL, "peer_dots")


def _sc_combine(expert_v, idx, coef):
    return _sc_call(_sc_combine_body, expert_v, idx, coef, D_MODEL, "peer_combine")


def _ple_body(h1_ref, peer_ref, p_ref, gp_ref, wg_ref, we_ref, gfin_ref, y_ref):
    h = h1_ref[...] + peer_ref[...]
    a = _rms(h, gp_ref[...], NORM_EPS).astype(BF16)
    gate = jax.nn.sigmoid(_mm(a, wg_ref[...]))
    h = h + _mm(p_ref[...].astype(BF16), we_ref[...]) * gate
    y_ref[...] = _rms(h, gfin_ref[...], NORM_EPS)


def _ple(h1, peer, p, g_ple, w_pgate_b, w_ple_b, g_final, b):
    t = h1.shape[0]
    ts = min(512, t)
    nst = t // ts
    row = pl.BlockSpec((ts, D_MODEL), lambda i: (i, 0))
    full = lambda *shape: pl.BlockSpec(shape, lambda i: (0,) * len(shape))
    return pl.pallas_call(
        _ple_body,
        grid=(t // ts,),
        in_specs=[row, row, pl.BlockSpec((ts, PLE_DIM), lambda i: (b * nst + i, 0)), full(1, D_MODEL),
                  full(D_MODEL, D_MODEL), full(PLE_DIM, D_MODEL), full(1, D_MODEL)],
        out_specs=row,
        out_shape=jax.ShapeDtypeStruct((t, D_MODEL), F32),
        compiler_params=_params("parallel"),
        name="ple_final",
    )(h1, peer, p, g_ple, w_pgate_b, w_ple_b, g_final)


def _rope_tables(pos):
    half = DIFF_COMP // 2
    inv = ROPE_THETA ** (-jnp.arange(0, DIFF_COMP, 2, dtype=F32) / DIFF_COMP)
    ang = pos.astype(F32)[:, None] * inv[None, :]
    cos = jnp.cos(ang)
    sin = jnp.sin(ang)
    reps = LANES // DIFF_COMP
    del half
    return (jnp.tile(jnp.concatenate([cos, cos], axis=-1), (1, reps)),
            jnp.tile(jnp.concatenate([-sin, sin], axis=-1), (1, reps)))


def kernel(x_prompt, x_sample, cache_diff_k, cache_diff_v, cache_sb_k, cache_sb_v, p_prompt, p_sample, g_mix, w_in, lambda_q1, lambda_k1, lambda_q2, lambda_k2, g_subln, w_out, g_ffn, w_query, sub_keys, expert_u, expert_v, g_ple, w_pgate, w_ple, g_final):
    assert w_in.shape[0] == 1, "single-layer encoder"
    nb, seq, _ = x_prompt.shape
    db, dq, _ = x_sample.shape
    past = cache_diff_k.shape[2]

    lam = (jnp.exp(jnp.sum(lambda_q1[0].astype(F32) * lambda_k1[0].astype(F32)))
           - jnp.exp(jnp.sum(lambda_q2[0].astype(F32) * lambda_k2[0].astype(F32))) + LAM_INIT).reshape(1)
    w_in_b = w_in[0].astype(BF16)
    w_out_b = w_out[0].astype(BF16)
    w_query_b = w_query[0].astype(BF16)
    sub_keys_b = sub_keys[0].astype(BF16)
    w_pgate_b = w_pgate[0].astype(BF16)
    w_ple_b = w_ple[0].astype(BF16)
    g_sub = g_subln[0].reshape(1, HEAD_DIM)
    g_fin = g_final.reshape(1, D_MODEL)

    def tail(x, p, dout, sout, b, s):
        h1, c, idx, gate = _post(x, dout, sout, w_out_b, g_ffn, w_query_b, sub_keys_b, b, s)
        coef = _coef(gate, _sc_dots(expert_u[0], idx, c))
        peer = _sc_combine(expert_v[0], idx, coef)
        return _ple(h1, peer, p, g_ple, w_pgate_b, w_ple_b, g_fin, b)

    xp = x_prompt.reshape(nb * seq, D_MODEL)
    pp = p_prompt[0].reshape(nb * seq, PLE_DIM)
    cos_p, sin_p = _rope_tables(jnp.arange(seq, dtype=jnp.int32))
    kd, vd, ks, vs, qd2, kdb, vdb, qsb, ksb, vsb = _proj(xp, g_mix, w_in_b, cos_p, sin_p, nb, seq)
    y_rows = []
    for b in range(nb):
        dout = _diff_attention(lam, qd2, kdb, vdb, g_sub, b)
        sout = _sb_attention(qsb, ksb, vsb, b)
        y_rows.append(tail(xp, pp, dout, sout, b, seq))
    y_prompt = jnp.stack(y_rows)
    rows_p = tuple(r.reshape(1, nb, seq, HEADS, HEAD_DIM) for r in (kd, vd, ks, vs))

    ts = db * dq
    xs = x_sample.reshape(ts, D_MODEL)
    cos_s, sin_s = _rope_tables(jnp.tile(past + jnp.arange(dq, dtype=jnp.int32), db))
    kd, vd, ks, vs, qd2, kdb, vdb, qsb, ksb, vsb = _proj(xs, g_mix, w_in_b, cos_s, sin_s, 1, ts)
    caches = [c[0].reshape(db, past, MIX) for c in (cache_diff_k, cache_diff_v, cache_sb_k, cache_sb_v)]
    dout, sout = _sample_attention(lam, qd2, kdb, vdb, qsb, ksb, vsb, *caches, g_sub, db, dq)
    y_sample = tail(xs, p_sample[0].reshape(ts, PLE_DIM), dout, sout, 0, ts).reshape(db, dq, D_MODEL)
    rows_s = tuple(r.reshape(1, db, dq, HEADS, HEAD_DIM) for r in (kd, vd, ks, vs))

    return (y_prompt, y_sample) + rows_p + rows_s
```

```python
import functools
import math

import jax
import jax.numpy as jnp
from jax import lax
from jax.experimental import pallas as pl
from jax.experimental.pallas import tpu as pltpu
from jax.experimental.pallas import tpu_sc as plsc

F32 = jnp.float32
BF16 = jnp.bfloat16

D_MODEL = 1024
HEADS = 8
HEAD_DIM = 64
DIFF_COMP = 32
MIX = HEADS * HEAD_DIM
CHUNK = 64
ROPE_THETA = 10000.0
NORM_EPS = 1e-6
SUBLN_EPS = 1e-5
PEER_HEADS = 8
PEER_KEYS = 128
PEER_TOPK = 16
PEER_HALF = 128
PEER_SEL = PEER_HEADS * PEER_TOPK
PLE_DIM = 256
LAM_INIT = 0.8 - 0.6 * math.exp(-0.3 * 0)
SB_LOG_FLOOR = -104.0

LANES = 128
VMEM_LIMIT = 48 * 1024 * 1024

NT_DIMS = (((1,), (1,)), ((), ()))


def _nt(a, b):
    return lax.dot_general(a, b, NT_DIMS, preferred_element_type=F32)


def _mm(a, b):
    return jnp.dot(a, b, preferred_element_type=F32)


def _rms(x, g, eps):
    return x * lax.rsqrt(jnp.mean(x * x, axis=-1, keepdims=True) + eps) * g


def _params(*sem):
    return pltpu.CompilerParams(dimension_semantics=sem, vmem_limit_bytes=VMEM_LIMIT)


def _with_ones(v):
    n = v.shape[0]
    ones = (lax.broadcasted_iota(jnp.int32, (n, HEAD_DIM), 1) == 0).astype(v.dtype)
    return jnp.concatenate([v, ones], axis=1)


def _proj_body(x_ref, g_ref, w_ref, cos_ref, sin_ref,
               kd_ref, vd_ref, ks_ref, vs_ref,
               qd2_ref, kdb_ref, vdb_ref, qsb_ref, ksb_ref, vsb_ref):
    ts = x_ref.shape[0]
    a = _rms(x_ref[...], g_ref[...], NORM_EPS).astype(BF16)
    cos = jnp.tile(cos_ref[...], (1, MIX // LANES))
    sin = jnp.tile(sin_ref[...], (1, MIX // LANES))
    lane = lax.broadcasted_iota(jnp.int32, (ts, MIX), 1)
    first_half = (lane % DIFF_COMP) < (DIFF_COMP // 2)

    def group(i):
        return _mm(a, w_ref[:, i * MIX:(i + 1) * MIX])

    def rope(t):
        partner = jnp.where(first_half,
                            pltpu.roll(t, MIX - DIFF_COMP // 2, 1),
                            pltpu.roll(t, DIFF_COMP // 2, 1))
        return t * cos + partner * sin

    qd = rope(group(0)) * (DIFF_COMP ** -0.5)
    kd = rope(group(1))
    vd = group(2)
    qs = group(3) * (HEAD_DIM ** -0.5)
    ks = group(4)
    vs = group(5)
    kd_ref[...] = kd
    vd_ref[...] = vd
    ks_ref[...] = ks
    vs_ref[...] = vs
    comp0 = lax.broadcasted_iota(jnp.int32, (ts, HEAD_DIM), 1) < DIFF_COMP
    for h in range(HEADS):
        sl = slice(h * HEAD_DIM, (h + 1) * HEAD_DIM)
        qh = qd[:, sl]
        qd2_ref[0, h, 0] = jnp.where(comp0, qh, 0.0).astype(BF16)
        qd2_ref[0, h, 1] = jnp.where(comp0, 0.0, qh).astype(BF16)
        kdb_ref[0, h] = kd[:, sl].astype(BF16)
        vdb_ref[0, h] = _with_ones(vd[:, sl].astype(BF16))
        qsb_ref[0, h] = qs[:, sl].astype(BF16)
        ksb_ref[0, h] = ks[:, sl].astype(BF16)
        vsb_ref[0, h] = vs[:, sl].astype(BF16)


def _proj(x, g_mix, w_in_b, cos_t, sin_t, nb, seq):
    t = nb * seq
    ts = min(256, seq)
    nst = seq // ts
    row = pl.BlockSpec((ts, MIX), lambda i: (i, 0))
    hm = pl.BlockSpec((1, HEADS, ts, HEAD_DIM), lambda i: (i // nst, 0, i % nst, 0))
    hm2 = pl.BlockSpec((1, HEADS, 2, ts, HEAD_DIM), lambda i: (i // nst, 0, 0, i % nst, 0))
    rows = jax.ShapeDtypeStruct((t, MIX), F32)
    heads = jax.ShapeDtypeStruct((nb, HEADS, seq, HEAD_DIM), BF16)
    heads2 = jax.ShapeDtypeStruct((nb, HEADS, 2, seq, HEAD_DIM), BF16)
    hm_ext = pl.BlockSpec((1, HEADS, ts, 2 * HEAD_DIM), lambda i: (i // nst, 0, i % nst, 0))
    heads_ext = jax.ShapeDtypeStruct((nb, HEADS, seq, 2 * HEAD_DIM), BF16)
    return pl.pallas_call(
        _proj_body,
        grid=(t // ts,),
        in_specs=[
            pl.BlockSpec((ts, D_MODEL), lambda i: (i, 0)),
            pl.BlockSpec((1, D_MODEL), lambda i: (0, 0)),
            pl.BlockSpec((D_MODEL, 6 * MIX), lambda i: (0, 0)),
            pl.BlockSpec((ts, LANES), lambda i: (i % nst, 0)),
            pl.BlockSpec((ts, LANES), lambda i: (i % nst, 0)),
        ],
        out_specs=[row, row, row, row, hm2, hm, hm_ext, hm, hm, hm],
        out_shape=[rows, rows, rows, rows, heads2, heads, heads_ext, heads, heads, heads],
        compiler_params=_params("parallel"),
        name="proj",
    )(x, g_mix, w_in_b, cos_t, sin_t)


def _diff_init(rows):
    return jnp.full((rows, 1), -jnp.inf, F32), jnp.zeros((rows, 2 * HEAD_DIM), F32)


def _diff_update(s, v_ext, carry):
    m, acc = carry
    m_new = jnp.maximum(m, jnp.max(s, axis=-1, keepdims=True))
    p = jnp.exp(s - m_new)
    acc = jnp.exp(m - m_new) * acc + _mm(p.astype(BF16), v_ext)
    return m_new, acc


def _diff_finish(carry, lam, g_subln, tq):
    _, acc = carry
    o = acc[:, :HEAD_DIM] / acc[:, HEAD_DIM:HEAD_DIM + 1]
    d = o[:tq] - lam * o[tq:]
    return _rms(d, g_subln, SUBLN_EPS) * (1.0 - LAM_INIT)


def _suffix_sums(lk, tri):
    hi = lk.astype(BF16)
    lo = (lk - hi.astype(F32)).astype(BF16)
    return _mm(hi, tri) + _mm(lo, tri)


def _sb_update(q, k, v, tri, earlier, carry):
    run, acc = carry
    z = _nt(q, k)
    sp = jnp.maximum(z, 0.0) + jnp.log1p(jnp.exp(-jnp.abs(z)))
    lk = -sp if earlier is None else jnp.where(earlier, -sp, 0.0)
    after = _suffix_sums(lk, tri)
    w = jnp.exp((z - sp) + after + run)
    if earlier is not None:
        w = jnp.where(earlier, w, 0.0)
    acc = acc + _mm(w.astype(BF16), v)
    run = run + after[:, 0:1] + lk[:, 0:1]
    return run, acc


def _tri(n):
    j = lax.broadcasted_iota(jnp.int32, (n, n), 0)
    s = lax.broadcasted_iota(jnp.int32, (n, n), 1)
    return (j > s).astype(BF16)


def _diff_body(lam_ref, q_ref, k_ref, v_ref, g_ref, o_ref, *, tq):
    qi = pl.program_id(1)
    q2 = q_ref[0, 0].reshape(2 * tq, HEAD_DIM)

    def scores(j):
        return _nt(q2, k_ref[0, 0, pl.ds(pl.multiple_of(j * tq, tq), tq), :])

    def step(j, state):
        s, carry = state
        s_next = scores(j + 1)
        return s_next, _diff_update(s, v_ref[0, 0, pl.ds(pl.multiple_of(j * tq, tq), tq), :], carry)

    s, carry = lax.fori_loop(0, qi, step, (scores(0), _diff_init(2 * tq)))
    r = lax.broadcasted_iota(jnp.int32, (2 * tq, tq), 0) % tq
    c = lax.broadcasted_iota(jnp.int32, (2 * tq, tq), 1)
    s = jnp.where((c // CHUNK) <= (r // CHUNK), s, -jnp.inf)
    carry = _diff_update(s, v_ref[0, 0, pl.ds(pl.multiple_of(qi * tq, tq), tq), :], carry)
    o_ref[0, 0] = _diff_finish(carry, lam_ref[0], g_ref[...], tq)


def _diff_attention(lam, qd2, kdb, vdb, g_subln, b):
    seq = qd2.shape[3]
    tq = min(512, seq)
    return pl.pallas_call(
        functools.partial(_diff_body, tq=tq),
        grid=(HEADS, seq // tq),
        in_specs=[
            pl.BlockSpec(memory_space=pltpu.SMEM),
            pl.BlockSpec((1, 1, 2, tq, HEAD_DIM), lambda h, i: (b, h, 0, i, 0)),
            pl.BlockSpec((1, 1, seq, HEAD_DIM), lambda h, i: (b, h, 0, 0)),
            pl.BlockSpec((1, 1, seq, 2 * HEAD_DIM), lambda h, i: (b, h, 0, 0)),
            pl.BlockSpec((1, HEAD_DIM), lambda h, i: (0, 0)),
        ],
        out_specs=pl.BlockSpec((1, 1, tq, HEAD_DIM), lambda h, i: (0, h, i, 0)),
        out_shape=jax.ShapeDtypeStruct((1, HEADS, seq, HEAD_DIM), F32),
        compiler_params=_params("parallel", "arbitrary"),
        name="diff_attention",
    )(lam, qd2, kdb, vdb, g_subln)


def _sb_body(q_ref, k_ref, v_ref, o_ref, *, tq):
    qi = pl.program_id(1)
    q = q_ref[0, 0]
    tri = _tri(tq)

    def tile(j):
        start = pl.multiple_of(j * tq, tq)
        return k_ref[0, 0, pl.ds(start, tq), :], v_ref[0, 0, pl.ds(start, tq), :]

    r = lax.broadcasted_iota(jnp.int32, (tq, tq), 0)
    c = lax.broadcasted_iota(jnp.int32, (tq, tq), 1)
    carry = (jnp.zeros((tq, 1), F32), jnp.zeros((tq, HEAD_DIM), F32))
    run, acc = _sb_update(q, *tile(qi), tri, c < r, carry)

    def live(state):
        j, run, _ = state
        return jnp.logical_and(j >= 0, jnp.max(run) > SB_LOG_FLOOR)

    def step(state):
        j, run, acc = state
        run, acc = _sb_update(q, *tile(j), tri, None, (run, acc))
        return j - 1, run, acc

    o_ref[0, 0] = lax.while_loop(live, step, (qi - 1, run, acc))[2]


def _sb_attention(qsb, ksb, vsb, b):
    seq = qsb.shape[2]
    tq = 256
    kv = pl.BlockSpec((1, 1, seq, HEAD_DIM), lambda h, i: (b, h, 0, 0))
    return pl.pallas_call(
        functools.partial(_sb_body, tq=tq),
        grid=(HEADS, seq // tq),
        in_specs=[pl.BlockSpec((1, 1, tq, HEAD_DIM), lambda h, i: (b, h, i, 0)), kv, kv],
        out_specs=pl.BlockSpec((1, 1, tq, HEAD_DIM), lambda h, i: (0, h, i, 0)),
        out_shape=jax.ShapeDtypeStruct((1, HEADS, seq, HEAD_DIM), F32),
        compiler_params=_params("parallel", "arbitrary"),
        name="sb_attention",
    )(qsb, ksb, vsb)


SAMPLE_HEADS = 4
SAMPLE_TILE = 256


def _sample_body(lam_ref, qd2_ref, kdn_ref, vdn_ref, qs_ref, ksn_ref, vsn_ref,
                 ckd_ref, cvd_ref, cks_ref, cvs_ref, g_ref, do_ref, so_ref, *, past, nq):
    lam = lam_ref[0]
    tri_c = _tri(SAMPLE_TILE)
    tri_n = _tri(nq)
    i2 = lax.broadcasted_iota(jnp.int32, (2 * nq, nq), 0) % nq
    j2 = lax.broadcasted_iota(jnp.int32, (2 * nq, nq), 1)
    visible_new = ((past + j2) // CHUNK) <= ((past + i2) // CHUNK)
    i1 = lax.broadcasted_iota(jnp.int32, (nq, nq), 0)
    j1 = lax.broadcasted_iota(jnp.int32, (nq, nq), 1)
    earlier_new = j1 < i1
    for h in range(SAMPLE_HEADS):
        sl = slice(h * HEAD_DIM, (h + 1) * HEAD_DIM)
        q2 = qd2_ref[0, h].reshape(2 * nq, HEAD_DIM)
        carry = _diff_update(_nt(q2, ckd_ref[0, :, sl].astype(BF16)),
                             _with_ones(cvd_ref[0, :, sl].astype(BF16)), _diff_init(2 * nq))
        s_new = jnp.where(visible_new, _nt(q2, kdn_ref[0, h]), -jnp.inf)
        carry = _diff_update(s_new, vdn_ref[0, h], carry)
        do_ref[0, h] = _diff_finish(carry, lam, g_ref[...], nq)
        q = qs_ref[0, h]
        carry = (jnp.zeros((nq, 1), F32), jnp.zeros((nq, HEAD_DIM), F32))
        carry = _sb_update(q, ksn_ref[0, h], vsn_ref[0, h], tri_n, earlier_new, carry)
        for t in reversed(range(past // SAMPLE_TILE)):
            rows = slice(t * SAMPLE_TILE, (t + 1) * SAMPLE_TILE)
            carry = _sb_update(q, cks_ref[0, rows, sl].astype(BF16), cvs_ref[0, rows, sl].astype(BF16),
                               tri_c, None, carry)
        so_ref[0, h] = carry[1]


def _sample_attention(lam, qd2, kdb, vdb, qsb, ksb, vsb, ckd, cvd, cks, cvs, g_subln, nb, nq):
    past = ckd.shape[1]
    nhg = HEADS // SAMPLE_HEADS
    hm = pl.BlockSpec((1, SAMPLE_HEADS, nq, HEAD_DIM), lambda b, g: (0, g, b, 0))
    hm2 = pl.BlockSpec((1, SAMPLE_HEADS, 2, nq, HEAD_DIM), lambda b, g: (0, g, 0, b, 0))
    cache = pl.BlockSpec((1, past, SAMPLE_HEADS * HEAD_DIM), lambda b, g: (b, 0, g))
    out = jax.ShapeDtypeStruct((1, HEADS, nb * nq, HEAD_DIM), F32)
    return pl.pallas_call(
        functools.partial(_sample_body, past=past, nq=nq),
        grid=(nb, nhg),
        in_specs=[pl.BlockSpec(memory_space=pltpu.SMEM), hm2, hm,
                  pl.BlockSpec((1, SAMPLE_HEADS, nq, 2 * HEAD_DIM), lambda b, g: (0, g, b, 0)), hm, hm, hm,
                  cache, cache, cache, cache,
                  pl.BlockSpec((1, HEAD_DIM), lambda b, g: (0, 0))],
        out_specs=[hm, hm],
        out_shape=[out, out],
        compiler_params=_params("parallel", "parallel"),
        name="sample_attention",
    )(lam, qd2, kdb, vdb, qsb, ksb, vsb, ckd, cvd, cks, cvs, g_subln)


def _topk_rows(s, k):
    n = s.shape[0]
    rows = lax.broadcasted_iota(jnp.int32, s.shape, 0)
    vals, ids = [], []
    for _ in range(k):
        m = jnp.max(s, axis=0, keepdims=True)
        i = jnp.min(jnp.where(s == m, rows, n), axis=0, keepdims=True)
        vals.append(m)
        ids.append(i)
        s = jnp.where(rows == i, -jnp.inf, s)
    return jnp.concatenate(vals, axis=0), jnp.concatenate(ids, axis=0)


_STAIR = [(i, j) for i in range(PEER_TOPK) for j in range(PEER_TOPK) if (i + 1) * (j + 1) <= PEER_TOPK]


def _post_body(x_ref, do_ref, so_ref, wo_ref, gf_ref, wq_ref, sk_ref,
               h1_ref, c_ref, idx_ref, gate_ref, q_scr, idx_scr, gate_scr):
    ts = x_ref.shape[0]
    mixed = jnp.zeros((ts, D_MODEL), F32)
    for h in range(HEADS):
        mixed += _mm(do_ref[0, h].astype(BF16), wo_ref[h * HEAD_DIM:(h + 1) * HEAD_DIM, :])
        mixed += _mm(so_ref[0, h].astype(BF16), wo_ref[MIX + h * HEAD_DIM:MIX + (h + 1) * HEAD_DIM, :])
    h1 = x_ref[...] + mixed
    h1_ref[...] = h1
    c = _rms(h1, gf_ref[...], NORM_EPS)
    c_ref[...] = c
    q = _mm(c.astype(BF16), wq_ref[...])
    for hp in range(2 * PEER_HEADS):
        q_scr[hp] = q[:, hp * PEER_HALF:(hp + 1) * PEER_HALF].astype(BF16)

    npad = -len(_STAIR) % 8

    def head(h, _):
        v1, i1 = _topk_rows(_nt(sk_ref[0], q_scr[2 * h]), PEER_TOPK)
        v2, i2 = _topk_rows(_nt(sk_ref[1], q_scr[2 * h + 1]), PEER_TOPK)
        cand = jnp.concatenate([v1[i:i + 1] + v2[j:j + 1] for i, j in _STAIR]
                               + [jnp.full((npad, ts), -jnp.inf, F32)], axis=0)
        eid = jnp.concatenate([i1[i:i + 1] * PEER_KEYS + i2[j:j + 1] for i, j in _STAIR]
                              + [jnp.zeros((npad, ts), jnp.int32)], axis=0)
        top, pos = _topk_rows(cand, PEER_TOPK)
        rows = lax.broadcasted_iota(jnp.int32, cand.shape, 0)
        sel = jnp.concatenate([jnp.sum(jnp.where(rows == pos[r:r + 1], eid, 0), axis=0, keepdims=True)
                               for r in range(PEER_TOPK)], axis=0)
        e = jnp.exp(top - top[0:1])
        gate_scr[h] = e / jnp.sum(e, axis=0, keepdims=True)
        idx_scr[h] = sel
        return 0

    lax.fori_loop(0, PEER_HEADS, head, 0)
    idx_ref[...] = idx_scr[...].reshape(PEER_SEL, ts).T
    gate_ref[...] = gate_scr[...].reshape(PEER_SEL, ts).T


def _post(x, dout, sout, w_out_b, g_ffn, w_query_b, sub_keys_b, b, seq):
    t = seq
    ts = min(256, seq)
    nst = seq // ts
    row = pl.BlockSpec((ts, D_MODEL), lambda i: (i, 0))
    hm = pl.BlockSpec((1, HEADS, ts, HEAD_DIM), lambda i: (0, 0, i, 0))
    sel = pl.BlockSpec((ts, PEER_SEL), lambda i: (i, 0))
    full = lambda *shape: pl.BlockSpec(shape, lambda i: (0,) * len(shape))
    return pl.pallas_call(
        _post_body,
        grid=(t // ts,),
        in_specs=[pl.BlockSpec((ts, D_MODEL), lambda i: (b * nst + i, 0)),
                  hm, hm, full(2 * MIX, D_MODEL), full(1, D_MODEL),
                  full(D_MODEL, 2 * PEER_HEADS * PEER_HALF), full(2, PEER_KEYS, PEER_HALF)],
        out_specs=[row, row, sel, sel],
        out_shape=[jax.ShapeDtypeStruct((t, D_MODEL), F32), jax.ShapeDtypeStruct((t, D_MODEL), F32),
                   jax.ShapeDtypeStruct((t, PEER_SEL), jnp.int32), jax.ShapeDtypeStruct((t, PEER_SEL), F32)],
        scratch_shapes=[pltpu.VMEM((2 * PEER_HEADS, ts, PEER_HALF), BF16),
                        pltpu.VMEM((PEER_HEADS, PEER_TOPK, ts), jnp.int32),
                        pltpu.VMEM((PEER_HEADS, PEER_TOPK, ts), F32)],
        compiler_params=_params("parallel"),
        name="post_peer_select",
    )(x, dout, sout, w_out_b, g_ffn, w_query_b, sub_keys_b)


def _coef_body(gate_ref, dots_ref, o_ref):
    d = dots_ref[...]
    o_ref[...] = gate_ref[...] * (0.5 * d * (1.0 + lax.erf(d * (2.0 ** -0.5))))


def _coef(gate, dots):
    t = gate.shape[0]
    ts = min(2048, t)
    blk = pl.BlockSpec((ts, PEER_SEL), lambda i: (i, 0))
    return pl.pallas_call(
        _coef_body, grid=(t // ts,), in_specs=[blk, blk], out_specs=blk,
        out_shape=jax.ShapeDtypeStruct((t, PEER_SEL), F32),
        compiler_params=_params("parallel"), name="peer_coef",
    )(gate, dots)


SC_CORES = 2
SC_SUBCORES = 16
SC_LANES = 16
SC_WORKERS = SC_CORES * SC_SUBCORES
SC_ROWS = 32
SC_BUFS = 3
SC_GROUP = 8
SC_CHUNKS = PEER_SEL // SC_ROWS
SC_STEPS = SC_GROUP * SC_CHUNKS
SC_VECS = D_MODEL // SC_LANES


def _sc_mesh():
    return plsc.VectorSubcoreMesh(core_axis_name="c", subcore_axis_name="s",
                                  num_cores=SC_CORES, num_subcores=SC_SUBCORES)


def _sc_walk(table_hbm, idx_hbm, aux_hbm, idx_v, aux_v, rows_v, sem, stage_sem, tpw, begin_group, compute, end_group):
    tok_base = (lax.axis_index("s") * SC_CORES + lax.axis_index("c")) * tpw
    ngroups = tpw // SC_GROUP
    nsteps = tpw * SC_CHUNKS

    def first_token(g):
        return pl.multiple_of(tok_base + g * SC_GROUP, SC_GROUP)

    def stage(g):
        tok0 = first_token(g)
        return (pltpu.make_async_copy(idx_hbm.at[pl.ds(tok0 * SC_CHUNKS, SC_STEPS)], idx_v.at[g % 2], stage_sem.at[0]),
                pltpu.make_async_copy(aux_hbm.at[pl.ds(tok0, SC_GROUP)], aux_v.at[g % 2], stage_sem.at[1]))

    def gather(step):
        idx = idx_v.at[(step // SC_STEPS) % 2, step % SC_STEPS]
        return pltpu.make_async_copy(table_hbm.at[idx], rows_v.at[step % SC_BUFS], sem.at[step % SC_BUFS])

    for cp in stage(0):
        cp.start()
    for cp in stage(0):
        cp.wait()
    for step in range(SC_BUFS - 1):
        gather(step).start()

    def walk(step, _):
        g = step // SC_STEPS
        local = step % SC_STEPS

        @pl.when(jnp.logical_and(local == 0, g + 1 < ngroups))
        def _():
            for cp in stage(g + 1):
                cp.start()

        ahead = step + (SC_BUFS - 1)

        @pl.when(ahead < nsteps)
        def _():
            @pl.when(ahead % SC_STEPS == 0)
            def _():
                for cp in stage(ahead // SC_STEPS):
                    cp.wait()

            gather(ahead).start()

        @pl.when(local == 0)
        def _():
            begin_group()

        gather(step).wait()
        compute(rows_v.at[step % SC_BUFS], g % 2, local // SC_CHUNKS, local % SC_CHUNKS)

        @pl.when(local == SC_STEPS - 1)
        def _():
            end_group(first_token(g))

        return 0

    lax.fori_loop(0, nsteps, walk, 0)


def _sc_dots_body(u_hbm, idx_hbm, c_hbm, out_hbm, idx_v, c_v, rows_v, dots_v, sem, stage_sem, *, tpw):
    lane = lax.broadcasted_iota(jnp.int32, (SC_LANES,), 0)
    zero = jnp.zeros((SC_LANES,), F32)

    def compute(rows, slot, tt, ch):
        for half in range(SC_ROWS // SC_LANES):
            outv = zero
            for q in range(2):
                r0 = half * SC_LANES + q * 8

                def vec(kk, accs):
                    off = pl.multiple_of(kk * SC_LANES, SC_LANES)
                    cv = c_v[slot, tt, pl.ds(off, SC_LANES)]
                    return tuple(a + rows[r0 + r, pl.ds(off, SC_LANES)] * cv for r, a in enumerate(accs))

                accs = lax.fori_loop(0, SC_VECS, vec, (zero,) * 8)
                for r in range(8):
                    outv = jnp.where(lane == q * 8 + r, jnp.sum(accs[r]), outv)
            dots_v[tt, pl.ds(pl.multiple_of(ch * SC_ROWS + half * SC_LANES, SC_LANES), SC_LANES)] = outv

    def end_group(tok0):
        pltpu.sync_copy(dots_v, out_hbm.at[pl.ds(tok0, SC_GROUP)])

    _sc_walk(u_hbm, idx_hbm, c_hbm, idx_v, c_v, rows_v, sem, stage_sem, tpw, lambda: None, compute, end_group)


def _sc_combine_body(v_hbm, idx_hbm, coef_hbm, out_hbm, idx_v, coef_v, rows_v, acc_v, sem, stage_sem, *, tpw):
    lane = lax.broadcasted_iota(jnp.int32, (SC_LANES,), 0)
    zero = jnp.zeros((SC_LANES,), F32)

    def compute(rows, slot, tt, ch):
        for half in range(SC_ROWS // SC_LANES):
            cf = coef_v[slot, tt, pl.ds(pl.multiple_of(ch * SC_ROWS + half * SC_LANES, SC_LANES), SC_LANES)]
            splat = [jnp.full((SC_LANES,), jnp.sum(jnp.where(lane == r, cf, 0.0)), F32) for r in range(SC_LANES)]

            @plsc.parallel_loop(0, SC_VECS, unroll=2)
            def _(kk):
                off = pl.multiple_of(kk * SC_LANES, SC_LANES)
                terms = [rows[half * SC_LANES + r, pl.ds(off, SC_LANES)] * splat[r] for r in range(SC_LANES)]
                while len(terms) > 1:
                    terms = [a + b for a, b in zip(terms[0::2], terms[1::2])]
                acc_v[tt, pl.ds(off, SC_LANES)] = acc_v[tt, pl.ds(off, SC_LANES)] + terms[0]

    def begin_group():
        def clear(i, _):
            acc_v[i // SC_VECS, pl.ds(pl.multiple_of((i % SC_VECS) * SC_LANES, SC_LANES), SC_LANES)] = zero
            return 0

        lax.fori_loop(0, SC_GROUP * SC_VECS, clear, 0)

    def end_group(tok0):
        pltpu.sync_copy(acc_v, out_hbm.at[pl.ds(tok0, SC_GROUP)])

    _sc_walk(v_hbm, idx_hbm, coef_hbm, idx_v, coef_v, rows_v, sem, stage_sem, tpw, begin_group, compute, end_group)


def _sc_call(body, table, idx, per_token, out_width, name):
    t = per_token.shape[0]
    tpw = t // SC_WORKERS
    assert tpw % SC_GROUP == 0
    return pl.kernel(
        functools.partial(body, tpw=tpw),
        out_type=jax.ShapeDtypeStruct((t, out_width), F32),
        mesh=_sc_mesh(),
        scratch_types=[pltpu.VMEM((2, SC_STEPS, SC_ROWS), jnp.int32),
                       pltpu.VMEM((2, SC_GROUP, per_token.shape[1]), F32),
                       pltpu.VMEM((SC_BUFS, SC_ROWS, D_MODEL), F32),
                       pltpu.VMEM((SC_GROUP, out_width), F32),
                       pltpu.SemaphoreType.DMA((SC_BUFS,)),
                       pltpu.SemaphoreType.DMA((2,))],
        compiler_params=pltpu.CompilerParams(needs_layout_passes=False),
        name=name,
    )(table, idx.reshape(t * SC_CHUNKS, SC_ROWS), per_token)


def _sc_dots(expert_u, idx, c):
    return _sc_call(_sc_dots_body, expert_u, idx, c, PEER_SEL, "peer_dots")


def _sc_combine(expert_v, idx, coef):
    return _sc_call(_sc_combine_body, expert_v, idx, coef, D_MODEL, "peer_combine")


def _ple_body(h1_ref, peer_ref, p_ref, gp_ref, wg_ref, we_ref, gfin_ref, y_ref):
    h = h1_ref[...] + peer_ref[...]
    a = _rms(h, gp_ref[...], NORM_EPS).astype(BF16)
    gate = jax.nn.sigmoid(_mm(a, wg_ref[...]))
    h = h + _mm(p_ref[...].astype(BF16), we_ref[...]) * gate
    y_ref[...] = _rms(h, gfin_ref[...], NORM_EPS)


def _ple(h1, peer, p, g_ple, w_pgate_b, w_ple_b, g_final, b):
    t = h1.shape[0]
    ts = min(512, t)
    nst = t // ts
    row = pl.BlockSpec((ts, D_MODEL), lambda i: (i, 0))
    full = lambda *shape: pl.BlockSpec(shape, lambda i: (0,) * len(shape))
    return pl.pallas_call(
        _ple_body,
        grid=(t // ts,),
        in_specs=[row, row, pl.BlockSpec((ts, PLE_DIM), lambda i: (b * nst + i, 0)), full(1, D_MODEL),
                  full(D_MODEL, D_MODEL), full(PLE_DIM, D_MODEL), full(1, D_MODEL)],
        out_specs=row,
        out_shape=jax.ShapeDtypeStruct((t, D_MODEL), F32),
        compiler_params=_params("parallel"),
        name="ple_final",
    )(h1, peer, p, g_ple, w_pgate_b, w_ple_b, g_final)


def _rope_tables(pos):
    half = DIFF_COMP // 2
    inv = ROPE_THETA ** (-jnp.arange(0, DIFF_COMP, 2, dtype=F32) / DIFF_COMP)
    ang = pos.astype(F32)[:, None] * inv[None, :]
    cos = jnp.cos(ang)
    sin = jnp.sin(ang)
    reps = LANES // DIFF_COMP
    del half
    return (jnp.tile(jnp.concatenate([cos, cos], axis=-1), (1, reps)),
            jnp.tile(jnp.concatenate([-sin, sin], axis=-1), (1, reps)))


def kernel(x_prompt, x_sample, cache_diff_k, cache_diff_v, cache_sb_k, cache_sb_v, p_prompt, p_sample, g_mix, w_in, lambda_q1, lambda_k1, lambda_q2, lambda_k2, g_subln, w_out, g_ffn, w_query, sub_keys, expert_u, expert_v, g_ple, w_pgate, w_ple, g_final):
    assert w_in.shape[0] == 1, "single-layer encoder"
    nb, seq, _ = x_prompt.shape
    db, dq, _ = x_sample.shape
    past = cache_diff_k.shape[2]

    lam = (jnp.exp(jnp.sum(lambda_q1[0].astype(F32) * lambda_k1[0].astype(F32)))
           - jnp.exp(jnp.sum(lambda_q2[0].astype(F32) * lambda_k2[0].astype(F32))) + LAM_INIT).reshape(1)
    w_in_b = w_in[0].astype(BF16)
    w_out_b = w_out[0].astype(BF16)
    w_query_b = w_query[0].astype(BF16)
    sub_keys_b = sub_keys[0].astype(BF16)
    w_pgate_b = w_pgate[0].astype(BF16)
    w_ple_b = w_ple[0].astype(BF16)
    g_sub = g_subln[0].reshape(1, HEAD_DIM)
    g_fin = g_final.reshape(1, D_MODEL)

    def tail(x, p, dout, sout, b, s):
        h1, c, idx, gate = _post(x, dout, sout, w_out_b, g_ffn, w_query_b, sub_keys_b, b, s)
        coef = _coef(gate, _sc_dots(expert_u[0], idx, c))
        peer = _sc_combine(expert_v[0], idx, coef)
        return _ple(h1, peer, p, g_ple, w_pgate_b, w_ple_b, g_fin, b)

    xp = x_prompt.reshape(nb * seq, D_MODEL)
    pp = p_prompt[0].reshape(nb * seq, PLE_DIM)
    cos_p, sin_p = _rope_tables(jnp.arange(seq, dtype=jnp.int32))
    kd, vd, ks, vs, qd2, kdb, vdb, qsb, ksb, vsb = _proj(xp, g_mix, w_in_b, cos_p, sin_p, nb, seq)
    y_rows = []
    for b in range(nb):
        dout = _diff_attention(lam, qd2, kdb, vdb, g_sub, b)
        sout = _sb_attention(qsb, ksb, vsb, b)
        y_rows.append(tail(xp, pp, dout, sout, b, seq))
    y_prompt = jnp.stack(y_rows)
    rows_p = tuple(r.reshape(1, nb, seq, HEADS, HEAD_DIM) for r in (kd, vd, ks, vs))

    ts = db * dq
    xs = x_sample.reshape(ts, D_MODEL)
    cos_s, sin_s = _rope_tables(jnp.tile(past + jnp.arange(dq, dtype=jnp.int32), db))
    kd, vd, ks, vs, qd2, kdb, vdb, qsb, ksb, vsb = _proj(xs, g_mix, w_in_b, cos_s, sin_s, 1, ts)
    caches = [c[0].reshape(db, past, MIX) for c in (cache_diff_k, cache_diff_v, cache_sb_k, cache_sb_v)]
    dout, sout = _sample_attention(lam, qd2, kdb, vdb, qsb, ksb, vsb, *caches, g_sub, db, dq)
    y_sample = tail(xs, p_sample[0].reshape(ts, PLE_DIM), dout, sout, 0, ts).reshape(db, dq, D_MODEL)
    rows_s = tuple(r.reshape(1, db, dq, HEADS, HEAD_DIM) for r in (kd, vd, ks, vs))

    return (y_prompt, y_sample) + rows_p + rows_s
```

```python
import functools
import math

import jax
import jax.numpy as jnp
from jax import lax
from jax.experimental import pallas as pl
from jax.experimental.pallas import tpu as pltpu
from jax.experimental.pallas import tpu_sc as plsc

F32 = jnp.float32
BF16 = jnp.bfloat16

D_MODEL = 1024
HEADS = 8
HEAD_DIM = 64
DIFF_COMP = 32
MIX = HEADS * HEAD_DIM
CHUNK = 64
ROPE_THETA = 10000.0
NORM_EPS = 1e-6
SUBLN_EPS = 1e-5
PEER_HEADS = 8
PEER_KEYS = 128
PEER_TOPK = 16
PEER_HALF = 128
PEER_SEL = PEER_HEADS * PEER_TOPK
PLE_DIM = 256
LAM_INIT = 0.8 - 0.6 * math.exp(-0.3 * 0)
SB_LOG_FLOOR = -104.0

LANES = 128
VMEM_LIMIT = 48 * 1024 * 1024

NT_DIMS = (((1,), (1,)), ((), ()))


def _nt(a, b):
    return lax.dot_general(a, b, NT_DIMS, preferred_element_type=F32)


def _mm(a, b):
    return jnp.dot(a, b, preferred_element_type=F32)


def _rms(x, g, eps):
    return x * lax.rsqrt(jnp.mean(x * x, axis=-1, keepdims=True) + eps) * g


def _params(*sem):
    return pltpu.CompilerParams(dimension_semantics=sem, vmem_limit_bytes=VMEM_LIMIT)


def _with_ones(v):
    n = v.shape[0]
    ones = (lax.broadcasted_iota(jnp.int32, (n, HEAD_DIM), 1) == 0).astype(v.dtype)
    return jnp.concatenate([v, ones], axis=1)


def _proj_body(x_ref, g_ref, w_ref, cos_ref, sin_ref,
               kd_ref, vd_ref, ks_ref, vs_ref,
               qd2_ref, kdb_ref, vdb_ref, qsb_ref, ksb_ref, vsb_ref):
    ts = x_ref.shape[0]
    a = _rms(x_ref[...], g_ref[...], NORM_EPS).astype(BF16)
    cos = jnp.tile(cos_ref[...], (1, MIX // LANES))
    sin = jnp.tile(sin_ref[...], (1, MIX // LANES))
    lane = lax.broadcasted_iota(jnp.int32, (ts, MIX), 1)
    first_half = (lane % DIFF_COMP) < (DIFF_COMP // 2)

    def group(i):
        return _mm(a, w_ref[:, i * MIX:(i + 1) * MIX])

    def rope(t):
        partner = jnp.where(first_half,
                            pltpu.roll(t, MIX - DIFF_COMP // 2, 1),
                            pltpu.roll(t, DIFF_COMP // 2, 1))
        return t * cos + partner * sin

    qd = rope(group(0)) * (DIFF_COMP ** -0.5)
    kd = rope(group(1))
    vd = group(2)
    qs = group(3) * (HEAD_DIM ** -0.5)
    ks = group(4)
    vs = group(5)
    kd_ref[...] = kd
    vd_ref[...] = vd
    ks_ref[...] = ks
    vs_ref[...] = vs
    comp0 = lax.broadcasted_iota(jnp.int32, (ts, HEAD_DIM), 1) < DIFF_COMP
    for h in range(HEADS):
        sl = slice(h * HEAD_DIM, (h + 1) * HEAD_DIM)
        qh = qd[:, sl]
        qd2_ref[0, h, 0] = jnp.where(comp0, qh, 0.0).astype(BF16)
        qd2_ref[0, h, 1] = jnp.where(comp0, 0.0, qh).astype(BF16)
        kdb_ref[0, h] = kd[:, sl].astype(BF16)
        vdb_ref[0, h] = _with_ones(vd[:, sl].astype(BF16))
        qsb_ref[0, h] = qs[:, sl].astype(BF16)
        ksb_ref[0, h] = ks[:, sl].astype(BF16)
        vsb_ref[0, h] = vs[:, sl].astype(BF16)


def _proj(x, g_mix, w_in_b, cos_t, sin_t, nb, seq):
    t = nb * seq
    ts = min(256, seq)
    nst = seq // ts
    row = pl.BlockSpec((ts, MIX), lambda i: (i, 0))
    hm = pl.BlockSpec((1, HEADS, ts, HEAD_DIM), lambda i: (i // nst, 0, i % nst, 0))
    hm2 = pl.BlockSpec((1, HEADS, 2, ts, HEAD_DIM), lambda i: (i // nst, 0, 0, i % nst, 0))
    rows = jax.ShapeDtypeStruct((t, MIX), F32)
    heads = jax.ShapeDtypeStruct((nb, HEADS, seq, HEAD_DIM), BF16)
    heads2 = jax.ShapeDtypeStruct((nb, HEADS, 2, seq, HEAD_DIM), BF16)
    hm_ext = pl.BlockSpec((1, HEADS, ts, 2 * HEAD_DIM), lambda i: (i // nst, 0, i % nst, 0))
    heads_ext = jax.ShapeDtypeStruct((nb, HEADS, seq, 2 * HEAD_DIM), BF16)
    return pl.pallas_call(
        _proj_body,
        grid=(t // ts,),
        in_specs=[
            pl.BlockSpec((ts, D_MODEL), lambda i: (i, 0)),
            pl.BlockSpec((1, D_MODEL), lambda i: (0, 0)),
            pl.BlockSpec((D_MODEL, 6 * MIX), lambda i: (0, 0)),
            pl.BlockSpec((ts, LANES), lambda i: (i % nst, 0)),
            pl.BlockSpec((ts, LANES), lambda i: (i % nst, 0)),
        ],
        out_specs=[row, row, row, row, hm2, hm, hm_ext, hm, hm, hm],
        out_shape=[rows, rows, rows, rows, heads2, heads, heads_ext, heads, heads, heads],
        compiler_params=_params("parallel"),
        name="proj",
    )(x, g_mix, w_in_b, cos_t, sin_t)


def _diff_init(rows):
    return jnp.full((rows, 1), -jnp.inf, F32), jnp.zeros((rows, 2 * HEAD_DIM), F32)


def _diff_update(s, v_ext, carry):
    m, acc = carry
    m_new = jnp.maximum(m, jnp.max(s, axis=-1, keepdims=True))
    p = jnp.exp(s - m_new)
    acc = jnp.exp(m - m_new) * acc + _mm(p.astype(BF16), v_ext)
    return m_new, acc


def _diff_finish(carry, lam, g_subln, tq):
    _, acc = carry
    o = acc[:, :HEAD_DIM] / acc[:, HEAD_DIM:HEAD_DIM + 1]
    d = o[:tq] - lam * o[tq:]
    return _rms(d, g_subln, SUBLN_EPS) * (1.0 - LAM_INIT)


def _suffix_sums(lk, tri):
    hi = lk.astype(BF16)
    lo = (lk - hi.astype(F32)).astype(BF16)
    return _mm(hi, tri) + _mm(lo, tri)


def _sb_update(q, k, v, tri, earlier, carry):
    run, acc = carry
    z = _nt(q, k)
    sp = jnp.maximum(z, 0.0) + jnp.log1p(jnp.exp(-jnp.abs(z)))
    lk = -sp if earlier is None else jnp.where(earlier, -sp, 0.0)
    after = _suffix_sums(lk, tri)
    w = jnp.exp((z - sp) + after + run)
    if earlier is not None:
        w = jnp.where(earlier, w, 0.0)
    acc = acc + _mm(w.astype(BF16), v)
    run = run + after[:, 0:1] + lk[:, 0:1]
    return run, acc


def _tri(n):
    j = lax.broadcasted_iota(jnp.int32, (n, n), 0)
    s = lax.broadcasted_iota(jnp.int32, (n, n), 1)
    return (j > s).astype(BF16)


def _diff_body(lam_ref, q_ref, k_ref, v_ref, g_ref, o_ref, *, tq):
    qi = pl.program_id(1)
    q2 = q_ref[0, 0].reshape(2 * tq, HEAD_DIM)

    def scores(j):
        return _nt(q2, k_ref[0, 0, pl.ds(pl.multiple_of(j * tq, tq), tq), :])

    def step(j, state):
        s, carry = state
        s_next = scores(j + 1)
        return s_next, _diff_update(s, v_ref[0, 0, pl.ds(pl.multiple_of(j * tq, tq), tq), :], carry)

    s, carry = lax.fori_loop(0, qi, step, (scores(0), _diff_init(2 * tq)))
    r = lax.broadcasted_iota(jnp.int32, (2 * tq, tq), 0) % tq
    c = lax.broadcasted_iota(jnp.int32, (2 * tq, tq), 1)
    s = jnp.where((c // CHUNK) <= (r // CHUNK), s, -jnp.inf)
    carry = _diff_update(s, v_ref[0, 0, pl.ds(pl.multiple_of(qi * tq, tq), tq), :], carry)
    o_ref[0, 0] = _diff_finish(carry, lam_ref[0], g_ref[...], tq)


def _diff_attention(lam, qd2, kdb, vdb, g_subln, b):
    seq = qd2.shape[3]
    tq = min(512, seq)
    return pl.pallas_call(
        functools.partial(_diff_body, tq=tq),
        grid=(HEADS, seq // tq),
        in_specs=[
            pl.BlockSpec(memory_space=pltpu.SMEM),
            pl.BlockSpec((1, 1, 2, tq, HEAD_DIM), lambda h, i: (b, h, 0, i, 0)),
            pl.BlockSpec((1, 1, seq, HEAD_DIM), lambda h, i: (b, h, 0, 0)),
            pl.BlockSpec((1, 1, seq, 2 * HEAD_DIM), lambda h, i: (b, h, 0, 0)),
            pl.BlockSpec((1, HEAD_DIM), lambda h, i: (0, 0)),
        ],
        out_specs=pl.BlockSpec((1, 1, tq, HEAD_DIM), lambda h, i: (0, h, i, 0)),
        out_shape=jax.ShapeDtypeStruct((1, HEADS, seq, HEAD_DIM), F32),
        compiler_params=_params("parallel", "arbitrary"),
        name="diff_attention",
    )(lam, qd2, kdb, vdb, g_subln)


def _sb_body(after_ref, q_ref, k_ref, v_ref, o_ref, *, tq):
    del after_ref
    qi = pl.program_id(1)
    q = q_ref[0, 0]
    tri = _tri(tq)

    def tile(j):
        start = pl.multiple_of(j * tq, tq)
        return k_ref[0, 0, pl.ds(start, tq), :], v_ref[0, 0, pl.ds(start, tq), :]

    r = lax.broadcasted_iota(jnp.int32, (tq, tq), 0)
    c = lax.broadcasted_iota(jnp.int32, (tq, tq), 1)
    carry = (jnp.zeros((tq, 1), F32), jnp.zeros((tq, HEAD_DIM), F32))
    run, acc = _sb_update(q, *tile(qi), tri, c < r, carry)

    def live(state):
        j, run, _ = state
        return jnp.logical_and(j >= 0, jnp.max(run) > SB_LOG_FLOOR)

    def step(state):
        j, run, acc = state
        run, acc = _sb_update(q, *tile(j), tri, None, (run, acc))
        return j - 1, run, acc

    o_ref[0, 0] = lax.while_loop(live, step, (qi - 1, run, acc))[2]


def _sb_attention(qsb, ksb, vsb, b, after):
    seq = qsb.shape[2]
    tq = 256
    kv = pl.BlockSpec((1, 1, seq, HEAD_DIM), lambda h, i: (b, h, 0, 0))
    return pl.pallas_call(
        functools.partial(_sb_body, tq=tq),
        grid=(HEADS, seq // tq),
        in_specs=[pl.BlockSpec(memory_space=pl.ANY),
                  pl.BlockSpec((1, 1, tq, HEAD_DIM), lambda h, i: (b, h, i, 0)), kv, kv],
        out_specs=pl.BlockSpec((1, 1, tq, HEAD_DIM), lambda h, i: (0, h, i, 0)),
        out_shape=jax.ShapeDtypeStruct((1, HEADS, seq, HEAD_DIM), F32),
        compiler_params=_params("parallel", "arbitrary"),
        name="sb_attention",
    )(after, qsb, ksb, vsb)


SAMPLE_HEADS = 4
SAMPLE_TILE = 256


def _sample_body(lam_ref, qd2_ref, kdn_ref, vdn_ref, qs_ref, ksn_ref, vsn_ref,
                 ckd_ref, cvd_ref, cks_ref, cvs_ref, g_ref, do_ref, so_ref, *, past, nq):
    lam = lam_ref[0]
    tri_c = _tri(SAMPLE_TILE)
    tri_n = _tri(nq)
    i2 = lax.broadcasted_iota(jnp.int32, (2 * nq, nq), 0) % nq
    j2 = lax.broadcasted_iota(jnp.int32, (2 * nq, nq), 1)
    visible_new = ((past + j2) // CHUNK) <= ((past + i2) // CHUNK)
    i1 = lax.broadcasted_iota(jnp.int32, (nq, nq), 0)
    j1 = lax.broadcasted_iota(jnp.int32, (nq, nq), 1)
    earlier_new = j1 < i1
    for h in range(SAMPLE_HEADS):
        sl = slice(h * HEAD_DIM, (h + 1) * HEAD_DIM)
        q2 = qd2_ref[0, h].reshape(2 * nq, HEAD_DIM)
        carry = _diff_update(_nt(q2, ckd_ref[0, :, sl].astype(BF16)),
                             _with_ones(cvd_ref[0, :, sl].astype(BF16)), _diff_init(2 * nq))
        s_new = jnp.where(visible_new, _nt(q2, kdn_ref[0, h]), -jnp.inf)
        carry = _diff_update(s_new, vdn_ref[0, h], carry)
        do_ref[0, h] = _diff_finish(carry, lam, g_ref[...], nq)
        q = qs_ref[0, h]
        carry = (jnp.zeros((nq, 1), F32), jnp.zeros((nq, HEAD_DIM), F32))
        carry = _sb_update(q, ksn_ref[0, h], vsn_ref[0, h], tri_n, earlier_new, carry)
        for t in reversed(range(past // SAMPLE_TILE)):
            rows = slice(t * SAMPLE_TILE, (t + 1) * SAMPLE_TILE)
            carry = _sb_update(q, cks_ref[0, rows, sl].astype(BF16), cvs_ref[0, rows, sl].astype(BF16),
                               tri_c, None, carry)
        so_ref[0, h] = carry[1]


def _sample_attention(lam, qd2, kdb, vdb, qsb, ksb, vsb, ckd, cvd, cks, cvs, g_subln, nb, nq):
    past = ckd.shape[1]
    nhg = HEADS // SAMPLE_HEADS
    hm = pl.BlockSpec((1, SAMPLE_HEADS, nq, HEAD_DIM), lambda b, g: (0, g, b, 0))
    hm2 = pl.BlockSpec((1, SAMPLE_HEADS, 2, nq, HEAD_DIM), lambda b, g: (0, g, 0, b, 0))
    cache = pl.BlockSpec((1, past, SAMPLE_HEADS * HEAD_DIM), lambda b, g: (b, 0, g))
    out = jax.ShapeDtypeStruct((1, HEADS, nb * nq, HEAD_DIM), F32)
    return pl.pallas_call(
        functools.partial(_sample_body, past=past, nq=nq),
        grid=(nb, nhg),
        in_specs=[pl.BlockSpec(memory_space=pltpu.SMEM), hm2, hm,
                  pl.BlockSpec((1, SAMPLE_HEADS, nq, 2 * HEAD_DIM), lambda b, g: (0, g, b, 0)), hm, hm, hm,
                  cache, cache, cache, cache,
                  pl.BlockSpec((1, HEAD_DIM), lambda b, g: (0, 0))],
        out_specs=[hm, hm],
        out_shape=[out, out],
        compiler_params=_params("parallel", "parallel"),
        name="sample_attention",
    )(lam, qd2, kdb, vdb, qsb, ksb, vsb, ckd, cvd, cks, cvs, g_subln)


def _topk_rows(s, k):
    n = s.shape[0]
    rows = lax.broadcasted_iota(jnp.int32, s.shape, 0)
    vals, ids = [], []
    for _ in range(k):
        m = jnp.max(s, axis=0, keepdims=True)
        i = jnp.min(jnp.where(s == m, rows, n), axis=0, keepdims=True)
        vals.append(m)
        ids.append(i)
        s = jnp.where(rows == i, -jnp.inf, s)
    return jnp.concatenate(vals, axis=0), jnp.concatenate(ids, axis=0)


_STAIR = [(i, j) for i in range(PEER_TOPK) for j in range(PEER_TOPK) if (i + 1) * (j + 1) <= PEER_TOPK]


def _post_body(x_ref, do_ref, so_ref, wo_ref, gf_ref, wq_ref, sk_ref,
               h1_ref, c_ref, idx_ref, gate_ref, q_scr, idx_scr, gate_scr):
    ts = x_ref.shape[0]
    mixed = jnp.zeros((ts, D_MODEL), F32)
    for h in range(HEADS):
        mixed += _mm(do_ref[0, h].astype(BF16), wo_ref[h * HEAD_DIM:(h + 1) * HEAD_DIM, :])
        mixed += _mm(so_ref[0, h].astype(BF16), wo_ref[MIX + h * HEAD_DIM:MIX + (h + 1) * HEAD_DIM, :])
    h1 = x_ref[...] + mixed
    h1_ref[...] = h1
    c = _rms(h1, gf_ref[...], NORM_EPS)
    c_ref[...] = c
    q = _mm(c.astype(BF16), wq_ref[...])
    for hp in range(2 * PEER_HEADS):
        q_scr[hp] = q[:, hp * PEER_HALF:(hp + 1) * PEER_HALF].astype(BF16)

    npad = -len(_STAIR) % 8

    def head(h, _):
        v1, i1 = _topk_rows(_nt(sk_ref[0], q_scr[2 * h]), PEER_TOPK)
        v2, i2 = _topk_rows(_nt(sk_ref[1], q_scr[2 * h + 1]), PEER_TOPK)
        cand = jnp.concatenate([v1[i:i + 1] + v2[j:j + 1] for i, j in _STAIR]
                               + [jnp.full((npad, ts), -jnp.inf, F32)], axis=0)
        eid = jnp.concatenate([i1[i:i + 1] * PEER_KEYS + i2[j:j + 1] for i, j in _STAIR]
                              + [jnp.zeros((npad, ts), jnp.int32)], axis=0)
        top, pos = _topk_rows(cand, PEER_TOPK)
        rows = lax.broadcasted_iota(jnp.int32, cand.shape, 0)
        sel = jnp.concatenate([jnp.sum(jnp.where(rows == pos[r:r + 1], eid, 0), axis=0, keepdims=True)
                               for r in range(PEER_TOPK)], axis=0)
        e = jnp.exp(top - top[0:1])
        gate_scr[h] = e / jnp.sum(e, axis=0, keepdims=True)
        idx_scr[h] = sel
        return 0

    lax.fori_loop(0, PEER_HEADS, head, 0)
    idx_ref[...] = idx_scr[...].reshape(PEER_SEL, ts).T
    gate_ref[...] = gate_scr[...].reshape(PEER_SEL, ts).T


def _post(x, dout, sout, w_out_b, g_ffn, w_query_b, sub_keys_b, b, seq):
    t = seq
    ts = min(256, seq)
    nst = seq // ts
    row = pl.BlockSpec((ts, D_MODEL), lambda i: (i, 0))
    hm = pl.BlockSpec((1, HEADS, ts, HEAD_DIM), lambda i: (0, 0, i, 0))
    sel = pl.BlockSpec((ts, PEER_SEL), lambda i: (i, 0))
    full = lambda *shape: pl.BlockSpec(shape, lambda i: (0,) * len(shape))
    return pl.pallas_call(
        _post_body,
        grid=(t // ts,),
        in_specs=[pl.BlockSpec((ts, D_MODEL), lambda i: (b * nst + i, 0)),
                  hm, hm, full(2 * MIX, D_MODEL), full(1, D_MODEL),
                  full(D_MODEL, 2 * PEER_HEADS * PEER_HALF), full(2, PEER_KEYS, PEER_HALF)],
        out_specs=[row, row, sel, sel],
        out_shape=[jax.ShapeDtypeStruct((t, D_MODEL), F32), jax.ShapeDtypeStruct((t, D_MODEL), F32),
                   jax.ShapeDtypeStruct((t, PEER_SEL), jnp.int32), jax.ShapeDtypeStruct((t, PEER_SEL), F32)],
        scratch_shapes=[pltpu.VMEM((2 * PEER_HEADS, ts, PEER_HALF), BF16),
                        pltpu.VMEM((PEER_HEADS, PEER_TOPK, ts), jnp.int32),
                        pltpu.VMEM((PEER_HEADS, PEER_TOPK, ts), F32)],
        compiler_params=_params("parallel"),
        name="post_peer_select",
    )(x, dout, sout, w_out_b, g_ffn, w_query_b, sub_keys_b)


def _coef_body(after_ref, gate_ref, dots_ref, o_ref):
    del after_ref
    d = dots_ref[...]
    o_ref[...] = gate_ref[...] * (0.5 * d * (1.0 + lax.erf(d * (2.0 ** -0.5))))


def _coef(gate, dots, after):
    t = gate.shape[0]
    ts = min(2048, t)
    blk = pl.BlockSpec((ts, PEER_SEL), lambda i: (i, 0))
    return pl.pallas_call(
        _coef_body, grid=(t // ts,), in_specs=[pl.BlockSpec(memory_space=pl.ANY), blk, blk], out_specs=blk,
        out_shape=jax.ShapeDtypeStruct((t, PEER_SEL), F32),
        compiler_params=_params("parallel"), name="peer_coef",
    )(after, gate, dots)


SC_CORES = 2
SC_SUBCORES = 16
SC_LANES = 16
SC_WORKERS = SC_CORES * SC_SUBCORES
SC_ROWS = 32
SC_BUFS = 3
SC_GROUP = 8
SC_CHUNKS = PEER_SEL // SC_ROWS
SC_STEPS = SC_GROUP * SC_CHUNKS
SC_VECS = D_MODEL // SC_LANES


def _sc_mesh():
    return plsc.VectorSubcoreMesh(core_axis_name="c", subcore_axis_name="s",
                                  num_cores=SC_CORES, num_subcores=SC_SUBCORES)


def _sc_walk(table_hbm, idx_hbm, aux_hbm, idx_v, aux_v, rows_v, sem, stage_sem, tpw, begin_group, compute, end_group):
    tok_base = (lax.axis_index("s") * SC_CORES + lax.axis_index("c")) * tpw
    ngroups = tpw // SC_GROUP
    nsteps = tpw * SC_CHUNKS

    def first_token(g):
        return pl.multiple_of(tok_base + g * SC_GROUP, SC_GROUP)

    def stage(g):
        tok0 = first_token(g)
        return (pltpu.make_async_copy(idx_hbm.at[pl.ds(tok0 * SC_CHUNKS, SC_STEPS)], idx_v.at[g % 2], stage_sem.at[0]),
                pltpu.make_async_copy(aux_hbm.at[pl.ds(tok0, SC_GROUP)], aux_v.at[g % 2], stage_sem.at[1]))

    def gather(step):
        idx = idx_v.at[(step // SC_STEPS) % 2, step % SC_STEPS]
        return pltpu.make_async_copy(table_hbm.at[idx], rows_v.at[step % SC_BUFS], sem.at[step % SC_BUFS])

    for cp in stage(0):
        cp.start()
    for cp in stage(0):
        cp.wait()
    for step in range(SC_BUFS - 1):
        gather(step).start()

    def walk(step, _):
        g = step // SC_STEPS
        local = step % SC_STEPS

        @pl.when(jnp.logical_and(local == 0, g + 1 < ngroups))
        def _():
            for cp in stage(g + 1):
                cp.start()

        ahead = step + (SC_BUFS - 1)

        @pl.when(ahead < nsteps)
        def _():
            @pl.when(ahead % SC_STEPS == 0)
            def _():
                for cp in stage(ahead // SC_STEPS):
                    cp.wait()

            gather(ahead).start()

        @pl.when(local == 0)
        def _():
            begin_group()

        gather(step).wait()
        compute(rows_v.at[step % SC_BUFS], g % 2, local // SC_CHUNKS, local % SC_CHUNKS)

        @pl.when(local == SC_STEPS - 1)
        def _():
            end_group(first_token(g))

        return 0

    lax.fori_loop(0, nsteps, walk, 0)


def _sc_dots_body(u_hbm, idx_hbm, c_hbm, out_hbm, idx_v, c_v, rows_v, dots_v, sem, stage_sem, *, tpw):
    lane = lax.broadcasted_iota(jnp.int32, (SC_LANES,), 0)
    zero = jnp.zeros((SC_LANES,), F32)

    def compute(rows, slot, tt, ch):
        for half in range(SC_ROWS // SC_LANES):
            outv = zero
            for q in range(2):
                r0 = half * SC_LANES + q * 8

                def vec(kk, accs):
                    off = pl.multiple_of(kk * SC_LANES, SC_LANES)
                    cv = c_v[slot, tt, pl.ds(off, SC_LANES)]
                    return tuple(a + rows[r0 + r, pl.ds(off, SC_LANES)] * cv for r, a in enumerate(accs))

                accs = lax.fori_loop(0, SC_VECS, vec, (zero,) * 8)
                for r in range(8):
                    outv = jnp.where(lane == q * 8 + r, jnp.sum(accs[r]), outv)
            dots_v[tt, pl.ds(pl.multiple_of(ch * SC_ROWS + half * SC_LANES, SC_LANES), SC_LANES)] = outv

    def end_group(tok0):
        pltpu.sync_copy(dots_v, out_hbm.at[pl.ds(tok0, SC_GROUP)])

    _sc_walk(u_hbm, idx_hbm, c_hbm, idx_v, c_v, rows_v, sem, stage_sem, tpw, lambda: None, compute, end_group)


def _sc_combine_body(v_hbm, idx_hbm, coef_hbm, out_hbm, idx_v, coef_v, rows_v, acc_v, sem, stage_sem, *, tpw):
    lane = lax.broadcasted_iota(jnp.int32, (SC_LANES,), 0)
    zero = jnp.zeros((SC_LANES,), F32)

    def compute(rows, slot, tt, ch):
        for half in range(SC_ROWS // SC_LANES):
            cf = coef_v[slot, tt, pl.ds(pl.multiple_of(ch * SC_ROWS + half * SC_LANES, SC_LANES), SC_LANES)]
            splat = [jnp.full((SC_LANES,), jnp.sum(jnp.where(lane == r, cf, 0.0)), F32) for r in range(SC_LANES)]

            @plsc.parallel_loop(0, SC_VECS, unroll=2)
            def _(kk):
                off = pl.multiple_of(kk * SC_LANES, SC_LANES)
                terms = [rows[half * SC_LANES + r, pl.ds(off, SC_LANES)] * splat[r] for r in range(SC_LANES)]
                while len(terms) > 1:
                    terms = [a + b for a, b in zip(terms[0::2], terms[1::2])]
                acc_v[tt, pl.ds(off, SC_LANES)] = acc_v[tt, pl.ds(off, SC_LANES)] + terms[0]

    def begin_group():
        def clear(i, _):
            acc_v[i // SC_VECS, pl.ds(pl.multiple_of((i % SC_VECS) * SC_LANES, SC_LANES), SC_LANES)] = zero
            return 0

        lax.fori_loop(0, SC_GROUP * SC_VECS, clear, 0)

    def end_group(tok0):
        pltpu.sync_copy(acc_v, out_hbm.at[pl.ds(tok0, SC_GROUP)])

    _sc_walk(v_hbm, idx_hbm, coef_hbm, idx_v, coef_v, rows_v, sem, stage_sem, tpw, begin_group, compute, end_group)


def _sc_call(body, table, idx, per_token, out_width, name):
    t = per_token.shape[0]
    tpw = t // SC_WORKERS
    assert tpw % SC_GROUP == 0
    return pl.kernel(
        functools.partial(body, tpw=tpw),
        out_type=jax.ShapeDtypeStruct((t, out_width), F32),
        mesh=_sc_mesh(),
        scratch_types=[pltpu.VMEM((2, SC_STEPS, SC_ROWS), jnp.int32),
                       pltpu.VMEM((2, SC_GROUP, per_token.shape[1]), F32),
                       pltpu.VMEM((SC_BUFS, SC_ROWS, D_MODEL), F32),
                       pltpu.VMEM((SC_GROUP, out_width), F32),
                       pltpu.SemaphoreType.DMA((SC_BUFS,)),
                       pltpu.SemaphoreType.DMA((2,))],
        compiler_params=pltpu.CompilerParams(needs_layout_passes=False),
        name=name,
    )(table, idx.reshape(t * SC_CHUNKS, SC_ROWS), per_token)


def _sc_dots(expert_u, idx, c):
    return _sc_call(_sc_dots_body, expert_u, idx, c, PEER_SEL, "peer_dots")


def _sc_combine(expert_v, idx, coef):
    return _sc_call(_sc_combine_body, expert_v, idx, coef, D_MODEL, "peer_combine")


def _ple_body(h1_ref, peer_ref, p_ref, gp_ref, wg_ref, we_ref, gfin_ref, y_ref):
    h = h1_ref[...] + peer_ref[...]
    a = _rms(h, gp_ref[...], NORM_EPS).astype(BF16)
    gate = jax.nn.sigmoid(_mm(a, wg_ref[...]))
    h = h + _mm(p_ref[...].astype(BF16), we_ref[...]) * gate
    y_ref[...] = _rms(h, gfin_ref[...], NORM_EPS)


def _ple(h1, peer, p, g_ple, w_pgate_b, w_ple_b, g_final, b):
    t = h1.shape[0]
    ts = min(512, t)
    nst = t // ts
    row = pl.BlockSpec((ts, D_MODEL), lambda i: (i, 0))
    full = lambda *shape: pl.BlockSpec(shape, lambda i: (0,) * len(shape))
    return pl.pallas_call(
        _ple_body,
        grid=(t // ts,),
        in_specs=[row, row, pl.BlockSpec((ts, PLE_DIM), lambda i: (b * nst + i, 0)), full(1, D_MODEL),
                  full(D_MODEL, D_MODEL), full(PLE_DIM, D_MODEL), full(1, D_MODEL)],
        out_specs=row,
        out_shape=jax.ShapeDtypeStruct((t, D_MODEL), F32),
        compiler_params=_params("parallel"),
        name="ple_final",
    )(h1, peer, p, g_ple, w_pgate_b, w_ple_b, g_final)


def _rope_tables(pos):
    half = DIFF_COMP // 2
    inv = ROPE_THETA ** (-jnp.arange(0, DIFF_COMP, 2, dtype=F32) / DIFF_COMP)
    ang = pos.astype(F32)[:, None] * inv[None, :]
    cos = jnp.cos(ang)
    sin = jnp.sin(ang)
    reps = LANES // DIFF_COMP
    del half
    return (jnp.tile(jnp.concatenate([cos, cos], axis=-1), (1, reps)),
            jnp.tile(jnp.concatenate([-sin, sin], axis=-1), (1, reps)))


def kernel(x_prompt, x_sample, cache_diff_k, cache_diff_v, cache_sb_k, cache_sb_v, p_prompt, p_sample, g_mix, w_in, lambda_q1, lambda_k1, lambda_q2, lambda_k2, g_subln, w_out, g_ffn, w_query, sub_keys, expert_u, expert_v, g_ple, w_pgate, w_ple, g_final):
    assert w_in.shape[0] == 1, "single-layer encoder"
    nb, seq, _ = x_prompt.shape
    db, dq, _ = x_sample.shape
    past = cache_diff_k.shape[2]

    lam = (jnp.exp(jnp.sum(lambda_q1[0].astype(F32) * lambda_k1[0].astype(F32)))
           - jnp.exp(jnp.sum(lambda_q2[0].astype(F32) * lambda_k2[0].astype(F32))) + LAM_INIT).reshape(1)
    w_in_b = w_in[0].astype(BF16)
    w_out_b = w_out[0].astype(BF16)
    w_query_b = w_query[0].astype(BF16)
    sub_keys_b = sub_keys[0].astype(BF16)
    w_pgate_b = w_pgate[0].astype(BF16)
    w_ple_b = w_ple[0].astype(BF16)
    g_sub = g_subln[0].reshape(1, HEAD_DIM)
    g_fin = g_final.reshape(1, D_MODEL)

    def select(x, dout, sout, b, s):
        h1, c, idx, gate = _post(x, dout, sout, w_out_b, g_ffn, w_query_b, sub_keys_b, b, s)
        return h1, idx, gate, _sc_dots(expert_u[0], idx, c)

    def experts(h1, idx, gate, dots, p, b, after):
        coef = _coef(gate, dots, after)
        peer = _sc_combine(expert_v[0], idx, coef)
        return coef, _ple(h1, peer, p, g_ple, w_pgate_b, w_ple_b, g_fin, b)

    ts = db * dq
    xs = x_sample.reshape(ts, D_MODEL)
    cos_s, sin_s = _rope_tables(jnp.tile(past + jnp.arange(dq, dtype=jnp.int32), db))
    kd, vd, ks, vs, qd2, kdb, vdb, qsb, ksb, vsb = _proj(xs, g_mix, w_in_b, cos_s, sin_s, 1, ts)
    caches = [c[0].reshape(db, past, MIX) for c in (cache_diff_k, cache_diff_v, cache_sb_k, cache_sb_v)]
    dout_s, sout_s = _sample_attention(lam, qd2, kdb, vdb, qsb, ksb, vsb, *caches, g_sub, db, dq)
    rows_s = tuple(r.reshape(1, db, dq, HEADS, HEAD_DIM) for r in (kd, vd, ks, vs))

    xp = x_prompt.reshape(nb * seq, D_MODEL)
    pp = p_prompt[0].reshape(nb * seq, PLE_DIM)
    cos_p, sin_p = _rope_tables(jnp.arange(seq, dtype=jnp.int32))
    kd, vd, ks, vs, qd2, kdb, vdb, qsb, ksb, vsb = _proj(xp, g_mix, w_in_b, cos_p, sin_p, nb, seq)
    y_rows = []
    dout = _diff_attention(lam, qd2, kdb, vdb, g_sub, 0)
    coef = lam
    for b in range(nb):
        sout = _sb_attention(qsb, ksb, vsb, b, coef)
        h1, idx, gate, dots = select(xp, dout, sout, b, seq)
        dout = _diff_attention(lam, qd2, kdb, vdb, g_sub, b + 1) if b + 1 < nb else dout_s
        coef, y = experts(h1, idx, gate, dots, pp, b, dout)
        y_rows.append(y)
    y_prompt = jnp.stack(y_rows)
    rows_p = tuple(r.reshape(1, nb, seq, HEADS, HEAD_DIM) for r in (kd, vd, ks, vs))

    h1, idx, gate, dots = select(xs, dout_s, sout_s, 0, ts)
    y_sample = experts(h1, idx, gate, dots, p_sample[0].reshape(ts, PLE_DIM), 0, coef)[1].reshape(db, dq, D_MODEL)

    return (y_prompt, y_sample) + rows_p + rows_s
```

```python
import functools
import math

import jax
import jax.numpy as jnp
from jax import lax
from jax.experimental import pallas as pl
from jax.experimental.pallas import tpu as pltpu
from jax.experimental.pallas import tpu_sc as plsc

F32 = jnp.float32
BF16 = jnp.bfloat16

D_MODEL = 1024
HEADS = 8
HEAD_DIM = 64
DIFF_COMP = 32
MIX = HEADS * HEAD_DIM
CHUNK = 64
ROPE_THETA = 10000.0
NORM_EPS = 1e-6
SUBLN_EPS = 1e-5
PEER_HEADS = 8
PEER_KEYS = 128
PEER_TOPK = 16
PEER_HALF = 128
PEER_SEL = PEER_HEADS * PEER_TOPK
PLE_DIM = 256
LAM_INIT = 0.8 - 0.6 * math.exp(-0.3 * 0)
SB_LOG_FLOOR = -104.0

LANES = 128
VMEM_LIMIT = 48 * 1024 * 1024

NT_DIMS = (((1,), (1,)), ((), ()))


def _nt(a, b):
    return lax.dot_general(a, b, NT_DIMS, preferred_element_type=F32)


def _mm(a, b):
    return jnp.dot(a, b, preferred_element_type=F32)


def _rms(x, g, eps):
    return x * lax.rsqrt(jnp.mean(x * x, axis=-1, keepdims=True) + eps) * g


def _params(*sem):
    return pltpu.CompilerParams(dimension_semantics=sem, vmem_limit_bytes=VMEM_LIMIT)


def _with_ones(v):
    n = v.shape[0]
    ones = (lax.broadcasted_iota(jnp.int32, (n, HEAD_DIM), 1) == 0).astype(v.dtype)
    return jnp.concatenate([v, ones], axis=1)


def _proj_body(x_ref, g_ref, w_ref, cos_ref, sin_ref,
               kd_ref, vd_ref, ks_ref, vs_ref,
               qd2_ref, kdb_ref, vdb_ref, qsb_ref, ksb_ref, vsb_ref):
    ts = x_ref.shape[0]
    a = _rms(x_ref[...], g_ref[...], NORM_EPS).astype(BF16)
    cos = jnp.tile(cos_ref[...], (1, MIX // LANES))
    sin = jnp.tile(sin_ref[...], (1, MIX // LANES))
    lane = lax.broadcasted_iota(jnp.int32, (ts, MIX), 1)
    first_half = (lane % DIFF_COMP) < (DIFF_COMP // 2)

    def group(i):
        return _mm(a, w_ref[:, i * MIX:(i + 1) * MIX])

    def rope(t):
        partner = jnp.where(first_half,
                            pltpu.roll(t, MIX - DIFF_COMP // 2, 1),
                            pltpu.roll(t, DIFF_COMP // 2, 1))
        return t * cos + partner * sin

    qd = rope(group(0)) * (DIFF_COMP ** -0.5)
    kd = rope(group(1))
    vd = group(2)
    qs = group(3) * (HEAD_DIM ** -0.5)
    ks = group(4)
    vs = group(5)
    kd_ref[...] = kd
    vd_ref[...] = vd
    ks_ref[...] = ks
    vs_ref[...] = vs
    comp0 = lax.broadcasted_iota(jnp.int32, (ts, HEAD_DIM), 1) < DIFF_COMP
    for h in range(HEADS):
        sl = slice(h * HEAD_DIM, (h + 1) * HEAD_DIM)
        qh = qd[:, sl]
        qd2_ref[0, h, 0] = jnp.where(comp0, qh, 0.0).astype(BF16)
        qd2_ref[0, h, 1] = jnp.where(comp0, 0.0, qh).astype(BF16)
        kdb_ref[0, h] = kd[:, sl].astype(BF16)
        vdb_ref[0, h] = _with_ones(vd[:, sl].astype(BF16))
        qsb_ref[0, h] = qs[:, sl].astype(BF16)
        ksb_ref[0, h] = ks[:, sl].astype(BF16)
        vsb_ref[0, h] = vs[:, sl].astype(BF16)


def _proj(x, g_mix, w_in_b, cos_t, sin_t, nb, seq):
    t = nb * seq
    ts = min(256, seq)
    nst = seq // ts
    row = pl.BlockSpec((ts, MIX), lambda i: (i, 0))
    hm = pl.BlockSpec((1, HEADS, ts, HEAD_DIM), lambda i: (i // nst, 0, i % nst, 0))
    hm2 = pl.BlockSpec((1, HEADS, 2, ts, HEAD_DIM), lambda i: (i // nst, 0, 0, i % nst, 0))
    rows = jax.ShapeDtypeStruct((t, MIX), F32)
    heads = jax.ShapeDtypeStruct((nb, HEADS, seq, HEAD_DIM), BF16)
    heads2 = jax.ShapeDtypeStruct((nb, HEADS, 2, seq, HEAD_DIM), BF16)
    hm_ext = pl.BlockSpec((1, HEADS, ts, 2 * HEAD_DIM), lambda i: (i // nst, 0, i % nst, 0))
    heads_ext = jax.ShapeDtypeStruct((nb, HEADS, seq, 2 * HEAD_DIM), BF16)
    return pl.pallas_call(
        _proj_body,
        grid=(t // ts,),
        in_specs=[
            pl.BlockSpec((ts, D_MODEL), lambda i: (i, 0)),
            pl.BlockSpec((1, D_MODEL), lambda i: (0, 0)),
            pl.BlockSpec((D_MODEL, 6 * MIX), lambda i: (0, 0)),
            pl.BlockSpec((ts, LANES), lambda i: (i % nst, 0)),
            pl.BlockSpec((ts, LANES), lambda i: (i % nst, 0)),
        ],
        out_specs=[row, row, row, row, hm2, hm, hm_ext, hm, hm, hm],
        out_shape=[rows, rows, rows, rows, heads2, heads, heads_ext, heads, heads, heads],
        compiler_params=_params("parallel"),
        name="proj",
    )(x, g_mix, w_in_b, cos_t, sin_t)


def _diff_init(rows):
    return jnp.full((rows, 1), -jnp.inf, F32), jnp.zeros((rows, 2 * HEAD_DIM), F32)


def _diff_update(s, v_ext, carry):
    m, acc = carry
    m_new = jnp.maximum(m, jnp.max(s, axis=-1, keepdims=True))
    p = jnp.exp(s - m_new)
    acc = jnp.exp(m - m_new) * acc + _mm(p.astype(BF16), v_ext)
    return m_new, acc


def _diff_finish(carry, lam, g_subln, tq):
    _, acc = carry
    o = acc[:, :HEAD_DIM] / acc[:, HEAD_DIM:HEAD_DIM + 1]
    d = o[:tq] - lam * o[tq:]
    return _rms(d, g_subln, SUBLN_EPS) * (1.0 - LAM_INIT)


def _suffix_sums(lk, tri):
    hi = lk.astype(BF16)
    lo = (lk - hi.astype(F32)).astype(BF16)
    return _mm(hi, tri) + _mm(lo, tri)


def _sb_update(q, k, v, tri, earlier, carry):
    run, acc = carry
    z = _nt(q, k)
    sp = jnp.maximum(z, 0.0) + jnp.log1p(jnp.exp(-jnp.abs(z)))
    lk = -sp if earlier is None else jnp.where(earlier, -sp, 0.0)
    after = _suffix_sums(lk, tri)
    w = jnp.exp((z - sp) + after + run)
    if earlier is not None:
        w = jnp.where(earlier, w, 0.0)
    acc = acc + _mm(w.astype(BF16), v)
    run = run + after[:, 0:1] + lk[:, 0:1]
    return run, acc


def _tri(n):
    j = lax.broadcasted_iota(jnp.int32, (n, n), 0)
    s = lax.broadcasted_iota(jnp.int32, (n, n), 1)
    return (j > s).astype(BF16)


def _diff_body(lam_ref, q_ref, k_ref, v_ref, g_ref, o_ref, *, tq):
    qi = pl.program_id(1)
    q2 = q_ref[0, 0].reshape(2 * tq, HEAD_DIM)

    def scores(j):
        return _nt(q2, k_ref[0, 0, pl.ds(pl.multiple_of(j * tq, tq), tq), :])

    def step(j, state):
        s, carry = state
        s_next = scores(j + 1)
        return s_next, _diff_update(s, v_ref[0, 0, pl.ds(pl.multiple_of(j * tq, tq), tq), :], carry)

    s, carry = lax.fori_loop(0, qi, step, (scores(0), _diff_init(2 * tq)))
    r = lax.broadcasted_iota(jnp.int32, (2 * tq, tq), 0) % tq
    c = lax.broadcasted_iota(jnp.int32, (2 * tq, tq), 1)
    s = jnp.where((c // CHUNK) <= (r // CHUNK), s, -jnp.inf)
    carry = _diff_update(s, v_ref[0, 0, pl.ds(pl.multiple_of(qi * tq, tq), tq), :], carry)
    o_ref[0, 0] = _diff_finish(carry, lam_ref[0], g_ref[...], tq)


def _diff_attention(lam, qd2, kdb, vdb, g_subln, b):
    seq = qd2.shape[3]
    tq = min(512, seq)
    return pl.pallas_call(
        functools.partial(_diff_body, tq=tq),
        grid=(HEADS, seq // tq),
        in_specs=[
            pl.BlockSpec(memory_space=pltpu.SMEM),
            pl.BlockSpec((1, 1, 2, tq, HEAD_DIM), lambda h, i: (b, h, 0, i, 0)),
            pl.BlockSpec((1, 1, seq, HEAD_DIM), lambda h, i: (b, h, 0, 0)),
            pl.BlockSpec((1, 1, seq, 2 * HEAD_DIM), lambda h, i: (b, h, 0, 0)),
            pl.BlockSpec((1, HEAD_DIM), lambda h, i: (0, 0)),
        ],
        out_specs=pl.BlockSpec((1, 1, tq, HEAD_DIM), lambda h, i: (0, h, i, 0)),
        out_shape=jax.ShapeDtypeStruct((1, HEADS, seq, HEAD_DIM), F32),
        compiler_params=_params("parallel", "arbitrary"),
        name="diff_attention",
    )(lam, qd2, kdb, vdb, g_subln)


def _sb_body(after_ref, q_ref, k_ref, v_ref, o_ref, *, tq):
    del after_ref
    qi = pl.program_id(1)
    q = q_ref[0, 0]
    tri = _tri(tq)

    def tile(j):
        start = pl.multiple_of(j * tq, tq)
        return k_ref[0, 0, pl.ds(start, tq), :], v_ref[0, 0, pl.ds(start, tq), :]

    r = lax.broadcasted_iota(jnp.int32, (tq, tq), 0)
    c = lax.broadcasted_iota(jnp.int32, (tq, tq), 1)
    carry = (jnp.zeros((tq, 1), F32), jnp.zeros((tq, HEAD_DIM), F32))
    run, acc = _sb_update(q, *tile(qi), tri, c < r, carry)

    def live(state):
        j, run, _ = state
        return jnp.logical_and(j >= 0, jnp.max(run) > SB_LOG_FLOOR)

    def step(state):
        j, run, acc = state
        run, acc = _sb_update(q, *tile(j), tri, None, (run, acc))
        return j - 1, run, acc

    o_ref[0, 0] = lax.while_loop(live, step, (qi - 1, run, acc))[2]


def _sb_attention(qsb, ksb, vsb, b, after):
    seq = qsb.shape[2]
    tq = 256
    kv = pl.BlockSpec((1, 1, seq, HEAD_DIM), lambda h, i: (b, h, 0, 0))
    return pl.pallas_call(
        functools.partial(_sb_body, tq=tq),
        grid=(HEADS, seq // tq),
        in_specs=[pl.BlockSpec(memory_space=pl.ANY),
                  pl.BlockSpec((1, 1, tq, HEAD_DIM), lambda h, i: (b, h, i, 0)), kv, kv],
        out_specs=pl.BlockSpec((1, 1, tq, HEAD_DIM), lambda h, i: (0, h, i, 0)),
        out_shape=jax.ShapeDtypeStruct((1, HEADS, seq, HEAD_DIM), F32),
        compiler_params=_params("parallel", "arbitrary"),
        name="sb_attention",
    )(after, qsb, ksb, vsb)


SAMPLE_HEADS = 4
SAMPLE_TILE = 256


def _sample_body(lam_ref, qd2_ref, kdn_ref, vdn_ref, qs_ref, ksn_ref, vsn_ref,
                 ckd_ref, cvd_ref, cks_ref, cvs_ref, g_ref, do_ref, so_ref, *, past, nq):
    lam = lam_ref[0]
    tri_c = _tri(SAMPLE_TILE)
    tri_n = _tri(nq)
    i2 = lax.broadcasted_iota(jnp.int32, (2 * nq, nq), 0) % nq
    j2 = lax.broadcasted_iota(jnp.int32, (2 * nq, nq), 1)
    visible_new = ((past + j2) // CHUNK) <= ((past + i2) // CHUNK)
    i1 = lax.broadcasted_iota(jnp.int32, (nq, nq), 0)
    j1 = lax.broadcasted_iota(jnp.int32, (nq, nq), 1)
    earlier_new = j1 < i1
    for h in range(SAMPLE_HEADS):
        sl = slice(h * HEAD_DIM, (h + 1) * HEAD_DIM)
        q2 = qd2_ref[0, h].reshape(2 * nq, HEAD_DIM)
        carry = _diff_update(_nt(q2, ckd_ref[0, :, sl].astype(BF16)),
                             _with_ones(cvd_ref[0, :, sl].astype(BF16)), _diff_init(2 * nq))
        s_new = jnp.where(visible_new, _nt(q2, kdn_ref[0, h]), -jnp.inf)
        carry = _diff_update(s_new, vdn_ref[0, h], carry)
        do_ref[0, h] = _diff_finish(carry, lam, g_ref[...], nq)
        q = qs_ref[0, h]
        carry = (jnp.zeros((nq, 1), F32), jnp.zeros((nq, HEAD_DIM), F32))
        carry = _sb_update(q, ksn_ref[0, h], vsn_ref[0, h], tri_n, earlier_new, carry)
        for t in reversed(range(past // SAMPLE_TILE)):
            rows = slice(t * SAMPLE_TILE, (t + 1) * SAMPLE_TILE)
            carry = _sb_update(q, cks_ref[0, rows, sl].astype(BF16), cvs_ref[0, rows, sl].astype(BF16),
                               tri_c, None, carry)
        so_ref[0, h] = carry[1]


def _sample_attention(lam, qd2, kdb, vdb, qsb, ksb, vsb, ckd, cvd, cks, cvs, g_subln, nb, nq):
    past = ckd.shape[1]
    nhg = HEADS // SAMPLE_HEADS
    hm = pl.BlockSpec((1, SAMPLE_HEADS, nq, HEAD_DIM), lambda b, g: (0, g, b, 0))
    hm2 = pl.BlockSpec((1, SAMPLE_HEADS, 2, nq, HEAD_DIM), lambda b, g: (0, g, 0, b, 0))
    cache = pl.BlockSpec((1, past, SAMPLE_HEADS * HEAD_DIM), lambda b, g: (b, 0, g))
    out = jax.ShapeDtypeStruct((1, HEADS, nb * nq, HEAD_DIM), F32)
    return pl.pallas_call(
        functools.partial(_sample_body, past=past, nq=nq),
        grid=(nb, nhg),
        in_specs=[pl.BlockSpec(memory_space=pltpu.SMEM), hm2, hm,
                  pl.BlockSpec((1, SAMPLE_HEADS, nq, 2 * HEAD_DIM), lambda b, g: (0, g, b, 0)), hm, hm, hm,
                  cache, cache, cache, cache,
                  pl.BlockSpec((1, HEAD_DIM), lambda b, g: (0, 0))],
        out_specs=[hm, hm],
        out_shape=[out, out],
        compiler_params=_params("parallel", "parallel"),
        name="sample_attention",
    )(lam, qd2, kdb, vdb, qsb, ksb, vsb, ckd, cvd, cks, cvs, g_subln)


def _topk_rows(s, k):
    n = s.shape[0]
    rows = lax.broadcasted_iota(jnp.int32, s.shape, 0)
    vals, ids = [], []
    for _ in range(k):
        m = jnp.max(s, axis=0, keepdims=True)
        i = jnp.min(jnp.where(s == m, rows, n), axis=0, keepdims=True)
        vals.append(m)
        ids.append(i)
        s = jnp.where(rows == i, -jnp.inf, s)
    return jnp.concatenate(vals, axis=0), jnp.concatenate(ids, axis=0)


_STAIR = [(i, j) for i in range(PEER_TOPK) for j in range(PEER_TOPK) if (i + 1) * (j + 1) <= PEER_TOPK]


def _post_body(x_ref, do_ref, so_ref, wo_ref, gf_ref, wq_ref, sk_ref,
               h1_ref, c_ref, idx_ref, gate_ref, q_scr, idx_scr, gate_scr):
    ts = x_ref.shape[0]
    mixed = jnp.zeros((ts, D_MODEL), F32)
    for h in range(HEADS):
        mixed += _mm(do_ref[0, h].astype(BF16), wo_ref[h * HEAD_DIM:(h + 1) * HEAD_DIM, :])
        mixed += _mm(so_ref[0, h].astype(BF16), wo_ref[MIX + h * HEAD_DIM:MIX + (h + 1) * HEAD_DIM, :])
    h1 = x_ref[...] + mixed
    h1_ref[...] = h1
    c = _rms(h1, gf_ref[...], NORM_EPS)
    c_ref[...] = c
    q = _mm(c.astype(BF16), wq_ref[...])
    for hp in range(2 * PEER_HEADS):
        q_scr[hp] = q[:, hp * PEER_HALF:(hp + 1) * PEER_HALF].astype(BF16)

    npad = -len(_STAIR) % 8

    def head(h, _):
        v1, i1 = _topk_rows(_nt(sk_ref[0], q_scr[2 * h]), PEER_TOPK)
        v2, i2 = _topk_rows(_nt(sk_ref[1], q_scr[2 * h + 1]), PEER_TOPK)
        cand = jnp.concatenate([v1[i:i + 1] + v2[j:j + 1] for i, j in _STAIR]
                               + [jnp.full((npad, ts), -jnp.inf, F32)], axis=0)
        eid = jnp.concatenate([i1[i:i + 1] * PEER_KEYS + i2[j:j + 1] for i, j in _STAIR]
                              + [jnp.zeros((npad, ts), jnp.int32)], axis=0)
        top, pos = _topk_rows(cand, PEER_TOPK)
        rows = lax.broadcasted_iota(jnp.int32, cand.shape, 0)
        sel = jnp.concatenate([jnp.sum(jnp.where(rows == pos[r:r + 1], eid, 0), axis=0, keepdims=True)
                               for r in range(PEER_TOPK)], axis=0)
        e = jnp.exp(top - top[0:1])
        gate_scr[h] = e / jnp.sum(e, axis=0, keepdims=True)
        idx_scr[h] = sel
        return 0

    lax.fori_loop(0, PEER_HEADS, head, 0)
    idx_ref[...] = idx_scr[...].reshape(PEER_SEL, ts).T
    gate_ref[...] = gate_scr[...].reshape(PEER_SEL, ts).T


def _post(x, dout, sout, w_out_b, g_ffn, w_query_b, sub_keys_b, b, seq):
    t = seq
    ts = min(256, seq)
    nst = seq // ts
    row = pl.BlockSpec((ts, D_MODEL), lambda i: (i, 0))
    hm = pl.BlockSpec((1, HEADS, ts, HEAD_DIM), lambda i: (0, 0, i, 0))
    sel = pl.BlockSpec((ts, PEER_SEL), lambda i: (i, 0))
    full = lambda *shape: pl.BlockSpec(shape, lambda i: (0,) * len(shape))
    return pl.pallas_call(
        _post_body,
        grid=(t // ts,),
        in_specs=[pl.BlockSpec((ts, D_MODEL), lambda i: (b * nst + i, 0)),
                  hm, hm, full(2 * MIX, D_MODEL), full(1, D_MODEL),
                  full(D_MODEL, 2 * PEER_HEADS * PEER_HALF), full(2, PEER_KEYS, PEER_HALF)],
        out_specs=[row, row, sel, sel],
        out_shape=[jax.ShapeDtypeStruct((t, D_MODEL), F32), jax.ShapeDtypeStruct((t, D_MODEL), F32),
                   jax.ShapeDtypeStruct((t, PEER_SEL), jnp.int32), jax.ShapeDtypeStruct((t, PEER_SEL), F32)],
        scratch_shapes=[pltpu.VMEM((2 * PEER_HEADS, ts, PEER_HALF), BF16),
                        pltpu.VMEM((PEER_HEADS, PEER_TOPK, ts), jnp.int32),
                        pltpu.VMEM((PEER_HEADS, PEER_TOPK, ts), F32)],
        compiler_params=_params("parallel"),
        name="post_peer_select",
    )(x, dout, sout, w_out_b, g_ffn, w_query_b, sub_keys_b)


def _coef_body(after_ref, gate_ref, dots_ref, o_ref):
    del after_ref
    d = dots_ref[...]
    o_ref[...] = gate_ref[...] * (0.5 * d * (1.0 + lax.erf(d * (2.0 ** -0.5))))


def _coef(gate, dots, after):
    t = gate.shape[0]
    ts = min(2048, t)
    blk = pl.BlockSpec((ts, PEER_SEL), lambda i: (i, 0))
    return pl.pallas_call(
        _coef_body, grid=(t // ts,), in_specs=[pl.BlockSpec(memory_space=pl.ANY), blk, blk], out_specs=blk,
        out_shape=jax.ShapeDtypeStruct((t, PEER_SEL), F32),
        compiler_params=_params("parallel"), name="peer_coef",
    )(after, gate, dots)


SC_CORES = 2
SC_SUBCORES = 16
SC_LANES = 16
SC_WORKERS = SC_CORES * SC_SUBCORES
SC_ROWS = 16
SC_BUFS = 5
SC_GROUP = 8
SC_CHUNKS = PEER_SEL // SC_ROWS
SC_STEPS = SC_GROUP * SC_CHUNKS
SC_VECS = D_MODEL // SC_LANES


def _sc_mesh():
    return plsc.VectorSubcoreMesh(core_axis_name="c", subcore_axis_name="s",
                                  num_cores=SC_CORES, num_subcores=SC_SUBCORES)


def _sc_walk(table_hbm, idx_hbm, aux_hbm, idx_v, aux_v, rows_v, sem, stage_sem, tpw, begin_group, compute, end_group):
    tok_base = (lax.axis_index("s") * SC_CORES + lax.axis_index("c")) * tpw
    ngroups = tpw // SC_GROUP
    nsteps = tpw * SC_CHUNKS

    def first_token(g):
        return pl.multiple_of(tok_base + g * SC_GROUP, SC_GROUP)

    def stage(g):
        tok0 = first_token(g)
        return (pltpu.make_async_copy(idx_hbm.at[pl.ds(tok0 * SC_CHUNKS, SC_STEPS)], idx_v.at[g % 2], stage_sem.at[0]),
                pltpu.make_async_copy(aux_hbm.at[pl.ds(tok0, SC_GROUP)],
                                      aux_v.at[g % 2, :, pl.ds(0, aux_hbm.shape[1])], stage_sem.at[1]))

    def gather(step):
        idx = idx_v.at[(step // SC_STEPS) % 2, step % SC_STEPS]
        return pltpu.make_async_copy(table_hbm.at[idx], rows_v.at[step % SC_BUFS], sem.at[step % SC_BUFS])

    for cp in stage(0):
        cp.start()
    for cp in stage(0):
        cp.wait()
    for step in range(SC_BUFS - 1):
        gather(step).start()

    def walk(step, _):
        g = step // SC_STEPS
        local = step % SC_STEPS

        @pl.when(jnp.logical_and(local == 0, g + 1 < ngroups))
        def _():
            for cp in stage(g + 1):
                cp.start()

        ahead = step + (SC_BUFS - 1)

        @pl.when(ahead < nsteps)
        def _():
            @pl.when(ahead % SC_STEPS == 0)
            def _():
                for cp in stage(ahead // SC_STEPS):
                    cp.wait()

            gather(ahead).start()

        @pl.when(local == 0)
        def _():
            begin_group()

        gather(step).wait()
        compute(rows_v.at[step % SC_BUFS], g % 2, local // SC_CHUNKS, local % SC_CHUNKS)

        @pl.when(local == SC_STEPS - 1)
        def _():
            end_group(first_token(g))

        return 0

    lax.fori_loop(0, nsteps, walk, 0)


def _sc_dots_body(u_hbm, idx_hbm, c_hbm, out_hbm, idx_v, c_v, rows_v, dots_v, sem, stage_sem, *, tpw):
    lane = lax.broadcasted_iota(jnp.int32, (SC_LANES,), 0)
    zero = jnp.zeros((SC_LANES,), F32)

    def compute(rows, slot, tt, ch):
        for half in range(SC_ROWS // SC_LANES):
            outv = zero
            for q in range(2):
                r0 = half * SC_LANES + q * 8

                def vec(kk, accs):
                    off = pl.multiple_of(kk * SC_LANES, SC_LANES)
                    cv = c_v[slot, tt, pl.ds(off, SC_LANES)]
                    return tuple(a + rows[r0 + r, pl.ds(off, SC_LANES)] * cv for r, a in enumerate(accs))

                accs = lax.fori_loop(0, SC_VECS, vec, (zero,) * 8)
                for r in range(8):
                    outv = jnp.where(lane == q * 8 + r, jnp.sum(accs[r]), outv)
            dots_v[tt, pl.ds(pl.multiple_of(ch * SC_ROWS + half * SC_LANES, SC_LANES), SC_LANES)] = outv

    def end_group(tok0):
        pltpu.sync_copy(dots_v.at[:, pl.ds(0, PEER_SEL)], out_hbm.at[pl.ds(tok0, SC_GROUP)])

    _sc_walk(u_hbm, idx_hbm, c_hbm, idx_v, c_v, rows_v, sem, stage_sem, tpw, lambda: None, compute, end_group)


def _sc_combine_body(v_hbm, idx_hbm, coef_hbm, out_hbm, idx_v, coef_v, rows_v, acc_v, sem, stage_sem, *, tpw):
    lane = lax.broadcasted_iota(jnp.int32, (SC_LANES,), 0)
    zero = jnp.zeros((SC_LANES,), F32)

    def compute(rows, slot, tt, ch):
        for half in range(SC_ROWS // SC_LANES):
            cf = coef_v[slot, tt, pl.ds(pl.multiple_of(ch * SC_ROWS + half * SC_LANES, SC_LANES), SC_LANES)]
            splat = [jnp.full((SC_LANES,), jnp.sum(jnp.where(lane == r, cf, 0.0)), F32) for r in range(SC_LANES)]

            @plsc.parallel_loop(0, SC_VECS, unroll=2)
            def _(kk):
                off = pl.multiple_of(kk * SC_LANES, SC_LANES)
                terms = [rows[half * SC_LANES + r, pl.ds(off, SC_LANES)] * splat[r] for r in range(SC_LANES)]
                while len(terms) > 1:
                    terms = [a + b for a, b in zip(terms[0::2], terms[1::2])]
                acc_v[tt, pl.ds(off, SC_LANES)] = acc_v[tt, pl.ds(off, SC_LANES)] + terms[0]

    def begin_group():
        def clear(i, _):
            acc_v[i // SC_VECS, pl.ds(pl.multiple_of((i % SC_VECS) * SC_LANES, SC_LANES), SC_LANES)] = zero
            return 0

        lax.fori_loop(0, SC_GROUP * SC_VECS, clear, 0)

    def end_group(tok0):
        pltpu.sync_copy(acc_v, out_hbm.at[pl.ds(tok0, SC_GROUP)])

    _sc_walk(v_hbm, idx_hbm, coef_hbm, idx_v, coef_v, rows_v, sem, stage_sem, tpw, begin_group, compute, end_group)


def _sc_call(body, table, idx, per_token, out_width, name):
    t = per_token.shape[0]
    tpw = t // SC_WORKERS
    assert tpw % SC_GROUP == 0
    return pl.kernel(
        functools.partial(body, tpw=tpw),
        out_type=jax.ShapeDtypeStruct((t, out_width), F32),
        mesh=_sc_mesh(),
        scratch_types=[pltpu.VMEM((2, SC_STEPS, SC_ROWS), jnp.int32),
                       pltpu.VMEM((2, SC_GROUP, per_token.shape[1]), F32),
                       pltpu.VMEM((SC_BUFS, SC_ROWS, D_MODEL), F32),
                       pltpu.VMEM((SC_GROUP, out_width), F32),
                       pltpu.SemaphoreType.DMA((SC_BUFS,)),
                       pltpu.SemaphoreType.DMA((2,))],
        compiler_params=pltpu.CompilerParams(needs_layout_passes=False),
        name=name,
    )(table, idx.reshape(t * SC_CHUNKS, SC_ROWS), per_token)


def _sc_step_body(v_hbm, idx_a_hbm, coef_hbm, u_hbm, idx_b_hbm, c_hbm, peer_hbm, dots_hbm,
                  idx_v, aux_v, rows_v, out_v, sem, stage_sem, *, tpw_a, tpw_b):
    _sc_combine_body(v_hbm, idx_a_hbm, coef_hbm, peer_hbm, idx_v, aux_v, rows_v, out_v, sem, stage_sem, tpw=tpw_a)
    _sc_dots_body(u_hbm, idx_b_hbm, c_hbm, dots_hbm, idx_v, aux_v, rows_v, out_v, sem, stage_sem, tpw=tpw_b)


def _sc_step(expert_v, idx_a, coef, expert_u, idx_b, c):
    ta, tb = coef.shape[0], c.shape[0]
    assert ta % (SC_WORKERS * SC_GROUP) == 0 and tb % (SC_WORKERS * SC_GROUP) == 0
    return pl.kernel(
        functools.partial(_sc_step_body, tpw_a=ta // SC_WORKERS, tpw_b=tb // SC_WORKERS),
        out_type=[jax.ShapeDtypeStruct((ta, D_MODEL), F32), jax.ShapeDtypeStruct((tb, PEER_SEL), F32)],
        mesh=_sc_mesh(),
        scratch_types=[pltpu.VMEM((2, SC_STEPS, SC_ROWS), jnp.int32),
                       pltpu.VMEM((2, SC_GROUP, D_MODEL), F32),
                       pltpu.VMEM((SC_BUFS, SC_ROWS, D_MODEL), F32),
                       pltpu.VMEM((SC_GROUP, D_MODEL), F32),
                       pltpu.SemaphoreType.DMA((SC_BUFS,)),
                       pltpu.SemaphoreType.DMA((2,))],
        compiler_params=pltpu.CompilerParams(needs_layout_passes=False),
        name="peer_step",
    )(expert_v, idx_a.reshape(ta * SC_CHUNKS, SC_ROWS), coef, expert_u, idx_b.reshape(tb * SC_CHUNKS, SC_ROWS), c)


def _sc_dots(expert_u, idx, c):
    return _sc_call(_sc_dots_body, expert_u, idx, c, PEER_SEL, "peer_dots")


def _sc_combine(expert_v, idx, coef):
    return _sc_call(_sc_combine_body, expert_v, idx, coef, D_MODEL, "peer_combine")


def _ple_body(h1_ref, peer_ref, p_ref, gp_ref, wg_ref, we_ref, gfin_ref, y_ref):
    h = h1_ref[...] + peer_ref[...]
    a = _rms(h, gp_ref[...], NORM_EPS).astype(BF16)
    gate = jax.nn.sigmoid(_mm(a, wg_ref[...]))
    h = h + _mm(p_ref[...].astype(BF16), we_ref[...]) * gate
    y_ref[...] = _rms(h, gfin_ref[...], NORM_EPS)


def _ple(h1, peer, p, g_ple, w_pgate_b, w_ple_b, g_final, b):
    t = h1.shape[0]
    ts = min(512, t)
    nst = t // ts
    row = pl.BlockSpec((ts, D_MODEL), lambda i: (i, 0))
    full = lambda *shape: pl.BlockSpec(shape, lambda i: (0,) * len(shape))
    return pl.pallas_call(
        _ple_body,
        grid=(t // ts,),
        in_specs=[row, row, pl.BlockSpec((ts, PLE_DIM), lambda i: (b * nst + i, 0)), full(1, D_MODEL),
                  full(D_MODEL, D_MODEL), full(PLE_DIM, D_MODEL), full(1, D_MODEL)],
        out_specs=row,
        out_shape=jax.ShapeDtypeStruct((t, D_MODEL), F32),
        compiler_params=_params("parallel"),
        name="ple_final",
    )(h1, peer, p, g_ple, w_pgate_b, w_ple_b, g_final)


def _rope_tables(pos):
    half = DIFF_COMP // 2
    inv = ROPE_THETA ** (-jnp.arange(0, DIFF_COMP, 2, dtype=F32) / DIFF_COMP)
    ang = pos.astype(F32)[:, None] * inv[None, :]
    cos = jnp.cos(ang)
    sin = jnp.sin(ang)
    reps = LANES // DIFF_COMP
    del half
    return (jnp.tile(jnp.concatenate([cos, cos], axis=-1), (1, reps)),
            jnp.tile(jnp.concatenate([-sin, sin], axis=-1), (1, reps)))


def kernel(x_prompt, x_sample, cache_diff_k, cache_diff_v, cache_sb_k, cache_sb_v, p_prompt, p_sample, g_mix, w_in, lambda_q1, lambda_k1, lambda_q2, lambda_k2, g_subln, w_out, g_ffn, w_query, sub_keys, expert_u, expert_v, g_ple, w_pgate, w_ple, g_final):
    assert w_in.shape[0] == 1, "single-layer encoder"
    nb, seq, _ = x_prompt.shape
    db, dq, _ = x_sample.shape
    past = cache_diff_k.shape[2]

    lam = (jnp.exp(jnp.sum(lambda_q1[0].astype(F32) * lambda_k1[0].astype(F32)))
           - jnp.exp(jnp.sum(lambda_q2[0].astype(F32) * lambda_k2[0].astype(F32))) + LAM_INIT).reshape(1)
    w_in_b = w_in[0].astype(BF16)
    w_out_b = w_out[0].astype(BF16)
    w_query_b = w_query[0].astype(BF16)
    sub_keys_b = sub_keys[0].astype(BF16)
    w_pgate_b = w_pgate[0].astype(BF16)
    w_ple_b = w_ple[0].astype(BF16)
    g_sub = g_subln[0].reshape(1, HEAD_DIM)
    g_fin = g_final.reshape(1, D_MODEL)

    def select(x, dout, sout, b, s):
        return _post(x, dout, sout, w_out_b, g_ffn, w_query_b, sub_keys_b, b, s)

    def finish(h1, peer, p, b):
        return _ple(h1, peer, p, g_ple, w_pgate_b, w_ple_b, g_fin, b)

    ts = db * dq
    xs = x_sample.reshape(ts, D_MODEL)
    cos_s, sin_s = _rope_tables(jnp.tile(past + jnp.arange(dq, dtype=jnp.int32), db))
    kd, vd, ks, vs, qd2, kdb, vdb, qsb, ksb, vsb = _proj(xs, g_mix, w_in_b, cos_s, sin_s, 1, ts)
    caches = [c[0].reshape(db, past, MIX) for c in (cache_diff_k, cache_diff_v, cache_sb_k, cache_sb_v)]
    dout_s, sout_s = _sample_attention(lam, qd2, kdb, vdb, qsb, ksb, vsb, *caches, g_sub, db, dq)
    rows_s = tuple(r.reshape(1, db, dq, HEADS, HEAD_DIM) for r in (kd, vd, ks, vs))

    xp = x_prompt.reshape(nb * seq, D_MODEL)
    pp = p_prompt[0].reshape(nb * seq, PLE_DIM)
    cos_p, sin_p = _rope_tables(jnp.arange(seq, dtype=jnp.int32))
    kd, vd, ks, vs, qd2, kdb, vdb, qsb, ksb, vsb = _proj(xp, g_mix, w_in_b, cos_p, sin_p, nb, seq)
    rows_p = tuple(r.reshape(1, nb, seq, HEADS, HEAD_DIM) for r in (kd, vd, ks, vs))

    def prompt_block(b):
        dout = _diff_attention(lam, qd2, kdb, vdb, g_sub, b)
        sout = _sb_attention(qsb, ksb, vsb, b, lam)
        return select(xp, dout, sout, b, seq)

    blocks = [functools.partial(prompt_block, b) for b in range(nb)]
    blocks.append(lambda: select(xs, dout_s, sout_s, 0, ts))
    ys = []
    h1, c, idx, gate = blocks[0]()
    dots = _sc_dots(expert_u[0], idx, c)
    for k in range(1, len(blocks)):
        nh1, nc, nidx, ngate = blocks[k]()
        coef = _coef(gate, dots, nh1)
        peer, ndots = _sc_step(expert_v[0], idx, coef, expert_u[0], nidx, nc)
        ys.append(finish(h1, peer, pp, k - 1))
        h1, idx, gate, dots = nh1, nidx, ngate, ndots
    coef = _coef(gate, dots, lam)
    y_sample = finish(h1, _sc_combine(expert_v[0], idx, coef), p_sample[0].reshape(ts, PLE_DIM), 0)
    y_sample = y_sample.reshape(db, dq, D_MODEL)
    y_prompt = jnp.stack(ys)

    return (y_prompt, y_sample) + rows_p + rows_s
```

```python
import functools
import math

import jax
import jax.numpy as jnp
from jax import lax
from jax.experimental import pallas as pl
from jax.experimental.pallas import tpu as pltpu
from jax.experimental.pallas import tpu_sc as plsc

F32 = jnp.float32
BF16 = jnp.bfloat16

D_MODEL = 1024
HEADS = 8
HEAD_DIM = 64
DIFF_COMP = 32
MIX = HEADS * HEAD_DIM
CHUNK = 64
ROPE_THETA = 10000.0
NORM_EPS = 1e-6
SUBLN_EPS = 1e-5
PEER_HEADS = 8
PEER_KEYS = 128
PEER_TOPK = 16
PEER_HALF = 128
PEER_SEL = PEER_HEADS * PEER_TOPK
PLE_DIM = 256
LAM_INIT = 0.8 - 0.6 * math.exp(-0.3 * 0)
SB_LOG_FLOOR = -104.0
PROMPT_FIRST_ROW_BLOCKS = 4

LANES = 128
VMEM_LIMIT = 48 * 1024 * 1024

NT_DIMS = (((1,), (1,)), ((), ()))


def _nt(a, b):
    return lax.dot_general(a, b, NT_DIMS, preferred_element_type=F32)


def _mm(a, b):
    return jnp.dot(a, b, preferred_element_type=F32)


def _rms(x, g, eps):
    return x * lax.rsqrt(jnp.mean(x * x, axis=-1, keepdims=True) + eps) * g


def _params(*sem):
    return pltpu.CompilerParams(dimension_semantics=sem, vmem_limit_bytes=VMEM_LIMIT)


def _with_ones(v):
    n = v.shape[0]
    ones = (lax.broadcasted_iota(jnp.int32, (n, HEAD_DIM), 1) == 0).astype(v.dtype)
    return jnp.concatenate([v, ones], axis=1)


def _proj_body(x_ref, g_ref, w_ref, cos_ref, sin_ref,
               kd_ref, vd_ref, ks_ref, vs_ref,
               qd2_ref, kdb_ref, vdb_ref, qsb_ref, ksb_ref, vsb_ref):
    ts = x_ref.shape[0]
    a = _rms(x_ref[...], g_ref[...], NORM_EPS).astype(BF16)
    cos = jnp.tile(cos_ref[...], (1, MIX // LANES))
    sin = jnp.tile(sin_ref[...], (1, MIX // LANES))
    lane = lax.broadcasted_iota(jnp.int32, (ts, MIX), 1)
    first_half = (lane % DIFF_COMP) < (DIFF_COMP // 2)

    def group(i):
        return _mm(a, w_ref[:, i * MIX:(i + 1) * MIX])

    def rope(t):
        partner = jnp.where(first_half,
                            pltpu.roll(t, MIX - DIFF_COMP // 2, 1),
                            pltpu.roll(t, DIFF_COMP // 2, 1))
        return t * cos + partner * sin

    qd = rope(group(0)) * (DIFF_COMP ** -0.5)
    kd = rope(group(1))
    vd = group(2)
    qs = group(3) * (HEAD_DIM ** -0.5)
    ks = group(4)
    vs = group(5)
    kd_ref[...] = kd
    vd_ref[...] = vd
    ks_ref[...] = ks
    vs_ref[...] = vs
    comp0 = lax.broadcasted_iota(jnp.int32, (ts, HEAD_DIM), 1) < DIFF_COMP
    for h in range(HEADS):
        sl = slice(h * HEAD_DIM, (h + 1) * HEAD_DIM)
        qh = qd[:, sl]
        qd2_ref[0, h, 0] = jnp.where(comp0, qh, 0.0).astype(BF16)
        qd2_ref[0, h, 1] = jnp.where(comp0, 0.0, qh).astype(BF16)
        kdb_ref[0, h] = kd[:, sl].astype(BF16)
        vdb_ref[0, h] = _with_ones(vd[:, sl].astype(BF16))
        qsb_ref[0, h] = qs[:, sl].astype(BF16)
        ksb_ref[0, h] = ks[:, sl].astype(BF16)
        vsb_ref[0, h] = vs[:, sl].astype(BF16)


def _proj(x, g_mix, w_in_b, cos_t, sin_t, nb, seq):
    t = nb * seq
    ts = min(256, seq)
    nst = seq // ts
    row = pl.BlockSpec((ts, MIX), lambda i: (i, 0))
    hm = pl.BlockSpec((1, HEADS, ts, HEAD_DIM), lambda i: (i // nst, 0, i % nst, 0))
    hm2 = pl.BlockSpec((1, HEADS, 2, ts, HEAD_DIM), lambda i: (i // nst, 0, 0, i % nst, 0))
    rows = jax.ShapeDtypeStruct((t, MIX), F32)
    heads = jax.ShapeDtypeStruct((nb, HEADS, seq, HEAD_DIM), BF16)
    heads2 = jax.ShapeDtypeStruct((nb, HEADS, 2, seq, HEAD_DIM), BF16)
    hm_ext = pl.BlockSpec((1, HEADS, ts, 2 * HEAD_DIM), lambda i: (i // nst, 0, i % nst, 0))
    heads_ext = jax.ShapeDtypeStruct((nb, HEADS, seq, 2 * HEAD_DIM), BF16)
    return pl.pallas_call(
        _proj_body,
        grid=(t // ts,),
        in_specs=[
            pl.BlockSpec((ts, D_MODEL), lambda i: (i, 0)),
            pl.BlockSpec((1, D_MODEL), lambda i: (0, 0)),
            pl.BlockSpec((D_MODEL, 6 * MIX), lambda i: (0, 0)),
            pl.BlockSpec((ts, LANES), lambda i: (i % nst, 0)),
            pl.BlockSpec((ts, LANES), lambda i: (i % nst, 0)),
        ],
        out_specs=[row, row, row, row, hm2, hm, hm_ext, hm, hm, hm],
        out_shape=[rows, rows, rows, rows, heads2, heads, heads_ext, heads, heads, heads],
        compiler_params=_params("parallel"),
        name="proj",
    )(x, g_mix, w_in_b, cos_t, sin_t)


def _diff_init(rows):
    return jnp.full((rows, 1), -jnp.inf, F32), jnp.zeros((rows, 2 * HEAD_DIM), F32)


def _diff_update(s, v_ext, carry):
    m, acc = carry
    m_new = jnp.maximum(m, jnp.max(s, axis=-1, keepdims=True))
    p = jnp.exp(s - m_new)
    acc = jnp.exp(m - m_new) * acc + _mm(p.astype(BF16), v_ext)
    return m_new, acc


def _diff_finish(carry, lam, g_subln, tq):
    _, acc = carry
    o = acc[:, :HEAD_DIM] / acc[:, HEAD_DIM:HEAD_DIM + 1]
    d = o[:tq] - lam * o[tq:]
    return _rms(d, g_subln, SUBLN_EPS) * (1.0 - LAM_INIT)


def _suffix_sums(lk, tri):
    hi = lk.astype(BF16)
    lo = (lk - hi.astype(F32)).astype(BF16)
    return _mm(hi, tri) + _mm(lo, tri)


def _sb_update(q, k, v, tri, earlier, carry):
    run, acc = carry
    z = _nt(q, k)
    sp = jnp.maximum(z, 0.0) + jnp.log1p(jnp.exp(-jnp.abs(z)))
    lk = -sp if earlier is None else jnp.where(earlier, -sp, 0.0)
    after = _suffix_sums(lk, tri)
    w = jnp.exp((z - sp) + after + run)
    if earlier is not None:
        w = jnp.where(earlier, w, 0.0)
    acc = acc + _mm(w.astype(BF16), v)
    run = run + after[:, 0:1] + lk[:, 0:1]
    return run, acc


def _tri(n):
    j = lax.broadcasted_iota(jnp.int32, (n, n), 0)
    s = lax.broadcasted_iota(jnp.int32, (n, n), 1)
    return (j > s).astype(BF16)


def _diff_body(lam_ref, q_ref, k_ref, v_ref, g_ref, o_ref, *, tq, q0):
    qi = q0 + pl.program_id(1)
    q2 = q_ref[0, 0].reshape(2 * tq, HEAD_DIM)

    def scores(j):
        return _nt(q2, k_ref[0, 0, pl.ds(pl.multiple_of(j * tq, tq), tq), :])

    def step(j, state):
        s, carry = state
        s_next = scores(j + 1)
        return s_next, _diff_update(s, v_ref[0, 0, pl.ds(pl.multiple_of(j * tq, tq), tq), :], carry)

    s, carry = lax.fori_loop(0, qi, step, (scores(0), _diff_init(2 * tq)))
    r = lax.broadcasted_iota(jnp.int32, (2 * tq, tq), 0) % tq
    c = lax.broadcasted_iota(jnp.int32, (2 * tq, tq), 1)
    s = jnp.where((c // CHUNK) <= (r // CHUNK), s, -jnp.inf)
    carry = _diff_update(s, v_ref[0, 0, pl.ds(pl.multiple_of(qi * tq, tq), tq), :], carry)
    o_ref[0, 0] = _diff_finish(carry, lam_ref[0], g_ref[...], tq)


def _diff_attention(lam, qd2, kdb, vdb, g_subln, b, t0, n):
    seq = qd2.shape[3]
    tq = min(512, n)
    q0 = t0 // tq
    return pl.pallas_call(
        functools.partial(_diff_body, tq=tq, q0=q0),
        grid=(HEADS, n // tq),
        in_specs=[
            pl.BlockSpec(memory_space=pltpu.SMEM),
            pl.BlockSpec((1, 1, 2, tq, HEAD_DIM), lambda h, i: (b, h, 0, q0 + i, 0)),
            pl.BlockSpec((1, 1, seq, HEAD_DIM), lambda h, i: (b, h, 0, 0)),
            pl.BlockSpec((1, 1, seq, 2 * HEAD_DIM), lambda h, i: (b, h, 0, 0)),
            pl.BlockSpec((1, HEAD_DIM), lambda h, i: (0, 0)),
        ],
        out_specs=pl.BlockSpec((1, 1, tq, HEAD_DIM), lambda h, i: (0, h, i, 0)),
        out_shape=jax.ShapeDtypeStruct((1, HEADS, n, HEAD_DIM), F32),
        compiler_params=_params("parallel", "arbitrary"),
        name="diff_attention",
    )(lam, qd2, kdb, vdb, g_subln)


def _sb_body(after_ref, q_ref, k_ref, v_ref, o_ref, *, tq, q0):
    del after_ref
    qi = q0 + pl.program_id(1)
    q = q_ref[0, 0]
    tri = _tri(tq)

    def tile(j):
        start = pl.multiple_of(j * tq, tq)
        return k_ref[0, 0, pl.ds(start, tq), :], v_ref[0, 0, pl.ds(start, tq), :]

    r = lax.broadcasted_iota(jnp.int32, (tq, tq), 0)
    c = lax.broadcasted_iota(jnp.int32, (tq, tq), 1)
    carry = (jnp.zeros((tq, 1), F32), jnp.zeros((tq, HEAD_DIM), F32))
    run, acc = _sb_update(q, *tile(qi), tri, c < r, carry)

    def live(state):
        j, run, _ = state
        return jnp.logical_and(j >= 0, jnp.max(run) > SB_LOG_FLOOR)

    def step(state):
        j, run, acc = state
        run, acc = _sb_update(q, *tile(j), tri, None, (run, acc))
        return j - 1, run, acc

    o_ref[0, 0] = lax.while_loop(live, step, (qi - 1, run, acc))[2]


def _sb_attention(qsb, ksb, vsb, b, t0, n, after):
    seq = qsb.shape[2]
    tq = min(256, n)
    q0 = t0 // tq
    kv = pl.BlockSpec((1, 1, seq, HEAD_DIM), lambda h, i: (b, h, 0, 0))
    return pl.pallas_call(
        functools.partial(_sb_body, tq=tq, q0=q0),
        grid=(HEADS, n // tq),
        in_specs=[pl.BlockSpec(memory_space=pl.ANY),
                  pl.BlockSpec((1, 1, tq, HEAD_DIM), lambda h, i: (b, h, q0 + i, 0)), kv, kv],
        out_specs=pl.BlockSpec((1, 1, tq, HEAD_DIM), lambda h, i: (0, h, i, 0)),
        out_shape=jax.ShapeDtypeStruct((1, HEADS, n, HEAD_DIM), F32),
        compiler_params=_params("parallel", "arbitrary"),
        name="sb_attention",
    )(after, qsb, ksb, vsb)


SAMPLE_HEADS = 4
SAMPLE_TILE = 256


def _sample_body(lam_ref, qd2_ref, kdn_ref, vdn_ref, qs_ref, ksn_ref, vsn_ref,
                 ckd_ref, cvd_ref, cks_ref, cvs_ref, g_ref, do_ref, so_ref, *, past, nq):
    lam = lam_ref[0]
    tri_c = _tri(SAMPLE_TILE)
    tri_n = _tri(nq)
    i2 = lax.broadcasted_iota(jnp.int32, (2 * nq, nq), 0) % nq
    j2 = lax.broadcasted_iota(jnp.int32, (2 * nq, nq), 1)
    visible_new = ((past + j2) // CHUNK) <= ((past + i2) // CHUNK)
    i1 = lax.broadcasted_iota(jnp.int32, (nq, nq), 0)
    j1 = lax.broadcasted_iota(jnp.int32, (nq, nq), 1)
    earlier_new = j1 < i1
    for h in range(SAMPLE_HEADS):
        sl = slice(h * HEAD_DIM, (h + 1) * HEAD_DIM)
        q2 = qd2_ref[0, h].reshape(2 * nq, HEAD_DIM)
        carry = _diff_update(_nt(q2, ckd_ref[0, :, sl].astype(BF16)),
                             _with_ones(cvd_ref[0, :, sl].astype(BF16)), _diff_init(2 * nq))
        s_new = jnp.where(visible_new, _nt(q2, kdn_ref[0, h]), -jnp.inf)
        carry = _diff_update(s_new, vdn_ref[0, h], carry)
        do_ref[0, h] = _diff_finish(carry, lam, g_ref[...], nq)
        q = qs_ref[0, h]
        carry = (jnp.zeros((nq, 1), F32), jnp.zeros((nq, HEAD_DIM), F32))
        carry = _sb_update(q, ksn_ref[0, h], vsn_ref[0, h], tri_n, earlier_new, carry)
        for t in reversed(range(past // SAMPLE_TILE)):
            rows = slice(t * SAMPLE_TILE, (t + 1) * SAMPLE_TILE)
            carry = _sb_update(q, cks_ref[0, rows, sl].astype(BF16), cvs_ref[0, rows, sl].astype(BF16),
                               tri_c, None, carry)
        so_ref[0, h] = carry[1]


def _sample_attention(lam, qd2, kdb, vdb, qsb, ksb, vsb, ckd, cvd, cks, cvs, g_subln, nb, nq):
    past = ckd.shape[1]
    nhg = HEADS // SAMPLE_HEADS
    hm = pl.BlockSpec((1, SAMPLE_HEADS, nq, HEAD_DIM), lambda b, g: (0, g, b, 0))
    hm2 = pl.BlockSpec((1, SAMPLE_HEADS, 2, nq, HEAD_DIM), lambda b, g: (0, g, 0, b, 0))
    cache = pl.BlockSpec((1, past, SAMPLE_HEADS * HEAD_DIM), lambda b, g: (b, 0, g))
    out = jax.ShapeDtypeStruct((1, HEADS, nb * nq, HEAD_DIM), F32)
    return pl.pallas_call(
        functools.partial(_sample_body, past=past, nq=nq),
        grid=(nb, nhg),
        in_specs=[pl.BlockSpec(memory_space=pltpu.SMEM), hm2, hm,
                  pl.BlockSpec((1, SAMPLE_HEADS, nq, 2 * HEAD_DIM), lambda b, g: (0, g, b, 0)), hm, hm, hm,
                  cache, cache, cache, cache,
                  pl.BlockSpec((1, HEAD_DIM), lambda b, g: (0, 0))],
        out_specs=[hm, hm],
        out_shape=[out, out],
        compiler_params=_params("parallel", "parallel"),
        name="sample_attention",
    )(lam, qd2, kdb, vdb, qsb, ksb, vsb, ckd, cvd, cks, cvs, g_subln)


def _topk_rows(s, k):
    n = s.shape[0]
    rows = lax.broadcasted_iota(jnp.int32, s.shape, 0)
    vals, ids = [], []
    for _ in range(k):
        m = jnp.max(s, axis=0, keepdims=True)
        i = jnp.min(jnp.where(s == m, rows, n), axis=0, keepdims=True)
        vals.append(m)
        ids.append(i)
        s = jnp.where(rows == i, -jnp.inf, s)
    return jnp.concatenate(vals, axis=0), jnp.concatenate(ids, axis=0)


_STAIR = [(i, j) for i in range(PEER_TOPK) for j in range(PEER_TOPK) if (i + 1) * (j + 1) <= PEER_TOPK]


def _post_body(x_ref, do_ref, so_ref, wo_ref, gf_ref, wq_ref, sk_ref,
               h1_ref, c_ref, idx_ref, gate_ref, q_scr, idx_scr, gate_scr):
    ts = x_ref.shape[0]
    mixed = jnp.zeros((ts, D_MODEL), F32)
    for h in range(HEADS):
        mixed += _mm(do_ref[0, h].astype(BF16), wo_ref[h * HEAD_DIM:(h + 1) * HEAD_DIM, :])
        mixed += _mm(so_ref[0, h].astype(BF16), wo_ref[MIX + h * HEAD_DIM:MIX + (h + 1) * HEAD_DIM, :])
    h1 = x_ref[...] + mixed
    h1_ref[...] = h1
    c = _rms(h1, gf_ref[...], NORM_EPS)
    c_ref[...] = c
    q = _mm(c.astype(BF16), wq_ref[...])
    for hp in range(2 * PEER_HEADS):
        q_scr[hp] = q[:, hp * PEER_HALF:(hp + 1) * PEER_HALF].astype(BF16)

    npad = -len(_STAIR) % 8

    def head(h, _):
        v1, i1 = _topk_rows(_nt(sk_ref[0], q_scr[2 * h]), PEER_TOPK)
        v2, i2 = _topk_rows(_nt(sk_ref[1], q_scr[2 * h + 1]), PEER_TOPK)
        cand = jnp.concatenate([v1[i:i + 1] + v2[j:j + 1] for i, j in _STAIR]
                               + [jnp.full((npad, ts), -jnp.inf, F32)], axis=0)
        eid = jnp.concatenate([i1[i:i + 1] * PEER_KEYS + i2[j:j + 1] for i, j in _STAIR]
                              + [jnp.zeros((npad, ts), jnp.int32)], axis=0)
        top, pos = _topk_rows(cand, PEER_TOPK)
        rows = lax.broadcasted_iota(jnp.int32, cand.shape, 0)
        sel = jnp.concatenate([jnp.sum(jnp.where(rows == pos[r:r + 1], eid, 0), axis=0, keepdims=True)
                               for r in range(PEER_TOPK)], axis=0)
        e = jnp.exp(top - top[0:1])
        gate_scr[h] = e / jnp.sum(e, axis=0, keepdims=True)
        idx_scr[h] = sel
        return 0

    lax.fori_loop(0, PEER_HEADS, head, 0)
    idx_ref[...] = idx_scr[...].reshape(PEER_SEL, ts).T
    gate_ref[...] = gate_scr[...].reshape(PEER_SEL, ts).T


def _post(x, dout, sout, w_out_b, g_ffn, w_query_b, sub_keys_b, tok0, t):
    ts = min(256, t)
    first = tok0 // ts
    row = pl.BlockSpec((ts, D_MODEL), lambda i: (i, 0))
    hm = pl.BlockSpec((1, HEADS, ts, HEAD_DIM), lambda i: (0, 0, i, 0))
    sel = pl.BlockSpec((ts, PEER_SEL), lambda i: (i, 0))
    full = lambda *shape: pl.BlockSpec(shape, lambda i: (0,) * len(shape))
    return pl.pallas_call(
        _post_body,
        grid=(t // ts,),
        in_specs=[pl.BlockSpec((ts, D_MODEL), lambda i: (first + i, 0)),
                  hm, hm, full(2 * MIX, D_MODEL), full(1, D_MODEL),
                  full(D_MODEL, 2 * PEER_HEADS * PEER_HALF), full(2, PEER_KEYS, PEER_HALF)],
        out_specs=[row, row, sel, sel],
        out_shape=[jax.ShapeDtypeStruct((t, D_MODEL), F32), jax.ShapeDtypeStruct((t, D_MODEL), F32),
                   jax.ShapeDtypeStruct((t, PEER_SEL), jnp.int32), jax.ShapeDtypeStruct((t, PEER_SEL), F32)],
        scratch_shapes=[pltpu.VMEM((2 * PEER_HEADS, ts, PEER_HALF), BF16),
                        pltpu.VMEM((PEER_HEADS, PEER_TOPK, ts), jnp.int32),
                        pltpu.VMEM((PEER_HEADS, PEER_TOPK, ts), F32)],
        compiler_params=_params("parallel"),
        name="post_peer_select",
    )(x, dout, sout, w_out_b, g_ffn, w_query_b, sub_keys_b)


def _coef_body(after_ref, gate_ref, dots_ref, o_ref):
    del after_ref
    d = dots_ref[...]
    o_ref[...] = gate_ref[...] * (0.5 * d * (1.0 + lax.erf(d * (2.0 ** -0.5))))


def _coef(gate, dots, after):
    t = gate.shape[0]
    ts = min(2048, t)
    blk = pl.BlockSpec((ts, PEER_SEL), lambda i: (i, 0))
    return pl.pallas_call(
        _coef_body, grid=(t // ts,), in_specs=[pl.BlockSpec(memory_space=pl.ANY), blk, blk], out_specs=blk,
        out_shape=jax.ShapeDtypeStruct((t, PEER_SEL), F32),
        compiler_params=_params("parallel"), name="peer_coef",
    )(after, gate, dots)


SC_CORES = 2
SC_SUBCORES = 16
SC_LANES = 16
SC_WORKERS = SC_CORES * SC_SUBCORES
SC_ROWS = 16
SC_BUFS = 5
SC_GROUP = 8
SC_CHUNKS = PEER_SEL // SC_ROWS
SC_STEPS = SC_GROUP * SC_CHUNKS
SC_VECS = D_MODEL // SC_LANES


def _sc_mesh():
    return plsc.VectorSubcoreMesh(core_axis_name="c", subcore_axis_name="s",
                                  num_cores=SC_CORES, num_subcores=SC_SUBCORES)


def _sc_walk(table_hbm, idx_hbm, aux_hbm, idx_v, aux_v, rows_v, sem, stage_sem, tpw, begin_group, compute, end_group):
    tok_base = (lax.axis_index("s") * SC_CORES + lax.axis_index("c")) * tpw
    ngroups = tpw // SC_GROUP
    nsteps = tpw * SC_CHUNKS

    def first_token(g):
        return pl.multiple_of(tok_base + g * SC_GROUP, SC_GROUP)

    def stage(g):
        tok0 = first_token(g)
        return (pltpu.make_async_copy(idx_hbm.at[pl.ds(tok0 * SC_CHUNKS, SC_STEPS)], idx_v.at[g % 2], stage_sem.at[0]),
                pltpu.make_async_copy(aux_hbm.at[pl.ds(tok0, SC_GROUP)],
                                      aux_v.at[g % 2, :, pl.ds(0, aux_hbm.shape[1])], stage_sem.at[1]))

    def gather(step):
        idx = idx_v.at[(step // SC_STEPS) % 2, step % SC_STEPS]
        return pltpu.make_async_copy(table_hbm.at[idx], rows_v.at[step % SC_BUFS], sem.at[step % SC_BUFS])

    for cp in stage(0):
        cp.start()
    for cp in stage(0):
        cp.wait()
    for step in range(SC_BUFS - 1):
        gather(step).start()

    def walk(step, _):
        g = step // SC_STEPS
        local = step % SC_STEPS

        @pl.when(jnp.logical_and(local == 0, g + 1 < ngroups))
        def _():
            for cp in stage(g + 1):
                cp.start()

        ahead = step + (SC_BUFS - 1)

        @pl.when(ahead < nsteps)
        def _():
            @pl.when(ahead % SC_STEPS == 0)
            def _():
                for cp in stage(ahead // SC_STEPS):
                    cp.wait()

            gather(ahead).start()

        @pl.when(local == 0)
        def _():
            begin_group()

        gather(step).wait()
        compute(rows_v.at[step % SC_BUFS], g % 2, local // SC_CHUNKS, local % SC_CHUNKS)

        @pl.when(local == SC_STEPS - 1)
        def _():
            end_group(first_token(g))

        return 0

    lax.fori_loop(0, nsteps, walk, 0)


def _sc_dots_body(u_hbm, idx_hbm, c_hbm, out_hbm, idx_v, c_v, rows_v, dots_v, sem, stage_sem, *, tpw):
    lane = lax.broadcasted_iota(jnp.int32, (SC_LANES,), 0)
    zero = jnp.zeros((SC_LANES,), F32)

    def compute(rows, slot, tt, ch):
        for half in range(SC_ROWS // SC_LANES):
            outv = zero
            for q in range(2):
                r0 = half * SC_LANES + q * 8

                def vec(kk, accs):
                    off = pl.multiple_of(kk * SC_LANES, SC_LANES)
                    cv = c_v[slot, tt, pl.ds(off, SC_LANES)]
                    return tuple(a + rows[r0 + r, pl.ds(off, SC_LANES)] * cv for r, a in enumerate(accs))

                accs = lax.fori_loop(0, SC_VECS, vec, (zero,) * 8)
                for r in range(8):
                    outv = jnp.where(lane == q * 8 + r, jnp.sum(accs[r]), outv)
            dots_v[tt, pl.ds(pl.multiple_of(ch * SC_ROWS + half * SC_LANES, SC_LANES), SC_LANES)] = outv

    def end_group(tok0):
        pltpu.sync_copy(dots_v.at[:, pl.ds(0, PEER_SEL)], out_hbm.at[pl.ds(tok0, SC_GROUP)])

    _sc_walk(u_hbm, idx_hbm, c_hbm, idx_v, c_v, rows_v, sem, stage_sem, tpw, lambda: None, compute, end_group)


def _sc_combine_body(v_hbm, idx_hbm, coef_hbm, out_hbm, idx_v, coef_v, rows_v, acc_v, sem, stage_sem, *, tpw):
    lane = lax.broadcasted_iota(jnp.int32, (SC_LANES,), 0)
    zero = jnp.zeros((SC_LANES,), F32)

    def compute(rows, slot, tt, ch):
        for half in range(SC_ROWS // SC_LANES):
            cf = coef_v[slot, tt, pl.ds(pl.multiple_of(ch * SC_ROWS + half * SC_LANES, SC_LANES), SC_LANES)]
            splat = [jnp.full((SC_LANES,), jnp.sum(jnp.where(lane == r, cf, 0.0)), F32) for r in range(SC_LANES)]

            @plsc.parallel_loop(0, SC_VECS, unroll=2)
            def _(kk):
                off = pl.multiple_of(kk * SC_LANES, SC_LANES)
                terms = [rows[half * SC_LANES + r, pl.ds(off, SC_LANES)] * splat[r] for r in range(SC_LANES)]
                while len(terms) > 1:
                    terms = [a + b for a, b in zip(terms[0::2], terms[1::2])]
                acc_v[tt, pl.ds(off, SC_LANES)] = acc_v[tt, pl.ds(off, SC_LANES)] + terms[0]

    def begin_group():
        def clear(i, _):
            acc_v[i // SC_VECS, pl.ds(pl.multiple_of((i % SC_VECS) * SC_LANES, SC_LANES), SC_LANES)] = zero
            return 0

        lax.fori_loop(0, SC_GROUP * SC_VECS, clear, 0)

    def end_group(tok0):
        pltpu.sync_copy(acc_v, out_hbm.at[pl.ds(tok0, SC_GROUP)])

    _sc_walk(v_hbm, idx_hbm, coef_hbm, idx_v, coef_v, rows_v, sem, stage_sem, tpw, begin_group, compute, end_group)


def _sc_call(body, table, idx, per_token, out_width, name):
    t = per_token.shape[0]
    tpw = t // SC_WORKERS
    assert tpw % SC_GROUP == 0
    return pl.kernel(
        functools.partial(body, tpw=tpw),
        out_type=jax.ShapeDtypeStruct((t, out_width), F32),
        mesh=_sc_mesh(),
        scratch_types=[pltpu.VMEM((2, SC_STEPS, SC_ROWS), jnp.int32),
                       pltpu.VMEM((2, SC_GROUP, per_token.shape[1]), F32),
                       pltpu.VMEM((SC_BUFS, SC_ROWS, D_MODEL), F32),
                       pltpu.VMEM((SC_GROUP, out_width), F32),
                       pltpu.SemaphoreType.DMA((SC_BUFS,)),
                       pltpu.SemaphoreType.DMA((2,))],
        compiler_params=pltpu.CompilerParams(needs_layout_passes=False),
        name=name,
    )(table, idx.reshape(t * SC_CHUNKS, SC_ROWS), per_token)


def _sc_step_body(v_hbm, idx_a_hbm, coef_hbm, u_hbm, idx_b_hbm, c_hbm, peer_hbm, dots_hbm,
                  idx_v, aux_v, rows_v, out_v, sem, stage_sem, *, tpw_a, tpw_b):
    _sc_combine_body(v_hbm, idx_a_hbm, coef_hbm, peer_hbm, idx_v, aux_v, rows_v, out_v, sem, stage_sem, tpw=tpw_a)
    _sc_dots_body(u_hbm, idx_b_hbm, c_hbm, dots_hbm, idx_v, aux_v, rows_v, out_v, sem, stage_sem, tpw=tpw_b)


def _sc_step(expert_v, idx_a, coef, expert_u, idx_b, c):
    ta, tb = coef.shape[0], c.shape[0]
    assert ta % (SC_WORKERS * SC_GROUP) == 0 and tb % (SC_WORKERS * SC_GROUP) == 0
    return pl.kernel(
        functools.partial(_sc_step_body, tpw_a=ta // SC_WORKERS, tpw_b=tb // SC_WORKERS),
        out_type=[jax.ShapeDtypeStruct((ta, D_MODEL), F32), jax.ShapeDtypeStruct((tb, PEER_SEL), F32)],
        mesh=_sc_mesh(),
        scratch_types=[pltpu.VMEM((2, SC_STEPS, SC_ROWS), jnp.int32),
                       pltpu.VMEM((2, SC_GROUP, D_MODEL), F32),
                       pltpu.VMEM((SC_BUFS, SC_ROWS, D_MODEL), F32),
                       pltpu.VMEM((SC_GROUP, D_MODEL), F32),
                       pltpu.SemaphoreType.DMA((SC_BUFS,)),
                       pltpu.SemaphoreType.DMA((2,))],
        compiler_params=pltpu.CompilerParams(needs_layout_passes=False),
        name="peer_step",
    )(expert_v, idx_a.reshape(ta * SC_CHUNKS, SC_ROWS), coef, expert_u, idx_b.reshape(tb * SC_CHUNKS, SC_ROWS), c)


def _sc_dots(expert_u, idx, c):
    return _sc_call(_sc_dots_body, expert_u, idx, c, PEER_SEL, "peer_dots")


def _sc_combine(expert_v, idx, coef):
    return _sc_call(_sc_combine_body, expert_v, idx, coef, D_MODEL, "peer_combine")


def _ple_body(h1_ref, peer_ref, p_ref, gp_ref, wg_ref, we_ref, gfin_ref, y_ref):
    h = h1_ref[...] + peer_ref[...]
    a = _rms(h, gp_ref[...], NORM_EPS).astype(BF16)
    gate = jax.nn.sigmoid(_mm(a, wg_ref[...]))
    h = h + _mm(p_ref[...].astype(BF16), we_ref[...]) * gate
    y_ref[...] = _rms(h, gfin_ref[...], NORM_EPS)


def _ple(h1, peer, p, g_ple, w_pgate_b, w_ple_b, g_final, tok0):
    t = h1.shape[0]
    ts = min(512, t)
    first = tok0 // ts
    row = pl.BlockSpec((ts, D_MODEL), lambda i: (i, 0))
    full = lambda *shape: pl.BlockSpec(shape, lambda i: (0,) * len(shape))
    return pl.pallas_call(
        _ple_body,
        grid=(t // ts,),
        in_specs=[row, row, pl.BlockSpec((ts, PLE_DIM), lambda i: (first + i, 0)), full(1, D_MODEL),
                  full(D_MODEL, D_MODEL), full(PLE_DIM, D_MODEL), full(1, D_MODEL)],
        out_specs=row,
        out_shape=jax.ShapeDtypeStruct((t, D_MODEL), F32),
        compiler_params=_params("parallel"),
        name="ple_final",
    )(h1, peer, p, g_ple, w_pgate_b, w_ple_b, g_final)


def _rope_tables(pos):
    half = DIFF_COMP // 2
    inv = ROPE_THETA ** (-jnp.arange(0, DIFF_COMP, 2, dtype=F32) / DIFF_COMP)
    ang = pos.astype(F32)[:, None] * inv[None, :]
    cos = jnp.cos(ang)
    sin = jnp.sin(ang)
    reps = LANES // DIFF_COMP
    del half
    return (jnp.tile(jnp.concatenate([cos, cos], axis=-1), (1, reps)),
            jnp.tile(jnp.concatenate([-sin, sin], axis=-1), (1, reps)))


def kernel(x_prompt, x_sample, cache_diff_k, cache_diff_v, cache_sb_k, cache_sb_v, p_prompt, p_sample, g_mix, w_in, lambda_q1, lambda_k1, lambda_q2, lambda_k2, g_subln, w_out, g_ffn, w_query, sub_keys, expert_u, expert_v, g_ple, w_pgate, w_ple, g_final):
    assert w_in.shape[0] == 1, "single-layer encoder"
    nb, seq, _ = x_prompt.shape
    db, dq, _ = x_sample.shape
    past = cache_diff_k.shape[2]

    lam = (jnp.exp(jnp.sum(lambda_q1[0].astype(F32) * lambda_k1[0].astype(F32)))
           - jnp.exp(jnp.sum(lambda_q2[0].astype(F32) * lambda_k2[0].astype(F32))) + LAM_INIT).reshape(1)
    w_in_b = w_in[0].astype(BF16)
    w_out_b = w_out[0].astype(BF16)
    w_query_b = w_query[0].astype(BF16)
    sub_keys_b = sub_keys[0].astype(BF16)
    w_pgate_b = w_pgate[0].astype(BF16)
    w_ple_b = w_ple[0].astype(BF16)
    g_sub = g_subln[0].reshape(1, HEAD_DIM)
    g_fin = g_final.reshape(1, D_MODEL)

    def select(x, dout, sout, b, s):
        return _post(x, dout, sout, w_out_b, g_ffn, w_query_b, sub_keys_b, b, s)

    def finish(h1, peer, p, b):
        return _ple(h1, peer, p, g_ple, w_pgate_b, w_ple_b, g_fin, b)

    ts = db * dq
    xs = x_sample.reshape(ts, D_MODEL)
    cos_s, sin_s = _rope_tables(jnp.tile(past + jnp.arange(dq, dtype=jnp.int32), db))
    kd, vd, ks, vs, qd2, kdb, vdb, qsb, ksb, vsb = _proj(xs, g_mix, w_in_b, cos_s, sin_s, 1, ts)
    caches = [c[0].reshape(db, past, MIX) for c in (cache_diff_k, cache_diff_v, cache_sb_k, cache_sb_v)]
    dout_s, sout_s = _sample_attention(lam, qd2, kdb, vdb, qsb, ksb, vsb, *caches, g_sub, db, dq)
    rows_s = tuple(r.reshape(1, db, dq, HEADS, HEAD_DIM) for r in (kd, vd, ks, vs))

    xp = x_prompt.reshape(nb * seq, D_MODEL)
    pp = p_prompt[0].reshape(nb * seq, PLE_DIM)
    cos_p, sin_p = _rope_tables(jnp.arange(seq, dtype=jnp.int32))
    kd, vd, ks, vs, qd2, kdb, vdb, qsb, ksb, vsb = _proj(xp, g_mix, w_in_b, cos_p, sin_p, nb, seq)
    rows_p = tuple(r.reshape(1, nb, seq, HEADS, HEAD_DIM) for r in (kd, vd, ks, vs))

    first_cut = seq // PROMPT_FIRST_ROW_BLOCKS
    spans = [(0, i * first_cut, first_cut) for i in range(PROMPT_FIRST_ROW_BLOCKS)]
    spans += [(b, 0, seq) for b in range(1, nb)]
    pins = [lam] * (len(spans) - len(rows_p)) + list(rows_p)

    def prompt_block(k):
        b, t0, n = spans[k]
        dout = _diff_attention(lam, qd2, kdb, vdb, g_sub, b, t0, n)
        sout = _sb_attention(qsb, ksb, vsb, b, t0, n, pins[k])
        return select(xp, dout, sout, b * seq + t0, n)

    blocks = [functools.partial(prompt_block, k) for k in range(len(spans))]
    blocks.append(lambda: select(xs, dout_s, sout_s, 0, ts))
    ys = []
    h1, c, idx, gate = blocks[0]()
    dots = _sc_dots(expert_u[0], idx, c)
    for k in range(1, len(blocks)):
        nh1, nc, nidx, ngate = blocks[k]()
        coef = _coef(gate, dots, nh1)
        peer, ndots = _sc_step(expert_v[0], idx, coef, expert_u[0], nidx, nc)
        b, t0, _ = spans[k - 1]
        ys.append(finish(h1, peer, pp, b * seq + t0))
        h1, idx, gate, dots = nh1, nidx, ngate, ndots
    coef = _coef(gate, dots, lam)
    y_sample = finish(h1, _sc_combine(expert_v[0], idx, coef), p_sample[0].reshape(ts, PLE_DIM), 0)
    y_sample = y_sample.reshape(db, dq, D_MODEL)
    y_prompt = jnp.concatenate(ys, axis=0).reshape(nb, seq, D_MODEL)

    return (y_prompt, y_sample) + rows_p + rows_s
```

```python
import functools
import math

import jax
import jax.numpy as jnp
from jax import lax
from jax.experimental import pallas as pl
from jax.experimental.pallas import tpu as pltpu
from jax.experimental.pallas import tpu_sc as plsc

F32 = jnp.float32
BF16 = jnp.bfloat16

D_MODEL = 1024
HEADS = 8
HEAD_DIM = 64
DIFF_COMP = 32
MIX = HEADS * HEAD_DIM
CHUNK = 64
ROPE_THETA = 10000.0
NORM_EPS = 1e-6
SUBLN_EPS = 1e-5
PEER_HEADS = 8
PEER_KEYS = 128
PEER_TOPK = 16
PEER_HALF = 128
PEER_SEL = PEER_HEADS * PEER_TOPK
PLE_DIM = 256
LAM_INIT = 0.8 - 0.6 * math.exp(-0.3 * 0)
SB_LOG_FLOOR = -104.0
PROMPT_ROW_BLOCKS = 4

LANES = 128
VMEM_LIMIT = 48 * 1024 * 1024

NT_DIMS = (((1,), (1,)), ((), ()))


def _nt(a, b):
    return lax.dot_general(a, b, NT_DIMS, preferred_element_type=F32)


def _mm(a, b):
    return jnp.dot(a, b, preferred_element_type=F32)


def _rms(x, g, eps):
    return x * lax.rsqrt(jnp.mean(x * x, axis=-1, keepdims=True) + eps) * g


def _params(*sem):
    return pltpu.CompilerParams(dimension_semantics=sem, vmem_limit_bytes=VMEM_LIMIT)


def _with_ones(v):
    n = v.shape[0]
    ones = (lax.broadcasted_iota(jnp.int32, (n, HEAD_DIM), 1) == 0).astype(v.dtype)
    return jnp.concatenate([v, ones], axis=1)


def _proj_body(x_ref, g_ref, w_ref, cos_ref, sin_ref,
               kd_ref, vd_ref, ks_ref, vs_ref,
               qd2_ref, kdb_ref, vdb_ref, qsb_ref, ksb_ref, vsb_ref):
    ts = x_ref.shape[0]
    a = _rms(x_ref[...], g_ref[...], NORM_EPS).astype(BF16)
    cos = jnp.tile(cos_ref[...], (1, MIX // LANES))
    sin = jnp.tile(sin_ref[...], (1, MIX // LANES))
    lane = lax.broadcasted_iota(jnp.int32, (ts, MIX), 1)
    first_half = (lane % DIFF_COMP) < (DIFF_COMP // 2)

    def group(i):
        return _mm(a, w_ref[:, i * MIX:(i + 1) * MIX])

    def rope(t):
        partner = jnp.where(first_half,
                            pltpu.roll(t, MIX - DIFF_COMP // 2, 1),
                            pltpu.roll(t, DIFF_COMP // 2, 1))
        return t * cos + partner * sin

    qd = rope(group(0)) * (DIFF_COMP ** -0.5)
    kd = rope(group(1))
    vd = group(2)
    qs = group(3) * (HEAD_DIM ** -0.5)
    ks = group(4)
    vs = group(5)
    kd_ref[...] = kd
    vd_ref[...] = vd
    ks_ref[...] = ks
    vs_ref[...] = vs
    comp0 = lax.broadcasted_iota(jnp.int32, (ts, HEAD_DIM), 1) < DIFF_COMP
    for h in range(HEADS):
        sl = slice(h * HEAD_DIM, (h + 1) * HEAD_DIM)
        qh = qd[:, sl]
        qd2_ref[0, h, 0] = jnp.where(comp0, qh, 0.0).astype(BF16)
        qd2_ref[0, h, 1] = jnp.where(comp0, 0.0, qh).astype(BF16)
        kdb_ref[0, h] = kd[:, sl].astype(BF16)
        vdb_ref[0, h] = _with_ones(vd[:, sl].astype(BF16))
        qsb_ref[0, h] = qs[:, sl].astype(BF16)
        ksb_ref[0, h] = ks[:, sl].astype(BF16)
        vsb_ref[0, h] = vs[:, sl].astype(BF16)


def _proj(x, g_mix, w_in_b, cos_t, sin_t, nb, seq):
    t = nb * seq
    ts = min(256, seq)
    nst = seq // ts
    row = pl.BlockSpec((ts, MIX), lambda i: (i, 0))
    hm = pl.BlockSpec((1, HEADS, ts, HEAD_DIM), lambda i: (i // nst, 0, i % nst, 0))
    hm2 = pl.BlockSpec((1, HEADS, 2, ts, HEAD_DIM), lambda i: (i // nst, 0, 0, i % nst, 0))
    rows = jax.ShapeDtypeStruct((t, MIX), F32)
    heads = jax.ShapeDtypeStruct((nb, HEADS, seq, HEAD_DIM), BF16)
    heads2 = jax.ShapeDtypeStruct((nb, HEADS, 2, seq, HEAD_DIM), BF16)
    hm_ext = pl.BlockSpec((1, HEADS, ts, 2 * HEAD_DIM), lambda i: (i // nst, 0, i % nst, 0))
    heads_ext = jax.ShapeDtypeStruct((nb, HEADS, seq, 2 * HEAD_DIM), BF16)
    return pl.pallas_call(
        _proj_body,
        grid=(t // ts,),
        in_specs=[
            pl.BlockSpec((ts, D_MODEL), lambda i: (i, 0)),
            pl.BlockSpec((1, D_MODEL), lambda i: (0, 0)),
            pl.BlockSpec((D_MODEL, 6 * MIX), lambda i: (0, 0)),
            pl.BlockSpec((ts, LANES), lambda i: (i % nst, 0)),
            pl.BlockSpec((ts, LANES), lambda i: (i % nst, 0)),
        ],
        out_specs=[row, row, row, row, hm2, hm, hm_ext, hm, hm, hm],
        out_shape=[rows, rows, rows, rows, heads2, heads, heads_ext, heads, heads, heads],
        compiler_params=_params("parallel"),
        name="proj",
    )(x, g_mix, w_in_b, cos_t, sin_t)


def _diff_init(rows):
    return jnp.full((rows, 1), -jnp.inf, F32), jnp.zeros((rows, 2 * HEAD_DIM), F32)


def _diff_update(s, v_ext, carry):
    m, acc = carry
    m_new = jnp.maximum(m, jnp.max(s, axis=-1, keepdims=True))
    p = jnp.exp(s - m_new)
    acc = jnp.exp(m - m_new) * acc + _mm(p.astype(BF16), v_ext)
    return m_new, acc


def _diff_finish(carry, lam, g_subln, tq):
    _, acc = carry
    o = acc[:, :HEAD_DIM] / acc[:, HEAD_DIM:HEAD_DIM + 1]
    d = o[:tq] - lam * o[tq:]
    return _rms(d, g_subln, SUBLN_EPS) * (1.0 - LAM_INIT)


def _suffix_sums(lk, tri):
    hi = lk.astype(BF16)
    lo = (lk - hi.astype(F32)).astype(BF16)
    return _mm(hi, tri) + _mm(lo, tri)


def _sb_update(q, k, v, tri, earlier, carry):
    run, acc = carry
    z = _nt(q, k)
    sp = jnp.maximum(z, 0.0) + jnp.log1p(jnp.exp(-jnp.abs(z)))
    lk = -sp if earlier is None else jnp.where(earlier, -sp, 0.0)
    after = _suffix_sums(lk, tri)
    w = jnp.exp((z - sp) + after + run)
    if earlier is not None:
        w = jnp.where(earlier, w, 0.0)
    acc = acc + _mm(w.astype(BF16), v)
    run = run + after[:, 0:1] + lk[:, 0:1]
    return run, acc


def _tri(n):
    j = lax.broadcasted_iota(jnp.int32, (n, n), 0)
    s = lax.broadcasted_iota(jnp.int32, (n, n), 1)
    return (j > s).astype(BF16)


def _diff_body(lam_ref, q_ref, k_ref, v_ref, g_ref, o_ref, *, tq, q0):
    qi = q0 + pl.program_id(1)
    q2 = q_ref[0, 0].reshape(2 * tq, HEAD_DIM)

    def scores(j):
        return _nt(q2, k_ref[0, 0, pl.ds(pl.multiple_of(j * tq, tq), tq), :])

    def step(j, state):
        s, carry = state
        s_next = scores(j + 1)
        return s_next, _diff_update(s, v_ref[0, 0, pl.ds(pl.multiple_of(j * tq, tq), tq), :], carry)

    s, carry = lax.fori_loop(0, qi, step, (scores(0), _diff_init(2 * tq)))
    r = lax.broadcasted_iota(jnp.int32, (2 * tq, tq), 0) % tq
    c = lax.broadcasted_iota(jnp.int32, (2 * tq, tq), 1)
    s = jnp.where((c // CHUNK) <= (r // CHUNK), s, -jnp.inf)
    carry = _diff_update(s, v_ref[0, 0, pl.ds(pl.multiple_of(qi * tq, tq), tq), :], carry)
    o_ref[0, 0] = _diff_finish(carry, lam_ref[0], g_ref[...], tq)


def _diff_attention(lam, qd2, kdb, vdb, g_subln, b, t0, n):
    seq = qd2.shape[3]
    tq = min(512, n)
    q0 = t0 // tq
    return pl.pallas_call(
        functools.partial(_diff_body, tq=tq, q0=q0),
        grid=(HEADS, n // tq),
        in_specs=[
            pl.BlockSpec(memory_space=pltpu.SMEM),
            pl.BlockSpec((1, 1, 2, tq, HEAD_DIM), lambda h, i: (b, h, 0, q0 + i, 0)),
            pl.BlockSpec((1, 1, seq, HEAD_DIM), lambda h, i: (b, h, 0, 0)),
            pl.BlockSpec((1, 1, seq, 2 * HEAD_DIM), lambda h, i: (b, h, 0, 0)),
            pl.BlockSpec((1, HEAD_DIM), lambda h, i: (0, 0)),
        ],
        out_specs=pl.BlockSpec((1, 1, tq, HEAD_DIM), lambda h, i: (0, h, i, 0)),
        out_shape=jax.ShapeDtypeStruct((1, HEADS, n, HEAD_DIM), F32),
        compiler_params=_params("parallel", "arbitrary"),
        name="diff_attention",
    )(lam, qd2, kdb, vdb, g_subln)


def _sb_body(after_ref, q_ref, k_ref, v_ref, o_ref, *, tq, q0):
    del after_ref
    qi = q0 + pl.program_id(1)
    q = q_ref[0, 0]
    tri = _tri(tq)

    def tile(j):
        start = pl.multiple_of(j * tq, tq)
        return k_ref[0, 0, pl.ds(start, tq), :], v_ref[0, 0, pl.ds(start, tq), :]

    r = lax.broadcasted_iota(jnp.int32, (tq, tq), 0)
    c = lax.broadcasted_iota(jnp.int32, (tq, tq), 1)
    carry = (jnp.zeros((tq, 1), F32), jnp.zeros((tq, HEAD_DIM), F32))
    run, acc = _sb_update(q, *tile(qi), tri, c < r, carry)

    def live(state):
        j, run, _ = state
        return jnp.logical_and(j >= 0, jnp.max(run) > SB_LOG_FLOOR)

    def step(state):
        j, run, acc = state
        run, acc = _sb_update(q, *tile(j), tri, None, (run, acc))
        return j - 1, run, acc

    o_ref[0, 0] = lax.while_loop(live, step, (qi - 1, run, acc))[2]


def _sb_attention(qsb, ksb, vsb, b, t0, n, after):
    seq = qsb.shape[2]
    tq = min(256, n)
    q0 = t0 // tq
    kv = pl.BlockSpec((1, 1, seq, HEAD_DIM), lambda h, i: (b, h, 0, 0))
    return pl.pallas_call(
        functools.partial(_sb_body, tq=tq, q0=q0),
        grid=(HEADS, n // tq),
        in_specs=[pl.BlockSpec(memory_space=pl.ANY),
                  pl.BlockSpec((1, 1, tq, HEAD_DIM), lambda h, i: (b, h, q0 + i, 0)), kv, kv],
        out_specs=pl.BlockSpec((1, 1, tq, HEAD_DIM), lambda h, i: (0, h, i, 0)),
        out_shape=jax.ShapeDtypeStruct((1, HEADS, n, HEAD_DIM), F32),
        compiler_params=_params("parallel", "arbitrary"),
        name="sb_attention",
    )(after, qsb, ksb, vsb)


SAMPLE_HEADS = 4
SAMPLE_TILE = 256


def _sample_body(lam_ref, qd2_ref, kdn_ref, vdn_ref, qs_ref, ksn_ref, vsn_ref,
                 ckd_ref, cvd_ref, cks_ref, cvs_ref, g_ref, do_ref, so_ref, *, past, nq):
    lam = lam_ref[0]
    tri_c = _tri(SAMPLE_TILE)
    tri_n = _tri(nq)
    i2 = lax.broadcasted_iota(jnp.int32, (2 * nq, nq), 0) % nq
    j2 = lax.broadcasted_iota(jnp.int32, (2 * nq, nq), 1)
    visible_new = ((past + j2) // CHUNK) <= ((past + i2) // CHUNK)
    i1 = lax.broadcasted_iota(jnp.int32, (nq, nq), 0)
    j1 = lax.broadcasted_iota(jnp.int32, (nq, nq), 1)
    earlier_new = j1 < i1
    for h in range(SAMPLE_HEADS):
        sl = slice(h * HEAD_DIM, (h + 1) * HEAD_DIM)
        q2 = qd2_ref[0, h].reshape(2 * nq, HEAD_DIM)
        carry = _diff_update(_nt(q2, ckd_ref[0, :, sl].astype(BF16)),
                             _with_ones(cvd_ref[0, :, sl].astype(BF16)), _diff_init(2 * nq))
        s_new = jnp.where(visible_new, _nt(q2, kdn_ref[0, h]), -jnp.inf)
        carry = _diff_update(s_new, vdn_ref[0, h], carry)
        do_ref[0, h] = _diff_finish(carry, lam, g_ref[...], nq)
        q = qs_ref[0, h]
        carry = (jnp.zeros((nq, 1), F32), jnp.zeros((nq, HEAD_DIM), F32))
        carry = _sb_update(q, ksn_ref[0, h], vsn_ref[0, h], tri_n, earlier_new, carry)
        for t in reversed(range(past // SAMPLE_TILE)):
            rows = slice(t * SAMPLE_TILE, (t + 1) * SAMPLE_TILE)
            carry = _sb_update(q, cks_ref[0, rows, sl].astype(BF16), cvs_ref[0, rows, sl].astype(BF16),
                               tri_c, None, carry)
        so_ref[0, h] = carry[1]


def _sample_attention(lam, qd2, kdb, vdb, qsb, ksb, vsb, ckd, cvd, cks, cvs, g_subln, nb, nq):
    past = ckd.shape[1]
    nhg = HEADS // SAMPLE_HEADS
    hm = pl.BlockSpec((1, SAMPLE_HEADS, nq, HEAD_DIM), lambda b, g: (0, g, b, 0))
    hm2 = pl.BlockSpec((1, SAMPLE_HEADS, 2, nq, HEAD_DIM), lambda b, g: (0, g, 0, b, 0))
    cache = pl.BlockSpec((1, past, SAMPLE_HEADS * HEAD_DIM), lambda b, g: (b, 0, g))
    out = jax.ShapeDtypeStruct((1, HEADS, nb * nq, HEAD_DIM), F32)
    return pl.pallas_call(
        functools.partial(_sample_body, past=past, nq=nq),
        grid=(nb, nhg),
        in_specs=[pl.BlockSpec(memory_space=pltpu.SMEM), hm2, hm,
                  pl.BlockSpec((1, SAMPLE_HEADS, nq, 2 * HEAD_DIM), lambda b, g: (0, g, b, 0)), hm, hm, hm,
                  cache, cache, cache, cache,
                  pl.BlockSpec((1, HEAD_DIM), lambda b, g: (0, 0))],
        out_specs=[hm, hm],
        out_shape=[out, out],
        compiler_params=_params("parallel", "parallel"),
        name="sample_attention",
    )(lam, qd2, kdb, vdb, qsb, ksb, vsb, ckd, cvd, cks, cvs, g_subln)


def _topk_rows(s, k):
    n = s.shape[0]
    rows = lax.broadcasted_iota(jnp.int32, s.shape, 0)
    vals, ids = [], []
    for _ in range(k):
        m = jnp.max(s, axis=0, keepdims=True)
        i = jnp.min(jnp.where(s == m, rows, n), axis=0, keepdims=True)
        vals.append(m)
        ids.append(i)
        s = jnp.where(rows == i, -jnp.inf, s)
    return jnp.concatenate(vals, axis=0), jnp.concatenate(ids, axis=0)


_STAIR = [(i, j) for i in range(PEER_TOPK) for j in range(PEER_TOPK) if (i + 1) * (j + 1) <= PEER_TOPK]


def _post_body(x_ref, do_ref, so_ref, wo_ref, gf_ref, wq_ref, sk_ref,
               h1_ref, c_ref, idx_ref, gate_ref, q_scr, idx_scr, gate_scr):
    ts = x_ref.shape[0]
    mixed = jnp.zeros((ts, D_MODEL), F32)
    for h in range(HEADS):
        mixed += _mm(do_ref[0, h].astype(BF16), wo_ref[h * HEAD_DIM:(h + 1) * HEAD_DIM, :])
        mixed += _mm(so_ref[0, h].astype(BF16), wo_ref[MIX + h * HEAD_DIM:MIX + (h + 1) * HEAD_DIM, :])
    h1 = x_ref[...] + mixed
    h1_ref[...] = h1
    c = _rms(h1, gf_ref[...], NORM_EPS)
    c_ref[...] = c
    q = _mm(c.astype(BF16), wq_ref[...])
    for hp in range(2 * PEER_HEADS):
        q_scr[hp] = q[:, hp * PEER_HALF:(hp + 1) * PEER_HALF].astype(BF16)

    npad = -len(_STAIR) % 8

    def head(h, _):
        v1, i1 = _topk_rows(_nt(sk_ref[0], q_scr[2 * h]), PEER_TOPK)
        v2, i2 = _topk_rows(_nt(sk_ref[1], q_scr[2 * h + 1]), PEER_TOPK)
        cand = jnp.concatenate([v1[i:i + 1] + v2[j:j + 1] for i, j in _STAIR]
                               + [jnp.full((npad, ts), -jnp.inf, F32)], axis=0)
        eid = jnp.concatenate([i1[i:i + 1] * PEER_KEYS + i2[j:j + 1] for i, j in _STAIR]
                              + [jnp.zeros((npad, ts), jnp.int32)], axis=0)
        top, pos = _topk_rows(cand, PEER_TOPK)
        rows = lax.broadcasted_iota(jnp.int32, cand.shape, 0)
        sel = jnp.concatenate([jnp.sum(jnp.where(rows == pos[r:r + 1], eid, 0), axis=0, keepdims=True)
                               for r in range(PEER_TOPK)], axis=0)
        e = jnp.exp(top - top[0:1])
        gate_scr[h] = e / jnp.sum(e, axis=0, keepdims=True)
        idx_scr[h] = sel
        return 0

    lax.fori_loop(0, PEER_HEADS, head, 0)
    idx_ref[...] = idx_scr[...].reshape(PEER_SEL, ts).T
    gate_ref[...] = gate_scr[...].reshape(PEER_SEL, ts).T


def _post(x, dout, sout, w_out_b, g_ffn, w_query_b, sub_keys_b, tok0, t):
    ts = min(256, t)
    first = tok0 // ts
    row = pl.BlockSpec((ts, D_MODEL), lambda i: (i, 0))
    hm = pl.BlockSpec((1, HEADS, ts, HEAD_DIM), lambda i: (0, 0, i, 0))
    sel = pl.BlockSpec((ts, PEER_SEL), lambda i: (i, 0))
    full = lambda *shape: pl.BlockSpec(shape, lambda i: (0,) * len(shape))
    return pl.pallas_call(
        _post_body,
        grid=(t // ts,),
        in_specs=[pl.BlockSpec((ts, D_MODEL), lambda i: (first + i, 0)),
                  hm, hm, full(2 * MIX, D_MODEL), full(1, D_MODEL),
                  full(D_MODEL, 2 * PEER_HEADS * PEER_HALF), full(2, PEER_KEYS, PEER_HALF)],
        out_specs=[row, row, sel, sel],
        out_shape=[jax.ShapeDtypeStruct((t, D_MODEL), F32), jax.ShapeDtypeStruct((t, D_MODEL), F32),
                   jax.ShapeDtypeStruct((t, PEER_SEL), jnp.int32), jax.ShapeDtypeStruct((t, PEER_SEL), F32)],
        scratch_shapes=[pltpu.VMEM((2 * PEER_HEADS, ts, PEER_HALF), BF16),
                        pltpu.VMEM((PEER_HEADS, PEER_TOPK, ts), jnp.int32),
                        pltpu.VMEM((PEER_HEADS, PEER_TOPK, ts), F32)],
        compiler_params=_params("parallel"),
        name="post_peer_select",
    )(x, dout, sout, w_out_b, g_ffn, w_query_b, sub_keys_b)


def _coef_body(after_ref, gate_ref, dots_ref, o_ref):
    del after_ref
    d = dots_ref[...]
    o_ref[...] = gate_ref[...] * (0.5 * d * (1.0 + lax.erf(d * (2.0 ** -0.5))))


def _coef(gate, dots, after):
    t = gate.shape[0]
    ts = min(2048, t)
    blk = pl.BlockSpec((ts, PEER_SEL), lambda i: (i, 0))
    return pl.pallas_call(
        _coef_body, grid=(t // ts,), in_specs=[pl.BlockSpec(memory_space=pl.ANY), blk, blk], out_specs=blk,
        out_shape=jax.ShapeDtypeStruct((t, PEER_SEL), F32),
        compiler_params=_params("parallel"), name="peer_coef",
    )(after, gate, dots)


SC_CORES = 2
SC_SUBCORES = 16
SC_LANES = 16
SC_WORKERS = SC_CORES * SC_SUBCORES
SC_ROWS = 16
SC_BUFS = 5
SC_GROUP = 8
SC_CHUNKS = PEER_SEL // SC_ROWS
SC_STEPS = SC_GROUP * SC_CHUNKS
SC_VECS = D_MODEL // SC_LANES


def _sc_mesh():
    return plsc.VectorSubcoreMesh(core_axis_name="c", subcore_axis_name="s",
                                  num_cores=SC_CORES, num_subcores=SC_SUBCORES)


def _sc_walk(table_hbm, idx_hbm, aux_hbm, idx_v, aux_v, rows_v, sem, stage_sem, tpw, begin_group, compute, end_group):
    tok_base = (lax.axis_index("s") * SC_CORES + lax.axis_index("c")) * tpw
    ngroups = tpw // SC_GROUP
    nsteps = tpw * SC_CHUNKS

    def first_token(g):
        return pl.multiple_of(tok_base + g * SC_GROUP, SC_GROUP)

    def stage(g):
        tok0 = first_token(g)
        return (pltpu.make_async_copy(idx_hbm.at[pl.ds(tok0 * SC_CHUNKS, SC_STEPS)], idx_v.at[g % 2], stage_sem.at[0]),
                pltpu.make_async_copy(aux_hbm.at[pl.ds(tok0, SC_GROUP)],
                                      aux_v.at[g % 2, :, pl.ds(0, aux_hbm.shape[1])], stage_sem.at[1]))

    def gather(step):
        idx = idx_v.at[(step // SC_STEPS) % 2, step % SC_STEPS]
        return pltpu.make_async_copy(table_hbm.at[idx], rows_v.at[step % SC_BUFS], sem.at[step % SC_BUFS])

    for cp in stage(0):
        cp.start()
    for cp in stage(0):
        cp.wait()
    for step in range(SC_BUFS - 1):
        gather(step).start()

    def walk(step, _):
        g = step // SC_STEPS
        local = step % SC_STEPS

        @pl.when(jnp.logical_and(local == 0, g + 1 < ngroups))
        def _():
            for cp in stage(g + 1):
                cp.start()

        ahead = step + (SC_BUFS - 1)

        @pl.when(ahead < nsteps)
        def _():
            @pl.when(ahead % SC_STEPS == 0)
            def _():
                for cp in stage(ahead // SC_STEPS):
                    cp.wait()

            gather(ahead).start()

        @pl.when(local == 0)
        def _():
            begin_group()

        gather(step).wait()
        compute(rows_v.at[step % SC_BUFS], g % 2, local // SC_CHUNKS, local % SC_CHUNKS)

        @pl.when(local == SC_STEPS - 1)
        def _():
            end_group(first_token(g))

        return 0

    lax.fori_loop(0, nsteps, walk, 0)


def _sc_dots_body(u_hbm, idx_hbm, c_hbm, out_hbm, idx_v, c_v, rows_v, dots_v, sem, stage_sem, *, tpw):
    lane = lax.broadcasted_iota(jnp.int32, (SC_LANES,), 0)
    zero = jnp.zeros((SC_LANES,), F32)

    def compute(rows, slot, tt, ch):
        for half in range(SC_ROWS // SC_LANES):
            outv = zero
            for q in range(2):
                r0 = half * SC_LANES + q * 8

                def vec(kk, accs):
                    off = pl.multiple_of(kk * SC_LANES, SC_LANES)
                    cv = c_v[slot, tt, pl.ds(off, SC_LANES)]
                    return tuple(a + rows[r0 + r, pl.ds(off, SC_LANES)] * cv for r, a in enumerate(accs))

                accs = lax.fori_loop(0, SC_VECS, vec, (zero,) * 8)
                for r in range(8):
                    outv = jnp.where(lane == q * 8 + r, jnp.sum(accs[r]), outv)
            dots_v[tt, pl.ds(pl.multiple_of(ch * SC_ROWS + half * SC_LANES, SC_LANES), SC_LANES)] = outv

    def end_group(tok0):
        pltpu.sync_copy(dots_v.at[:, pl.ds(0, PEER_SEL)], out_hbm.at[pl.ds(tok0, SC_GROUP)])

    _sc_walk(u_hbm, idx_hbm, c_hbm, idx_v, c_v, rows_v, sem, stage_sem, tpw, lambda: None, compute, end_group)


def _sc_combine_body(v_hbm, idx_hbm, coef_hbm, out_hbm, idx_v, coef_v, rows_v, acc_v, sem, stage_sem, *, tpw):
    lane = lax.broadcasted_iota(jnp.int32, (SC_LANES,), 0)
    zero = jnp.zeros((SC_LANES,), F32)

    def compute(rows, slot, tt, ch):
        for half in range(SC_ROWS // SC_LANES):
            cf = coef_v[slot, tt, pl.ds(pl.multiple_of(ch * SC_ROWS + half * SC_LANES, SC_LANES), SC_LANES)]
            splat = [jnp.full((SC_LANES,), jnp.sum(jnp.where(lane == r, cf, 0.0)), F32) for r in range(SC_LANES)]

            @plsc.parallel_loop(0, SC_VECS, unroll=2)
            def _(kk):
                off = pl.multiple_of(kk * SC_LANES, SC_LANES)
                terms = [rows[half * SC_LANES + r, pl.ds(off, SC_LANES)] * splat[r] for r in range(SC_LANES)]
                while len(terms) > 1:
                    terms = [a + b for a, b in zip(terms[0::2], terms[1::2])]
                acc_v[tt, pl.ds(off, SC_LANES)] = acc_v[tt, pl.ds(off, SC_LANES)] + terms[0]

    def begin_group():
        def clear(i, _):
            acc_v[i // SC_VECS, pl.ds(pl.multiple_of((i % SC_VECS) * SC_LANES, SC_LANES), SC_LANES)] = zero
            return 0

        lax.fori_loop(0, SC_GROUP * SC_VECS, clear, 0)

    def end_group(tok0):
        pltpu.sync_copy(acc_v, out_hbm.at[pl.ds(tok0, SC_GROUP)])

    _sc_walk(v_hbm, idx_hbm, coef_hbm, idx_v, coef_v, rows_v, sem, stage_sem, tpw, begin_group, compute, end_group)


def _sc_call(body, table, idx, per_token, out_width, name):
    t = per_token.shape[0]
    tpw = t // SC_WORKERS
    assert tpw % SC_GROUP == 0
    return pl.kernel(
        functools.partial(body, tpw=tpw),
        out_type=jax.ShapeDtypeStruct((t, out_width), F32),
        mesh=_sc_mesh(),
        scratch_types=[pltpu.VMEM((2, SC_STEPS, SC_ROWS), jnp.int32),
                       pltpu.VMEM((2, SC_GROUP, per_token.shape[1]), F32),
                       pltpu.VMEM((SC_BUFS, SC_ROWS, D_MODEL), F32),
                       pltpu.VMEM((SC_GROUP, out_width), F32),
                       pltpu.SemaphoreType.DMA((SC_BUFS,)),
                       pltpu.SemaphoreType.DMA((2,))],
        compiler_params=pltpu.CompilerParams(needs_layout_passes=False),
        name=name,
    )(table, idx.reshape(t * SC_CHUNKS, SC_ROWS), per_token)


def _sc_step_body(v_hbm, idx_a_hbm, coef_hbm, u_hbm, idx_b_hbm, c_hbm, peer_hbm, dots_hbm,
                  idx_v, aux_v, rows_v, out_v, sem, stage_sem, *, tpw_a, tpw_b):
    _sc_combine_body(v_hbm, idx_a_hbm, coef_hbm, peer_hbm, idx_v, aux_v, rows_v, out_v, sem, stage_sem, tpw=tpw_a)
    _sc_dots_body(u_hbm, idx_b_hbm, c_hbm, dots_hbm, idx_v, aux_v, rows_v, out_v, sem, stage_sem, tpw=tpw_b)


def _sc_step(expert_v, idx_a, coef, expert_u, idx_b, c):
    ta, tb = coef.shape[0], c.shape[0]
    assert ta % (SC_WORKERS * SC_GROUP) == 0 and tb % (SC_WORKERS * SC_GROUP) == 0
    return pl.kernel(
        functools.partial(_sc_step_body, tpw_a=ta // SC_WORKERS, tpw_b=tb // SC_WORKERS),
        out_type=[jax.ShapeDtypeStruct((ta, D_MODEL), F32), jax.ShapeDtypeStruct((tb, PEER_SEL), F32)],
        mesh=_sc_mesh(),
        scratch_types=[pltpu.VMEM((2, SC_STEPS, SC_ROWS), jnp.int32),
                       pltpu.VMEM((2, SC_GROUP, D_MODEL), F32),
                       pltpu.VMEM((SC_BUFS, SC_ROWS, D_MODEL), F32),
                       pltpu.VMEM((SC_GROUP, D_MODEL), F32),
                       pltpu.SemaphoreType.DMA((SC_BUFS,)),
                       pltpu.SemaphoreType.DMA((2,))],
        compiler_params=pltpu.CompilerParams(needs_layout_passes=False),
        name="peer_step",
    )(expert_v, idx_a.reshape(ta * SC_CHUNKS, SC_ROWS), coef, expert_u, idx_b.reshape(tb * SC_CHUNKS, SC_ROWS), c)


def _sc_dots(expert_u, idx, c):
    return _sc_call(_sc_dots_body, expert_u, idx, c, PEER_SEL, "peer_dots")


def _sc_combine(expert_v, idx, coef):
    return _sc_call(_sc_combine_body, expert_v, idx, coef, D_MODEL, "peer_combine")


def _ple_body(h1_ref, peer_ref, p_ref, gp_ref, wg_ref, we_ref, gfin_ref, y_ref):
    h = h1_ref[...] + peer_ref[...]
    a = _rms(h, gp_ref[...], NORM_EPS).astype(BF16)
    gate = jax.nn.sigmoid(_mm(a, wg_ref[...]))
    h = h + _mm(p_ref[...].astype(BF16), we_ref[...]) * gate
    y_ref[...] = _rms(h, gfin_ref[...], NORM_EPS)


def _ple(h1, peer, p, g_ple, w_pgate_b, w_ple_b, g_final, tok0):
    t = h1.shape[0]
    ts = min(512, t)
    first = tok0 // ts
    row = pl.BlockSpec((ts, D_MODEL), lambda i: (i, 0))
    full = lambda *shape: pl.BlockSpec(shape, lambda i: (0,) * len(shape))
    return pl.pallas_call(
        _ple_body,
        grid=(t // ts,),
        in_specs=[row, row, pl.BlockSpec((ts, PLE_DIM), lambda i: (first + i, 0)), full(1, D_MODEL),
                  full(D_MODEL, D_MODEL), full(PLE_DIM, D_MODEL), full(1, D_MODEL)],
        out_specs=row,
        out_shape=jax.ShapeDtypeStruct((t, D_MODEL), F32),
        compiler_params=_params("parallel"),
        name="ple_final",
    )(h1, peer, p, g_ple, w_pgate_b, w_ple_b, g_final)


def _rope_tables(pos):
    half = DIFF_COMP // 2
    inv = ROPE_THETA ** (-jnp.arange(0, DIFF_COMP, 2, dtype=F32) / DIFF_COMP)
    ang = pos.astype(F32)[:, None] * inv[None, :]
    cos = jnp.cos(ang)
    sin = jnp.sin(ang)
    reps = LANES // DIFF_COMP
    del half
    return (jnp.tile(jnp.concatenate([cos, cos], axis=-1), (1, reps)),
            jnp.tile(jnp.concatenate([-sin, sin], axis=-1), (1, reps)))


def kernel(x_prompt, x_sample, cache_diff_k, cache_diff_v, cache_sb_k, cache_sb_v, p_prompt, p_sample, g_mix, w_in, lambda_q1, lambda_k1, lambda_q2, lambda_k2, g_subln, w_out, g_ffn, w_query, sub_keys, expert_u, expert_v, g_ple, w_pgate, w_ple, g_final):
    assert w_in.shape[0] == 1, "single-layer encoder"
    nb, seq, _ = x_prompt.shape
    db, dq, _ = x_sample.shape
    past = cache_diff_k.shape[2]

    lam = (jnp.exp(jnp.sum(lambda_q1[0].astype(F32) * lambda_k1[0].astype(F32)))
           - jnp.exp(jnp.sum(lambda_q2[0].astype(F32) * lambda_k2[0].astype(F32))) + LAM_INIT).reshape(1)
    w_in_b = w_in[0].astype(BF16)
    w_out_b = w_out[0].astype(BF16)
    w_query_b = w_query[0].astype(BF16)
    sub_keys_b = sub_keys[0].astype(BF16)
    w_pgate_b = w_pgate[0].astype(BF16)
    w_ple_b = w_ple[0].astype(BF16)
    g_sub = g_subln[0].reshape(1, HEAD_DIM)
    g_fin = g_final.reshape(1, D_MODEL)

    def select(x, dout, sout, b, s):
        return _post(x, dout, sout, w_out_b, g_ffn, w_query_b, sub_keys_b, b, s)

    def finish(h1, peer, p, b):
        return _ple(h1, peer, p, g_ple, w_pgate_b, w_ple_b, g_fin, b)

    ts = db * dq
    xs = x_sample.reshape(ts, D_MODEL)
    cos_s, sin_s = _rope_tables(jnp.tile(past + jnp.arange(dq, dtype=jnp.int32), db))
    kd, vd, ks, vs, qd2, kdb, vdb, qsb, ksb, vsb = _proj(xs, g_mix, w_in_b, cos_s, sin_s, 1, ts)
    caches = [c[0].reshape(db, past, MIX) for c in (cache_diff_k, cache_diff_v, cache_sb_k, cache_sb_v)]
    dout_s, sout_s = _sample_attention(lam, qd2, kdb, vdb, qsb, ksb, vsb, *caches, g_sub, db, dq)
    rows_s = tuple(r.reshape(1, db, dq, HEADS, HEAD_DIM) for r in (kd, vd, ks, vs))

    xp = x_prompt.reshape(nb * seq, D_MODEL)
    pp = p_prompt[0].reshape(nb * seq, PLE_DIM)
    cos_p, sin_p = _rope_tables(jnp.arange(seq, dtype=jnp.int32))
    kd, vd, ks, vs, qd2, kdb, vdb, qsb, ksb, vsb = _proj(xp, g_mix, w_in_b, cos_p, sin_p, nb, seq)
    rows_p = tuple(r.reshape(1, nb, seq, HEADS, HEAD_DIM) for r in (kd, vd, ks, vs))

    cut = seq // PROMPT_ROW_BLOCKS
    spans = [(b, i * cut, cut) for b in range(nb) for i in range(PROMPT_ROW_BLOCKS)]
    pins = [lam] * len(spans)
    pins[len(spans) // 4:len(spans) // 4 + len(rows_p)] = rows_p

    def prompt_block(k):
        b, t0, n = spans[k]
        dout = _diff_attention(lam, qd2, kdb, vdb, g_sub, b, t0, n)
        sout = _sb_attention(qsb, ksb, vsb, b, t0, n, pins[k])
        return select(xp, dout, sout, b * seq + t0, n)

    blocks = [functools.partial(prompt_block, k) for k in range(len(spans))]
    blocks.append(lambda: select(xs, dout_s, sout_s, 0, ts))
    ys = []
    h1, c, idx, gate = blocks[0]()
    dots = _sc_dots(expert_u[0], idx, c)
    for k in range(1, len(blocks)):
        nh1, nc, nidx, ngate = blocks[k]()
        coef = _coef(gate, dots, nh1)
        peer, ndots = _sc_step(expert_v[0], idx, coef, expert_u[0], nidx, nc)
        b, t0, _ = spans[k - 1]
        ys.append(finish(h1, peer, pp, b * seq + t0))
        h1, idx, gate, dots = nh1, nidx, ngate, ndots
    coef = _coef(gate, dots, lam)
    y_sample = finish(h1, _sc_combine(expert_v[0], idx, coef), p_sample[0].reshape(ts, PLE_DIM), 0)
    y_sample = y_sample.reshape(db, dq, D_MODEL)
    y_prompt = jnp.concatenate(ys, axis=0).reshape(nb, seq, D_MODEL)

    return (y_prompt, y_sample) + rows_p + rows_s
```

```python
import functools
import math

import jax
import jax.numpy as jnp
from jax import lax
from jax.experimental import pallas as pl
from jax.experimental.pallas import tpu as pltpu
from jax.experimental.pallas import tpu_sc as plsc

F32 = jnp.float32
BF16 = jnp.bfloat16

D_MODEL = 1024
HEADS = 8
HEAD_DIM = 64
DIFF_COMP = 32
MIX = HEADS * HEAD_DIM
CHUNK = 64
ROPE_THETA = 10000.0
NORM_EPS = 1e-6
SUBLN_EPS = 1e-5
PEER_HEADS = 8
PEER_KEYS = 128
PEER_TOPK = 16
PEER_HALF = 128
PEER_SEL = PEER_HEADS * PEER_TOPK
PLE_DIM = 256
LAM_INIT = 0.8 - 0.6 * math.exp(-0.3 * 0)
SB_LOG_FLOOR = -104.0
PROMPT_ROW_BLOCKS = 4

LANES = 128
VMEM_LIMIT = 48 * 1024 * 1024

NT_DIMS = (((1,), (1,)), ((), ()))


def _nt(a, b):
    return lax.dot_general(a, b, NT_DIMS, preferred_element_type=F32)


def _mm(a, b):
    return jnp.dot(a, b, preferred_element_type=F32)


def _rms(x, g, eps):
    return x * lax.rsqrt(jnp.mean(x * x, axis=-1, keepdims=True) + eps) * g


def _params(*sem):
    return pltpu.CompilerParams(dimension_semantics=sem, vmem_limit_bytes=VMEM_LIMIT)


def _with_ones(v):
    n = v.shape[0]
    ones = (lax.broadcasted_iota(jnp.int32, (n, HEAD_DIM), 1) == 0).astype(v.dtype)
    return jnp.concatenate([v, ones], axis=1)


def _proj_body(x_ref, g_ref, w_ref, cos_ref, sin_ref,
               kd_ref, vd_ref, ks_ref, vs_ref,
               qd2_ref, kdb_ref, vdb_ref, qsb_ref, ksb_ref, vsb_ref, *, feature_major):
    ts = x_ref.shape[0]
    a = _rms(x_ref[...], g_ref[...], NORM_EPS).astype(BF16)
    cos = jnp.tile(cos_ref[...], (1, MIX // LANES))
    sin = jnp.tile(sin_ref[...], (1, MIX // LANES))
    lane = lax.broadcasted_iota(jnp.int32, (ts, MIX), 1)
    first_half = (lane % DIFF_COMP) < (DIFF_COMP // 2)

    def group(i):
        return _mm(a, w_ref[:, i * MIX:(i + 1) * MIX])

    def rope(t):
        partner = jnp.where(first_half,
                            pltpu.roll(t, MIX - DIFF_COMP // 2, 1),
                            pltpu.roll(t, DIFF_COMP // 2, 1))
        return t * cos + partner * sin

    qd = rope(group(0)) * (DIFF_COMP ** -0.5)
    kd = rope(group(1))
    vd = group(2)
    qs = group(3) * (HEAD_DIM ** -0.5)
    ks = group(4)
    vs = group(5)
    for ref, rows in ((kd_ref, kd), (vd_ref, vd), (ks_ref, ks), (vs_ref, vs)):
        if feature_major:
            ref[0] = rows.T
        else:
            ref[...] = rows
    comp0 =lax.broadcasted_iota(jnp.int32, (ts, HEAD_DIM), 1) < DIFF_COMP
    for h in range(HEADS):
        sl = slice(h * HEAD_DIM, (h + 1) * HEAD_DIM)
        qh = qd[:, sl]
        qd2_ref[0, h, 0] = jnp.where(comp0, qh, 0.0).astype(BF16)
        qd2_ref[0, h, 1] = jnp.where(comp0, 0.0, qh).astype(BF16)
        kdb_ref[0, h] = kd[:, sl].astype(BF16)
        vdb_ref[0, h] = _with_ones(vd[:, sl].astype(BF16))
        qsb_ref[0, h] = qs[:, sl].astype(BF16)
        ksb_ref[0, h] = ks[:, sl].astype(BF16)
        vsb_ref[0, h] = vs[:, sl].astype(BF16)


def _proj(x, g_mix, w_in_b, cos_t, sin_t, nb, seq, feature_major):
    t = nb * seq
    ts = min(256, seq)
    nst = seq // ts
    if feature_major:
        row = pl.BlockSpec((1, MIX, ts), lambda i: (i // nst, 0, i % nst))
        rows = jax.ShapeDtypeStruct((nb, MIX, seq), F32)
    else:
        row = pl.BlockSpec((ts, MIX), lambda i: (i, 0))
        rows = jax.ShapeDtypeStruct((t, MIX), F32)
    hm = pl.BlockSpec((1, HEADS, ts, HEAD_DIM), lambda i: (i // nst, 0, i % nst, 0))
    hm2 = pl.BlockSpec((1, HEADS, 2, ts, HEAD_DIM), lambda i: (i // nst, 0, 0, i % nst, 0))
    heads =jax.ShapeDtypeStruct((nb, HEADS, seq, HEAD_DIM), BF16)
    heads2 = jax.ShapeDtypeStruct((nb, HEADS, 2, seq, HEAD_DIM), BF16)
    hm_ext = pl.BlockSpec((1, HEADS, ts, 2 * HEAD_DIM), lambda i: (i // nst, 0, i % nst, 0))
    heads_ext = jax.ShapeDtypeStruct((nb, HEADS, seq, 2 * HEAD_DIM), BF16)
    return pl.pallas_call(
        functools.partial(_proj_body, feature_major=feature_major),
        grid=(t // ts,),
        in_specs=[
            pl.BlockSpec((ts, D_MODEL), lambda i: (i, 0)),
            pl.BlockSpec((1, D_MODEL), lambda i: (0, 0)),
            pl.BlockSpec((D_MODEL, 6 * MIX), lambda i: (0, 0)),
            pl.BlockSpec((ts, LANES), lambda i: (i % nst, 0)),
            pl.BlockSpec((ts, LANES), lambda i: (i % nst, 0)),
        ],
        out_specs=[row, row, row, row, hm2, hm, hm_ext, hm, hm, hm],
        out_shape=[rows, rows, rows, rows, heads2, heads, heads_ext, heads, heads, heads],
        compiler_params=_params("parallel"),
        name="proj",
    )(x, g_mix, w_in_b, cos_t, sin_t)


def _diff_init(rows):
    return jnp.full((rows, 1), -jnp.inf, F32), jnp.zeros((rows, 2 * HEAD_DIM), F32)


def _diff_update(s, v_ext, carry):
    m, acc = carry
    m_new = jnp.maximum(m, jnp.max(s, axis=-1, keepdims=True))
    p = jnp.exp(s - m_new)
    acc = jnp.exp(m - m_new) * acc + _mm(p.astype(BF16), v_ext)
    return m_new, acc


def _diff_finish(carry, lam, g_subln, tq):
    _, acc = carry
    o = acc[:, :HEAD_DIM] / acc[:, HEAD_DIM:HEAD_DIM + 1]
    d = o[:tq] - lam * o[tq:]
    return _rms(d, g_subln, SUBLN_EPS) * (1.0 - LAM_INIT)


def _suffix_sums(lk, tri):
    hi = lk.astype(BF16)
    lo = (lk - hi.astype(F32)).astype(BF16)
    return _mm(hi, tri) + _mm(lo, tri)


def _sb_update(q, k, v, tri, earlier, carry):
    run, acc = carry
    z = _nt(q, k)
    sp = jnp.maximum(z, 0.0) + jnp.log1p(jnp.exp(-jnp.abs(z)))
    lk = -sp if earlier is None else jnp.where(earlier, -sp, 0.0)
    after = _suffix_sums(lk, tri)
    w = jnp.exp((z - sp) + after + run)
    if earlier is not None:
        w = jnp.where(earlier, w, 0.0)
    acc = acc + _mm(w.astype(BF16), v)
    run = run + after[:, 0:1] + lk[:, 0:1]
    return run, acc


def _tri(n):
    j = lax.broadcasted_iota(jnp.int32, (n, n), 0)
    s = lax.broadcasted_iota(jnp.int32, (n, n), 1)
    return (j > s).astype(BF16)


def _diff_body(lam_ref, q_ref, k_ref, v_ref, g_ref, o_ref, *, tq, q0):
    qi = q0 + pl.program_id(1)
    q2 = q_ref[0, 0].reshape(2 * tq, HEAD_DIM)

    def scores(j):
        return _nt(q2, k_ref[0, 0, pl.ds(pl.multiple_of(j * tq, tq), tq), :])

    def step(j, state):
        s, carry = state
        s_next = scores(j + 1)
        return s_next, _diff_update(s, v_ref[0, 0, pl.ds(pl.multiple_of(j * tq, tq), tq), :], carry)

    s, carry = lax.fori_loop(0, qi, step, (scores(0), _diff_init(2 * tq)))
    r = lax.broadcasted_iota(jnp.int32, (2 * tq, tq), 0) % tq
    c = lax.broadcasted_iota(jnp.int32, (2 * tq, tq), 1)
    s = jnp.where((c // CHUNK) <= (r // CHUNK), s, -jnp.inf)
    carry = _diff_update(s, v_ref[0, 0, pl.ds(pl.multiple_of(qi * tq, tq), tq), :], carry)
    o_ref[0, 0] = _diff_finish(carry, lam_ref[0], g_ref[...], tq)


def _diff_attention(lam, qd2, kdb, vdb, g_subln, b, t0, n):
    seq = qd2.shape[3]
    tq = min(512, n)
    q0 = t0 // tq
    return pl.pallas_call(
        functools.partial(_diff_body, tq=tq, q0=q0),
        grid=(HEADS, n // tq),
        in_specs=[
            pl.BlockSpec(memory_space=pltpu.SMEM),
            pl.BlockSpec((1, 1, 2, tq, HEAD_DIM), lambda h, i: (b, h, 0, q0 + i, 0)),
            pl.BlockSpec((1, 1, seq, HEAD_DIM), lambda h, i: (b, h, 0, 0)),
            pl.BlockSpec((1, 1, seq, 2 * HEAD_DIM), lambda h, i: (b, h, 0, 0)),
            pl.BlockSpec((1, HEAD_DIM), lambda h, i: (0, 0)),
        ],
        out_specs=pl.BlockSpec((1, 1, tq, HEAD_DIM), lambda h, i: (0, h, i, 0)),
        out_shape=jax.ShapeDtypeStruct((1, HEADS, n, HEAD_DIM), F32),
        compiler_params=_params("parallel", "arbitrary"),
        name="diff_attention",
    )(lam, qd2, kdb, vdb, g_subln)


def _sb_body(q_ref, k_ref, v_ref, o_ref, *, tq, q0):
    qi = q0 + pl.program_id(1)
    q = q_ref[0, 0]
    tri = _tri(tq)

    def tile(j):
        start = pl.multiple_of(j * tq, tq)
        return k_ref[0, 0, pl.ds(start, tq), :], v_ref[0, 0, pl.ds(start, tq), :]

    r = lax.broadcasted_iota(jnp.int32, (tq, tq), 0)
    c = lax.broadcasted_iota(jnp.int32, (tq, tq), 1)
    carry = (jnp.zeros((tq, 1), F32), jnp.zeros((tq, HEAD_DIM), F32))
    run, acc = _sb_update(q, *tile(qi), tri, c < r, carry)

    def live(state):
        j, run, _ = state
        return jnp.logical_and(j >= 0, jnp.max(run) > SB_LOG_FLOOR)

    def step(state):
        j, run, acc = state
        run, acc = _sb_update(q, *tile(j), tri, None, (run, acc))
        return j - 1, run, acc

    o_ref[0, 0] = lax.while_loop(live, step, (qi - 1, run, acc))[2]


def _sb_attention(qsb, ksb, vsb, b, t0, n):
    seq = qsb.shape[2]
    tq = min(256, n)
    q0 = t0 // tq
    kv = pl.BlockSpec((1, 1, seq, HEAD_DIM), lambda h, i: (b, h, 0, 0))
    return pl.pallas_call(
        functools.partial(_sb_body, tq=tq, q0=q0),
        grid=(HEADS, n // tq),
        in_specs=[pl.BlockSpec((1, 1, tq, HEAD_DIM), lambda h, i: (b, h, q0 + i, 0)), kv, kv],
        out_specs=pl.BlockSpec((1, 1, tq, HEAD_DIM), lambda h, i: (0, h, i, 0)),
        out_shape=jax.ShapeDtypeStruct((1, HEADS, n, HEAD_DIM), F32),
        compiler_params=_params("parallel", "arbitrary"),
        name="sb_attention",
    )(qsb, ksb, vsb)


SAMPLE_HEADS = 4
SAMPLE_TILE = 256


def _sample_body(lam_ref, qd2_ref, kdn_ref, vdn_ref, qs_ref, ksn_ref, vsn_ref,
                 ckd_ref, cvd_ref, cks_ref, cvs_ref, g_ref, do_ref, so_ref, *, past, nq):
    lam = lam_ref[0]
    tri_c = _tri(SAMPLE_TILE)
    tri_n = _tri(nq)
    i2 = lax.broadcasted_iota(jnp.int32, (2 * nq, nq), 0) % nq
    j2 = lax.broadcasted_iota(jnp.int32, (2 * nq, nq), 1)
    visible_new = ((past + j2) // CHUNK) <= ((past + i2) // CHUNK)
    i1 = lax.broadcasted_iota(jnp.int32, (nq, nq), 0)
    j1 = lax.broadcasted_iota(jnp.int32, (nq, nq), 1)
    earlier_new = j1 < i1
    for h in range(SAMPLE_HEADS):
        sl = slice(h * HEAD_DIM, (h + 1) * HEAD_DIM)
        q2 = qd2_ref[0, h].reshape(2 * nq, HEAD_DIM)
        carry = _diff_update(_nt(q2, ckd_ref[0, :, sl].astype(BF16)),
                             _with_ones(cvd_ref[0, :, sl].astype(BF16)), _diff_init(2 * nq))
        s_new = jnp.where(visible_new, _nt(q2, kdn_ref[0, h]), -jnp.inf)
        carry = _diff_update(s_new, vdn_ref[0, h], carry)
        do_ref[0, h] = _diff_finish(carry, lam, g_ref[...], nq)
        q = qs_ref[0, h]
        carry = (jnp.zeros((nq, 1), F32), jnp.zeros((nq, HEAD_DIM), F32))
        carry = _sb_update(q, ksn_ref[0, h], vsn_ref[0, h], tri_n, earlier_new, carry)
        for t in reversed(range(past // SAMPLE_TILE)):
            rows = slice(t * SAMPLE_TILE, (t + 1) * SAMPLE_TILE)
            carry = _sb_update(q, cks_ref[0, rows, sl].astype(BF16), cvs_ref[0, rows, sl].astype(BF16),
                               tri_c, None, carry)
        so_ref[0, h] = carry[1]


def _sample_attention(lam, qd2, kdb, vdb, qsb, ksb, vsb, ckd, cvd, cks, cvs, g_subln, nb, nq):
    past = ckd.shape[1]
    nhg = HEADS // SAMPLE_HEADS
    hm = pl.BlockSpec((1, SAMPLE_HEADS, nq, HEAD_DIM), lambda b, g: (0, g, b, 0))
    hm2 = pl.BlockSpec((1, SAMPLE_HEADS, 2, nq, HEAD_DIM), lambda b, g: (0, g, 0, b, 0))
    cache = pl.BlockSpec((1, past, SAMPLE_HEADS * HEAD_DIM), lambda b, g: (b, 0, g))
    out = jax.ShapeDtypeStruct((1, HEADS, nb * nq, HEAD_DIM), F32)
    return pl.pallas_call(
        functools.partial(_sample_body, past=past, nq=nq),
        grid=(nb, nhg),
        in_specs=[pl.BlockSpec(memory_space=pltpu.SMEM), hm2, hm,
                  pl.BlockSpec((1, SAMPLE_HEADS, nq, 2 * HEAD_DIM), lambda b, g: (0, g, b, 0)), hm, hm, hm,
                  cache, cache, cache, cache,
                  pl.BlockSpec((1, HEAD_DIM), lambda b, g: (0, 0))],
        out_specs=[hm, hm],
        out_shape=[out, out],
        compiler_params=_params("parallel", "parallel"),
        name="sample_attention",
    )(lam, qd2, kdb, vdb, qsb, ksb, vsb, ckd, cvd, cks, cvs, g_subln)


def _topk_rows(s, k):
    n = s.shape[0]
    rows = lax.broadcasted_iota(jnp.int32, s.shape, 0)
    vals, ids = [], []
    for _ in range(k):
        m = jnp.max(s, axis=0, keepdims=True)
        i = jnp.min(jnp.where(s == m, rows, n), axis=0, keepdims=True)
        vals.append(m)
        ids.append(i)
        s = jnp.where(rows == i, -jnp.inf, s)
    return jnp.concatenate(vals, axis=0), jnp.concatenate(ids, axis=0)


_STAIR = [(i, j) for i in range(PEER_TOPK) for j in range(PEER_TOPK) if (i + 1) * (j + 1) <= PEER_TOPK]


def _post_body(x_ref, do_ref, so_ref, wo_ref, gf_ref, wq_ref, sk_ref,
               h1_ref, c_ref, idx_ref, gate_ref, q_scr, idx_scr, gate_scr):
    ts = x_ref.shape[0]
    mixed = jnp.zeros((ts, D_MODEL), F32)
    for h in range(HEADS):
        mixed += _mm(do_ref[0, h].astype(BF16), wo_ref[h * HEAD_DIM:(h + 1) * HEAD_DIM, :])
        mixed += _mm(so_ref[0, h].astype(BF16), wo_ref[MIX + h * HEAD_DIM:MIX + (h + 1) * HEAD_DIM, :])
    h1 = x_ref[...] + mixed
    h1_ref[...] = h1
    c = _rms(h1, gf_ref[...], NORM_EPS)
    c_ref[...] = c
    q = _mm(c.astype(BF16), wq_ref[...])
    for hp in range(2 * PEER_HEADS):
        q_scr[hp] = q[:, hp * PEER_HALF:(hp + 1) * PEER_HALF].astype(BF16)

    npad = -len(_STAIR) % 8

    def head(h, _):
        v1, i1 = _topk_rows(_nt(sk_ref[0], q_scr[2 * h]), PEER_TOPK)
        v2, i2 = _topk_rows(_nt(sk_ref[1], q_scr[2 * h + 1]), PEER_TOPK)
        cand = jnp.concatenate([v1[i:i + 1] + v2[j:j + 1] for i, j in _STAIR]
                               + [jnp.full((npad, ts), -jnp.inf, F32)], axis=0)
        eid = jnp.concatenate([i1[i:i + 1] * PEER_KEYS + i2[j:j + 1] for i, j in _STAIR]
                              + [jnp.zeros((npad, ts), jnp.int32)], axis=0)
        top, pos = _topk_rows(cand, PEER_TOPK)
        rows = lax.broadcasted_iota(jnp.int32, cand.shape, 0)
        sel = jnp.concatenate([jnp.sum(jnp.where(rows == pos[r:r + 1], eid, 0), axis=0, keepdims=True)
                               for r in range(PEER_TOPK)], axis=0)
        e = jnp.exp(top - top[0:1])
        gate_scr[h] = e / jnp.sum(e, axis=0, keepdims=True)
        idx_scr[h] = sel
        return 0

    lax.fori_loop(0, PEER_HEADS, head, 0)
    idx_ref[...] = idx_scr[...].reshape(PEER_SEL, ts).T
    gate_ref[...] = gate_scr[...].reshape(PEER_SEL, ts).T


def _post(x, dout, sout, w_out_b, g_ffn, w_query_b, sub_keys_b, tok0, t):
    ts = min(256, t)
    first = tok0 // ts
    row = pl.BlockSpec((ts, D_MODEL), lambda i: (i, 0))
    hm = pl.BlockSpec((1, HEADS, ts, HEAD_DIM), lambda i: (0, 0, i, 0))
    sel = pl.BlockSpec((ts, PEER_SEL), lambda i: (i, 0))
    full = lambda *shape: pl.BlockSpec(shape, lambda i: (0,) * len(shape))
    return pl.pallas_call(
        _post_body,
        grid=(t // ts,),
        in_specs=[pl.BlockSpec((ts, D_MODEL), lambda i: (first + i, 0)),
                  hm, hm, full(2 * MIX, D_MODEL), full(1, D_MODEL),
                  full(D_MODEL, 2 * PEER_HEADS * PEER_HALF), full(2, PEER_KEYS, PEER_HALF)],
        out_specs=[row, row, sel, sel],
        out_shape=[jax.ShapeDtypeStruct((t, D_MODEL), F32), jax.ShapeDtypeStruct((t, D_MODEL), F32),
                   jax.ShapeDtypeStruct((t, PEER_SEL), jnp.int32), jax.ShapeDtypeStruct((t, PEER_SEL), F32)],
        scratch_shapes=[pltpu.VMEM((2 * PEER_HEADS, ts, PEER_HALF), BF16),
                        pltpu.VMEM((PEER_HEADS, PEER_TOPK, ts), jnp.int32),
                        pltpu.VMEM((PEER_HEADS, PEER_TOPK, ts), F32)],
        compiler_params=_params("parallel"),
        name="post_peer_select",
    )(x, dout, sout, w_out_b, g_ffn, w_query_b, sub_keys_b)


def _coef_body(after_ref, gate_ref, dots_ref, o_ref):
    del after_ref
    d = dots_ref[...]
    o_ref[...] = gate_ref[...] * (0.5 * d * (1.0 + lax.erf(d * (2.0 ** -0.5))))


def _coef(gate, dots, after):
    t = gate.shape[0]
    ts = min(2048, t)
    blk = pl.BlockSpec((ts, PEER_SEL), lambda i: (i, 0))
    return pl.pallas_call(
        _coef_body, grid=(t // ts,), in_specs=[pl.BlockSpec(memory_space=pl.ANY), blk, blk], out_specs=blk,
        out_shape=jax.ShapeDtypeStruct((t, PEER_SEL), F32),
        compiler_params=_params("parallel"), name="peer_coef",
    )(after, gate, dots)


SC_CORES = 2
SC_SUBCORES = 16
SC_LANES = 16
SC_WORKERS = SC_CORES * SC_SUBCORES
SC_ROWS = 16
SC_BUFS = 5
SC_GROUP = 8
SC_CHUNKS = PEER_SEL // SC_ROWS
SC_STEPS = SC_GROUP * SC_CHUNKS
SC_VECS = D_MODEL // SC_LANES


def _sc_mesh():
    return plsc.VectorSubcoreMesh(core_axis_name="c", subcore_axis_name="s",
                                  num_cores=SC_CORES, num_subcores=SC_SUBCORES)


def _sc_walk(table_hbm, idx_hbm, aux_hbm, idx_v, aux_v, rows_v, sem, stage_sem, tpw, begin_group, compute, end_group):
    tok_base = (lax.axis_index("s") * SC_CORES + lax.axis_index("c")) * tpw
    ngroups = tpw // SC_GROUP
    nsteps = tpw * SC_CHUNKS

    def first_token(g):
        return pl.multiple_of(tok_base + g * SC_GROUP, SC_GROUP)

    def stage(g):
        tok0 = first_token(g)
        return (pltpu.make_async_copy(idx_hbm.at[pl.ds(tok0 * SC_CHUNKS, SC_STEPS)], idx_v.at[g % 2], stage_sem.at[0]),
                pltpu.make_async_copy(aux_hbm.at[pl.ds(tok0, SC_GROUP)],
                                      aux_v.at[g % 2, :, pl.ds(0, aux_hbm.shape[1])], stage_sem.at[1]))

    def gather(step):
        idx = idx_v.at[(step // SC_STEPS) % 2, step % SC_STEPS]
        return pltpu.make_async_copy(table_hbm.at[idx], rows_v.at[step % SC_BUFS], sem.at[step % SC_BUFS])

    for cp in stage(0):
        cp.start()
    for cp in stage(0):
        cp.wait()
    for step in range(SC_BUFS - 1):
        gather(step).start()

    def walk(step, _):
        g = step // SC_STEPS
        local = step % SC_STEPS

        @pl.when(jnp.logical_and(local == 0, g + 1 < ngroups))
        def _():
            for cp in stage(g + 1):
                cp.start()

        ahead = step + (SC_BUFS - 1)

        @pl.when(ahead < nsteps)
        def _():
            @pl.when(ahead % SC_STEPS == 0)
            def _():
                for cp in stage(ahead // SC_STEPS):
                    cp.wait()

            gather(ahead).start()

        @pl.when(local == 0)
        def _():
            begin_group()

        gather(step).wait()
        compute(rows_v.at[step % SC_BUFS], g % 2, local // SC_CHUNKS, local % SC_CHUNKS)

        @pl.when(local == SC_STEPS - 1)
        def _():
            end_group(first_token(g))

        return 0

    lax.fori_loop(0, nsteps, walk, 0)


def _sc_dots_body(u_hbm, idx_hbm, c_hbm, out_hbm, idx_v, c_v, rows_v, dots_v, sem, stage_sem, *, tpw):
    lane = lax.broadcasted_iota(jnp.int32, (SC_LANES,), 0)
    zero = jnp.zeros((SC_LANES,), F32)

    def compute(rows, slot, tt, ch):
        for half in range(SC_ROWS // SC_LANES):
            outv = zero
            for q in range(2):
                r0 = half * SC_LANES + q * 8

                def vec(kk, accs):
                    off = pl.multiple_of(kk * SC_LANES, SC_LANES)
                    cv = c_v[slot, tt, pl.ds(off, SC_LANES)]
                    return tuple(a + rows[r0 + r, pl.ds(off, SC_LANES)] * cv for r, a in enumerate(accs))

                accs = lax.fori_loop(0, SC_VECS, vec, (zero,) * 8)
                for r in range(8):
                    outv = jnp.where(lane == q * 8 + r, jnp.sum(accs[r]), outv)
            dots_v[tt, pl.ds(pl.multiple_of(ch * SC_ROWS + half * SC_LANES, SC_LANES), SC_LANES)] = outv

    def end_group(tok0):
        pltpu.sync_copy(dots_v.at[:, pl.ds(0, PEER_SEL)], out_hbm.at[pl.ds(tok0, SC_GROUP)])

    _sc_walk(u_hbm, idx_hbm, c_hbm, idx_v, c_v, rows_v, sem, stage_sem, tpw, lambda: None, compute, end_group)


def _sc_combine_body(v_hbm, idx_hbm, coef_hbm, out_hbm, idx_v, coef_v, rows_v, acc_v, sem, stage_sem, *, tpw):
    lane = lax.broadcasted_iota(jnp.int32, (SC_LANES,), 0)
    zero = jnp.zeros((SC_LANES,), F32)

    def compute(rows, slot, tt, ch):
        for half in range(SC_ROWS // SC_LANES):
            cf = coef_v[slot, tt, pl.ds(pl.multiple_of(ch * SC_ROWS + half * SC_LANES, SC_LANES), SC_LANES)]
            splat = [jnp.full((SC_LANES,), jnp.sum(jnp.where(lane == r, cf, 0.0)), F32) for r in range(SC_LANES)]

            @plsc.parallel_loop(0, SC_VECS, unroll=2)
            def _(kk):
                off = pl.multiple_of(kk * SC_LANES, SC_LANES)
                terms = [rows[half * SC_LANES + r, pl.ds(off, SC_LANES)] * splat[r] for r in range(SC_LANES)]
                while len(terms) > 1:
                    terms = [a + b for a, b in zip(terms[0::2], terms[1::2])]
                acc_v[tt, pl.ds(off, SC_LANES)] = acc_v[tt, pl.ds(off, SC_LANES)] + terms[0]

    def begin_group():
        def clear(i, _):
            acc_v[i // SC_VECS, pl.ds(pl.multiple_of((i % SC_VECS) * SC_LANES, SC_LANES), SC_LANES)] = zero
            return 0

        lax.fori_loop(0, SC_GROUP * SC_VECS, clear, 0)

    def end_group(tok0):
        pltpu.sync_copy(acc_v, out_hbm.at[pl.ds(tok0, SC_GROUP)])

    _sc_walk(v_hbm, idx_hbm, coef_hbm, idx_v, coef_v, rows_v, sem, stage_sem, tpw, begin_group, compute, end_group)


def _sc_call(body, table, idx, per_token, out_width, name):
    t = per_token.shape[0]
    tpw = t // SC_WORKERS
    assert tpw % SC_GROUP == 0
    return pl.kernel(
        functools.partial(body, tpw=tpw),
        out_type=jax.ShapeDtypeStruct((t, out_width), F32),
        mesh=_sc_mesh(),
        scratch_types=[pltpu.VMEM((2, SC_STEPS, SC_ROWS), jnp.int32),
                       pltpu.VMEM((2, SC_GROUP, per_token.shape[1]), F32),
                       pltpu.VMEM((SC_BUFS, SC_ROWS, D_MODEL), F32),
                       pltpu.VMEM((SC_GROUP, out_width), F32),
                       pltpu.SemaphoreType.DMA((SC_BUFS,)),
                       pltpu.SemaphoreType.DMA((2,))],
        compiler_params=pltpu.CompilerParams(needs_layout_passes=False),
        name=name,
    )(table, idx.reshape(t * SC_CHUNKS, SC_ROWS), per_token)


def _sc_step_body(v_hbm, idx_a_hbm, coef_hbm, u_hbm, idx_b_hbm, c_hbm, peer_hbm, dots_hbm,
                  idx_v, aux_v, rows_v, out_v, sem, stage_sem, *, tpw_a, tpw_b):
    _sc_combine_body(v_hbm, idx_a_hbm, coef_hbm, peer_hbm, idx_v, aux_v, rows_v, out_v, sem, stage_sem, tpw=tpw_a)
    _sc_dots_body(u_hbm, idx_b_hbm, c_hbm, dots_hbm, idx_v, aux_v, rows_v, out_v, sem, stage_sem, tpw=tpw_b)


def _sc_step(expert_v, idx_a, coef, expert_u, idx_b, c):
    ta, tb = coef.shape[0], c.shape[0]
    assert ta % (SC_WORKERS * SC_GROUP) == 0 and tb % (SC_WORKERS * SC_GROUP) == 0
    return pl.kernel(
        functools.partial(_sc_step_body, tpw_a=ta // SC_WORKERS, tpw_b=tb // SC_WORKERS),
        out_type=[jax.ShapeDtypeStruct((ta, D_MODEL), F32), jax.ShapeDtypeStruct((tb, PEER_SEL), F32)],
        mesh=_sc_mesh(),
        scratch_types=[pltpu.VMEM((2, SC_STEPS, SC_ROWS), jnp.int32),
                       pltpu.VMEM((2, SC_GROUP, D_MODEL), F32),
                       pltpu.VMEM((SC_BUFS, SC_ROWS, D_MODEL), F32),
                       pltpu.VMEM((SC_GROUP, D_MODEL), F32),
                       pltpu.SemaphoreType.DMA((SC_BUFS,)),
                       pltpu.SemaphoreType.DMA((2,))],
        compiler_params=pltpu.CompilerParams(needs_layout_passes=False),
        name="peer_step",
    )(expert_v, idx_a.reshape(ta * SC_CHUNKS, SC_ROWS), coef, expert_u, idx_b.reshape(tb * SC_CHUNKS, SC_ROWS), c)


def _sc_dots(expert_u, idx, c):
    return _sc_call(_sc_dots_body, expert_u, idx, c, PEER_SEL, "peer_dots")


def _sc_combine(expert_v, idx, coef):
    return _sc_call(_sc_combine_body, expert_v, idx, coef, D_MODEL, "peer_combine")


def _ple_body(h1_ref, peer_ref, p_ref, gp_ref, wg_ref, we_ref, gfin_ref, y_ref):
    h = h1_ref[...] + peer_ref[...]
    a = _rms(h, gp_ref[...], NORM_EPS).astype(BF16)
    gate = jax.nn.sigmoid(_mm(a, wg_ref[...]))
    h = h + _mm(p_ref[...].astype(BF16), we_ref[...]) * gate
    y_ref[...] = _rms(h, gfin_ref[...], NORM_EPS)


def _ple(h1, peer, p, g_ple, w_pgate_b, w_ple_b, g_final, tok0):
    t = h1.shape[0]
    ts = min(512, t)
    first = tok0 // ts
    row = pl.BlockSpec((ts, D_MODEL), lambda i: (i, 0))
    full = lambda *shape: pl.BlockSpec(shape, lambda i: (0,) * len(shape))
    return pl.pallas_call(
        _ple_body,
        grid=(t // ts,),
        in_specs=[row, row, pl.BlockSpec((ts, PLE_DIM), lambda i: (first + i, 0)), full(1, D_MODEL),
                  full(D_MODEL, D_MODEL), full(PLE_DIM, D_MODEL), full(1, D_MODEL)],
        out_specs=row,
        out_shape=jax.ShapeDtypeStruct((t, D_MODEL), F32),
        compiler_params=_params("parallel"),
        name="ple_final",
    )(h1, peer, p, g_ple, w_pgate_b, w_ple_b, g_final)


def _rope_tables(pos):
    half = DIFF_COMP // 2
    inv = ROPE_THETA ** (-jnp.arange(0, DIFF_COMP, 2, dtype=F32) / DIFF_COMP)
    ang = pos.astype(F32)[:, None] * inv[None, :]
    cos = jnp.cos(ang)
    sin = jnp.sin(ang)
    reps = LANES // DIFF_COMP
    del half
    return (jnp.tile(jnp.concatenate([cos, cos], axis=-1), (1, reps)),
            jnp.tile(jnp.concatenate([-sin, sin], axis=-1), (1, reps)))


def kernel(x_prompt, x_sample, cache_diff_k, cache_diff_v, cache_sb_k, cache_sb_v, p_prompt, p_sample, g_mix, w_in, lambda_q1, lambda_k1, lambda_q2, lambda_k2, g_subln, w_out, g_ffn, w_query, sub_keys, expert_u, expert_v, g_ple, w_pgate, w_ple, g_final):
    assert w_in.shape[0] == 1, "single-layer encoder"
    nb, seq, _ = x_prompt.shape
    db, dq, _ = x_sample.shape
    past = cache_diff_k.shape[2]

    lam = (jnp.exp(jnp.sum(lambda_q1[0].astype(F32) * lambda_k1[0].astype(F32)))
           - jnp.exp(jnp.sum(lambda_q2[0].astype(F32) * lambda_k2[0].astype(F32))) + LAM_INIT).reshape(1)
    w_in_b = w_in[0].astype(BF16)
    w_out_b = w_out[0].astype(BF16)
    w_query_b = w_query[0].astype(BF16)
    sub_keys_b = sub_keys[0].astype(BF16)
    w_pgate_b = w_pgate[0].astype(BF16)
    w_ple_b = w_ple[0].astype(BF16)
    g_sub = g_subln[0].reshape(1, HEAD_DIM)
    g_fin = g_final.reshape(1, D_MODEL)

    def select(x, dout, sout, b, s):
        return _post(x, dout, sout, w_out_b, g_ffn, w_query_b, sub_keys_b, b, s)

    def finish(h1, peer, p, b):
        return _ple(h1, peer, p, g_ple, w_pgate_b, w_ple_b, g_fin, b)

    ts = db * dq
    xs = x_sample.reshape(ts, D_MODEL)
    cos_s, sin_s = _rope_tables(jnp.tile(past + jnp.arange(dq, dtype=jnp.int32), db))
    kd, vd, ks, vs, qd2, kdb, vdb, qsb, ksb, vsb = _proj(xs, g_mix, w_in_b, cos_s, sin_s, 1, ts, False)
    caches = [c[0].reshape(db, past, MIX) for c in (cache_diff_k, cache_diff_v, cache_sb_k, cache_sb_v)]
    dout_s, sout_s = _sample_attention(lam, qd2, kdb, vdb, qsb, ksb, vsb, *caches, g_sub, db, dq)
    rows_s = tuple(r.reshape(1, db, dq, HEADS, HEAD_DIM) for r in (kd, vd, ks, vs))

    xp = x_prompt.reshape(nb * seq, D_MODEL)
    pp = p_prompt[0].reshape(nb * seq, PLE_DIM)
    cos_p, sin_p = _rope_tables(jnp.arange(seq, dtype=jnp.int32))
    kd, vd, ks, vs, qd2, kdb, vdb, qsb, ksb, vsb = _proj(xp, g_mix, w_in_b, cos_p, sin_p, nb, seq, True)
    rows_p = tuple(r.reshape(nb, HEADS, HEAD_DIM, seq).transpose(0, 3, 1, 2)[None] for r in (kd, vd, ks, vs))

    cut = seq // PROMPT_ROW_BLOCKS
    spans = [(b, i * cut, cut) for b in range(nb) for i in range(PROMPT_ROW_BLOCKS)]

    def prompt_block(k):
        b, t0, n = spans[k]
        dout = _diff_attention(lam, qd2, kdb, vdb, g_sub, b, t0, n)
        sout = _sb_attention(qsb, ksb, vsb, b, t0, n)
        return select(xp, dout, sout, b * seq + t0, n)

    blocks = [functools.partial(prompt_block, k) for k in range(len(spans))]
    blocks.append(lambda: select(xs, dout_s, sout_s, 0, ts))
    ys = []
    h1, c, idx, gate = blocks[0]()
    dots = _sc_dots(expert_u[0], idx, c)
    for k in range(1, len(blocks)):
        nh1, nc, nidx, ngate = blocks[k]()
        coef = _coef(gate, dots, nh1)
        peer, ndots = _sc_step(expert_v[0], idx, coef, expert_u[0], nidx, nc)
        b, t0, _ = spans[k - 1]
        ys.append(finish(h1, peer, pp, b * seq + t0))
        h1, idx, gate, dots = nh1, nidx, ngate, ndots
    coef = _coef(gate, dots, lam)
    y_sample = finish(h1, _sc_combine(expert_v[0], idx, coef), p_sample[0].reshape(ts, PLE_DIM), 0)
    y_sample = y_sample.reshape(db, dq, D_MODEL)
    y_prompt = jnp.concatenate(ys, axis=0).reshape(nb, seq, D_MODEL)

    return (y_prompt, y_sample) + rows_p + rows_s
```

```python
import functools
import math

import jax
import jax.numpy as jnp
from jax import lax
from jax.experimental import pallas as pl
from jax.experimental.pallas import tpu as pltpu
from jax.experimental.pallas import tpu_sc as plsc

F32 = jnp.float32
BF16 = jnp.bfloat16

D_MODEL = 1024
HEADS = 8
HEAD_DIM = 64
DIFF_COMP = 32
MIX = HEADS * HEAD_DIM
CHUNK = 64
ROPE_THETA = 10000.0
NORM_EPS = 1e-6
SUBLN_EPS = 1e-5
PEER_HEADS = 8
PEER_KEYS = 128
PEER_TOPK = 16
PEER_HALF = 128
PEER_SEL = PEER_HEADS * PEER_TOPK
PLE_DIM = 256
LAM_INIT = 0.8 - 0.6 * math.exp(-0.3 * 0)
SB_LOG_FLOOR = -104.0
PROMPT_ROW_BLOCKS = 4

LANES = 128
VMEM_LIMIT = 48 * 1024 * 1024

NT_DIMS = (((1,), (1,)), ((), ()))


def _nt(a, b):
    return lax.dot_general(a, b, NT_DIMS, preferred_element_type=F32)


def _mm(a, b):
    return jnp.dot(a, b, preferred_element_type=F32)


def _rms(x, g, eps):
    return x * lax.rsqrt(jnp.mean(x * x, axis=-1, keepdims=True) + eps) * g


def _params(*sem):
    return pltpu.CompilerParams(dimension_semantics=sem, vmem_limit_bytes=VMEM_LIMIT)


def _with_ones(v):
    n = v.shape[0]
    ones = (lax.broadcasted_iota(jnp.int32, (n, HEAD_DIM), 1) == 0).astype(v.dtype)
    return jnp.concatenate([v, ones], axis=1)


def _proj_body(x_ref, g_ref, w_ref, cos_ref, sin_ref,
               kd_ref, vd_ref, ks_ref, vs_ref,
               qd2_ref, kdb_ref, vdb_ref, qsb_ref, ksb_ref, vsb_ref, *, feature_major):
    ts = x_ref.shape[0]
    a = _rms(x_ref[...], g_ref[...], NORM_EPS).astype(BF16)
    cos = jnp.tile(cos_ref[...], (1, MIX // LANES))
    sin = jnp.tile(sin_ref[...], (1, MIX // LANES))
    lane = lax.broadcasted_iota(jnp.int32, (ts, MIX), 1)
    first_half = (lane % DIFF_COMP) < (DIFF_COMP // 2)

    def group(i):
        return _mm(a, w_ref[:, i * MIX:(i + 1) * MIX])

    def rope(t):
        partner = jnp.where(first_half,
                            pltpu.roll(t, MIX - DIFF_COMP // 2, 1),
                            pltpu.roll(t, DIFF_COMP // 2, 1))
        return t * cos + partner * sin

    qd = rope(group(0)) * (DIFF_COMP ** -0.5)
    kd = rope(group(1))
    vd = group(2)
    qs = group(3) * (HEAD_DIM ** -0.5)
    ks = group(4)
    vs = group(5)
    for ref, rows in ((kd_ref, kd), (vd_ref, vd), (ks_ref, ks), (vs_ref, vs)):
        if feature_major:
            ref[0] = rows.T
        else:
            ref[...] = rows
    comp0 =lax.broadcasted_iota(jnp.int32, (ts, HEAD_DIM), 1) < DIFF_COMP
    for h in range(HEADS):
        sl = slice(h * HEAD_DIM, (h + 1) * HEAD_DIM)
        qh = qd[:, sl]
        qd2_ref[0, h, 0] = jnp.where(comp0, qh, 0.0).astype(BF16)
        qd2_ref[0, h, 1] = jnp.where(comp0, 0.0, qh).astype(BF16)
        kdb_ref[0, h] = kd[:, sl].astype(BF16)
        vdb_ref[0, h] = _with_ones(vd[:, sl].astype(BF16))
        qsb_ref[0, h] = qs[:, sl].astype(BF16)
        ksb_ref[0, h] = ks[:, sl].astype(BF16)
        vsb_ref[0, h] = vs[:, sl].astype(BF16)


def _proj(x, g_mix, w_in_b, cos_t, sin_t, nb, seq, feature_major):
    t = nb * seq
    ts = min(256, seq)
    nst = seq // ts
    if feature_major:
        row = pl.BlockSpec((1, MIX, ts), lambda i: (i // nst, 0, i % nst))
        rows = jax.ShapeDtypeStruct((nb, MIX, seq), F32)
    else:
        row = pl.BlockSpec((ts, MIX), lambda i: (i, 0))
        rows = jax.ShapeDtypeStruct((t, MIX), F32)
    hm = pl.BlockSpec((1, HEADS, ts, HEAD_DIM), lambda i: (i // nst, 0, i % nst, 0))
    hm2 = pl.BlockSpec((1, HEADS, 2, ts, HEAD_DIM), lambda i: (i // nst, 0, 0, i % nst, 0))
    heads =jax.ShapeDtypeStruct((nb, HEADS, seq, HEAD_DIM), BF16)
    heads2 = jax.ShapeDtypeStruct((nb, HEADS, 2, seq, HEAD_DIM), BF16)
    hm_ext = pl.BlockSpec((1, HEADS, ts, 2 * HEAD_DIM), lambda i: (i // nst, 0, i % nst, 0))
    heads_ext = jax.ShapeDtypeStruct((nb, HEADS, seq, 2 * HEAD_DIM), BF16)
    return pl.pallas_call(
        functools.partial(_proj_body, feature_major=feature_major),
        grid=(t // ts,),
        in_specs=[
            pl.BlockSpec((ts, D_MODEL), lambda i: (i, 0)),
            pl.BlockSpec((1, D_MODEL), lambda i: (0, 0)),
            pl.BlockSpec((D_MODEL, 6 * MIX), lambda i: (0, 0)),
            pl.BlockSpec((ts, LANES), lambda i: (i % nst, 0)),
            pl.BlockSpec((ts, LANES), lambda i: (i % nst, 0)),
        ],
        out_specs=[row, row, row, row, hm2, hm, hm_ext, hm, hm, hm],
        out_shape=[rows, rows, rows, rows, heads2, heads, heads_ext, heads, heads, heads],
        compiler_params=_params("parallel"),
        name="proj",
    )(x, g_mix, w_in_b, cos_t, sin_t)


def _diff_init(rows):
    return jnp.full((rows, 1), -jnp.inf, F32), jnp.zeros((rows, 2 * HEAD_DIM), F32)


def _diff_update(s, v_ext, carry):
    m, acc = carry
    m_new = jnp.maximum(m, jnp.max(s, axis=-1, keepdims=True))
    p = jnp.exp(s - m_new)
    acc = jnp.exp(m - m_new) * acc + _mm(p.astype(BF16), v_ext)
    return m_new, acc


def _diff_finish(carry, lam, g_subln, tq):
    _, acc = carry
    o = acc[:, :HEAD_DIM] / acc[:, HEAD_DIM:HEAD_DIM + 1]
    d = o[:tq] - lam * o[tq:]
    return _rms(d, g_subln, SUBLN_EPS) * (1.0 - LAM_INIT)


def _suffix_sums(lk, tri):
    hi = lk.astype(BF16)
    lo = (lk - hi.astype(F32)).astype(BF16)
    return _mm(hi, tri) + _mm(lo, tri)


def _sb_update(q, k, v, tri, earlier, carry):
    run, acc = carry
    z = _nt(q, k)
    sp = jnp.maximum(z, 0.0) + jnp.log1p(jnp.exp(-jnp.abs(z)))
    lk = -sp if earlier is None else jnp.where(earlier, -sp, 0.0)
    after = _suffix_sums(lk, tri)
    w = jnp.exp((z - sp) + after + run)
    if earlier is not None:
        w = jnp.where(earlier, w, 0.0)
    acc = acc + _mm(w.astype(BF16), v)
    run = run + after[:, 0:1] + lk[:, 0:1]
    return run, acc


def _tri(n):
    j = lax.broadcasted_iota(jnp.int32, (n, n), 0)
    s = lax.broadcasted_iota(jnp.int32, (n, n), 1)
    return (j > s).astype(BF16)


def _diff_body(lam_ref, q_ref, k_ref, v_ref, g_ref, o_ref, *, tq, q0):
    qi = q0 + pl.program_id(1)
    q2 = q_ref[0, 0].reshape(2 * tq, HEAD_DIM)

    def scores(j):
        return _nt(q2, k_ref[0, 0, pl.ds(pl.multiple_of(j * tq, tq), tq), :])

    def step(j, state):
        s, carry = state
        s_next = scores(j + 1)
        return s_next, _diff_update(s, v_ref[0, 0, pl.ds(pl.multiple_of(j * tq, tq), tq), :], carry)

    s, carry = lax.fori_loop(0, qi, step, (scores(0), _diff_init(2 * tq)))
    r = lax.broadcasted_iota(jnp.int32, (2 * tq, tq), 0) % tq
    c = lax.broadcasted_iota(jnp.int32, (2 * tq, tq), 1)
    s = jnp.where((c // CHUNK) <= (r // CHUNK), s, -jnp.inf)
    carry = _diff_update(s, v_ref[0, 0, pl.ds(pl.multiple_of(qi * tq, tq), tq), :], carry)
    o_ref[0, 0] = _diff_finish(carry, lam_ref[0], g_ref[...], tq)


def _diff_attention(lam, qd2, kdb, vdb, g_subln, b, t0, n):
    seq = qd2.shape[3]
    tq = min(512, n)
    q0 = t0 // tq
    return pl.pallas_call(
        functools.partial(_diff_body, tq=tq, q0=q0),
        grid=(HEADS, n // tq),
        in_specs=[
            pl.BlockSpec(memory_space=pltpu.SMEM),
            pl.BlockSpec((1, 1, 2, tq, HEAD_DIM), lambda h, i: (b, h, 0, q0 + i, 0)),
            pl.BlockSpec((1, 1, seq, HEAD_DIM), lambda h, i: (b, h, 0, 0)),
            pl.BlockSpec((1, 1, seq, 2 * HEAD_DIM), lambda h, i: (b, h, 0, 0)),
            pl.BlockSpec((1, HEAD_DIM), lambda h, i: (0, 0)),
        ],
        out_specs=pl.BlockSpec((1, 1, tq, HEAD_DIM), lambda h, i: (0, h, i, 0)),
        out_shape=jax.ShapeDtypeStruct((1, HEADS, n, HEAD_DIM), F32),
        compiler_params=_params("parallel", "arbitrary"),
        name="diff_attention",
    )(lam, qd2, kdb, vdb, g_subln)


def _sb_body(q_ref, k_ref, v_ref, o_ref, *, tq, q0):
    qi = q0 + pl.program_id(1)
    q = q_ref[0, 0]
    tri = _tri(tq)

    def tile(j):
        start = pl.multiple_of(j * tq, tq)
        return k_ref[0, 0, pl.ds(start, tq), :], v_ref[0, 0, pl.ds(start, tq), :]

    r = lax.broadcasted_iota(jnp.int32, (tq, tq), 0)
    c = lax.broadcasted_iota(jnp.int32, (tq, tq), 1)
    carry = (jnp.zeros((tq, 1), F32), jnp.zeros((tq, HEAD_DIM), F32))
    run, acc = _sb_update(q, *tile(qi), tri, c < r, carry)

    def live(state):
        j, run, _ = state
        return jnp.logical_and(j >= 0, jnp.max(run) > SB_LOG_FLOOR)

    def step(state):
        j, run, acc = state
        run, acc = _sb_update(q, *tile(j), tri, None, (run, acc))
        return j - 1, run, acc

    o_ref[0, 0] = lax.while_loop(live, step, (qi - 1, run, acc))[2]


def _sb_attention(qsb, ksb, vsb, b, t0, n):
    seq = qsb.shape[2]
    tq = min(256, n)
    q0 = t0 // tq
    kv = pl.BlockSpec((1, 1, seq, HEAD_DIM), lambda h, i: (b, h, 0, 0))
    return pl.pallas_call(
        functools.partial(_sb_body, tq=tq, q0=q0),
        grid=(HEADS, n // tq),
        in_specs=[pl.BlockSpec((1, 1, tq, HEAD_DIM), lambda h, i: (b, h, q0 + i, 0)), kv, kv],
        out_specs=pl.BlockSpec((1, 1, tq, HEAD_DIM), lambda h, i: (0, h, i, 0)),
        out_shape=jax.ShapeDtypeStruct((1, HEADS, n, HEAD_DIM), F32),
        compiler_params=_params("parallel", "arbitrary"),
        name="sb_attention",
    )(qsb, ksb, vsb)


SAMPLE_HEADS = 4
SAMPLE_TILE = 256


def _sample_body(lam_ref, qd2_ref, kdn_ref, vdn_ref, qs_ref, ksn_ref, vsn_ref,
                 ckd_ref, cvd_ref, cks_ref, cvs_ref, g_ref, do_ref, so_ref, *, past, nq):
    lam = lam_ref[0]
    tri_c = _tri(SAMPLE_TILE)
    tri_n = _tri(nq)
    i2 = lax.broadcasted_iota(jnp.int32, (2 * nq, nq), 0) % nq
    j2 = lax.broadcasted_iota(jnp.int32, (2 * nq, nq), 1)
    visible_new = ((past + j2) // CHUNK) <= ((past + i2) // CHUNK)
    i1 = lax.broadcasted_iota(jnp.int32, (nq, nq), 0)
    j1 = lax.broadcasted_iota(jnp.int32, (nq, nq), 1)
    earlier_new = j1 < i1
    for h in range(SAMPLE_HEADS):
        sl = slice(h * HEAD_DIM, (h + 1) * HEAD_DIM)
        q2 = qd2_ref[0, h].reshape(2 * nq, HEAD_DIM)
        carry = _diff_update(_nt(q2, ckd_ref[0, :, sl].astype(BF16)),
                             _with_ones(cvd_ref[0, :, sl].astype(BF16)), _diff_init(2 * nq))
        s_new = jnp.where(visible_new, _nt(q2, kdn_ref[0, h]), -jnp.inf)
        carry = _diff_update(s_new, vdn_ref[0, h], carry)
        do_ref[0, h] = _diff_finish(carry, lam, g_ref[...], nq)
        q = qs_ref[0, h]
        carry = (jnp.zeros((nq, 1), F32), jnp.zeros((nq, HEAD_DIM), F32))
        carry = _sb_update(q, ksn_ref[0, h], vsn_ref[0, h], tri_n, earlier_new, carry)
        for t in reversed(range(past // SAMPLE_TILE)):
            rows = slice(t * SAMPLE_TILE, (t + 1) * SAMPLE_TILE)
            carry = _sb_update(q, cks_ref[0, rows, sl].astype(BF16), cvs_ref[0, rows, sl].astype(BF16),
                               tri_c, None, carry)
        so_ref[0, h] = carry[1]


def _sample_attention(lam, qd2, kdb, vdb, qsb, ksb, vsb, ckd, cvd, cks, cvs, g_subln, nb, nq):
    past = ckd.shape[1]
    nhg = HEADS // SAMPLE_HEADS
    hm = pl.BlockSpec((1, SAMPLE_HEADS, nq, HEAD_DIM), lambda b, g: (0, g, b, 0))
    hm2 = pl.BlockSpec((1, SAMPLE_HEADS, 2, nq, HEAD_DIM), lambda b, g: (0, g, 0, b, 0))
    cache = pl.BlockSpec((1, past, SAMPLE_HEADS * HEAD_DIM), lambda b, g: (b, 0, g))
    out = jax.ShapeDtypeStruct((1, HEADS, nb * nq, HEAD_DIM), F32)
    return pl.pallas_call(
        functools.partial(_sample_body, past=past, nq=nq),
        grid=(nb, nhg),
        in_specs=[pl.BlockSpec(memory_space=pltpu.SMEM), hm2, hm,
                  pl.BlockSpec((1, SAMPLE_HEADS, nq, 2 * HEAD_DIM), lambda b, g: (0, g, b, 0)), hm, hm, hm,
                  cache, cache, cache, cache,
                  pl.BlockSpec((1, HEAD_DIM), lambda b, g: (0, 0))],
        out_specs=[hm, hm],
        out_shape=[out, out],
        compiler_params=_params("parallel", "parallel"),
        name="sample_attention",
    )(lam, qd2, kdb, vdb, qsb, ksb, vsb, ckd, cvd, cks, cvs, g_subln)


def _topk_rows(s, k):
    n = s.shape[0]
    rows = lax.broadcasted_iota(jnp.int32, s.shape, 0)
    vals, ids = [], []
    for _ in range(k):
        m = jnp.max(s, axis=0, keepdims=True)
        i = jnp.min(jnp.where(s == m, rows, n), axis=0, keepdims=True)
        vals.append(m)
        ids.append(i)
        s = jnp.where(rows == i, -jnp.inf, s)
    return jnp.concatenate(vals, axis=0), jnp.concatenate(ids, axis=0)


_STAIR = [(i, j) for i in range(PEER_TOPK) for j in range(PEER_TOPK) if (i + 1) * (j + 1) <= PEER_TOPK]


def _post_body(x_ref, do_ref, so_ref, wo_ref, gf_ref, wq_ref, sk_ref,
               h1_ref, c_ref, idx_ref, gate_ref, q_scr, idx_scr, gate_scr):
    ts = x_ref.shape[0]
    mixed = jnp.zeros((ts, D_MODEL), F32)
    for h in range(HEADS):
        mixed += _mm(do_ref[0, h].astype(BF16), wo_ref[h * HEAD_DIM:(h + 1) * HEAD_DIM, :])
        mixed += _mm(so_ref[0, h].astype(BF16), wo_ref[MIX + h * HEAD_DIM:MIX + (h + 1) * HEAD_DIM, :])
    h1 = x_ref[...] + mixed
    h1_ref[...] = h1
    c = _rms(h1, gf_ref[...], NORM_EPS)
    c_ref[...] = c
    q = _mm(c.astype(BF16), wq_ref[...])
    for hp in range(2 * PEER_HEADS):
        q_scr[hp] = q[:, hp * PEER_HALF:(hp + 1) * PEER_HALF].astype(BF16)

    npad = -len(_STAIR) % 8

    def head(h, _):
        v1, i1 = _topk_rows(_nt(sk_ref[0], q_scr[2 * h]), PEER_TOPK)
        v2, i2 = _topk_rows(_nt(sk_ref[1], q_scr[2 * h + 1]), PEER_TOPK)
        cand = jnp.concatenate([v1[i:i + 1] + v2[j:j + 1] for i, j in _STAIR]
                               + [jnp.full((npad, ts), -jnp.inf, F32)], axis=0)
        eid = jnp.concatenate([i1[i:i + 1] * PEER_KEYS + i2[j:j + 1] for i, j in _STAIR]
                              + [jnp.zeros((npad, ts), jnp.int32)], axis=0)
        top, pos = _topk_rows(cand, PEER_TOPK)
        rows = lax.broadcasted_iota(jnp.int32, cand.shape, 0)
        sel = jnp.concatenate([jnp.sum(jnp.where(rows == pos[r:r + 1], eid, 0), axis=0, keepdims=True)
                               for r in range(PEER_TOPK)], axis=0)
        e = jnp.exp(top - top[0:1])
        gate_scr[h] = e / jnp.sum(e, axis=0, keepdims=True)
        idx_scr[h] = sel
        return 0

    lax.fori_loop(0, PEER_HEADS, head, 0)
    idx_ref[...] = idx_scr[...].reshape(PEER_SEL, ts).T
    gate_ref[...] = gate_scr[...].reshape(PEER_SEL, ts).T


def _post(x, dout, sout, w_out_b, g_ffn, w_query_b, sub_keys_b, tok0, t):
    ts = min(256, t)
    first = tok0 // ts
    row = pl.BlockSpec((ts, D_MODEL), lambda i: (i, 0))
    hm = pl.BlockSpec((1, HEADS, ts, HEAD_DIM), lambda i: (0, 0, i, 0))
    sel = pl.BlockSpec((ts, PEER_SEL), lambda i: (i, 0))
    full = lambda *shape: pl.BlockSpec(shape, lambda i: (0,) * len(shape))
    return pl.pallas_call(
        _post_body,
        grid=(t // ts,),
        in_specs=[pl.BlockSpec((ts, D_MODEL), lambda i: (first + i, 0)),
                  hm, hm, full(2 * MIX, D_MODEL), full(1, D_MODEL),
                  full(D_MODEL, 2 * PEER_HEADS * PEER_HALF), full(2, PEER_KEYS, PEER_HALF)],
        out_specs=[row, row, sel, sel],
        out_shape=[jax.ShapeDtypeStruct((t, D_MODEL), F32), jax.ShapeDtypeStruct((t, D_MODEL), F32),
                   jax.ShapeDtypeStruct((t, PEER_SEL), jnp.int32), jax.ShapeDtypeStruct((t, PEER_SEL), F32)],
        scratch_shapes=[pltpu.VMEM((2 * PEER_HEADS, ts, PEER_HALF), BF16),
                        pltpu.VMEM((PEER_HEADS, PEER_TOPK, ts), jnp.int32),
                        pltpu.VMEM((PEER_HEADS, PEER_TOPK, ts), F32)],
        compiler_params=_params("parallel"),
        name="post_peer_select",
    )(x, dout, sout, w_out_b, g_ffn, w_query_b, sub_keys_b)


def _coef_body(after_ref, gate_ref, dots_ref, o_ref):
    del after_ref
    d = dots_ref[...]
    o_ref[...] = gate_ref[...] * (0.5 * d * (1.0 + lax.erf(d * (2.0 ** -0.5))))


def _coef(gate, dots, after):
    t = gate.shape[0]
    ts = min(2048, t)
    blk = pl.BlockSpec((ts, PEER_SEL), lambda i: (i, 0))
    return pl.pallas_call(
        _coef_body, grid=(t // ts,), in_specs=[pl.BlockSpec(memory_space=pl.ANY), blk, blk], out_specs=blk,
        out_shape=jax.ShapeDtypeStruct((t, PEER_SEL), F32),
        compiler_params=_params("parallel"), name="peer_coef",
    )(after, gate, dots)


SC_CORES = 2
SC_SUBCORES = 16
SC_LANES = 16
SC_WORKERS = SC_CORES * SC_SUBCORES
SC_ROWS = 16
SC_BUFS = 5
SC_GROUP = 8
SC_CHUNKS = PEER_SEL // SC_ROWS
SC_STEPS = SC_GROUP * SC_CHUNKS
SC_VECS = D_MODEL // SC_LANES


def _sc_mesh():
    return plsc.VectorSubcoreMesh(core_axis_name="c", subcore_axis_name="s",
                                  num_cores=SC_CORES, num_subcores=SC_SUBCORES)


def _sc_walk(table_hbm, idx_hbm, aux_hbm, idx_v, aux_v, rows_v, sem, stage_sem, tpw, begin_group, compute, end_group):
    tok_base = (lax.axis_index("s") * SC_CORES + lax.axis_index("c")) * tpw
    ngroups = tpw // SC_GROUP
    nsteps = tpw * SC_CHUNKS

    def first_token(g):
        return pl.multiple_of(tok_base + g * SC_GROUP, SC_GROUP)

    def stage(g):
        tok0 = first_token(g)
        return (pltpu.make_async_copy(idx_hbm.at[pl.ds(tok0 * SC_CHUNKS, SC_STEPS)], idx_v.at[g % 2], stage_sem.at[0]),
                pltpu.make_async_copy(aux_hbm.at[pl.ds(tok0, SC_GROUP)],
                                      aux_v.at[g % 2, :, pl.ds(0, aux_hbm.shape[1])], stage_sem.at[1]))

    def gather(step):
        idx = idx_v.at[(step // SC_STEPS) % 2, step % SC_STEPS]
        return pltpu.make_async_copy(table_hbm.at[idx], rows_v.at[step % SC_BUFS], sem.at[step % SC_BUFS])

    for cp in stage(0):
        cp.start()
    for cp in stage(0):
        cp.wait()
    for step in range(SC_BUFS - 1):
        gather(step).start()

    def walk(step, _):
        g = step // SC_STEPS
        local = step % SC_STEPS

        @pl.when(jnp.logical_and(local == 0, g + 1 < ngroups))
        def _():
            for cp in stage(g + 1):
                cp.start()

        ahead = step + (SC_BUFS - 1)

        @pl.when(ahead < nsteps)
        def _():
            @pl.when(ahead % SC_STEPS == 0)
            def _():
                for cp in stage(ahead // SC_STEPS):
                    cp.wait()

            gather(ahead).start()

        @pl.when(local == 0)
        def _():
            begin_group()

        gather(step).wait()
        compute(rows_v.at[step % SC_BUFS], g % 2, local // SC_CHUNKS, local % SC_CHUNKS)

        @pl.when(local == SC_STEPS - 1)
        def _():
            end_group(first_token(g))

        return 0

    lax.fori_loop(0, nsteps, walk, 0)


def _sc_dots_body(u_hbm, idx_hbm, c_hbm, out_hbm, idx_v, c_v, rows_v, dots_v, sem, stage_sem, *, tpw):
    lane = lax.broadcasted_iota(jnp.int32, (SC_LANES,), 0)
    zero = jnp.zeros((SC_LANES,), F32)

    def compute(rows, slot, tt, ch):
        for half in range(SC_ROWS // SC_LANES):
            r0 = half * SC_LANES

            def vec(kk, accs):
                off = pl.multiple_of(kk * SC_LANES, SC_LANES)
                cv = c_v[slot, tt, pl.ds(off, SC_LANES)]
                return tuple(a + rows[r0 + r, pl.ds(off, SC_LANES)] * cv for r, a in enumerate(accs))

            accs = lax.fori_loop(0, SC_VECS, vec, (zero,) * SC_LANES)
            outv = zero
            for r in range(SC_LANES):
                outv = jnp.where(lane == r, jnp.sum(accs[r]), outv)
            dots_v[tt, pl.ds(pl.multiple_of(ch * SC_ROWS + half * SC_LANES, SC_LANES), SC_LANES)] = outv

    def end_group(tok0):
        pltpu.sync_copy(dots_v.at[:, pl.ds(0, PEER_SEL)], out_hbm.at[pl.ds(tok0, SC_GROUP)])

    _sc_walk(u_hbm, idx_hbm, c_hbm, idx_v, c_v, rows_v, sem, stage_sem, tpw, lambda: None, compute, end_group)


def _sc_combine_body(v_hbm, idx_hbm, coef_hbm, out_hbm, idx_v, coef_v, rows_v, acc_v, sem, stage_sem, *, tpw):
    lane = lax.broadcasted_iota(jnp.int32, (SC_LANES,), 0)
    zero = jnp.zeros((SC_LANES,), F32)

    def compute(rows, slot, tt, ch):
        for half in range(SC_ROWS // SC_LANES):
            cf = coef_v[slot, tt, pl.ds(pl.multiple_of(ch * SC_ROWS + half * SC_LANES, SC_LANES), SC_LANES)]
            splat = [jnp.full((SC_LANES,), jnp.sum(jnp.where(lane == r, cf, 0.0)), F32) for r in range(SC_LANES)]

            @plsc.parallel_loop(0, SC_VECS, unroll=2)
            def _(kk):
                off = pl.multiple_of(kk * SC_LANES, SC_LANES)
                terms = [rows[half * SC_LANES + r, pl.ds(off, SC_LANES)] * splat[r] for r in range(SC_LANES)]
                while len(terms) > 1:
                    terms = [a + b for a, b in zip(terms[0::2], terms[1::2])]
                acc_v[tt, pl.ds(off, SC_LANES)] = acc_v[tt, pl.ds(off, SC_LANES)] + terms[0]

    def begin_group():
        def clear(i, _):
            acc_v[i // SC_VECS, pl.ds(pl.multiple_of((i % SC_VECS) * SC_LANES, SC_LANES), SC_LANES)] = zero
            return 0

        lax.fori_loop(0, SC_GROUP * SC_VECS, clear, 0)

    def end_group(tok0):
        pltpu.sync_copy(acc_v, out_hbm.at[pl.ds(tok0, SC_GROUP)])

    _sc_walk(v_hbm, idx_hbm, coef_hbm, idx_v, coef_v, rows_v, sem, stage_sem, tpw, begin_group, compute, end_group)


def _sc_call(body, table, idx, per_token, out_width, name):
    t = per_token.shape[0]
    tpw = t // SC_WORKERS
    assert tpw % SC_GROUP == 0
    return pl.kernel(
        functools.partial(body, tpw=tpw),
        out_type=jax.ShapeDtypeStruct((t, out_width), F32),
        mesh=_sc_mesh(),
        scratch_types=[pltpu.VMEM((2, SC_STEPS, SC_ROWS), jnp.int32),
                       pltpu.VMEM((2, SC_GROUP, per_token.shape[1]), F32),
                       pltpu.VMEM((SC_BUFS, SC_ROWS, D_MODEL), F32),
                       pltpu.VMEM((SC_GROUP, out_width), F32),
                       pltpu.SemaphoreType.DMA((SC_BUFS,)),
                       pltpu.SemaphoreType.DMA((2,))],
        compiler_params=pltpu.CompilerParams(needs_layout_passes=False),
        name=name,
    )(table, idx.reshape(t * SC_CHUNKS, SC_ROWS), per_token)


def _sc_step_body(v_hbm, idx_a_hbm, coef_hbm, u_hbm, idx_b_hbm, c_hbm, peer_hbm, dots_hbm,
                  idx_v, aux_v, rows_v, out_v, sem, stage_sem, *, tpw_a, tpw_b):
    _sc_combine_body(v_hbm, idx_a_hbm, coef_hbm, peer_hbm, idx_v, aux_v, rows_v, out_v, sem, stage_sem, tpw=tpw_a)
    _sc_dots_body(u_hbm, idx_b_hbm, c_hbm, dots_hbm, idx_v, aux_v, rows_v, out_v, sem, stage_sem, tpw=tpw_b)


def _sc_step(expert_v, idx_a, coef, expert_u, idx_b, c):
    ta, tb = coef.shape[0], c.shape[0]
    assert ta % (SC_WORKERS * SC_GROUP) == 0 and tb % (SC_WORKERS * SC_GROUP) == 0
    return pl.kernel(
        functools.partial(_sc_step_body, tpw_a=ta // SC_WORKERS, tpw_b=tb // SC_WORKERS),
        out_type=[jax.ShapeDtypeStruct((ta, D_MODEL), F32), jax.ShapeDtypeStruct((tb, PEER_SEL), F32)],
        mesh=_sc_mesh(),
        scratch_types=[pltpu.VMEM((2, SC_STEPS, SC_ROWS), jnp.int32),
                       pltpu.VMEM((2, SC_GROUP, D_MODEL), F32),
                       pltpu.VMEM((SC_BUFS, SC_ROWS, D_MODEL), F32),
                       pltpu.VMEM((SC_GROUP, D_MODEL), F32),
                       pltpu.SemaphoreType.DMA((SC_BUFS,)),
                       pltpu.SemaphoreType.DMA((2,))],
        compiler_params=pltpu.CompilerParams(needs_layout_passes=False),
        name="peer_step",
    )(expert_v, idx_a.reshape(ta * SC_CHUNKS, SC_ROWS), coef, expert_u, idx_b.reshape(tb * SC_CHUNKS, SC_ROWS), c)


def _sc_dots(expert_u, idx, c):
    return _sc_call(_sc_dots_body, expert_u, idx, c, PEER_SEL, "peer_dots")


def _sc_combine(expert_v, idx, coef):
    return _sc_call(_sc_combine_body, expert_v, idx, coef, D_MODEL, "peer_combine")


def _ple_body(h1_ref, peer_ref, p_ref, gp_ref, wg_ref, we_ref, gfin_ref, y_ref):
    h = h1_ref[...] + peer_ref[...]
    a = _rms(h, gp_ref[...], NORM_EPS).astype(BF16)
    gate = jax.nn.sigmoid(_mm(a, wg_ref[...]))
    h = h + _mm(p_ref[...].astype(BF16), we_ref[...]) * gate
    y_ref[...] = _rms(h, gfin_ref[...], NORM_EPS)


def _ple(h1, peer, p, g_ple, w_pgate_b, w_ple_b, g_final, tok0):
    t = h1.shape[0]
    ts = min(512, t)
    first = tok0 // ts
    row = pl.BlockSpec((ts, D_MODEL), lambda i: (i, 0))
    full = lambda *shape: pl.BlockSpec(shape, lambda i: (0,) * len(shape))
    return pl.pallas_call(
        _ple_body,
        grid=(t // ts,),
        in_specs=[row, row, pl.BlockSpec((ts, PLE_DIM), lambda i: (first + i, 0)), full(1, D_MODEL),
                  full(D_MODEL, D_MODEL), full(PLE_DIM, D_MODEL), full(1, D_MODEL)],
        out_specs=row,
        out_shape=jax.ShapeDtypeStruct((t, D_MODEL), F32),
        compiler_params=_params("parallel"),
        name="ple_final",
    )(h1, peer, p, g_ple, w_pgate_b, w_ple_b, g_final)


def _rope_tables(pos):
    half = DIFF_COMP // 2
    inv = ROPE_THETA ** (-jnp.arange(0, DIFF_COMP, 2, dtype=F32) / DIFF_COMP)
    ang = pos.astype(F32)[:, None] * inv[None, :]
    cos = jnp.cos(ang)
    sin = jnp.sin(ang)
    reps = LANES // DIFF_COMP
    del half
    return (jnp.tile(jnp.concatenate([cos, cos], axis=-1), (1, reps)),
            jnp.tile(jnp.concatenate([-sin, sin], axis=-1), (1, reps)))


def kernel(x_prompt, x_sample, cache_diff_k, cache_diff_v, cache_sb_k, cache_sb_v, p_prompt, p_sample, g_mix, w_in, lambda_q1, lambda_k1, lambda_q2, lambda_k2, g_subln, w_out, g_ffn, w_query, sub_keys, expert_u, expert_v, g_ple, w_pgate, w_ple, g_final):
    assert w_in.shape[0] == 1, "single-layer encoder"
    nb, seq, _ = x_prompt.shape
    db, dq, _ = x_sample.shape
    past = cache_diff_k.shape[2]

    lam = (jnp.exp(jnp.sum(lambda_q1[0].astype(F32) * lambda_k1[0].astype(F32)))
           - jnp.exp(jnp.sum(lambda_q2[0].astype(F32) * lambda_k2[0].astype(F32))) + LAM_INIT).reshape(1)
    w_in_b = w_in[0].astype(BF16)
    w_out_b = w_out[0].astype(BF16)
    w_query_b = w_query[0].astype(BF16)
    sub_keys_b = sub_keys[0].astype(BF16)
    w_pgate_b = w_pgate[0].astype(BF16)
    w_ple_b = w_ple[0].astype(BF16)
    g_sub = g_subln[0].reshape(1, HEAD_DIM)
    g_fin = g_final.reshape(1, D_MODEL)

    def select(x, dout, sout, b, s):
        return _post(x, dout, sout, w_out_b, g_ffn, w_query_b, sub_keys_b, b, s)

    def finish(h1, peer, p, b):
        return _ple(h1, peer, p, g_ple, w_pgate_b, w_ple_b, g_fin, b)

    ts = db * dq
    xs = x_sample.reshape(ts, D_MODEL)
    cos_s, sin_s = _rope_tables(jnp.tile(past + jnp.arange(dq, dtype=jnp.int32), db))
    kd, vd, ks, vs, qd2, kdb, vdb, qsb, ksb, vsb = _proj(xs, g_mix, w_in_b, cos_s, sin_s, 1, ts, False)
    caches = [c[0].reshape(db, past, MIX) for c in (cache_diff_k, cache_diff_v, cache_sb_k, cache_sb_v)]
    dout_s, sout_s = _sample_attention(lam, qd2, kdb, vdb, qsb, ksb, vsb, *caches, g_sub, db, dq)
    rows_s = tuple(r.reshape(1, db, dq, HEADS, HEAD_DIM) for r in (kd, vd, ks, vs))

    xp = x_prompt.reshape(nb * seq, D_MODEL)
    pp = p_prompt[0].reshape(nb * seq, PLE_DIM)
    cos_p, sin_p = _rope_tables(jnp.arange(seq, dtype=jnp.int32))
    kd, vd, ks, vs, qd2, kdb, vdb, qsb, ksb, vsb = _proj(xp, g_mix, w_in_b, cos_p, sin_p, nb, seq, True)
    rows_p = tuple(r.reshape(nb, HEADS, HEAD_DIM, seq).transpose(0, 3, 1, 2)[None] for r in (kd, vd, ks, vs))

    cut = seq // PROMPT_ROW_BLOCKS
    spans = [(b, i * cut, cut) for b in range(nb) for i in range(PROMPT_ROW_BLOCKS)]
    for end in (0, -1):
        b, t0, n = spans.pop(end)
        spans[end if end == 0 else len(spans):0] = [(b, t0, n // 2), (b, t0 + n // 2, n // 2)]

    def prompt_block(b, t0, n):
        dout = _diff_attention(lam, qd2, kdb, vdb, g_sub, b, t0, n)
        sout = _sb_attention(qsb, ksb, vsb, b, t0, n)
        return select(xp, dout, sout, b * seq + t0, n)

    blocks = [(functools.partial(prompt_block, *span), pp, span[0] * seq + span[1]) for span in spans]
    sample_at = min(PROMPT_ROW_BLOCKS + 1, len(blocks))
    blocks.insert(sample_at, (lambda: select(xs, dout_s, sout_s, 0, ts), p_sample[0].reshape(ts, PLE_DIM), 0))
    ys = []
    h1, c, idx, gate = blocks[0][0]()
    dots = _sc_dots(expert_u[0], idx, c)
    for k in range(1, len(blocks)):
        nh1, nc, nidx, ngate = blocks[k][0]()
        coef = _coef(gate, dots, nh1)
        peer, ndots = _sc_step(expert_v[0], idx, coef, expert_u[0], nidx, nc)
        ys.append(finish(h1, peer, *blocks[k - 1][1:]))
        h1, idx, gate, dots = nh1, nidx, ngate, ndots
    coef = _coef(gate, dots, lam)
    ys.append(finish(h1, _sc_combine(expert_v[0], idx, coef), *blocks[-1][1:]))
    y_sample = ys.pop(sample_at).reshape(db, dq, D_MODEL)
    y_prompt = jnp.concatenate(ys, axis=0).reshape(nb, seq, D_MODEL)

    return (y_prompt, y_sample) + rows_p + rows_s
```

```python
import functools
import math

import jax
import jax.numpy as jnp
from jax import lax
from jax.experimental import pallas as pl
from jax.experimental.pallas import tpu as pltpu
from jax.experimental.pallas import tpu_sc as plsc

F32 = jnp.float32
BF16 = jnp.bfloat16

D_MODEL = 1024
HEADS = 8
HEAD_DIM = 64
DIFF_COMP = 32
MIX = HEADS * HEAD_DIM
CHUNK = 64
ROPE_THETA = 10000.0
NORM_EPS = 1e-6
SUBLN_EPS = 1e-5
PEER_HEADS = 8
PEER_KEYS = 128
PEER_TOPK = 16
PEER_HALF = 128
PEER_SEL = PEER_HEADS * PEER_TOPK
PLE_DIM = 256
LAM_INIT = 0.8 - 0.6 * math.exp(-0.3 * 0)
SB_LOG_FLOOR = -104.0
PROMPT_ROW_BLOCKS = 4

LANES = 128
VMEM_LIMIT = 48 * 1024 * 1024

NT_DIMS = (((1,), (1,)), ((), ()))


def _nt(a, b):
    return lax.dot_general(a, b, NT_DIMS, preferred_element_type=F32)


def _mm(a, b):
    return jnp.dot(a, b, preferred_element_type=F32)


def _rms(x, g, eps):
    return x * lax.rsqrt(jnp.mean(x * x, axis=-1, keepdims=True) + eps) * g


def _params(*sem):
    return pltpu.CompilerParams(dimension_semantics=sem, vmem_limit_bytes=VMEM_LIMIT)


def _with_ones(v):
    n = v.shape[0]
    ones = (lax.broadcasted_iota(jnp.int32, (n, HEAD_DIM), 1) == 0).astype(v.dtype)
    return jnp.concatenate([v, ones], axis=1)


def _proj_body(x_ref, g_ref, w_ref, cos_ref, sin_ref,
               kd_ref, vd_ref, ks_ref, vs_ref,
               qd2_ref, kdb_ref, vdb_ref, qsb_ref, ksb_ref, vsb_ref, *, feature_major):
    ts = x_ref.shape[0]
    a = _rms(x_ref[...], g_ref[...], NORM_EPS).astype(BF16)
    cos = jnp.tile(cos_ref[...], (1, MIX // LANES))
    sin = jnp.tile(sin_ref[...], (1, MIX // LANES))
    lane = lax.broadcasted_iota(jnp.int32, (ts, MIX), 1)
    first_half = (lane % DIFF_COMP) < (DIFF_COMP // 2)

    def group(i):
        return _mm(a, w_ref[:, i * MIX:(i + 1) * MIX])

    def rope(t):
        partner = jnp.where(first_half,
                            pltpu.roll(t, MIX - DIFF_COMP // 2, 1),
                            pltpu.roll(t, DIFF_COMP // 2, 1))
        return t * cos + partner * sin

    qd = rope(group(0)) * (DIFF_COMP ** -0.5)
    kd = rope(group(1))
    vd = group(2)
    qs = group(3) * (HEAD_DIM ** -0.5)
    ks = group(4)
    vs = group(5)
    for ref, rows in ((kd_ref, kd), (vd_ref, vd), (ks_ref, ks), (vs_ref, vs)):
        if feature_major:
            ref[0] = rows.T
        else:
            ref[...] = rows
    comp0 =lax.broadcasted_iota(jnp.int32, (ts, HEAD_DIM), 1) < DIFF_COMP
    for h in range(HEADS):
        sl = slice(h * HEAD_DIM, (h + 1) * HEAD_DIM)
        qh = qd[:, sl]
        qd2_ref[0, h, 0] = jnp.where(comp0, qh, 0.0).astype(BF16)
        qd2_ref[0, h, 1] = jnp.where(comp0, 0.0, qh).astype(BF16)
        kdb_ref[0, h] = kd[:, sl].astype(BF16)
        vdb_ref[0, h] = _with_ones(vd[:, sl].astype(BF16))
        qsb_ref[0, h] = qs[:, sl].astype(BF16)
        ksb_ref[0, h] = ks[:, sl].astype(BF16)
        vsb_ref[0, h] = vs[:, sl].astype(BF16)


def _proj(x, g_mix, w_in_b, cos_t, sin_t, nb, seq, feature_major):
    t = nb * seq
    ts = min(256, seq)
    nst = seq // ts
    if feature_major:
        row = pl.BlockSpec((1, MIX, ts), lambda i: (i // nst, 0, i % nst))
        rows = jax.ShapeDtypeStruct((nb, MIX, seq), F32)
    else:
        row = pl.BlockSpec((ts, MIX), lambda i: (i, 0))
        rows = jax.ShapeDtypeStruct((t, MIX), F32)
    hm = pl.BlockSpec((1, HEADS, ts, HEAD_DIM), lambda i: (i // nst, 0, i % nst, 0))
    hm2 = pl.BlockSpec((1, HEADS, 2, ts, HEAD_DIM), lambda i: (i // nst, 0, 0, i % nst, 0))
    heads =jax.ShapeDtypeStruct((nb, HEADS, seq, HEAD_DIM), BF16)
    heads2 = jax.ShapeDtypeStruct((nb, HEADS, 2, seq, HEAD_DIM), BF16)
    hm_ext = pl.BlockSpec((1, HEADS, ts, 2 * HEAD_DIM), lambda i: (i // nst, 0, i % nst, 0))
    heads_ext = jax.ShapeDtypeStruct((nb, HEADS, seq, 2 * HEAD_DIM), BF16)
    return pl.pallas_call(
        functools.partial(_proj_body, feature_major=feature_major),
        grid=(t // ts,),
        in_specs=[
            pl.BlockSpec((ts, D_MODEL), lambda i: (i, 0)),
            pl.BlockSpec((1, D_MODEL), lambda i: (0, 0)),
            pl.BlockSpec((D_MODEL, 6 * MIX), lambda i: (0, 0)),
            pl.BlockSpec((ts, LANES), lambda i: (i % nst, 0)),
            pl.BlockSpec((ts, LANES), lambda i: (i % nst, 0)),
        ],
        out_specs=[row, row, row, row, hm2, hm, hm_ext, hm, hm, hm],
        out_shape=[rows, rows, rows, rows, heads2, heads, heads_ext, heads, heads, heads],
        compiler_params=_params("parallel"),
        name="proj",
    )(x, g_mix, w_in_b, cos_t, sin_t)


def _diff_init(rows):
    return jnp.full((rows, 1), -jnp.inf, F32), jnp.zeros((rows, 2 * HEAD_DIM), F32)


def _diff_update(s, v_ext, carry):
    m, acc = carry
    m_new = jnp.maximum(m, jnp.max(s, axis=-1, keepdims=True))
    p = jnp.exp(s - m_new)
    acc = jnp.exp(m - m_new) * acc + _mm(p.astype(BF16), v_ext)
    return m_new, acc


def _diff_finish(carry, lam, g_subln, tq):
    _, acc = carry
    o = acc[:, :HEAD_DIM] / acc[:, HEAD_DIM:HEAD_DIM + 1]
    d = o[:tq] - lam * o[tq:]
    return _rms(d, g_subln, SUBLN_EPS) * (1.0 - LAM_INIT)


def _suffix_sums(lk, tri):
    hi = lk.astype(BF16)
    lo = (lk - hi.astype(F32)).astype(BF16)
    return _mm(hi, tri) + _mm(lo, tri)


def _sb_update(q, k, v, tri, earlier, carry):
    run, acc = carry
    z = _nt(q, k)
    sp = jnp.maximum(z, 0.0) + jnp.log1p(jnp.exp(-jnp.abs(z)))
    lk = -sp if earlier is None else jnp.where(earlier, -sp, 0.0)
    after = _suffix_sums(lk, tri)
    w = jnp.exp((z - sp) + after + run)
    if earlier is not None:
        w = jnp.where(earlier, w, 0.0)
    acc = acc + _mm(w.astype(BF16), v)
    run = run + after[:, 0:1] + lk[:, 0:1]
    return run, acc


def _tri(n):
    j = lax.broadcasted_iota(jnp.int32, (n, n), 0)
    s = lax.broadcasted_iota(jnp.int32, (n, n), 1)
    return (j > s).astype(BF16)


def _diff_body(lam_ref, q_ref, k_ref, v_ref, g_ref, o_ref, *, tq, q0):
    qi = q0 + pl.program_id(1)
    q2 = q_ref[0, 0].reshape(2 * tq, HEAD_DIM)

    def scores(j):
        return _nt(q2, k_ref[0, 0, pl.ds(pl.multiple_of(j * tq, tq), tq), :])

    def step(j, state):
        s, carry = state
        s_next = scores(j + 1)
        return s_next, _diff_update(s, v_ref[0, 0, pl.ds(pl.multiple_of(j * tq, tq), tq), :], carry)

    s, carry = lax.fori_loop(0, qi, step, (scores(0), _diff_init(2 * tq)))
    r = lax.broadcasted_iota(jnp.int32, (2 * tq, tq), 0) % tq
    c = lax.broadcasted_iota(jnp.int32, (2 * tq, tq), 1)
    s = jnp.where((c // CHUNK) <= (r // CHUNK), s, -jnp.inf)
    carry = _diff_update(s, v_ref[0, 0, pl.ds(pl.multiple_of(qi * tq, tq), tq), :], carry)
    o_ref[0, 0] = _diff_finish(carry, lam_ref[0], g_ref[...], tq)


def _diff_attention(lam, qd2, kdb, vdb, g_subln, b, t0, n):
    seq = qd2.shape[3]
    tq = min(512, n)
    q0 = t0 // tq
    return pl.pallas_call(
        functools.partial(_diff_body, tq=tq, q0=q0),
        grid=(HEADS, n // tq),
        in_specs=[
            pl.BlockSpec(memory_space=pltpu.SMEM),
            pl.BlockSpec((1, 1, 2, tq, HEAD_DIM), lambda h, i: (b, h, 0, q0 + i, 0)),
            pl.BlockSpec((1, 1, seq, HEAD_DIM), lambda h, i: (b, h, 0, 0)),
            pl.BlockSpec((1, 1, seq, 2 * HEAD_DIM), lambda h, i: (b, h, 0, 0)),
            pl.BlockSpec((1, HEAD_DIM), lambda h, i: (0, 0)),
        ],
        out_specs=pl.BlockSpec((1, 1, tq, HEAD_DIM), lambda h, i: (0, h, i, 0)),
        out_shape=jax.ShapeDtypeStruct((1, HEADS, n, HEAD_DIM), F32),
        compiler_params=_params("parallel", "arbitrary"),
        name="diff_attention",
    )(lam, qd2, kdb, vdb, g_subln)


def _sb_body(q_ref, k_ref, v_ref, o_ref, *, tq, q0):
    qi = q0 + pl.program_id(1)
    q = q_ref[0, 0]
    tri = _tri(tq)

    def tile(j):
        start = pl.multiple_of(j * tq, tq)
        return k_ref[0, 0, pl.ds(start, tq), :], v_ref[0, 0, pl.ds(start, tq), :]

    r = lax.broadcasted_iota(jnp.int32, (tq, tq), 0)
    c = lax.broadcasted_iota(jnp.int32, (tq, tq), 1)
    carry = (jnp.zeros((tq, 1), F32), jnp.zeros((tq, HEAD_DIM), F32))
    run, acc = _sb_update(q, *tile(qi), tri, c < r, carry)

    def live(state):
        j, run, _ = state
        return jnp.logical_and(j >= 0, jnp.max(run) > SB_LOG_FLOOR)

    def step(state):
        j, run, acc = state
        run, acc = _sb_update(q, *tile(j), tri, None, (run, acc))
        return j - 1, run, acc

    o_ref[0, 0] = lax.while_loop(live, step, (qi - 1, run, acc))[2]


def _sb_attention(qsb, ksb, vsb, b, t0, n):
    seq = qsb.shape[2]
    tq = min(256, n)
    q0 = t0 // tq
    kv = pl.BlockSpec((1, 1, seq, HEAD_DIM), lambda h, i: (b, h, 0, 0))
    return pl.pallas_call(
        functools.partial(_sb_body, tq=tq, q0=q0),
        grid=(HEADS, n // tq),
        in_specs=[pl.BlockSpec((1, 1, tq, HEAD_DIM), lambda h, i: (b, h, q0 + i, 0)), kv, kv],
        out_specs=pl.BlockSpec((1, 1, tq, HEAD_DIM), lambda h, i: (0, h, i, 0)),
        out_shape=jax.ShapeDtypeStruct((1, HEADS, n, HEAD_DIM), F32),
        compiler_params=_params("parallel", "arbitrary"),
        name="sb_attention",
    )(qsb, ksb, vsb)


SAMPLE_HEADS = 4
SAMPLE_TILE = 256


def _sample_body(lam_ref, qd2_ref, kdn_ref, vdn_ref, qs_ref, ksn_ref, vsn_ref,
                 ckd_ref, cvd_ref, cks_ref, cvs_ref, g_ref, do_ref, so_ref, *, past, nq):
    lam = lam_ref[0]
    tri_c = _tri(SAMPLE_TILE)
    tri_n = _tri(nq)
    i2 = lax.broadcasted_iota(jnp.int32, (2 * nq, nq), 0) % nq
    j2 = lax.broadcasted_iota(jnp.int32, (2 * nq, nq), 1)
    visible_new = ((past + j2) // CHUNK) <= ((past + i2) // CHUNK)
    i1 = lax.broadcasted_iota(jnp.int32, (nq, nq), 0)
    j1 = lax.broadcasted_iota(jnp.int32, (nq, nq), 1)
    earlier_new = j1 < i1
    for h in range(SAMPLE_HEADS):
        sl = slice(h * HEAD_DIM, (h + 1) * HEAD_DIM)
        q2 = qd2_ref[0, h].reshape(2 * nq, HEAD_DIM)
        carry = _diff_update(_nt(q2, ckd_ref[0, :, sl].astype(BF16)),
                             _with_ones(cvd_ref[0, :, sl].astype(BF16)), _diff_init(2 * nq))
        s_new = jnp.where(visible_new, _nt(q2, kdn_ref[0, h]), -jnp.inf)
        carry = _diff_update(s_new, vdn_ref[0, h], carry)
        do_ref[0, h] = _diff_finish(carry, lam, g_ref[...], nq)
        q = qs_ref[0, h]
        carry = (jnp.zeros((nq, 1), F32), jnp.zeros((nq, HEAD_DIM), F32))
        carry = _sb_update(q, ksn_ref[0, h], vsn_ref[0, h], tri_n, earlier_new, carry)
        for t in reversed(range(past // SAMPLE_TILE)):
            rows = slice(t * SAMPLE_TILE, (t + 1) * SAMPLE_TILE)
            carry = _sb_update(q, cks_ref[0, rows, sl].astype(BF16), cvs_ref[0, rows, sl].astype(BF16),
                               tri_c, None, carry)
        so_ref[0, h] = carry[1]


def _sample_attention(lam, qd2, kdb, vdb, qsb, ksb, vsb, ckd, cvd, cks, cvs, g_subln, nb, nq):
    past = ckd.shape[1]
    nhg = HEADS // SAMPLE_HEADS
    hm = pl.BlockSpec((1, SAMPLE_HEADS, nq, HEAD_DIM), lambda b, g: (0, g, b, 0))
    hm2 = pl.BlockSpec((1, SAMPLE_HEADS, 2, nq, HEAD_DIM), lambda b, g: (0, g, 0, b, 0))
    cache = pl.BlockSpec((1, past, SAMPLE_HEADS * HEAD_DIM), lambda b, g: (b, 0, g))
    out = jax.ShapeDtypeStruct((1, HEADS, nb * nq, HEAD_DIM), F32)
    return pl.pallas_call(
        functools.partial(_sample_body, past=past, nq=nq),
        grid=(nb, nhg),
        in_specs=[pl.BlockSpec(memory_space=pltpu.SMEM), hm2, hm,
                  pl.BlockSpec((1, SAMPLE_HEADS, nq, 2 * HEAD_DIM), lambda b, g: (0, g, b, 0)), hm, hm, hm,
                  cache, cache, cache, cache,
                  pl.BlockSpec((1, HEAD_DIM), lambda b, g: (0, 0))],
        out_specs=[hm, hm],
        out_shape=[out, out],
        compiler_params=_params("parallel", "parallel"),
        name="sample_attention",
    )(lam, qd2, kdb, vdb, qsb, ksb, vsb, ckd, cvd, cks, cvs, g_subln)


def _topk_rows(s, k):
    n = s.shape[0]
    rows = lax.broadcasted_iota(jnp.int32, s.shape, 0)
    vals, ids = [], []
    for _ in range(k):
        m = jnp.max(s, axis=0, keepdims=True)
        i = jnp.min(jnp.where(s == m, rows, n), axis=0, keepdims=True)
        vals.append(m)
        ids.append(i)
        s = jnp.where(rows == i, -jnp.inf, s)
    return jnp.concatenate(vals, axis=0), jnp.concatenate(ids, axis=0)


_STAIR = [(i, j) for i in range(PEER_TOPK) for j in range(PEER_TOPK) if (i + 1) * (j + 1) <= PEER_TOPK]


def _post_body(x_ref, do_ref, so_ref, wo_ref, gf_ref, wq_ref, sk_ref,
               h1_ref, c_ref, idx_ref, gate_ref, q_scr, idx_scr, gate_scr):
    ts = x_ref.shape[0]
    mixed = jnp.zeros((ts, D_MODEL), F32)
    for h in range(HEADS):
        mixed += _mm(do_ref[0, h].astype(BF16), wo_ref[h * HEAD_DIM:(h + 1) * HEAD_DIM, :])
        mixed += _mm(so_ref[0, h].astype(BF16), wo_ref[MIX + h * HEAD_DIM:MIX + (h + 1) * HEAD_DIM, :])
    h1 = x_ref[...] + mixed
    h1_ref[...] = h1
    c = _rms(h1, gf_ref[...], NORM_EPS)
    c_ref[...] = c
    q = _mm(c.astype(BF16), wq_ref[...])
    for hp in range(2 * PEER_HEADS):
        q_scr[hp] = q[:, hp * PEER_HALF:(hp + 1) * PEER_HALF].astype(BF16)

    npad = -len(_STAIR) % 8

    def head(h, _):
        v1, i1 = _topk_rows(_nt(sk_ref[0], q_scr[2 * h]), PEER_TOPK)
        v2, i2 = _topk_rows(_nt(sk_ref[1], q_scr[2 * h + 1]), PEER_TOPK)
        cand = jnp.concatenate([v1[i:i + 1] + v2[j:j + 1] for i, j in _STAIR]
                               + [jnp.full((npad, ts), -jnp.inf, F32)], axis=0)
        eid = jnp.concatenate([i1[i:i + 1] * PEER_KEYS + i2[j:j + 1] for i, j in _STAIR]
                              + [jnp.zeros((npad, ts), jnp.int32)], axis=0)
        top, pos = _topk_rows(cand, PEER_TOPK)
        rows = lax.broadcasted_iota(jnp.int32, cand.shape, 0)
        sel = jnp.concatenate([jnp.sum(jnp.where(rows == pos[r:r + 1], eid, 0), axis=0, keepdims=True)
                               for r in range(PEER_TOPK)], axis=0)
        e = jnp.exp(top - top[0:1])
        gate_scr[h] = e / jnp.sum(e, axis=0, keepdims=True)
        idx_scr[h] = sel
        return 0

    lax.fori_loop(0, PEER_HEADS, head, 0)
    idx_ref[...] = idx_scr[...].reshape(PEER_SEL, ts).T
    gate_ref[...] = gate_scr[...].reshape(PEER_SEL, ts).T


def _post(x, dout, sout, w_out_b, g_ffn, w_query_b, sub_keys_b, tok0, t):
    ts = min(256, t)
    first = tok0 // ts
    row = pl.BlockSpec((ts, D_MODEL), lambda i: (i, 0))
    hm = pl.BlockSpec((1, HEADS, ts, HEAD_DIM), lambda i: (0, 0, i, 0))
    sel = pl.BlockSpec((ts, PEER_SEL), lambda i: (i, 0))
    full = lambda *shape: pl.BlockSpec(shape, lambda i: (0,) * len(shape))
    return pl.pallas_call(
        _post_body,
        grid=(t // ts,),
        in_specs=[pl.BlockSpec((ts, D_MODEL), lambda i: (first + i, 0)),
                  hm, hm, full(2 * MIX, D_MODEL), full(1, D_MODEL),
                  full(D_MODEL, 2 * PEER_HEADS * PEER_HALF), full(2, PEER_KEYS, PEER_HALF)],
        out_specs=[row, row, sel, sel],
        out_shape=[jax.ShapeDtypeStruct((t, D_MODEL), F32), jax.ShapeDtypeStruct((t, D_MODEL), F32),
                   jax.ShapeDtypeStruct((t, PEER_SEL), jnp.int32), jax.ShapeDtypeStruct((t, PEER_SEL), F32)],
        scratch_shapes=[pltpu.VMEM((2 * PEER_HEADS, ts, PEER_HALF), BF16),
                        pltpu.VMEM((PEER_HEADS, PEER_TOPK, ts), jnp.int32),
                        pltpu.VMEM((PEER_HEADS, PEER_TOPK, ts), F32)],
        compiler_params=_params("parallel"),
        name="post_peer_select",
    )(x, dout, sout, w_out_b, g_ffn, w_query_b, sub_keys_b)


def _coef_body(after_ref, gate_ref, dots_ref, o_ref):
    del after_ref
    d = dots_ref[...]
    o_ref[...] = gate_ref[...] * (0.5 * d * (1.0 + lax.erf(d * (2.0 ** -0.5))))


def _coef(gate, dots, after):
    t = gate.shape[0]
    ts = min(2048, t)
    blk = pl.BlockSpec((ts, PEER_SEL), lambda i: (i, 0))
    return pl.pallas_call(
        _coef_body, grid=(t // ts,), in_specs=[pl.BlockSpec(memory_space=pl.ANY), blk, blk], out_specs=blk,
        out_shape=jax.ShapeDtypeStruct((t, PEER_SEL), F32),
        compiler_params=_params("parallel"), name="peer_coef",
    )(after, gate, dots)


SC_CORES = 2
SC_SUBCORES = 16
SC_LANES = 16
SC_WORKERS = SC_CORES * SC_SUBCORES
SC_ROWS = 16
SC_BUFS = 5
SC_GROUP = 8
SC_CHUNKS = PEER_SEL // SC_ROWS
SC_STEPS = SC_GROUP * SC_CHUNKS
SC_VECS = D_MODEL // SC_LANES


def _sc_mesh():
    return plsc.VectorSubcoreMesh(core_axis_name="c", subcore_axis_name="s",
                                  num_cores=SC_CORES, num_subcores=SC_SUBCORES)


def _sc_walk(table_hbm, idx_hbm, aux_hbm, idx_v, aux_v, rows_v, sem, stage_sem, tpw, begin_group, compute, end_group):
    tok_base = (lax.axis_index("s") * SC_CORES + lax.axis_index("c")) * tpw
    ngroups = tpw // SC_GROUP
    nsteps = tpw * SC_CHUNKS

    def first_token(g):
        return pl.multiple_of(tok_base + g * SC_GROUP, SC_GROUP)

    def stage(g):
        tok0 = first_token(g)
        return (pltpu.make_async_copy(idx_hbm.at[pl.ds(tok0 * SC_CHUNKS, SC_STEPS)], idx_v.at[g % 2], stage_sem.at[0]),
                pltpu.make_async_copy(aux_hbm.at[pl.ds(tok0, SC_GROUP)],
                                      aux_v.at[g % 2, :, pl.ds(0, aux_hbm.shape[1])], stage_sem.at[1]))

    def gather(step):
        idx = idx_v.at[(step // SC_STEPS) % 2, step % SC_STEPS]
        return pltpu.make_async_copy(table_hbm.at[idx], rows_v.at[step % SC_BUFS], sem.at[step % SC_BUFS])

    for cp in stage(0):
        cp.start()
    for cp in stage(0):
        cp.wait()
    for step in range(SC_BUFS - 1):
        gather(step).start()

    def walk(step, _):
        g = step // SC_STEPS
        local = step % SC_STEPS

        @pl.when(jnp.logical_and(local == 0, g + 1 < ngroups))
        def _():
            for cp in stage(g + 1):
                cp.start()

        ahead = step + (SC_BUFS - 1)

        @pl.when(ahead < nsteps)
        def _():
            @pl.when(ahead % SC_STEPS == 0)
            def _():
                for cp in stage(ahead // SC_STEPS):
                    cp.wait()

            gather(ahead).start()

        @pl.when(local == 0)
        def _():
            begin_group()

        gather(step).wait()
        compute(rows_v.at[step % SC_BUFS], g % 2, local // SC_CHUNKS, local % SC_CHUNKS)

        @pl.when(local == SC_STEPS - 1)
        def _():
            end_group(first_token(g))

        return 0

    lax.fori_loop(0, nsteps, walk, 0)


def _sc_dots_body(u_hbm, idx_hbm, c_hbm, out_hbm, idx_v, c_v, rows_v, dots_v, red_v, sem, stage_sem, *, tpw):
    lane = lax.broadcasted_iota(jnp.int32, (SC_LANES,), 0)
    zero = jnp.zeros((SC_LANES,), F32)
    columns = [jnp.full((SC_LANES,), k, jnp.int32) for k in range(SC_LANES)]

    def compute(rows, slot, tt, ch):
        for half in range(SC_ROWS // SC_LANES):
            r0 = half * SC_LANES

            def vec(kk, accs):
                off = pl.multiple_of(kk * SC_LANES, SC_LANES)
                cv = c_v[slot, tt, pl.ds(off, SC_LANES)]
                return tuple(a + rows[r0 + r, pl.ds(off, SC_LANES)] * cv for r, a in enumerate(accs))

            accs = lax.fori_loop(0, SC_VECS, vec, (zero,) * SC_LANES)
            for r in range(SC_LANES):
                red_v[r, :] = accs[r]
            terms = [plsc.load_gather(red_v, [lane, col]) for col in columns]
            while len(terms) > 1:
                terms = [a + b for a, b in zip(terms[0::2], terms[1::2])]
            outv = terms[0]
            dots_v[tt, pl.ds(pl.multiple_of(ch * SC_ROWS + half * SC_LANES, SC_LANES), SC_LANES)] = outv

    def end_group(tok0):
        pltpu.sync_copy(dots_v.at[:, pl.ds(0, PEER_SEL)], out_hbm.at[pl.ds(tok0, SC_GROUP)])

    _sc_walk(u_hbm, idx_hbm, c_hbm, idx_v, c_v, rows_v, sem, stage_sem, tpw, lambda: None, compute, end_group)


def _sc_combine_body(v_hbm, idx_hbm, coef_hbm, out_hbm, idx_v, coef_v, rows_v, acc_v, red_v, sem, stage_sem, *, tpw):
    del red_v
    zero = jnp.zeros((SC_LANES,), F32)

    def compute(rows, slot, tt, ch):
        slot_v = jnp.full((SC_LANES,), slot, jnp.int32)
        tok_v = jnp.full((SC_LANES,), tt, jnp.int32)
        for half in range(SC_ROWS // SC_LANES):
            col0 = ch * SC_ROWS + half * SC_LANES
            splat = [plsc.load_gather(coef_v, [slot_v, tok_v, jnp.full((SC_LANES,), col0 + r, jnp.int32)])
                     for r in range(SC_LANES)]

            @plsc.parallel_loop(0, SC_VECS, unroll=2)
            def _(kk):
                off = pl.multiple_of(kk * SC_LANES, SC_LANES)
                terms = [rows[half * SC_LANES + r, pl.ds(off, SC_LANES)] * splat[r] for r in range(SC_LANES)]
                while len(terms) > 1:
                    terms = [a + b for a, b in zip(terms[0::2], terms[1::2])]
                acc_v[tt, pl.ds(off, SC_LANES)] = acc_v[tt, pl.ds(off, SC_LANES)] + terms[0]

    def begin_group():
        def clear(i, _):
            acc_v[i // SC_VECS, pl.ds(pl.multiple_of((i % SC_VECS) * SC_LANES, SC_LANES), SC_LANES)] = zero
            return 0

        lax.fori_loop(0, SC_GROUP * SC_VECS, clear, 0)

    def end_group(tok0):
        pltpu.sync_copy(acc_v, out_hbm.at[pl.ds(tok0, SC_GROUP)])

    _sc_walk(v_hbm, idx_hbm, coef_hbm, idx_v, coef_v, rows_v, sem, stage_sem, tpw, begin_group, compute, end_group)


def _sc_call(body, table, idx, per_token, out_width, name):
    t = per_token.shape[0]
    tpw = t // SC_WORKERS
    assert tpw % SC_GROUP == 0
    return pl.kernel(
        functools.partial(body, tpw=tpw),
        out_type=jax.ShapeDtypeStruct((t, out_width), F32),
        mesh=_sc_mesh(),
        scratch_types=[pltpu.VMEM((2, SC_STEPS, SC_ROWS), jnp.int32),
                       pltpu.VMEM((2, SC_GROUP, per_token.shape[1]), F32),
                       pltpu.VMEM((SC_BUFS, SC_ROWS, D_MODEL), F32),
                       pltpu.VMEM((SC_GROUP, out_width), F32),
                       pltpu.VMEM((SC_LANES, SC_LANES), F32),
                       pltpu.SemaphoreType.DMA((SC_BUFS,)),
                       pltpu.SemaphoreType.DMA((2,))],
        compiler_params=pltpu.CompilerParams(needs_layout_passes=False),
        name=name,
    )(table, idx.reshape(t * SC_CHUNKS, SC_ROWS), per_token)


def _sc_step_body(v_hbm, idx_a_hbm, coef_hbm, u_hbm, idx_b_hbm, c_hbm, peer_hbm, dots_hbm,
                  idx_v, aux_v, rows_v, out_v, red_v, sem, stage_sem, *, tpw_a, tpw_b):
    _sc_combine_body(v_hbm, idx_a_hbm, coef_hbm, peer_hbm, idx_v, aux_v, rows_v, out_v, red_v, sem, stage_sem,
                     tpw=tpw_a)
    _sc_dots_body(u_hbm, idx_b_hbm, c_hbm, dots_hbm, idx_v, aux_v, rows_v, out_v, red_v, sem, stage_sem, tpw=tpw_b)


def _sc_step(expert_v, idx_a, coef, expert_u, idx_b, c):
    ta, tb = coef.shape[0], c.shape[0]
    assert ta % (SC_WORKERS * SC_GROUP) == 0 and tb % (SC_WORKERS * SC_GROUP) == 0
    return pl.kernel(
        functools.partial(_sc_step_body, tpw_a=ta // SC_WORKERS, tpw_b=tb // SC_WORKERS),
        out_type=[jax.ShapeDtypeStruct((ta, D_MODEL), F32), jax.ShapeDtypeStruct((tb, PEER_SEL), F32)],
        mesh=_sc_mesh(),
        scratch_types=[pltpu.VMEM((2, SC_STEPS, SC_ROWS), jnp.int32),
                       pltpu.VMEM((2, SC_GROUP, D_MODEL), F32),
                       pltpu.VMEM((SC_BUFS, SC_ROWS, D_MODEL), F32),
                       pltpu.VMEM((SC_GROUP, D_MODEL), F32),
                       pltpu.VMEM((SC_LANES, SC_LANES), F32),
                       pltpu.SemaphoreType.DMA((SC_BUFS,)),
                       pltpu.SemaphoreType.DMA((2,))],
        compiler_params=pltpu.CompilerParams(needs_layout_passes=False),
        name="peer_step",
    )(expert_v, idx_a.reshape(ta * SC_CHUNKS, SC_ROWS), coef, expert_u, idx_b.reshape(tb * SC_CHUNKS, SC_ROWS), c)


def _sc_dots(expert_u, idx, c):
    return _sc_call(_sc_dots_body, expert_u, idx, c, PEER_SEL, "peer_dots")


def _sc_combine(expert_v, idx, coef):
    return _sc_call(_sc_combine_body, expert_v, idx, coef, D_MODEL, "peer_combine")


def _ple_body(h1_ref, peer_ref, p_ref, gp_ref, wg_ref, we_ref, gfin_ref, y_ref):
    h = h1_ref[...] + peer_ref[...]
    a = _rms(h, gp_ref[...], NORM_EPS).astype(BF16)
    gate = jax.nn.sigmoid(_mm(a, wg_ref[...]))
    h = h + _mm(p_ref[...].astype(BF16), we_ref[...]) * gate
    y_ref[...] = _rms(h, gfin_ref[...], NORM_EPS)


def _ple(h1, peer, p, g_ple, w_pgate_b, w_ple_b, g_final, tok0):
    t = h1.shape[0]
    ts = min(512, t)
    first = tok0 // ts
    row = pl.BlockSpec((ts, D_MODEL), lambda i: (i, 0))
    full = lambda *shape: pl.BlockSpec(shape, lambda i: (0,) * len(shape))
    return pl.pallas_call(
        _ple_body,
        grid=(t // ts,),
        in_specs=[row, row, pl.BlockSpec((ts, PLE_DIM), lambda i: (first + i, 0)), full(1, D_MODEL),
                  full(D_MODEL, D_MODEL), full(PLE_DIM, D_MODEL), full(1, D_MODEL)],
        out_specs=row,
        out_shape=jax.ShapeDtypeStruct((t, D_MODEL), F32),
        compiler_params=_params("parallel"),
        name="ple_final",
    )(h1, peer, p, g_ple, w_pgate_b, w_ple_b, g_final)


def _rope_tables(pos):
    half = DIFF_COMP // 2
    inv = ROPE_THETA ** (-jnp.arange(0, DIFF_COMP, 2, dtype=F32) / DIFF_COMP)
    ang = pos.astype(F32)[:, None] * inv[None, :]
    cos = jnp.cos(ang)
    sin = jnp.sin(ang)
    reps = LANES // DIFF_COMP
    del half
    return (jnp.tile(jnp.concatenate([cos, cos], axis=-1), (1, reps)),
            jnp.tile(jnp.concatenate([-sin, sin], axis=-1), (1, reps)))


def kernel(x_prompt, x_sample, cache_diff_k, cache_diff_v, cache_sb_k, cache_sb_v, p_prompt, p_sample, g_mix, w_in, lambda_q1, lambda_k1, lambda_q2, lambda_k2, g_subln, w_out, g_ffn, w_query, sub_keys, expert_u, expert_v, g_ple, w_pgate, w_ple, g_final):
    assert w_in.shape[0] == 1, "single-layer encoder"
    nb, seq, _ = x_prompt.shape
    db, dq, _ = x_sample.shape
    past = cache_diff_k.shape[2]

    lam = (jnp.exp(jnp.sum(lambda_q1[0].astype(F32) * lambda_k1[0].astype(F32)))
           - jnp.exp(jnp.sum(lambda_q2[0].astype(F32) * lambda_k2[0].astype(F32))) + LAM_INIT).reshape(1)
    w_in_b = w_in[0].astype(BF16)
    w_out_b = w_out[0].astype(BF16)
    w_query_b = w_query[0].astype(BF16)
    sub_keys_b = sub_keys[0].astype(BF16)
    w_pgate_b = w_pgate[0].astype(BF16)
    w_ple_b = w_ple[0].astype(BF16)
    g_sub = g_subln[0].reshape(1, HEAD_DIM)
    g_fin = g_final.reshape(1, D_MODEL)

    def select(x, dout, sout, b, s):
        return _post(x, dout, sout, w_out_b, g_ffn, w_query_b, sub_keys_b, b, s)

    def finish(h1, peer, p, b):
        return _ple(h1, peer, p, g_ple, w_pgate_b, w_ple_b, g_fin, b)

    ts = db * dq
    xs = x_sample.reshape(ts, D_MODEL)
    cos_s, sin_s = _rope_tables(jnp.tile(past + jnp.arange(dq, dtype=jnp.int32), db))
    kd, vd, ks, vs, qd2, kdb, vdb, qsb, ksb, vsb = _proj(xs, g_mix, w_in_b, cos_s, sin_s, 1, ts, False)
    caches = [c[0].reshape(db, past, MIX) for c in (cache_diff_k, cache_diff_v, cache_sb_k, cache_sb_v)]
    dout_s, sout_s = _sample_attention(lam, qd2, kdb, vdb, qsb, ksb, vsb, *caches, g_sub, db, dq)
    rows_s = tuple(r.reshape(1, db, dq, HEADS, HEAD_DIM) for r in (kd, vd, ks, vs))

    xp = x_prompt.reshape(nb * seq, D_MODEL)
    pp = p_prompt[0].reshape(nb * seq, PLE_DIM)
    cos_p, sin_p = _rope_tables(jnp.arange(seq, dtype=jnp.int32))
    kd, vd, ks, vs, qd2, kdb, vdb, qsb, ksb, vsb = _proj(xp, g_mix, w_in_b, cos_p, sin_p, nb, seq, True)
    rows_p = tuple(r.reshape(nb, HEADS, HEAD_DIM, seq).transpose(0, 3, 1, 2)[None] for r in (kd, vd, ks, vs))

    cut = seq // PROMPT_ROW_BLOCKS
    spans = [(b, i * cut, cut) for b in range(nb) for i in range(PROMPT_ROW_BLOCKS)]
    for end in (0, -1):
        b, t0, n = spans.pop(end)
        spans[end if end == 0 else len(spans):0] = [(b, t0, n // 2), (b, t0 + n // 2, n // 2)]

    def prompt_block(b, t0, n):
        dout = _diff_attention(lam, qd2, kdb, vdb, g_sub, b, t0, n)
        sout = _sb_attention(qsb, ksb, vsb, b, t0, n)
        return select(xp, dout, sout, b * seq + t0, n)

    blocks = [(functools.partial(prompt_block, *span), pp, span[0] * seq + span[1]) for span in spans]
    sample_at = min(PROMPT_ROW_BLOCKS + 1, len(blocks))
    blocks.insert(sample_at, (lambda: select(xs, dout_s, sout_s, 0, ts), p_sample[0].reshape(ts, PLE_DIM), 0))
    ys = []
    h1, c, idx, gate = blocks[0][0]()
    dots = _sc_dots(expert_u[0], idx, c)
    for k in range(1, len(blocks)):
        nh1, nc, nidx, ngate = blocks[k][0]()
        coef = _coef(gate, dots, nh1)
        peer, ndots = _sc_step(expert_v[0], idx, coef, expert_u[0], nidx, nc)
        ys.append(finish(h1, peer, *blocks[k - 1][1:]))
        h1, idx, gate, dots = nh1, nidx, ngate, ndots
    coef = _coef(gate, dots, lam)
    ys.append(finish(h1, _sc_combine(expert_v[0], idx, coef), *blocks[-1][1:]))
    y_sample = ys.pop(sample_at).reshape(db, dq, D_MODEL)
    y_prompt = jnp.concatenate(ys, axis=0).reshape(nb, seq, D_MODEL)

    return (y_prompt, y_sample) + rows_p + rows_s
```

```python
import functools
import math

import jax
import jax.numpy as jnp
from jax import lax
from jax.experimental import pallas as pl
from jax.experimental.pallas import tpu as pltpu
from jax.experimental.pallas import tpu_sc as plsc

F32 = jnp.float32
BF16 = jnp.bfloat16

D_MODEL = 1024
HEADS = 8
HEAD_DIM = 64
DIFF_COMP = 32
MIX = HEADS * HEAD_DIM
CHUNK = 64
ROPE_THETA = 10000.0
NORM_EPS = 1e-6
SUBLN_EPS = 1e-5
PEER_HEADS = 8
PEER_KEYS = 128
PEER_TOPK = 16
PEER_HALF = 128
PEER_SEL = PEER_HEADS * PEER_TOPK
PLE_DIM = 256
LAM_INIT = 0.8 - 0.6 * math.exp(-0.3 * 0)
SB_LOG_FLOOR = -104.0
PROMPT_ROW_BLOCKS = 4

LANES = 128
VMEM_LIMIT = 48 * 1024 * 1024

NT_DIMS = (((1,), (1,)), ((), ()))


def _nt(a, b):
    return lax.dot_general(a, b, NT_DIMS, preferred_element_type=F32)


def _mm(a, b):
    return jnp.dot(a, b, preferred_element_type=F32)


def _rms(x, g, eps):
    return x * lax.rsqrt(jnp.mean(x * x, axis=-1, keepdims=True) + eps) * g


def _params(*sem):
    return pltpu.CompilerParams(dimension_semantics=sem, vmem_limit_bytes=VMEM_LIMIT)


def _with_ones(v):
    n = v.shape[0]
    ones = (lax.broadcasted_iota(jnp.int32, (n, HEAD_DIM), 1) == 0).astype(v.dtype)
    return jnp.concatenate([v, ones], axis=1)


def _proj_body(x_ref, g_ref, w_ref, cos_ref, sin_ref, *refs, feature_major, aliased):
    (kd_ref, vd_ref, ks_ref, vs_ref,
     qd2_ref, kdb_ref, vdb_ref, qsb_ref, ksb_ref, vsb_ref) = refs[aliased:]
    ts = x_ref.shape[0]
    a = _rms(x_ref[...], g_ref[...], NORM_EPS).astype(BF16)
    cos = jnp.tile(cos_ref[...], (1, MIX // LANES))
    sin = jnp.tile(sin_ref[...], (1, MIX // LANES))
    lane = lax.broadcasted_iota(jnp.int32, (ts, MIX), 1)
    first_half = (lane % DIFF_COMP) < (DIFF_COMP // 2)

    def group(i):
        return _mm(a, w_ref[:, i * MIX:(i + 1) * MIX])

    def rope(t):
        partner = jnp.where(first_half,
                            pltpu.roll(t, MIX - DIFF_COMP // 2, 1),
                            pltpu.roll(t, DIFF_COMP // 2, 1))
        return t * cos + partner * sin

    qd = rope(group(0)) * (DIFF_COMP ** -0.5)
    kd = rope(group(1))
    vd = group(2)
    qs = group(3) * (HEAD_DIM ** -0.5)
    ks = group(4)
    vs = group(5)
    for ref, rows in ((kd_ref, kd), (vd_ref, vd), (ks_ref, ks), (vs_ref, vs)):
        if feature_major:
            ref[0] = rows.T
        else:
            ref[...] = rows
    comp0 =lax.broadcasted_iota(jnp.int32, (ts, HEAD_DIM), 1) < DIFF_COMP
    for h in range(HEADS):
        sl = slice(h * HEAD_DIM, (h + 1) * HEAD_DIM)
        qh = qd[:, sl]
        qd2_ref[0, h, 0] = jnp.where(comp0, qh, 0.0).astype(BF16)
        qd2_ref[0, h, 1] = jnp.where(comp0, 0.0, qh).astype(BF16)
        kdb_ref[0, h] = kd[:, sl].astype(BF16)
        vdb_ref[0, h] = _with_ones(vd[:, sl].astype(BF16))
        qsb_ref[0, h] = qs[:, sl].astype(BF16)
        ksb_ref[0, h] = ks[:, sl].astype(BF16)
        vsb_ref[0, h] = vs[:, sl].astype(BF16)


def _proj(x, g_mix, w_in_b, cos_t, sin_t, nb, seq, feature_major, first=0, count=None, into=()):
    count = nb - first if count is None else count
    ts = min(256, seq)
    nst = seq // ts
    if feature_major:
        row = pl.BlockSpec((1, MIX, ts), lambda i: (first + i // nst, 0, i % nst))
        rows = jax.ShapeDtypeStruct((nb, MIX, seq), F32)
    else:
        row = pl.BlockSpec((ts, MIX), lambda i: (first * nst + i, 0))
        rows = jax.ShapeDtypeStruct((nb * seq, MIX), F32)
    hm = pl.BlockSpec((1, HEADS, ts, HEAD_DIM), lambda i: (i // nst, 0, i % nst, 0))
    hm2 = pl.BlockSpec((1, HEADS, 2, ts, HEAD_DIM), lambda i: (i // nst, 0, 0, i % nst, 0))
    heads = jax.ShapeDtypeStruct((count, HEADS, seq, HEAD_DIM), BF16)
    heads2 = jax.ShapeDtypeStruct((count, HEADS, 2, seq, HEAD_DIM), BF16)
    hm_ext = pl.BlockSpec((1, HEADS, ts, 2 * HEAD_DIM), lambda i: (i // nst, 0, i % nst, 0))
    heads_ext = jax.ShapeDtypeStruct((count, HEADS, seq, 2 * HEAD_DIM), BF16)
    n_in = 5
    return pl.pallas_call(
        functools.partial(_proj_body, feature_major=feature_major, aliased=len(into)),
        grid=(count * nst,),
        in_specs=[
            pl.BlockSpec((ts, D_MODEL), lambda i: (first * nst + i, 0)),
            pl.BlockSpec((1, D_MODEL), lambda i: (0, 0)),
            pl.BlockSpec((D_MODEL, 6 * MIX), lambda i: (0, 0)),
            pl.BlockSpec((ts, LANES), lambda i: (i % nst, 0)),
            pl.BlockSpec((ts, LANES), lambda i: (i % nst, 0)),
        ] + [pl.BlockSpec(memory_space=pl.ANY)] * len(into),
        out_specs=[row, row, row, row, hm2, hm, hm_ext, hm, hm, hm],
        out_shape=[rows, rows, rows, rows, heads2, heads, heads_ext, heads, heads, heads],
        input_output_aliases={n_in + j: j for j in range(len(into))},
        compiler_params=_params("parallel"),
        name="proj",
    )(x, g_mix, w_in_b, cos_t, sin_t, *into)


def _diff_init(rows):
    return jnp.full((rows, 1), -jnp.inf, F32), jnp.zeros((rows, 2 * HEAD_DIM), F32)


def _diff_update(s, v_ext, carry):
    m, acc = carry
    m_new = jnp.maximum(m, jnp.max(s, axis=-1, keepdims=True))
    p = jnp.exp(s - m_new)
    acc = jnp.exp(m - m_new) * acc + _mm(p.astype(BF16), v_ext)
    return m_new, acc


def _diff_finish(carry, lam, g_subln, tq):
    _, acc = carry
    o = acc[:, :HEAD_DIM] / acc[:, HEAD_DIM:HEAD_DIM + 1]
    d = o[:tq] - lam * o[tq:]
    return _rms(d, g_subln, SUBLN_EPS) * (1.0 - LAM_INIT)


def _suffix_sums(lk, tri):
    hi = lk.astype(BF16)
    lo = (lk - hi.astype(F32)).astype(BF16)
    return _mm(hi, tri) + _mm(lo, tri)


def _sb_update(q, k, v, tri, earlier, carry):
    run, acc = carry
    z = _nt(q, k)
    sp = jnp.maximum(z, 0.0) + jnp.log1p(jnp.exp(-jnp.abs(z)))
    lk = -sp if earlier is None else jnp.where(earlier, -sp, 0.0)
    after = _suffix_sums(lk, tri)
    w = jnp.exp((z - sp) + after + run)
    if earlier is not None:
        w = jnp.where(earlier, w, 0.0)
    acc = acc + _mm(w.astype(BF16), v)
    run = run + after[:, 0:1] + lk[:, 0:1]
    return run, acc


def _tri(n):
    j = lax.broadcasted_iota(jnp.int32, (n, n), 0)
    s = lax.broadcasted_iota(jnp.int32, (n, n), 1)
    return (j > s).astype(BF16)


def _diff_body(lam_ref, q_ref, k_ref, v_ref, g_ref, o_ref, *, tq, q0):
    qi = q0 + pl.program_id(1)
    q2 = q_ref[0, 0].reshape(2 * tq, HEAD_DIM)

    def scores(j):
        return _nt(q2, k_ref[0, 0, pl.ds(pl.multiple_of(j * tq, tq), tq), :])

    def step(j, state):
        s, carry = state
        s_next = scores(j + 1)
        return s_next, _diff_update(s, v_ref[0, 0, pl.ds(pl.multiple_of(j * tq, tq), tq), :], carry)

    s, carry = lax.fori_loop(0, qi, step, (scores(0), _diff_init(2 * tq)))
    r = lax.broadcasted_iota(jnp.int32, (2 * tq, tq), 0) % tq
    c = lax.broadcasted_iota(jnp.int32, (2 * tq, tq), 1)
    s = jnp.where((c // CHUNK) <= (r // CHUNK), s, -jnp.inf)
    carry = _diff_update(s, v_ref[0, 0, pl.ds(pl.multiple_of(qi * tq, tq), tq), :], carry)
    o_ref[0, 0] = _diff_finish(carry, lam_ref[0], g_ref[...], tq)


def _diff_attention(lam, qd2, kdb, vdb, g_subln, b, t0, n):
    seq = qd2.shape[3]
    tq = min(512, n)
    q0 = t0 // tq
    return pl.pallas_call(
        functools.partial(_diff_body, tq=tq, q0=q0),
        grid=(HEADS, n // tq),
        in_specs=[
            pl.BlockSpec(memory_space=pltpu.SMEM),
            pl.BlockSpec((1, 1, 2, tq, HEAD_DIM), lambda h, i: (b, h, 0, q0 + i, 0)),
            pl.BlockSpec((1, 1, seq, HEAD_DIM), lambda h, i: (b, h, 0, 0)),
            pl.BlockSpec((1, 1, seq, 2 * HEAD_DIM), lambda h, i: (b, h, 0, 0)),
            pl.BlockSpec((1, HEAD_DIM), lambda h, i: (0, 0)),
        ],
        out_specs=pl.BlockSpec((1, 1, tq, HEAD_DIM), lambda h, i: (0, h, i, 0)),
        out_shape=jax.ShapeDtypeStruct((1, HEADS, n, HEAD_DIM), F32),
        compiler_params=_params("parallel", "arbitrary"),
        name="diff_attention",
    )(lam, qd2, kdb, vdb, g_subln)


def _sb_body(q_ref, k_ref, v_ref, o_ref, *, tq, q0):
    qi = q0 + pl.program_id(1)
    q = q_ref[0, 0]
    tri = _tri(tq)

    def tile(j):
        start = pl.multiple_of(j * tq, tq)
        return k_ref[0, 0, pl.ds(start, tq), :], v_ref[0, 0, pl.ds(start, tq), :]

    r = lax.broadcasted_iota(jnp.int32, (tq, tq), 0)
    c = lax.broadcasted_iota(jnp.int32, (tq, tq), 1)
    carry = (jnp.zeros((tq, 1), F32), jnp.zeros((tq, HEAD_DIM), F32))
    run, acc = _sb_update(q, *tile(qi), tri, c < r, carry)

    def live(state):
        j, run, _ = state
        return jnp.logical_and(j >= 0, jnp.max(run) > SB_LOG_FLOOR)

    def step(state):
        j, run, acc = state
        run, acc = _sb_update(q, *tile(j), tri, None, (run, acc))
        return j - 1, run, acc

    o_ref[0, 0] = lax.while_loop(live, step, (qi - 1, run, acc))[2]


def _sb_attention(qsb, ksb, vsb, b, t0, n):
    seq = qsb.shape[2]
    tq = min(256, n)
    q0 = t0 // tq
    kv = pl.BlockSpec((1, 1, seq, HEAD_DIM), lambda h, i: (b, h, 0, 0))
    return pl.pallas_call(
        functools.partial(_sb_body, tq=tq, q0=q0),
        grid=(HEADS, n // tq),
        in_specs=[pl.BlockSpec((1, 1, tq, HEAD_DIM), lambda h, i: (b, h, q0 + i, 0)), kv, kv],
        out_specs=pl.BlockSpec((1, 1, tq, HEAD_DIM), lambda h, i: (0, h, i, 0)),
        out_shape=jax.ShapeDtypeStruct((1, HEADS, n, HEAD_DIM), F32),
        compiler_params=_params("parallel", "arbitrary"),
        name="sb_attention",
    )(qsb, ksb, vsb)


SAMPLE_HEADS = 4
SAMPLE_TILE = 256


def _sample_body(lam_ref, qd2_ref, kdn_ref, vdn_ref, qs_ref, ksn_ref, vsn_ref,
                 ckd_ref, cvd_ref, cks_ref, cvs_ref, g_ref, do_ref, so_ref, *, past, nq):
    lam = lam_ref[0]
    tri_c = _tri(SAMPLE_TILE)
    tri_n = _tri(nq)
    i2 = lax.broadcasted_iota(jnp.int32, (2 * nq, nq), 0) % nq
    j2 = lax.broadcasted_iota(jnp.int32, (2 * nq, nq), 1)
    visible_new = ((past + j2) // CHUNK) <= ((past + i2) // CHUNK)
    i1 = lax.broadcasted_iota(jnp.int32, (nq, nq), 0)
    j1 = lax.broadcasted_iota(jnp.int32, (nq, nq), 1)
    earlier_new = j1 < i1
    for h in range(SAMPLE_HEADS):
        sl = slice(h * HEAD_DIM, (h + 1) * HEAD_DIM)
        q2 = qd2_ref[0, h].reshape(2 * nq, HEAD_DIM)
        carry = _diff_update(_nt(q2, ckd_ref[0, :, sl].astype(BF16)),
                             _with_ones(cvd_ref[0, :, sl].astype(BF16)), _diff_init(2 * nq))
        s_new = jnp.where(visible_new, _nt(q2, kdn_ref[0, h]), -jnp.inf)
        carry = _diff_update(s_new, vdn_ref[0, h], carry)
        do_ref[0, h] = _diff_finish(carry, lam, g_ref[...], nq)
        q = qs_ref[0, h]
        carry = (jnp.zeros((nq, 1), F32), jnp.zeros((nq, HEAD_DIM), F32))
        carry = _sb_update(q, ksn_ref[0, h], vsn_ref[0, h], tri_n, earlier_new, carry)
        for t in reversed(range(past // SAMPLE_TILE)):
            rows = slice(t * SAMPLE_TILE, (t + 1) * SAMPLE_TILE)
            carry = _sb_update(q, cks_ref[0, rows, sl].astype(BF16), cvs_ref[0, rows, sl].astype(BF16),
                               tri_c, None, carry)
        so_ref[0, h] = carry[1]


def _sample_attention(lam, qd2, kdb, vdb, qsb, ksb, vsb, ckd, cvd, cks, cvs, g_subln, nb, nq):
    past = ckd.shape[1]
    nhg = HEADS // SAMPLE_HEADS
    hm = pl.BlockSpec((1, SAMPLE_HEADS, nq, HEAD_DIM), lambda b, g: (0, g, b, 0))
    hm2 = pl.BlockSpec((1, SAMPLE_HEADS, 2, nq, HEAD_DIM), lambda b, g: (0, g, 0, b, 0))
    cache = pl.BlockSpec((1, past, SAMPLE_HEADS * HEAD_DIM), lambda b, g: (b, 0, g))
    out = jax.ShapeDtypeStruct((1, HEADS, nb * nq, HEAD_DIM), F32)
    return pl.pallas_call(
        functools.partial(_sample_body, past=past, nq=nq),
        grid=(nb, nhg),
        in_specs=[pl.BlockSpec(memory_space=pltpu.SMEM), hm2, hm,
                  pl.BlockSpec((1, SAMPLE_HEADS, nq, 2 * HEAD_DIM), lambda b, g: (0, g, b, 0)), hm, hm, hm,
                  cache, cache, cache, cache,
                  pl.BlockSpec((1, HEAD_DIM), lambda b, g: (0, 0))],
        out_specs=[hm, hm],
        out_shape=[out, out],
        compiler_params=_params("parallel", "parallel"),
        name="sample_attention",
    )(lam, qd2, kdb, vdb, qsb, ksb, vsb, ckd, cvd, cks, cvs, g_subln)


def _topk_rows(s, k):
    n = s.shape[0]
    rows = lax.broadcasted_iota(jnp.int32, s.shape, 0)
    vals, ids = [], []
    for _ in range(k):
        m = jnp.max(s, axis=0, keepdims=True)
        i = jnp.min(jnp.where(s == m, rows, n), axis=0, keepdims=True)
        vals.append(m)
        ids.append(i)
        s = jnp.where(rows == i, -jnp.inf, s)
    return jnp.concatenate(vals, axis=0), jnp.concatenate(ids, axis=0)


_STAIR = [(i, j) for i in range(PEER_TOPK) for j in range(PEER_TOPK) if (i + 1) * (j + 1) <= PEER_TOPK]


def _post_body(x_ref, do_ref, so_ref, wo_ref, gf_ref, wq_ref, sk_ref,
               h1_ref, c_ref, idx_ref, gate_ref, q_scr, idx_scr, gate_scr):
    ts = x_ref.shape[0]
    mixed = jnp.zeros((ts, D_MODEL), F32)
    for h in range(HEADS):
        mixed += _mm(do_ref[0, h].astype(BF16), wo_ref[h * HEAD_DIM:(h + 1) * HEAD_DIM, :])
        mixed += _mm(so_ref[0, h].astype(BF16), wo_ref[MIX + h * HEAD_DIM:MIX + (h + 1) * HEAD_DIM, :])
    h1 = x_ref[...] + mixed
    h1_ref[...] = h1
    c = _rms(h1, gf_ref[...], NORM_EPS)
    c_ref[...] = c
    q = _mm(c.astype(BF16), wq_ref[...])
    for hp in range(2 * PEER_HEADS):
        q_scr[hp] = q[:, hp * PEER_HALF:(hp + 1) * PEER_HALF].astype(BF16)

    npad = -len(_STAIR) % 8

    def head(h, _):
        v1, i1 = _topk_rows(_nt(sk_ref[0], q_scr[2 * h]), PEER_TOPK)
        v2, i2 = _topk_rows(_nt(sk_ref[1], q_scr[2 * h + 1]), PEER_TOPK)
        cand = jnp.concatenate([v1[i:i + 1] + v2[j:j + 1] for i, j in _STAIR]
                               + [jnp.full((npad, ts), -jnp.inf, F32)], axis=0)
        eid = jnp.concatenate([i1[i:i + 1] * PEER_KEYS + i2[j:j + 1] for i, j in _STAIR]
                              + [jnp.zeros((npad, ts), jnp.int32)], axis=0)
        top, pos = _topk_rows(cand, PEER_TOPK)
        rows = lax.broadcasted_iota(jnp.int32, cand.shape, 0)
        sel = jnp.concatenate([jnp.sum(jnp.where(rows == pos[r:r + 1], eid, 0), axis=0, keepdims=True)
                               for r in range(PEER_TOPK)], axis=0)
        e = jnp.exp(top - top[0:1])
        gate_scr[h] = e / jnp.sum(e, axis=0, keepdims=True)
        idx_scr[h] = sel
        return 0

    lax.fori_loop(0, PEER_HEADS, head, 0)
    idx_ref[...] = idx_scr[...].reshape(PEER_SEL, ts).T
    gate_ref[...] = gate_scr[...].reshape(PEER_SEL, ts).T


def _post(x, dout, sout, w_out_b, g_ffn, w_query_b, sub_keys_b, tok0, t):
    ts = min(256, t)
    first = tok0 // ts
    row = pl.BlockSpec((ts, D_MODEL), lambda i: (i, 0))
    hm = pl.BlockSpec((1, HEADS, ts, HEAD_DIM), lambda i: (0, 0, i, 0))
    sel = pl.BlockSpec((ts, PEER_SEL), lambda i: (i, 0))
    full = lambda *shape: pl.BlockSpec(shape, lambda i: (0,) * len(shape))
    return pl.pallas_call(
        _post_body,
        grid=(t // ts,),
        in_specs=[pl.BlockSpec((ts, D_MODEL), lambda i: (first + i, 0)),
                  hm, hm, full(2 * MIX, D_MODEL), full(1, D_MODEL),
                  full(D_MODEL, 2 * PEER_HEADS * PEER_HALF), full(2, PEER_KEYS, PEER_HALF)],
        out_specs=[row, row, sel, sel],
        out_shape=[jax.ShapeDtypeStruct((t, D_MODEL), F32), jax.ShapeDtypeStruct((t, D_MODEL), F32),
                   jax.ShapeDtypeStruct((t, PEER_SEL), jnp.int32), jax.ShapeDtypeStruct((t, PEER_SEL), F32)],
        scratch_shapes=[pltpu.VMEM((2 * PEER_HEADS, ts, PEER_HALF), BF16),
                        pltpu.VMEM((PEER_HEADS, PEER_TOPK, ts), jnp.int32),
                        pltpu.VMEM((PEER_HEADS, PEER_TOPK, ts), F32)],
        compiler_params=_params("parallel"),
        name="post_peer_select",
    )(x, dout, sout, w_out_b, g_ffn, w_query_b, sub_keys_b)


def _coef_body(after_ref, gate_ref, dots_ref, o_ref):
    del after_ref
    d = dots_ref[...]
    o_ref[...] = gate_ref[...] * (0.5 * d * (1.0 + lax.erf(d * (2.0 ** -0.5))))


def _coef(gate, dots, after):
    t = gate.shape[0]
    ts = min(2048, t)
    blk = pl.BlockSpec((ts, PEER_SEL), lambda i: (i, 0))
    return pl.pallas_call(
        _coef_body, grid=(t // ts,), in_specs=[pl.BlockSpec(memory_space=pl.ANY), blk, blk], out_specs=blk,
        out_shape=jax.ShapeDtypeStruct((t, PEER_SEL), F32),
        compiler_params=_params("parallel"), name="peer_coef",
    )(after, gate, dots)


SC_CORES = 2
SC_SUBCORES = 16
SC_LANES = 16
SC_WORKERS = SC_CORES * SC_SUBCORES
SC_ROWS = 16
SC_BUFS = 5
SC_GROUP = 8
SC_CHUNKS = PEER_SEL // SC_ROWS
SC_STEPS = SC_GROUP * SC_CHUNKS
SC_VECS = D_MODEL // SC_LANES


def _sc_mesh():
    return plsc.VectorSubcoreMesh(core_axis_name="c", subcore_axis_name="s",
                                  num_cores=SC_CORES, num_subcores=SC_SUBCORES)


def _sc_walk(table_hbm, idx_hbm, aux_hbm, idx_v, aux_v, rows_v, sem, stage_sem, tpw, begin_group, compute, end_group):
    tok_base = (lax.axis_index("s") * SC_CORES + lax.axis_index("c")) * tpw
    ngroups = tpw // SC_GROUP
    nsteps = tpw * SC_CHUNKS

    def first_token(g):
        return pl.multiple_of(tok_base + g * SC_GROUP, SC_GROUP)

    def stage(g):
        tok0 = first_token(g)
        return (pltpu.make_async_copy(idx_hbm.at[pl.ds(tok0 * SC_CHUNKS, SC_STEPS)], idx_v.at[g % 2], stage_sem.at[0]),
                pltpu.make_async_copy(aux_hbm.at[pl.ds(tok0, SC_GROUP)],
                                      aux_v.at[g % 2, :, pl.ds(0, aux_hbm.shape[1])], stage_sem.at[1]))

    def gather(step):
        idx = idx_v.at[(step // SC_STEPS) % 2, step % SC_STEPS]
        return pltpu.make_async_copy(table_hbm.at[idx], rows_v.at[step % SC_BUFS], sem.at[step % SC_BUFS])

    for cp in stage(0):
        cp.start()
    for cp in stage(0):
        cp.wait()
    for step in range(SC_BUFS - 1):
        gather(step).start()

    def walk(step, _):
        g = step // SC_STEPS
        local = step % SC_STEPS

        @pl.when(jnp.logical_and(local == 0, g + 1 < ngroups))
        def _():
            for cp in stage(g + 1):
                cp.start()

        ahead = step + (SC_BUFS - 1)

        @pl.when(ahead < nsteps)
        def _():
            @pl.when(ahead % SC_STEPS == 0)
            def _():
                for cp in stage(ahead // SC_STEPS):
                    cp.wait()

            gather(ahead).start()

        @pl.when(local == 0)
        def _():
            begin_group()

        gather(step).wait()
        compute(rows_v.at[step % SC_BUFS], g % 2, local // SC_CHUNKS, local % SC_CHUNKS)

        @pl.when(local == SC_STEPS - 1)
        def _():
            end_group(first_token(g))

        return 0

    lax.fori_loop(0, nsteps, walk, 0)


def _sc_dots_body(u_hbm, idx_hbm, c_hbm, out_hbm, idx_v, c_v, rows_v, dots_v, sem, stage_sem, *, tpw):
    lane = lax.broadcasted_iota(jnp.int32, (SC_LANES,), 0)
    zero = jnp.zeros((SC_LANES,), F32)

    def compute(rows, slot, tt, ch):
        for half in range(SC_ROWS // SC_LANES):
            r0 = half * SC_LANES

            def vec(kk, accs):
                off = pl.multiple_of(kk * SC_LANES, SC_LANES)
                cv = c_v[slot, tt, pl.ds(off, SC_LANES)]
                return tuple(a + rows[r0 + r, pl.ds(off, SC_LANES)] * cv for r, a in enumerate(accs))

            accs = lax.fori_loop(0, SC_VECS, vec, (zero,) * SC_LANES)
            outv = zero
            for r in range(SC_LANES):
                outv = jnp.where(lane == r, jnp.sum(accs[r]), outv)
            dots_v[tt, pl.ds(pl.multiple_of(ch * SC_ROWS + half * SC_LANES, SC_LANES), SC_LANES)] = outv

    def end_group(tok0):
        pltpu.sync_copy(dots_v.at[:, pl.ds(0, PEER_SEL)], out_hbm.at[pl.ds(tok0, SC_GROUP)])

    _sc_walk(u_hbm, idx_hbm, c_hbm, idx_v, c_v, rows_v, sem, stage_sem, tpw, lambda: None, compute, end_group)


def _sc_combine_body(v_hbm, idx_hbm, coef_hbm, out_hbm, idx_v, coef_v, rows_v, acc_v, sem, stage_sem, *, tpw):
    zero = jnp.zeros((SC_LANES,), F32)

    def compute(rows, slot, tt, ch):
        slot_v = jnp.full((SC_LANES,), slot, jnp.int32)
        tok_v = jnp.full((SC_LANES,), tt, jnp.int32)
        for half in range(SC_ROWS // SC_LANES):
            col0 = ch * SC_ROWS + half * SC_LANES
            splat = [plsc.load_gather(coef_v, [slot_v, tok_v, jnp.full((SC_LANES,), col0 + r, jnp.int32)])
                     for r in range(SC_LANES)]

            @plsc.parallel_loop(0, SC_VECS, unroll=2)
            def _(kk):
                off = pl.multiple_of(kk * SC_LANES, SC_LANES)
                terms = [rows[half * SC_LANES + r, pl.ds(off, SC_LANES)] * splat[r] for r in range(SC_LANES)]
                while len(terms) > 1:
                    terms = [a + b for a, b in zip(terms[0::2], terms[1::2])]
                acc_v[tt, pl.ds(off, SC_LANES)] = acc_v[tt, pl.ds(off, SC_LANES)] + terms[0]

    def begin_group():
        def clear(i, _):
            acc_v[i // SC_VECS, pl.ds(pl.multiple_of((i % SC_VECS) * SC_LANES, SC_LANES), SC_LANES)] = zero
            return 0

        lax.fori_loop(0, SC_GROUP * SC_VECS, clear, 0)

    def end_group(tok0):
        pltpu.sync_copy(acc_v, out_hbm.at[pl.ds(tok0, SC_GROUP)])

    _sc_walk(v_hbm, idx_hbm, coef_hbm, idx_v, coef_v, rows_v, sem, stage_sem, tpw, begin_group, compute, end_group)


def _sc_call(body, table, idx, per_token, out_width, name):
    t = per_token.shape[0]
    tpw = t // SC_WORKERS
    assert tpw % SC_GROUP == 0
    return pl.kernel(
        functools.partial(body, tpw=tpw),
        out_type=jax.ShapeDtypeStruct((t, out_width), F32),
        mesh=_sc_mesh(),
        scratch_types=[pltpu.VMEM((2, SC_STEPS, SC_ROWS), jnp.int32),
                       pltpu.VMEM((2, SC_GROUP, per_token.shape[1]), F32),
                       pltpu.VMEM((SC_BUFS, SC_ROWS, D_MODEL), F32),
                       pltpu.VMEM((SC_GROUP, out_width), F32),
                       pltpu.SemaphoreType.DMA((SC_BUFS,)),
                       pltpu.SemaphoreType.DMA((2,))],
        compiler_params=pltpu.CompilerParams(needs_layout_passes=False),
        name=name,
    )(table, idx.reshape(t * SC_CHUNKS, SC_ROWS), per_token)


def _sc_step_body(v_hbm, idx_a_hbm, coef_hbm, u_hbm, idx_b_hbm, c_hbm, peer_hbm, dots_hbm,
                  idx_v, aux_v, rows_v, out_v, sem, stage_sem, *, tpw_a, tpw_b):
    _sc_combine_body(v_hbm, idx_a_hbm, coef_hbm, peer_hbm, idx_v, aux_v, rows_v, out_v, sem, stage_sem, tpw=tpw_a)
    _sc_dots_body(u_hbm, idx_b_hbm, c_hbm, dots_hbm, idx_v, aux_v, rows_v, out_v, sem, stage_sem, tpw=tpw_b)


def _sc_step(expert_v, idx_a, coef, expert_u, idx_b, c):
    ta, tb = coef.shape[0], c.shape[0]
    assert ta % (SC_WORKERS * SC_GROUP) == 0 and tb % (SC_WORKERS * SC_GROUP) == 0
    return pl.kernel(
        functools.partial(_sc_step_body, tpw_a=ta // SC_WORKERS, tpw_b=tb // SC_WORKERS),
        out_type=[jax.ShapeDtypeStruct((ta, D_MODEL), F32), jax.ShapeDtypeStruct((tb, PEER_SEL), F32)],
        mesh=_sc_mesh(),
        scratch_types=[pltpu.VMEM((2, SC_STEPS, SC_ROWS), jnp.int32),
                       pltpu.VMEM((2, SC_GROUP, D_MODEL), F32),
                       pltpu.VMEM((SC_BUFS, SC_ROWS, D_MODEL), F32),
                       pltpu.VMEM((SC_GROUP, D_MODEL), F32),
                       pltpu.SemaphoreType.DMA((SC_BUFS,)),
                       pltpu.SemaphoreType.DMA((2,))],
        compiler_params=pltpu.CompilerParams(needs_layout_passes=False),
        name="peer_step",
    )(expert_v, idx_a.reshape(ta * SC_CHUNKS, SC_ROWS), coef, expert_u, idx_b.reshape(tb * SC_CHUNKS, SC_ROWS), c)


def _sc_dots(expert_u, idx, c):
    return _sc_call(_sc_dots_body, expert_u, idx, c, PEER_SEL, "peer_dots")


def _sc_combine(expert_v, idx, coef):
    return _sc_call(_sc_combine_body, expert_v, idx, coef, D_MODEL, "peer_combine")


def _ple_body(h1_ref, peer_ref, p_ref, gp_ref, wg_ref, we_ref, gfin_ref, y_ref):
    h = h1_ref[...] + peer_ref[...]
    a = _rms(h, gp_ref[...], NORM_EPS).astype(BF16)
    gate = jax.nn.sigmoid(_mm(a, wg_ref[...]))
    h = h + _mm(p_ref[...].astype(BF16), we_ref[...]) * gate
    y_ref[...] = _rms(h, gfin_ref[...], NORM_EPS)


def _ple(h1, peer, p, g_ple, w_pgate_b, w_ple_b, g_final, tok0):
    t = h1.shape[0]
    ts = min(512, t)
    first = tok0 // ts
    row = pl.BlockSpec((ts, D_MODEL), lambda i: (i, 0))
    full = lambda *shape: pl.BlockSpec(shape, lambda i: (0,) * len(shape))
    return pl.pallas_call(
        _ple_body,
        grid=(t // ts,),
        in_specs=[row, row, pl.BlockSpec((ts, PLE_DIM), lambda i: (first + i, 0)), full(1, D_MODEL),
                  full(D_MODEL, D_MODEL), full(PLE_DIM, D_MODEL), full(1, D_MODEL)],
        out_specs=row,
        out_shape=jax.ShapeDtypeStruct((t, D_MODEL), F32),
        compiler_params=_params("parallel"),
        name="ple_final",
    )(h1, peer, p, g_ple, w_pgate_b, w_ple_b, g_final)


def _rope_tables(pos):
    half = DIFF_COMP // 2
    inv = ROPE_THETA ** (-jnp.arange(0, DIFF_COMP, 2, dtype=F32) / DIFF_COMP)
    ang = pos.astype(F32)[:, None] * inv[None, :]
    cos = jnp.cos(ang)
    sin = jnp.sin(ang)
    reps = LANES // DIFF_COMP
    del half
    return (jnp.tile(jnp.concatenate([cos, cos], axis=-1), (1, reps)),
            jnp.tile(jnp.concatenate([-sin, sin], axis=-1), (1, reps)))


def kernel(x_prompt, x_sample, cache_diff_k, cache_diff_v, cache_sb_k, cache_sb_v, p_prompt, p_sample, g_mix, w_in, lambda_q1, lambda_k1, lambda_q2, lambda_k2, g_subln, w_out, g_ffn, w_query, sub_keys, expert_u, expert_v, g_ple, w_pgate, w_ple, g_final):
    assert w_in.shape[0] == 1, "single-layer encoder"
    nb, seq, _ = x_prompt.shape
    db, dq, _ = x_sample.shape
    past = cache_diff_k.shape[2]

    lam = (jnp.exp(jnp.sum(lambda_q1[0].astype(F32) * lambda_k1[0].astype(F32)))
           - jnp.exp(jnp.sum(lambda_q2[0].astype(F32) * lambda_k2[0].astype(F32))) + LAM_INIT).reshape(1)
    w_in_b = w_in[0].astype(BF16)
    w_out_b = w_out[0].astype(BF16)
    w_query_b = w_query[0].astype(BF16)
    sub_keys_b = sub_keys[0].astype(BF16)
    w_pgate_b = w_pgate[0].astype(BF16)
    w_ple_b = w_ple[0].astype(BF16)
    g_sub = g_subln[0].reshape(1, HEAD_DIM)
    g_fin = g_final.reshape(1, D_MODEL)

    def select(x, dout, sout, b, s):
        return _post(x, dout, sout, w_out_b, g_ffn, w_query_b, sub_keys_b, b, s)

    def finish(h1, peer, p, b):
        return _ple(h1, peer, p, g_ple, w_pgate_b, w_ple_b, g_fin, b)

    ts = db * dq
    xs = x_sample.reshape(ts, D_MODEL)
    cos_s, sin_s = _rope_tables(jnp.tile(past + jnp.arange(dq, dtype=jnp.int32), db))
    kd, vd, ks, vs, qd2, kdb, vdb, qsb, ksb, vsb = _proj(xs, g_mix, w_in_b, cos_s, sin_s, 1, ts, False)
    caches = [c[0].reshape(db, past, MIX) for c in (cache_diff_k, cache_diff_v, cache_sb_k, cache_sb_v)]
    dout_s, sout_s = _sample_attention(lam, qd2, kdb, vdb, qsb, ksb, vsb, *caches, g_sub, db, dq)
    rows_s = tuple(r.reshape(1, db, dq, HEADS, HEAD_DIM) for r in (kd, vd, ks, vs))

    xp = x_prompt.reshape(nb * seq, D_MODEL)
    pp = p_prompt[0].reshape(nb * seq, PLE_DIM)
    cos_p, sin_p = _rope_tables(jnp.arange(seq, dtype=jnp.int32))
    first_row = _proj(xp, g_mix, w_in_b, cos_p, sin_p, nb, seq, True, first=0, count=1)
    cache_rows = [first_row[:4]]
    operands = {0: (first_row[4:], 0)}

    def operands_for(b):
        if b not in operands:
            rest = _proj(xp, g_mix, w_in_b, cos_p, sin_p, nb, seq, True, first=1, count=nb - 1, into=cache_rows[0])
            cache_rows[0] = rest[:4]
            operands.update({r: (rest[4:], r - 1) for r in range(1, nb)})
        return operands[b]

    cut = seq // PROMPT_ROW_BLOCKS
    spans = [(b, i * cut, cut) for b in range(nb) for i in range(PROMPT_ROW_BLOCKS)]
    for end in (0, -1):
        b, t0, n = spans.pop(end)
        spans[end if end == 0 else len(spans):0] = [(b, t0, n // 2), (b, t0 + n // 2, n // 2)]

    def prompt_block(b, t0, n):
        (qd2, kdb, vdb, qsb, ksb, vsb), local = operands_for(b)
        dout = _diff_attention(lam, qd2, kdb, vdb, g_sub, local, t0, n)
        sout = _sb_attention(qsb, ksb, vsb, local, t0, n)
        return select(xp, dout, sout, b * seq + t0, n)

    blocks = [(functools.partial(prompt_block, *span), pp, span[0] * seq + span[1]) for span in spans]
    sample_at = min(PROMPT_ROW_BLOCKS + 1, len(blocks))
    blocks.insert(sample_at, (lambda: select(xs, dout_s, sout_s, 0, ts), p_sample[0].reshape(ts, PLE_DIM), 0))
    ys = []
    h1, c, idx, gate = blocks[0][0]()
    dots = _sc_dots(expert_u[0], idx, c)
    for k in range(1, len(blocks)):
        nh1, nc, nidx, ngate = blocks[k][0]()
        coef = _coef(gate, dots, nh1)
        peer, ndots = _sc_step(expert_v[0], idx, coef, expert_u[0], nidx, nc)
        ys.append(finish(h1, peer, *blocks[k - 1][1:]))
        h1, idx, gate, dots = nh1, nidx, ngate, ndots
    coef = _coef(gate, dots, lam)
    ys.append(finish(h1, _sc_combine(expert_v[0], idx, coef), *blocks[-1][1:]))
    y_sample = ys.pop(sample_at).reshape(db, dq, D_MODEL)
    y_prompt = jnp.concatenate(ys, axis=0).reshape(nb, seq, D_MODEL)
    rows_p = tuple(r.reshape(nb, HEADS, HEAD_DIM, seq).transpose(0, 3, 1, 2)[None] for r in cache_rows[0])

    return (y_prompt, y_sample) + rows_p + rows_s
```

```python
import functools
import math

import jax
import jax.numpy as jnp
from jax import lax
from jax.experimental import pallas as pl
from jax.experimental.pallas import tpu as pltpu
from jax.experimental.pallas import tpu_sc as plsc

F32 = jnp.float32
BF16 = jnp.bfloat16

D_MODEL = 1024
HEADS = 8
HEAD_DIM = 64
DIFF_COMP = 32
MIX = HEADS * HEAD_DIM
CHUNK = 64
ROPE_THETA = 10000.0
NORM_EPS = 1e-6
SUBLN_EPS = 1e-5
PEER_HEADS = 8
PEER_KEYS = 128
PEER_TOPK = 16
PEER_HALF = 128
PEER_SEL = PEER_HEADS * PEER_TOPK
PLE_DIM = 256
LAM_INIT = 0.8 - 0.6 * math.exp(-0.3 * 0)
SB_LOG_FLOOR = -104.0
PROMPT_ROW_BLOCKS = 4
PIPELINE_DEEPEN_AT = 3

LANES = 128
VMEM_LIMIT = 48 * 1024 * 1024

NT_DIMS = (((1,), (1,)), ((), ()))


def _nt(a, b):
    return lax.dot_general(a, b, NT_DIMS, preferred_element_type=F32)


def _mm(a, b):
    return jnp.dot(a, b, preferred_element_type=F32)


def _rms(x, g, eps):
    return x * lax.rsqrt(jnp.mean(x * x, axis=-1, keepdims=True) + eps) * g


def _params(*sem):
    return pltpu.CompilerParams(dimension_semantics=sem, vmem_limit_bytes=VMEM_LIMIT)


def _with_ones(v):
    n = v.shape[0]
    ones = (lax.broadcasted_iota(jnp.int32, (n, HEAD_DIM), 1) == 0).astype(v.dtype)
    return jnp.concatenate([v, ones], axis=1)


def _proj_body(x_ref, g_ref, w_ref, cos_ref, sin_ref, *refs, feature_major, aliased):
    (kd_ref, vd_ref, ks_ref, vs_ref,
     qd2_ref, kdb_ref, vdb_ref, qsb_ref, ksb_ref, vsb_ref) = refs[aliased:]
    ts = x_ref.shape[0]
    a = _rms(x_ref[...], g_ref[...], NORM_EPS).astype(BF16)
    cos = jnp.tile(cos_ref[...], (1, MIX // LANES))
    sin = jnp.tile(sin_ref[...], (1, MIX // LANES))
    lane = lax.broadcasted_iota(jnp.int32, (ts, MIX), 1)
    first_half = (lane % DIFF_COMP) < (DIFF_COMP // 2)

    def group(i):
        return _mm(a, w_ref[:, i * MIX:(i + 1) * MIX])

    def rope(t):
        partner = jnp.where(first_half,
                            pltpu.roll(t, MIX - DIFF_COMP // 2, 1),
                            pltpu.roll(t, DIFF_COMP // 2, 1))
        return t * cos + partner * sin

    qd = rope(group(0)) * (DIFF_COMP ** -0.5)
    kd = rope(group(1))
    vd = group(2)
    qs = group(3) * (HEAD_DIM ** -0.5)
    ks = group(4)
    vs = group(5)
    for ref, rows in ((kd_ref, kd), (vd_ref, vd), (ks_ref, ks), (vs_ref, vs)):
        if feature_major:
            ref[0] = rows.T
        else:
            ref[...] = rows
    comp0 =lax.broadcasted_iota(jnp.int32, (ts, HEAD_DIM), 1) < DIFF_COMP
    for h in range(HEADS):
        sl = slice(h * HEAD_DIM, (h + 1) * HEAD_DIM)
        qh = qd[:, sl]
        qd2_ref[0, h, 0] = jnp.where(comp0, qh, 0.0).astype(BF16)
        qd2_ref[0, h, 1] = jnp.where(comp0, 0.0, qh).astype(BF16)
        kdb_ref[0, h] = kd[:, sl].astype(BF16)
        vdb_ref[0, h] = _with_ones(vd[:, sl].astype(BF16))
        qsb_ref[0, h] = qs[:, sl].astype(BF16)
        ksb_ref[0, h] = ks[:, sl].astype(BF16)
        vsb_ref[0, h] = vs[:, sl].astype(BF16)


def _proj(x, g_mix, w_in_b, cos_t, sin_t, nb, seq, feature_major, first=0, count=None, into=()):
    count = nb - first if count is None else count
    ts = min(256, seq)
    nst = seq // ts
    if feature_major:
        row = pl.BlockSpec((1, MIX, ts), lambda i: (first + i // nst, 0, i % nst))
        rows = jax.ShapeDtypeStruct((nb, MIX, seq), F32)
    else:
        row = pl.BlockSpec((ts, MIX), lambda i: (first * nst + i, 0))
        rows = jax.ShapeDtypeStruct((nb * seq, MIX), F32)
    hm = pl.BlockSpec((1, HEADS, ts, HEAD_DIM), lambda i: (i // nst, 0, i % nst, 0))
    hm2 = pl.BlockSpec((1, HEADS, 2, ts, HEAD_DIM), lambda i: (i // nst, 0, 0, i % nst, 0))
    heads = jax.ShapeDtypeStruct((count, HEADS, seq, HEAD_DIM), BF16)
    heads2 = jax.ShapeDtypeStruct((count, HEADS, 2, seq, HEAD_DIM), BF16)
    hm_ext = pl.BlockSpec((1, HEADS, ts, 2 * HEAD_DIM), lambda i: (i // nst, 0, i % nst, 0))
    heads_ext = jax.ShapeDtypeStruct((count, HEADS, seq, 2 * HEAD_DIM), BF16)
    n_in = 5
    return pl.pallas_call(
        functools.partial(_proj_body, feature_major=feature_major, aliased=len(into)),
        grid=(count * nst,),
        in_specs=[
            pl.BlockSpec((ts, D_MODEL), lambda i: (first * nst + i, 0)),
            pl.BlockSpec((1, D_MODEL), lambda i: (0, 0)),
            pl.BlockSpec((D_MODEL, 6 * MIX), lambda i: (0, 0)),
            pl.BlockSpec((ts, LANES), lambda i: (i % nst, 0)),
            pl.BlockSpec((ts, LANES), lambda i: (i % nst, 0)),
        ] + [pl.BlockSpec(memory_space=pl.ANY)] * len(into),
        out_specs=[row, row, row, row, hm2, hm, hm_ext, hm, hm, hm],
        out_shape=[rows, rows, rows, rows, heads2, heads, heads_ext, heads, heads, heads],
        input_output_aliases={n_in + j: j for j in range(len(into))},
        compiler_params=_params("parallel"),
        name="proj",
    )(x, g_mix, w_in_b, cos_t, sin_t, *into)


def _diff_init(rows):
    return jnp.full((rows, 1), -jnp.inf, F32), jnp.zeros((rows, 2 * HEAD_DIM), F32)


def _diff_update(s, v_ext, carry):
    m, acc = carry
    m_new = jnp.maximum(m, jnp.max(s, axis=-1, keepdims=True))
    p = jnp.exp(s - m_new)
    acc = jnp.exp(m - m_new) * acc + _mm(p.astype(BF16), v_ext)
    return m_new, acc


def _diff_finish(carry, lam, g_subln, tq):
    _, acc = carry
    o = acc[:, :HEAD_DIM] / acc[:, HEAD_DIM:HEAD_DIM + 1]
    d = o[:tq] - lam * o[tq:]
    return _rms(d, g_subln, SUBLN_EPS) * (1.0 - LAM_INIT)


def _suffix_sums(lk, tri):
    hi = lk.astype(BF16)
    lo = (lk - hi.astype(F32)).astype(BF16)
    return _mm(hi, tri) + _mm(lo, tri)


def _sb_update(q, k, v, tri, earlier, carry):
    run, acc = carry
    z = _nt(q, k)
    sp = jnp.maximum(z, 0.0) + jnp.log1p(jnp.exp(-jnp.abs(z)))
    lk = -sp if earlier is None else jnp.where(earlier, -sp, 0.0)
    after = _suffix_sums(lk, tri)
    w = jnp.exp((z - sp) + after + run)
    if earlier is not None:
        w = jnp.where(earlier, w, 0.0)
    acc = acc + _mm(w.astype(BF16), v)
    run = run + after[:, 0:1] + lk[:, 0:1]
    return run, acc


def _tri(n):
    j = lax.broadcasted_iota(jnp.int32, (n, n), 0)
    s = lax.broadcasted_iota(jnp.int32, (n, n), 1)
    return (j > s).astype(BF16)


def _diff_body(lam_ref, q_ref, k_ref, v_ref, g_ref, o_ref, *, tq, q0):
    qi = q0 + pl.program_id(1)
    q2 = q_ref[0, 0].reshape(2 * tq, HEAD_DIM)

    def scores(j):
        return _nt(q2, k_ref[0, 0, pl.ds(pl.multiple_of(j * tq, tq), tq), :])

    def step(j, state):
        s, carry = state
        s_next = scores(j + 1)
        return s_next, _diff_update(s, v_ref[0, 0, pl.ds(pl.multiple_of(j * tq, tq), tq), :], carry)

    s, carry = lax.fori_loop(0, qi, step, (scores(0), _diff_init(2 * tq)))
    r = lax.broadcasted_iota(jnp.int32, (2 * tq, tq), 0) % tq
    c = lax.broadcasted_iota(jnp.int32, (2 * tq, tq), 1)
    s = jnp.where((c // CHUNK) <= (r // CHUNK), s, -jnp.inf)
    carry = _diff_update(s, v_ref[0, 0, pl.ds(pl.multiple_of(qi * tq, tq), tq), :], carry)
    o_ref[0, 0] = _diff_finish(carry, lam_ref[0], g_ref[...], tq)


def _diff_attention(lam, qd2, kdb, vdb, g_subln, b, t0, n):
    seq = qd2.shape[3]
    tq = min(512, n)
    q0 = t0 // tq
    return pl.pallas_call(
        functools.partial(_diff_body, tq=tq, q0=q0),
        grid=(HEADS, n // tq),
        in_specs=[
            pl.BlockSpec(memory_space=pltpu.SMEM),
            pl.BlockSpec((1, 1, 2, tq, HEAD_DIM), lambda h, i: (b, h, 0, q0 + i, 0)),
            pl.BlockSpec((1, 1, seq, HEAD_DIM), lambda h, i: (b, h, 0, 0)),
            pl.BlockSpec((1, 1, seq, 2 * HEAD_DIM), lambda h, i: (b, h, 0, 0)),
            pl.BlockSpec((1, HEAD_DIM), lambda h, i: (0, 0)),
        ],
        out_specs=pl.BlockSpec((1, 1, tq, HEAD_DIM), lambda h, i: (0, h, i, 0)),
        out_shape=jax.ShapeDtypeStruct((1, HEADS, n, HEAD_DIM), F32),
        compiler_params=_params("parallel", "arbitrary"),
        name="diff_attention",
    )(lam, qd2, kdb, vdb, g_subln)


def _sb_body(q_ref, k_ref, v_ref, o_ref, *, tq, q0):
    qi = q0 + pl.program_id(1)
    q = q_ref[0, 0]
    tri = _tri(tq)

    def tile(j):
        start = pl.multiple_of(j * tq, tq)
        return k_ref[0, 0, pl.ds(start, tq), :], v_ref[0, 0, pl.ds(start, tq), :]

    r = lax.broadcasted_iota(jnp.int32, (tq, tq), 0)
    c = lax.broadcasted_iota(jnp.int32, (tq, tq), 1)
    carry = (jnp.zeros((tq, 1), F32), jnp.zeros((tq, HEAD_DIM), F32))
    run, acc = _sb_update(q, *tile(qi), tri, c < r, carry)

    def live(state):
        j, run, _ = state
        return jnp.logical_and(j >= 0, jnp.max(run) > SB_LOG_FLOOR)

    def step(state):
        j, run, acc = state
        run, acc = _sb_update(q, *tile(j), tri, None, (run, acc))
        return j - 1, run, acc

    o_ref[0, 0] = lax.while_loop(live, step, (qi - 1, run, acc))[2]


def _sb_attention(qsb, ksb, vsb, b, t0, n):
    seq = qsb.shape[2]
    tq = min(256, n)
    q0 = t0 // tq
    kv = pl.BlockSpec((1, 1, seq, HEAD_DIM), lambda h, i: (b, h, 0, 0))
    return pl.pallas_call(
        functools.partial(_sb_body, tq=tq, q0=q0),
        grid=(HEADS, n // tq),
        in_specs=[pl.BlockSpec((1, 1, tq, HEAD_DIM), lambda h, i: (b, h, q0 + i, 0)), kv, kv],
        out_specs=pl.BlockSpec((1, 1, tq, HEAD_DIM), lambda h, i: (0, h, i, 0)),
        out_shape=jax.ShapeDtypeStruct((1, HEADS, n, HEAD_DIM), F32),
        compiler_params=_params("parallel", "arbitrary"),
        name="sb_attention",
    )(qsb, ksb, vsb)


SAMPLE_HEADS = 4
SAMPLE_TILE = 256


def _sample_body(lam_ref, qd2_ref, kdn_ref, vdn_ref, qs_ref, ksn_ref, vsn_ref,
                 ckd_ref, cvd_ref, cks_ref, cvs_ref, g_ref, do_ref, so_ref, *, past, nq):
    lam = lam_ref[0]
    tri_c = _tri(SAMPLE_TILE)
    tri_n = _tri(nq)
    i2 = lax.broadcasted_iota(jnp.int32, (2 * nq, nq), 0) % nq
    j2 = lax.broadcasted_iota(jnp.int32, (2 * nq, nq), 1)
    visible_new = ((past + j2) // CHUNK) <= ((past + i2) // CHUNK)
    i1 = lax.broadcasted_iota(jnp.int32, (nq, nq), 0)
    j1 = lax.broadcasted_iota(jnp.int32, (nq, nq), 1)
    earlier_new = j1 < i1
    for h in range(SAMPLE_HEADS):
        sl = slice(h * HEAD_DIM, (h + 1) * HEAD_DIM)
        q2 = qd2_ref[0, h].reshape(2 * nq, HEAD_DIM)
        carry = _diff_update(_nt(q2, ckd_ref[0, :, sl].astype(BF16)),
                             _with_ones(cvd_ref[0, :, sl].astype(BF16)), _diff_init(2 * nq))
        s_new = jnp.where(visible_new, _nt(q2, kdn_ref[0, h]), -jnp.inf)
        carry = _diff_update(s_new, vdn_ref[0, h], carry)
        do_ref[0, h] = _diff_finish(carry, lam, g_ref[...], nq)
        q = qs_ref[0, h]
        carry = (jnp.zeros((nq, 1), F32), jnp.zeros((nq, HEAD_DIM), F32))
        carry = _sb_update(q, ksn_ref[0, h], vsn_ref[0, h], tri_n, earlier_new, carry)
        for t in reversed(range(past // SAMPLE_TILE)):
            rows = slice(t * SAMPLE_TILE, (t + 1) * SAMPLE_TILE)
            carry = _sb_update(q, cks_ref[0, rows, sl].astype(BF16), cvs_ref[0, rows, sl].astype(BF16),
                               tri_c, None, carry)
        so_ref[0, h] = carry[1]


def _sample_attention(lam, qd2, kdb, vdb, qsb, ksb, vsb, ckd, cvd, cks, cvs, g_subln, nb, nq):
    past = ckd.shape[1]
    nhg = HEADS // SAMPLE_HEADS
    hm = pl.BlockSpec((1, SAMPLE_HEADS, nq, HEAD_DIM), lambda b, g: (0, g, b, 0))
    hm2 = pl.BlockSpec((1, SAMPLE_HEADS, 2, nq, HEAD_DIM), lambda b, g: (0, g, 0, b, 0))
    cache = pl.BlockSpec((1, past, SAMPLE_HEADS * HEAD_DIM), lambda b, g: (b, 0, g))
    out = jax.ShapeDtypeStruct((1, HEADS, nb * nq, HEAD_DIM), F32)
    return pl.pallas_call(
        functools.partial(_sample_body, past=past, nq=nq),
        grid=(nb, nhg),
        in_specs=[pl.BlockSpec(memory_space=pltpu.SMEM), hm2, hm,
                  pl.BlockSpec((1, SAMPLE_HEADS, nq, 2 * HEAD_DIM), lambda b, g: (0, g, b, 0)), hm, hm, hm,
                  cache, cache, cache, cache,
                  pl.BlockSpec((1, HEAD_DIM), lambda b, g: (0, 0))],
        out_specs=[hm, hm],
        out_shape=[out, out],
        compiler_params=_params("parallel", "parallel"),
        name="sample_attention",
    )(lam, qd2, kdb, vdb, qsb, ksb, vsb, ckd, cvd, cks, cvs, g_subln)


def _topk_rows(s, k):
    n = s.shape[0]
    rows = lax.broadcasted_iota(jnp.int32, s.shape, 0)
    vals, ids = [], []
    for _ in range(k):
        m = jnp.max(s, axis=0, keepdims=True)
        i = jnp.min(jnp.where(s == m, rows, n), axis=0, keepdims=True)
        vals.append(m)
        ids.append(i)
        s = jnp.where(rows == i, -jnp.inf, s)
    return jnp.concatenate(vals, axis=0), jnp.concatenate(ids, axis=0)


_STAIR = [(i, j) for i in range(PEER_TOPK) for j in range(PEER_TOPK) if (i + 1) * (j + 1) <= PEER_TOPK]


def _post_body(x_ref, do_ref, so_ref, wo_ref, gf_ref, wq_ref, sk_ref,
               h1_ref, c_ref, idx_ref, gate_ref, q_scr, idx_scr, gate_scr):
    ts = x_ref.shape[0]
    mixed = jnp.zeros((ts, D_MODEL), F32)
    for h in range(HEADS):
        mixed += _mm(do_ref[0, h].astype(BF16), wo_ref[h * HEAD_DIM:(h + 1) * HEAD_DIM, :])
        mixed += _mm(so_ref[0, h].astype(BF16), wo_ref[MIX + h * HEAD_DIM:MIX + (h + 1) * HEAD_DIM, :])
    h1 = x_ref[...] + mixed
    h1_ref[...] = h1
    c = _rms(h1, gf_ref[...], NORM_EPS)
    c_ref[...] = c
    q = _mm(c.astype(BF16), wq_ref[...])
    for hp in range(2 * PEER_HEADS):
        q_scr[hp] = q[:, hp * PEER_HALF:(hp + 1) * PEER_HALF].astype(BF16)

    npad = -len(_STAIR) % 8

    def head(h, _):
        v1, i1 = _topk_rows(_nt(sk_ref[0], q_scr[2 * h]), PEER_TOPK)
        v2, i2 = _topk_rows(_nt(sk_ref[1], q_scr[2 * h + 1]), PEER_TOPK)
        cand = jnp.concatenate([v1[i:i + 1] + v2[j:j + 1] for i, j in _STAIR]
                               + [jnp.full((npad, ts), -jnp.inf, F32)], axis=0)
        eid = jnp.concatenate([i1[i:i + 1] * PEER_KEYS + i2[j:j + 1] for i, j in _STAIR]
                              + [jnp.zeros((npad, ts), jnp.int32)], axis=0)
        top, pos = _topk_rows(cand, PEER_TOPK)
        rows = lax.broadcasted_iota(jnp.int32, cand.shape, 0)
        sel = jnp.concatenate([jnp.sum(jnp.where(rows == pos[r:r + 1], eid, 0), axis=0, keepdims=True)
                               for r in range(PEER_TOPK)], axis=0)
        e = jnp.exp(top - top[0:1])
        gate_scr[h] = e / jnp.sum(e, axis=0, keepdims=True)
        idx_scr[h] = sel
        return 0

    lax.fori_loop(0, PEER_HEADS, head, 0)
    idx_ref[...] = idx_scr[...].reshape(PEER_SEL, ts).T
    gate_ref[...] = gate_scr[...].reshape(PEER_SEL, ts).T


def _post(x, dout, sout, w_out_b, g_ffn, w_query_b, sub_keys_b, tok0, t):
    ts = min(256, t)
    first = tok0 // ts
    row = pl.BlockSpec((ts, D_MODEL), lambda i: (i, 0))
    hm = pl.BlockSpec((1, HEADS, ts, HEAD_DIM), lambda i: (0, 0, i, 0))
    sel = pl.BlockSpec((ts, PEER_SEL), lambda i: (i, 0))
    full = lambda *shape: pl.BlockSpec(shape, lambda i: (0,) * len(shape))
    return pl.pallas_call(
        _post_body,
        grid=(t // ts,),
        in_specs=[pl.BlockSpec((ts, D_MODEL), lambda i: (first + i, 0)),
                  hm, hm, full(2 * MIX, D_MODEL), full(1, D_MODEL),
                  full(D_MODEL, 2 * PEER_HEADS * PEER_HALF), full(2, PEER_KEYS, PEER_HALF)],
        out_specs=[row, row, sel, sel],
        out_shape=[jax.ShapeDtypeStruct((t, D_MODEL), F32), jax.ShapeDtypeStruct((t, D_MODEL), F32),
                   jax.ShapeDtypeStruct((t, PEER_SEL), jnp.int32), jax.ShapeDtypeStruct((t, PEER_SEL), F32)],
        scratch_shapes=[pltpu.VMEM((2 * PEER_HEADS, ts, PEER_HALF), BF16),
                        pltpu.VMEM((PEER_HEADS, PEER_TOPK, ts), jnp.int32),
                        pltpu.VMEM((PEER_HEADS, PEER_TOPK, ts), F32)],
        compiler_params=_params("parallel"),
        name="post_peer_select",
    )(x, dout, sout, w_out_b, g_ffn, w_query_b, sub_keys_b)


def _coef_body(after_ref, gate_ref, dots_ref, o_ref):
    del after_ref
    d = dots_ref[...]
    o_ref[...] = gate_ref[...] * (0.5 * d * (1.0 + lax.erf(d * (2.0 ** -0.5))))


def _coef(gate, dots, after):
    t = gate.shape[0]
    ts = min(2048, t)
    blk = pl.BlockSpec((ts, PEER_SEL), lambda i: (i, 0))
    return pl.pallas_call(
        _coef_body, grid=(t // ts,), in_specs=[pl.BlockSpec(memory_space=pl.ANY), blk, blk], out_specs=blk,
        out_shape=jax.ShapeDtypeStruct((t, PEER_SEL), F32),
        compiler_params=_params("parallel"), name="peer_coef",
    )(after, gate, dots)


SC_CORES = 2
SC_SUBCORES = 16
SC_LANES = 16
SC_WORKERS = SC_CORES * SC_SUBCORES
SC_ROWS = 16
SC_BUFS = 5
SC_GROUP = 8
SC_CHUNKS = PEER_SEL // SC_ROWS
SC_STEPS = SC_GROUP * SC_CHUNKS
SC_VECS = D_MODEL // SC_LANES


def _sc_mesh():
    return plsc.VectorSubcoreMesh(core_axis_name="c", subcore_axis_name="s",
                                  num_cores=SC_CORES, num_subcores=SC_SUBCORES)


def _sc_walk(table_hbm, idx_hbm, aux_hbm, idx_v, aux_v, rows_v, sem, stage_sem, tpw, begin_group, compute, end_group):
    tok_base = (lax.axis_index("s") * SC_CORES + lax.axis_index("c")) * tpw
    ngroups = tpw // SC_GROUP
    nsteps = tpw * SC_CHUNKS

    def first_token(g):
        return pl.multiple_of(tok_base + g * SC_GROUP, SC_GROUP)

    def stage(g):
        tok0 = first_token(g)
        return (pltpu.make_async_copy(idx_hbm.at[pl.ds(tok0 * SC_CHUNKS, SC_STEPS)], idx_v.at[g % 2], stage_sem.at[0]),
                pltpu.make_async_copy(aux_hbm.at[pl.ds(tok0, SC_GROUP)],
                                      aux_v.at[g % 2, :, pl.ds(0, aux_hbm.shape[1])], stage_sem.at[1]))

    def gather(step):
        idx = idx_v.at[(step // SC_STEPS) % 2, step % SC_STEPS]
        return pltpu.make_async_copy(table_hbm.at[idx], rows_v.at[step % SC_BUFS], sem.at[step % SC_BUFS])

    for cp in stage(0):
        cp.start()
    for cp in stage(0):
        cp.wait()
    for step in range(SC_BUFS - 1):
        gather(step).start()

    def walk(step, _):
        g = step // SC_STEPS
        local = step % SC_STEPS

        @pl.when(jnp.logical_and(local == 0, g + 1 < ngroups))
        def _():
            for cp in stage(g + 1):
                cp.start()

        ahead = step + (SC_BUFS - 1)

        @pl.when(ahead < nsteps)
        def _():
            @pl.when(ahead % SC_STEPS == 0)
            def _():
                for cp in stage(ahead // SC_STEPS):
                    cp.wait()

            gather(ahead).start()

        @pl.when(local == 0)
        def _():
            begin_group()

        gather(step).wait()
        compute(rows_v.at[step % SC_BUFS], g % 2, local // SC_CHUNKS, local % SC_CHUNKS)

        @pl.when(local == SC_STEPS - 1)
        def _():
            end_group(first_token(g))

        return 0

    lax.fori_loop(0, nsteps, walk, 0)


def _sc_dots_body(u_hbm, idx_hbm, c_hbm, out_hbm, idx_v, c_v, rows_v, dots_v, sem, stage_sem, *, tpw):
    lane = lax.broadcasted_iota(jnp.int32, (SC_LANES,), 0)
    zero = jnp.zeros((SC_LANES,), F32)

    def compute(rows, slot, tt, ch):
        for half in range(SC_ROWS // SC_LANES):
            r0 = half * SC_LANES

            def vec(kk, accs):
                off = pl.multiple_of(kk * SC_LANES, SC_LANES)
                cv = c_v[slot, tt, pl.ds(off, SC_LANES)]
                return tuple(a + rows[r0 + r, pl.ds(off, SC_LANES)] * cv for r, a in enumerate(accs))

            accs = lax.fori_loop(0, SC_VECS, vec, (zero,) * SC_LANES)
            outv = zero
            for r in range(SC_LANES):
                outv = jnp.where(lane == r, jnp.sum(accs[r]), outv)
            dots_v[tt, pl.ds(pl.multiple_of(ch * SC_ROWS + half * SC_LANES, SC_LANES), SC_LANES)] = outv

    def end_group(tok0):
        pltpu.sync_copy(dots_v.at[:, pl.ds(0, PEER_SEL)], out_hbm.at[pl.ds(tok0, SC_GROUP)])

    _sc_walk(u_hbm, idx_hbm, c_hbm, idx_v, c_v, rows_v, sem, stage_sem, tpw, lambda: None, compute, end_group)


def _sc_combine_body(v_hbm, idx_hbm, coef_hbm, out_hbm, idx_v, coef_v, rows_v, acc_v, sem, stage_sem, *, tpw):
    zero = jnp.zeros((SC_LANES,), F32)

    def compute(rows, slot, tt, ch):
        slot_v = jnp.full((SC_LANES,), slot, jnp.int32)
        tok_v = jnp.full((SC_LANES,), tt, jnp.int32)
        for half in range(SC_ROWS // SC_LANES):
            col0 = ch * SC_ROWS + half * SC_LANES
            splat = [plsc.load_gather(coef_v, [slot_v, tok_v, jnp.full((SC_LANES,), col0 + r, jnp.int32)])
                     for r in range(SC_LANES)]

            @plsc.parallel_loop(0, SC_VECS, unroll=2)
            def _(kk):
                off = pl.multiple_of(kk * SC_LANES, SC_LANES)
                terms = [rows[half * SC_LANES + r, pl.ds(off, SC_LANES)] * splat[r] for r in range(SC_LANES)]
                while len(terms) > 1:
                    terms = [a + b for a, b in zip(terms[0::2], terms[1::2])]
                acc_v[tt, pl.ds(off, SC_LANES)] = acc_v[tt, pl.ds(off, SC_LANES)] + terms[0]

    def begin_group():
        def clear(i, _):
            acc_v[i // SC_VECS, pl.ds(pl.multiple_of((i % SC_VECS) * SC_LANES, SC_LANES), SC_LANES)] = zero
            return 0

        lax.fori_loop(0, SC_GROUP * SC_VECS, clear, 0)

    def end_group(tok0):
        pltpu.sync_copy(acc_v, out_hbm.at[pl.ds(tok0, SC_GROUP)])

    _sc_walk(v_hbm, idx_hbm, coef_hbm, idx_v, coef_v, rows_v, sem, stage_sem, tpw, begin_group, compute, end_group)


def _sc_call(body, table, idx, per_token, out_width, name):
    t = per_token.shape[0]
    tpw = t // SC_WORKERS
    assert tpw % SC_GROUP == 0
    return pl.kernel(
        functools.partial(body, tpw=tpw),
        out_type=jax.ShapeDtypeStruct((t, out_width), F32),
        mesh=_sc_mesh(),
        scratch_types=[pltpu.VMEM((2, SC_STEPS, SC_ROWS), jnp.int32),
                       pltpu.VMEM((2, SC_GROUP, per_token.shape[1]), F32),
                       pltpu.VMEM((SC_BUFS, SC_ROWS, D_MODEL), F32),
                       pltpu.VMEM((SC_GROUP, out_width), F32),
                       pltpu.SemaphoreType.DMA((SC_BUFS,)),
                       pltpu.SemaphoreType.DMA((2,))],
        compiler_params=pltpu.CompilerParams(needs_layout_passes=False),
        name=name,
    )(table, idx.reshape(t * SC_CHUNKS, SC_ROWS), per_token)


def _sc_step_body(v_hbm, idx_a_hbm, coef_hbm, u_hbm, idx_b_hbm, c_hbm, peer_hbm, dots_hbm,
                  idx_v, aux_v, rows_v, out_v, sem, stage_sem, *, tpw_a, tpw_b):
    _sc_combine_body(v_hbm, idx_a_hbm, coef_hbm, peer_hbm, idx_v, aux_v, rows_v, out_v, sem, stage_sem, tpw=tpw_a)
    _sc_dots_body(u_hbm, idx_b_hbm, c_hbm, dots_hbm, idx_v, aux_v, rows_v, out_v, sem, stage_sem, tpw=tpw_b)


def _sc_step(expert_v, idx_a, coef, expert_u, idx_b, c):
    ta, tb = coef.shape[0], c.shape[0]
    assert ta % (SC_WORKERS * SC_GROUP) == 0 and tb % (SC_WORKERS * SC_GROUP) == 0
    return pl.kernel(
        functools.partial(_sc_step_body, tpw_a=ta // SC_WORKERS, tpw_b=tb // SC_WORKERS),
        out_type=[jax.ShapeDtypeStruct((ta, D_MODEL), F32), jax.ShapeDtypeStruct((tb, PEER_SEL), F32)],
        mesh=_sc_mesh(),
        scratch_types=[pltpu.VMEM((2, SC_STEPS, SC_ROWS), jnp.int32),
                       pltpu.VMEM((2, SC_GROUP, D_MODEL), F32),
                       pltpu.VMEM((SC_BUFS, SC_ROWS, D_MODEL), F32),
                       pltpu.VMEM((SC_GROUP, D_MODEL), F32),
                       pltpu.SemaphoreType.DMA((SC_BUFS,)),
                       pltpu.SemaphoreType.DMA((2,))],
        compiler_params=pltpu.CompilerParams(needs_layout_passes=False),
        name="peer_step",
    )(expert_v, idx_a.reshape(ta * SC_CHUNKS, SC_ROWS), coef, expert_u, idx_b.reshape(tb * SC_CHUNKS, SC_ROWS), c)


def _sc_dots(expert_u, idx, c):
    return _sc_call(_sc_dots_body, expert_u, idx, c, PEER_SEL, "peer_dots")


def _sc_combine(expert_v, idx, coef):
    return _sc_call(_sc_combine_body, expert_v, idx, coef, D_MODEL, "peer_combine")


def _ple_body(h1_ref, peer_ref, p_ref, gp_ref, wg_ref, we_ref, gfin_ref, y_ref):
    h = h1_ref[...] + peer_ref[...]
    a = _rms(h, gp_ref[...], NORM_EPS).astype(BF16)
    gate = jax.nn.sigmoid(_mm(a, wg_ref[...]))
    h = h + _mm(p_ref[...].astype(BF16), we_ref[...]) * gate
    y_ref[...] = _rms(h, gfin_ref[...], NORM_EPS)


def _ple(h1, peer, p, g_ple, w_pgate_b, w_ple_b, g_final, tok0):
    t = h1.shape[0]
    ts = min(512, t)
    first = tok0 // ts
    row = pl.BlockSpec((ts, D_MODEL), lambda i: (i, 0))
    full = lambda *shape: pl.BlockSpec(shape, lambda i: (0,) * len(shape))
    return pl.pallas_call(
        _ple_body,
        grid=(t // ts,),
        in_specs=[row, row, pl.BlockSpec((ts, PLE_DIM), lambda i: (first + i, 0)), full(1, D_MODEL),
                  full(D_MODEL, D_MODEL), full(PLE_DIM, D_MODEL), full(1, D_MODEL)],
        out_specs=row,
        out_shape=jax.ShapeDtypeStruct((t, D_MODEL), F32),
        compiler_params=_params("parallel"),
        name="ple_final",
    )(h1, peer, p, g_ple, w_pgate_b, w_ple_b, g_final)


def _rope_tables(pos):
    half = DIFF_COMP // 2
    inv = ROPE_THETA ** (-jnp.arange(0, DIFF_COMP, 2, dtype=F32) / DIFF_COMP)
    ang = pos.astype(F32)[:, None] * inv[None, :]
    cos = jnp.cos(ang)
    sin = jnp.sin(ang)
    reps = LANES // DIFF_COMP
    del half
    return (jnp.tile(jnp.concatenate([cos, cos], axis=-1), (1, reps)),
            jnp.tile(jnp.concatenate([-sin, sin], axis=-1), (1, reps)))


def kernel(x_prompt, x_sample, cache_diff_k, cache_diff_v, cache_sb_k, cache_sb_v, p_prompt, p_sample, g_mix, w_in, lambda_q1, lambda_k1, lambda_q2, lambda_k2, g_subln, w_out, g_ffn, w_query, sub_keys, expert_u, expert_v, g_ple, w_pgate, w_ple, g_final):
    assert w_in.shape[0] == 1, "single-layer encoder"
    nb, seq, _ = x_prompt.shape
    db, dq, _ = x_sample.shape
    past = cache_diff_k.shape[2]

    lam = (jnp.exp(jnp.sum(lambda_q1[0].astype(F32) * lambda_k1[0].astype(F32)))
           - jnp.exp(jnp.sum(lambda_q2[0].astype(F32) * lambda_k2[0].astype(F32))) + LAM_INIT).reshape(1)
    w_in_b = w_in[0].astype(BF16)
    w_out_b = w_out[0].astype(BF16)
    w_query_b = w_query[0].astype(BF16)
    sub_keys_b = sub_keys[0].astype(BF16)
    w_pgate_b = w_pgate[0].astype(BF16)
    w_ple_b = w_ple[0].astype(BF16)
    g_sub = g_subln[0].reshape(1, HEAD_DIM)
    g_fin = g_final.reshape(1, D_MODEL)

    def select(x, dout, sout, b, s):
        return _post(x, dout, sout, w_out_b, g_ffn, w_query_b, sub_keys_b, b, s)

    def finish(h1, peer, p, b):
        return _ple(h1, peer, p, g_ple, w_pgate_b, w_ple_b, g_fin, b)

    ts = db * dq
    xs = x_sample.reshape(ts, D_MODEL)
    cos_s, sin_s = _rope_tables(jnp.tile(past + jnp.arange(dq, dtype=jnp.int32), db))
    kd, vd, ks, vs, qd2, kdb, vdb, qsb, ksb, vsb = _proj(xs, g_mix, w_in_b, cos_s, sin_s, 1, ts, False)
    caches = [c[0].reshape(db, past, MIX) for c in (cache_diff_k, cache_diff_v, cache_sb_k, cache_sb_v)]
    dout_s, sout_s = _sample_attention(lam, qd2, kdb, vdb, qsb, ksb, vsb, *caches, g_sub, db, dq)
    rows_s = tuple(r.reshape(1, db, dq, HEADS, HEAD_DIM) for r in (kd, vd, ks, vs))

    xp = x_prompt.reshape(nb * seq, D_MODEL)
    pp = p_prompt[0].reshape(nb * seq, PLE_DIM)
    cos_p, sin_p = _rope_tables(jnp.arange(seq, dtype=jnp.int32))
    first_row = _proj(xp, g_mix, w_in_b, cos_p, sin_p, nb, seq, True, first=0, count=1)
    cache_rows = [first_row[:4]]
    operands = {0: (first_row[4:], 0)}

    def operands_for(b):
        if b not in operands:
            rest = _proj(xp, g_mix, w_in_b, cos_p, sin_p, nb, seq, True, first=1, count=nb - 1, into=cache_rows[0])
            cache_rows[0] = rest[:4]
            operands.update({r: (rest[4:], r - 1) for r in range(1, nb)})
        return operands[b]

    cut = seq // PROMPT_ROW_BLOCKS
    spans = [(b, i * cut, cut) for b in range(nb) for i in range(PROMPT_ROW_BLOCKS)]
    for end in (0, -1):
        b, t0, n = spans.pop(end)
        spans[end if end == 0 else len(spans):0] = [(b, t0, n // 2), (b, t0 + n // 2, n // 2)]

    def prompt_block(b, t0, n):
        (qd2, kdb, vdb, qsb, ksb, vsb), local = operands_for(b)
        dout = _diff_attention(lam, qd2, kdb, vdb, g_sub, local, t0, n)
        sout = _sb_attention(qsb, ksb, vsb, local, t0, n)
        return select(xp, dout, sout, b * seq + t0, n)

    blocks = [(functools.partial(prompt_block, *span), pp, span[0] * seq + span[1]) for span in spans]
    sample_at = min(PROMPT_ROW_BLOCKS + 1, len(blocks))
    blocks.insert(sample_at, (lambda: select(xs, dout_s, sout_s, 0, ts), p_sample[0].reshape(ts, PLE_DIM), 0))
    ys = []
    made = {}

    def block(k):
        if k not in made:
            made[k] = blocks[k][0]()
        return made[k]

    h1, c, idx, gate = block(0)
    dots = _sc_dots(expert_u[0], idx, c)
    for k in range(1, len(blocks)):
        nh1, nc, nidx, ngate = block(k)
        ahead = block(min(k + 1, len(blocks) - 1))[0] if k >= PIPELINE_DEEPEN_AT else nh1
        coef = _coef(gate, dots, ahead)
        peer, ndots = _sc_step(expert_v[0], idx, coef, expert_u[0], nidx, nc)
        ys.append(finish(h1, peer, *blocks[k - 1][1:]))
        h1, idx, gate, dots = nh1, nidx, ngate, ndots
    coef = _coef(gate, dots, lam)
    ys.append(finish(h1, _sc_combine(expert_v[0], idx, coef), *blocks[-1][1:]))
    y_sample = ys.pop(sample_at).reshape(db, dq, D_MODEL)
    y_prompt = jnp.concatenate(ys, axis=0).reshape(nb, seq, D_MODEL)
    rows_p = tuple(r.reshape(nb, HEADS, HEAD_DIM, seq).transpose(0, 3, 1, 2)[None] for r in cache_rows[0])

    return (y_prompt, y_sample) + rows_p + rows_s
```

```python
import functools
import math

import jax
import jax.numpy as jnp
from jax import lax
from jax.experimental import pallas as pl
from jax.experimental.pallas import tpu as pltpu
from jax.experimental.pallas import tpu_sc as plsc

F32 = jnp.float32
BF16 = jnp.bfloat16

D_MODEL = 1024
HEADS = 8
HEAD_DIM = 64
DIFF_COMP = 32
MIX = HEADS * HEAD_DIM
CHUNK = 64
ROPE_THETA = 10000.0
NORM_EPS = 1e-6
SUBLN_EPS = 1e-5
PEER_HEADS = 8
PEER_KEYS = 128
PEER_TOPK = 16
PEER_HALF = 128
PEER_SEL = PEER_HEADS * PEER_TOPK
PLE_DIM = 256
LAM_INIT = 0.8 - 0.6 * math.exp(-0.3 * 0)
SB_LOG_FLOOR = -104.0
PROMPT_ROW_BLOCKS = 4

LANES = 128
VMEM_LIMIT = 48 * 1024 * 1024

NT_DIMS = (((1,), (1,)), ((), ()))


def _nt(a, b):
    return lax.dot_general(a, b, NT_DIMS, preferred_element_type=F32)


def _mm(a, b):
    return jnp.dot(a, b, preferred_element_type=F32)


def _rms(x, g, eps):
    return x * lax.rsqrt(jnp.mean(x * x, axis=-1, keepdims=True) + eps) * g


def _params(*sem):
    return pltpu.CompilerParams(dimension_semantics=sem, vmem_limit_bytes=VMEM_LIMIT)


def _with_ones(v):
    n = v.shape[0]
    ones = (lax.broadcasted_iota(jnp.int32, (n, HEAD_DIM), 1) == 0).astype(v.dtype)
    return jnp.concatenate([v, ones], axis=1)


def _proj_body(x_ref, g_ref, w_ref, cos_ref, sin_ref,
               kd_ref, vd_ref, ks_ref, vs_ref,
               qd2_ref, kdb_ref, vdb_ref, qsb_ref, ksb_ref, vsb_ref, *, feature_major):
    ts = x_ref.shape[0]
    a = _rms(x_ref[...], g_ref[...], NORM_EPS).astype(BF16)
    cos = jnp.tile(cos_ref[...], (1, MIX // LANES))
    sin = jnp.tile(sin_ref[...], (1, MIX // LANES))
    lane = lax.broadcasted_iota(jnp.int32, (ts, MIX), 1)
    first_half = (lane % DIFF_COMP) < (DIFF_COMP // 2)

    def group(i):
        return _mm(a, w_ref[:, i * MIX:(i + 1) * MIX])

    def rope(t):
        partner = jnp.where(first_half,
                            pltpu.roll(t, MIX - DIFF_COMP // 2, 1),
                            pltpu.roll(t, DIFF_COMP // 2, 1))
        return t * cos + partner * sin

    qd = rope(group(0)) * (DIFF_COMP ** -0.5)
    kd = rope(group(1))
    vd = group(2)
    qs = group(3) * (HEAD_DIM ** -0.5)
    ks = group(4)
    vs = group(5)
    for ref, rows in ((kd_ref, kd), (vd_ref, vd), (ks_ref, ks), (vs_ref, vs)):
        if feature_major:
            ref[0] = rows.T
        else:
            ref[...] = rows
    comp0 = lax.broadcasted_iota(jnp.int32, (ts, HEAD_DIM), 1) < DIFF_COMP
    for h in range(HEADS):
        sl = slice(h * HEAD_DIM, (h + 1) * HEAD_DIM)
        qh = qd[:, sl]
        qd2_ref[0, h, 0] = jnp.where(comp0, qh, 0.0).astype(BF16)
        qd2_ref[0, h, 1] = jnp.where(comp0, 0.0, qh).astype(BF16)
        kdb_ref[0, h] = kd[:, sl].astype(BF16)
        vdb_ref[0, h] = _with_ones(vd[:, sl].astype(BF16))
        qsb_ref[0, h] = qs[:, sl].astype(BF16)
        ksb_ref[0, h] = ks[:, sl].astype(BF16)
        vsb_ref[0, h] = vs[:, sl].astype(BF16)


def _proj(x, g_mix, w_in_b, cos_t, sin_t, nb, seq, feature_major):
    t = nb * seq
    ts = min(256, seq)
    nst = seq // ts
    if feature_major:
        row = pl.BlockSpec((1, MIX, ts), lambda i: (i // nst, 0, i % nst))
        rows = jax.ShapeDtypeStruct((nb, MIX, seq), F32)
    else:
        row = pl.BlockSpec((ts, MIX), lambda i: (i, 0))
        rows = jax.ShapeDtypeStruct((t, MIX), F32)
    hm = pl.BlockSpec((1, HEADS, ts, HEAD_DIM), lambda i: (i // nst, 0, i % nst, 0))
    hm2 = pl.BlockSpec((1, HEADS, 2, ts, HEAD_DIM), lambda i: (i // nst, 0, 0, i % nst, 0))
    heads = jax.ShapeDtypeStruct((nb, HEADS, seq, HEAD_DIM), BF16)
    heads2 = jax.ShapeDtypeStruct((nb, HEADS, 2, seq, HEAD_DIM), BF16)
    hm_ext = pl.BlockSpec((1, HEADS, ts, 2 * HEAD_DIM), lambda i: (i // nst, 0, i % nst, 0))
    heads_ext = jax.ShapeDtypeStruct((nb, HEADS, seq, 2 * HEAD_DIM), BF16)
    return pl.pallas_call(
        functools.partial(_proj_body, feature_major=feature_major),
        grid=(t // ts,),
        in_specs=[
            pl.BlockSpec((ts, D_MODEL), lambda i: (i, 0)),
            pl.BlockSpec((1, D_MODEL), lambda i: (0, 0)),
            pl.BlockSpec((D_MODEL, 6 * MIX), lambda i: (0, 0)),
            pl.BlockSpec((ts, LANES), lambda i: (i % nst, 0)),
            pl.BlockSpec((ts, LANES), lambda i: (i % nst, 0)),
        ],
        out_specs=[row, row, row, row, hm2, hm, hm_ext, hm, hm, hm],
        out_shape=[rows, rows, rows, rows, heads2, heads, heads_ext, heads, heads, heads],
        compiler_params=_params("parallel"),
        name="proj",
    )(x, g_mix, w_in_b, cos_t, sin_t)


def _diff_init(rows):
    return jnp.full((rows, 1), -jnp.inf, F32), jnp.zeros((rows, 2 * HEAD_DIM), F32)


def _diff_update(s, v_ext, carry):
    m, acc = carry
    m_new = jnp.maximum(m, jnp.max(s, axis=-1, keepdims=True))
    p = jnp.exp(s - m_new)
    acc = jnp.exp(m - m_new) * acc + _mm(p.astype(BF16), v_ext)
    return m_new, acc


def _diff_finish(carry, lam, g_subln, tq):
    _, acc = carry
    o = acc[:, :HEAD_DIM] / acc[:, HEAD_DIM:HEAD_DIM + 1]
    d = o[:tq] - lam * o[tq:]
    return _rms(d, g_subln, SUBLN_EPS) * (1.0 - LAM_INIT)


def _suffix_sums(lk, tri):
    hi = lk.astype(BF16)
    lo = (lk - hi.astype(F32)).astype(BF16)
    return _mm(hi, tri) + _mm(lo, tri)


def _sb_update(q, k, v, tri, earlier, carry):
    run, acc = carry
    z = _nt(q, k)
    sp = jnp.maximum(z, 0.0) + jnp.log1p(jnp.exp(-jnp.abs(z)))
    lk = -sp if earlier is None else jnp.where(earlier, -sp, 0.0)
    after = _suffix_sums(lk, tri)
    w = jnp.exp((z - sp) + after + run)
    if earlier is not None:
        w = jnp.where(earlier, w, 0.0)
    acc = acc + _mm(w.astype(BF16), v)
    run = run + after[:, 0:1] + lk[:, 0:1]
    return run, acc


def _tri(n):
    j = lax.broadcasted_iota(jnp.int32, (n, n), 0)
    s = lax.broadcasted_iota(jnp.int32, (n, n), 1)
    return (j > s).astype(BF16)


def _diff_body(lam_ref, q_ref, k_ref, v_ref, g_ref, o_ref, *, tq, q0):
    qi = q0 + pl.program_id(1)
    q2 = q_ref[0, 0].reshape(2 * tq, HEAD_DIM)

    def scores(j):
        return _nt(q2, k_ref[0, 0, pl.ds(pl.multiple_of(j * tq, tq), tq), :])

    def step(j, state):
        s, carry = state
        s_next = scores(j + 1)
        return s_next, _diff_update(s, v_ref[0, 0, pl.ds(pl.multiple_of(j * tq, tq), tq), :], carry)

    s, carry = lax.fori_loop(0, qi, step, (scores(0), _diff_init(2 * tq)))
    r = lax.broadcasted_iota(jnp.int32, (2 * tq, tq), 0) % tq
    c = lax.broadcasted_iota(jnp.int32, (2 * tq, tq), 1)
    s = jnp.where((c // CHUNK) <= (r // CHUNK), s, -jnp.inf)
    carry = _diff_update(s, v_ref[0, 0, pl.ds(pl.multiple_of(qi * tq, tq), tq), :], carry)
    o_ref[0, 0] = _diff_finish(carry, lam_ref[0], g_ref[...], tq)


def _diff_attention(lam, qd2, kdb, vdb, g_subln, b, t0, n):
    seq = qd2.shape[3]
    tq = min(512, n)
    q0 = t0 // tq
    return pl.pallas_call(
        functools.partial(_diff_body, tq=tq, q0=q0),
        grid=(HEADS, n // tq),
        in_specs=[
            pl.BlockSpec(memory_space=pltpu.SMEM),
            pl.BlockSpec((1, 1, 2, tq, HEAD_DIM), lambda h, i: (b, h, 0, q0 + i, 0)),
            pl.BlockSpec((1, 1, seq, HEAD_DIM), lambda h, i: (b, h, 0, 0)),
            pl.BlockSpec((1, 1, seq, 2 * HEAD_DIM), lambda h, i: (b, h, 0, 0)),
            pl.BlockSpec((1, HEAD_DIM), lambda h, i: (0, 0)),
        ],
        out_specs=pl.BlockSpec((1, 1, tq, HEAD_DIM), lambda h, i: (0, h, i, 0)),
        out_shape=jax.ShapeDtypeStruct((1, HEADS, n, HEAD_DIM), F32),
        compiler_params=_params("parallel", "arbitrary"),
        name="diff_attention",
    )(lam, qd2, kdb, vdb, g_subln)


def _sb_body(q_ref, k_ref, v_ref, o_ref, *, tq, q0):
    qi = q0 + pl.program_id(1)
    q = q_ref[0, 0]
    tri = _tri(tq)

    def tile(j):
        start = pl.multiple_of(j * tq, tq)
        return k_ref[0, 0, pl.ds(start, tq), :], v_ref[0, 0, pl.ds(start, tq), :]

    r = lax.broadcasted_iota(jnp.int32, (tq, tq), 0)
    c = lax.broadcasted_iota(jnp.int32, (tq, tq), 1)
    carry = (jnp.zeros((tq, 1), F32), jnp.zeros((tq, HEAD_DIM), F32))
    run, acc = _sb_update(q, *tile(qi), tri, c < r, carry)

    def live(state):
        j, run, _ = state
        return jnp.logical_and(j >= 0, jnp.max(run) > SB_LOG_FLOOR)

    def step(state):
        j, run, acc = state
        run, acc = _sb_update(q, *tile(j), tri, None, (run, acc))
        return j - 1, run, acc

    o_ref[0, 0] = lax.while_loop(live, step, (qi - 1, run, acc))[2]


def _sb_attention(qsb, ksb, vsb, b, t0, n):
    seq = qsb.shape[2]
    tq = min(256, n)
    q0 = t0 // tq
    kv = pl.BlockSpec((1, 1, seq, HEAD_DIM), lambda h, i: (b, h, 0, 0))
    return pl.pallas_call(
        functools.partial(_sb_body, tq=tq, q0=q0),
        grid=(HEADS, n // tq),
        in_specs=[pl.BlockSpec((1, 1, tq, HEAD_DIM), lambda h, i: (b, h, q0 + i, 0)), kv, kv],
        out_specs=pl.BlockSpec((1, 1, tq, HEAD_DIM), lambda h, i: (0, h, i, 0)),
        out_shape=jax.ShapeDtypeStruct((1, HEADS, n, HEAD_DIM), F32),
        compiler_params=_params("parallel", "arbitrary"),
        name="sb_attention",
    )(qsb, ksb, vsb)


SAMPLE_HEADS = 4
SAMPLE_TILE = 256


def _sample_body(lam_ref, qd2_ref, kdn_ref, vdn_ref, qs_ref, ksn_ref, vsn_ref,
                 ckd_ref, cvd_ref, cks_ref, cvs_ref, g_ref, do_ref, so_ref, *, past, nq):
    lam = lam_ref[0]
    tri_c = _tri(SAMPLE_TILE)
    tri_n = _tri(nq)
    i2 = lax.broadcasted_iota(jnp.int32, (2 * nq, nq), 0) % nq
    j2 = lax.broadcasted_iota(jnp.int32, (2 * nq, nq), 1)
    visible_new = ((past + j2) // CHUNK) <= ((past + i2) // CHUNK)
    i1 = lax.broadcasted_iota(jnp.int32, (nq, nq), 0)
    j1 = lax.broadcasted_iota(jnp.int32, (nq, nq), 1)
    earlier_new = j1 < i1
    for h in range(SAMPLE_HEADS):
        sl = slice(h * HEAD_DIM, (h + 1) * HEAD_DIM)
        q2 = qd2_ref[0, h].reshape(2 * nq, HEAD_DIM)
        carry = _diff_update(_nt(q2, ckd_ref[0, :, sl].astype(BF16)),
                             _with_ones(cvd_ref[0, :, sl].astype(BF16)), _diff_init(2 * nq))
        s_new = jnp.where(visible_new, _nt(q2, kdn_ref[0, h]), -jnp.inf)
        carry = _diff_update(s_new, vdn_ref[0, h], carry)
        do_ref[0, h] = _diff_finish(carry, lam, g_ref[...], nq)
        q = qs_ref[0, h]
        carry = (jnp.zeros((nq, 1), F32), jnp.zeros((nq, HEAD_DIM), F32))
        carry = _sb_update(q, ksn_ref[0, h], vsn_ref[0, h], tri_n, earlier_new, carry)
        for t in reversed(range(past // SAMPLE_TILE)):
            rows = slice(t * SAMPLE_TILE, (t + 1) * SAMPLE_TILE)
            carry = _sb_update(q, cks_ref[0, rows, sl].astype(BF16), cvs_ref[0, rows, sl].astype(BF16),
                               tri_c, None, carry)
        so_ref[0, h] = carry[1]


def _sample_attention(lam, qd2, kdb, vdb, qsb, ksb, vsb, ckd, cvd, cks, cvs, g_subln, nb, nq):
    past = ckd.shape[1]
    nhg = HEADS // SAMPLE_HEADS
    hm = pl.BlockSpec((1, SAMPLE_HEADS, nq, HEAD_DIM), lambda b, g: (0, g, b, 0))
    hm2 = pl.BlockSpec((1, SAMPLE_HEADS, 2, nq, HEAD_DIM), lambda b, g: (0, g, 0, b, 0))
    cache = pl.BlockSpec((1, past, SAMPLE_HEADS * HEAD_DIM), lambda b, g: (b, 0, g))
    out = jax.ShapeDtypeStruct((1, HEADS, nb * nq, HEAD_DIM), F32)
    return pl.pallas_call(
        functools.partial(_sample_body, past=past, nq=nq),
        grid=(nb, nhg),
        in_specs=[pl.BlockSpec(memory_space=pltpu.SMEM), hm2, hm,
                  pl.BlockSpec((1, SAMPLE_HEADS, nq, 2 * HEAD_DIM), lambda b, g: (0, g, b, 0)), hm, hm, hm,
                  cache, cache, cache, cache,
                  pl.BlockSpec((1, HEAD_DIM), lambda b, g: (0, 0))],
        out_specs=[hm, hm],
        out_shape=[out, out],
        compiler_params=_params("parallel", "parallel"),
        name="sample_attention",
    )(lam, qd2, kdb, vdb, qsb, ksb, vsb, ckd, cvd, cks, cvs, g_subln)


def _topk_rows(s, k):
    n = s.shape[0]
    rows = lax.broadcasted_iota(jnp.int32, s.shape, 0)
    vals, ids = [], []
    for _ in range(k):
        m = jnp.max(s, axis=0, keepdims=True)
        i = jnp.min(jnp.where(s == m, rows, n), axis=0, keepdims=True)
        vals.append(m)
        ids.append(i)
        s = jnp.where(rows == i, -jnp.inf, s)
    return jnp.concatenate(vals, axis=0), jnp.concatenate(ids, axis=0)


_STAIR = [(i, j) for i in range(PEER_TOPK) for j in range(PEER_TOPK) if (i + 1) * (j + 1) <= PEER_TOPK]


def _post_body(x_ref, do_ref, so_ref, wo_ref, gf_ref, wq_ref, sk_ref,
               h1_ref, c_ref, idx_ref, gate_ref, q_scr, idx_scr, gate_scr):
    ts = x_ref.shape[0]
    mixed = jnp.zeros((ts, D_MODEL), F32)
    for h in range(HEADS):
        mixed += _mm(do_ref[0, h].astype(BF16), wo_ref[h * HEAD_DIM:(h + 1) * HEAD_DIM, :])
        mixed += _mm(so_ref[0, h].astype(BF16), wo_ref[MIX + h * HEAD_DIM:MIX + (h + 1) * HEAD_DIM, :])
    h1 = x_ref[...] + mixed
    h1_ref[...] = h1
    c = _rms(h1, gf_ref[...], NORM_EPS)
    c_ref[...] = c
    q = _mm(c.astype(BF16), wq_ref[...])
    for hp in range(2 * PEER_HEADS):
        q_scr[hp] = q[:, hp * PEER_HALF:(hp + 1) * PEER_HALF].astype(BF16)

    npad = -len(_STAIR) % 8

    def head(h, _):
        v1, i1 = _topk_rows(_nt(sk_ref[0], q_scr[2 * h]), PEER_TOPK)
        v2, i2 = _topk_rows(_nt(sk_ref[1], q_scr[2 * h + 1]), PEER_TOPK)
        cand = jnp.concatenate([v1[i:i + 1] + v2[j:j + 1] for i, j in _STAIR]
                               + [jnp.full((npad, ts), -jnp.inf, F32)], axis=0)
        eid = jnp.concatenate([i1[i:i + 1] * PEER_KEYS + i2[j:j + 1] for i, j in _STAIR]
                              + [jnp.zeros((npad, ts), jnp.int32)], axis=0)
        top, pos = _topk_rows(cand, PEER_TOPK)
        rows = lax.broadcasted_iota(jnp.int32, cand.shape, 0)
        sel = jnp.concatenate([jnp.sum(jnp.where(rows == pos[r:r + 1], eid, 0), axis=0, keepdims=True)
                               for r in range(PEER_TOPK)], axis=0)
        e = jnp.exp(top - top[0:1])
        gate_scr[h] = e / jnp.sum(e, axis=0, keepdims=True)
        idx_scr[h] = sel
        return 0

    lax.fori_loop(0, PEER_HEADS, head, 0)
    idx_ref[...] = idx_scr[...].reshape(PEER_SEL, ts).T
    gate_ref[...] = gate_scr[...].reshape(PEER_SEL, ts).T


def _post(x, dout, sout, w_out_b, g_ffn, w_query_b, sub_keys_b, tok0, t):
    ts = min(256, t)
    first = tok0 // ts
    row = pl.BlockSpec((ts, D_MODEL), lambda i: (i, 0))
    hm = pl.BlockSpec((1, HEADS, ts, HEAD_DIM), lambda i: (0, 0, i, 0))
    sel = pl.BlockSpec((ts, PEER_SEL), lambda i: (i, 0))
    full = lambda *shape: pl.BlockSpec(shape, lambda i: (0,) * len(shape))
    return pl.pallas_call(
        _post_body,
        grid=(t // ts,),
        in_specs=[pl.BlockSpec((ts, D_MODEL), lambda i: (first + i, 0)),
                  hm, hm, full(2 * MIX, D_MODEL), full(1, D_MODEL),
                  full(D_MODEL, 2 * PEER_HEADS * PEER_HALF), full(2, PEER_KEYS, PEER_HALF)],
        out_specs=[row, row, sel, sel],
        out_shape=[jax.ShapeDtypeStruct((t, D_MODEL), F32), jax.ShapeDtypeStruct((t, D_MODEL), F32),
                   jax.ShapeDtypeStruct((t, PEER_SEL), jnp.int32), jax.ShapeDtypeStruct((t, PEER_SEL), F32)],
        scratch_shapes=[pltpu.VMEM((2 * PEER_HEADS, ts, PEER_HALF), BF16),
                        pltpu.VMEM((PEER_HEADS, PEER_TOPK, ts), jnp.int32),
                        pltpu.VMEM((PEER_HEADS, PEER_TOPK, ts), F32)],
        compiler_params=_params("parallel"),
        name="post_peer_select",
    )(x, dout, sout, w_out_b, g_ffn, w_query_b, sub_keys_b)


def _coef_body(after_ref, gate_ref, dots_ref, o_ref):
    del after_ref
    d = dots_ref[...]
    o_ref[...] = gate_ref[...] * (0.5 * d * (1.0 + lax.erf(d * (2.0 ** -0.5))))


def _coef(gate, dots, after):
    t = gate.shape[0]
    ts = min(2048, t)
    blk = pl.BlockSpec((ts, PEER_SEL), lambda i: (i, 0))
    return pl.pallas_call(
        _coef_body, grid=(t // ts,), in_specs=[pl.BlockSpec(memory_space=pl.ANY), blk, blk], out_specs=blk,
        out_shape=jax.ShapeDtypeStruct((t, PEER_SEL), F32),
        compiler_params=_params("parallel"), name="peer_coef",
    )(after, gate, dots)


SC_CORES = 2
SC_SUBCORES = 16
SC_LANES = 16
SC_WORKERS = SC_CORES * SC_SUBCORES
SC_ROWS = 16
SC_BUFS = 5
SC_GROUP = 8
SC_CHUNKS = PEER_SEL // SC_ROWS
SC_STEPS = SC_GROUP * SC_CHUNKS
SC_VECS = D_MODEL // SC_LANES


def _sc_mesh():
    return plsc.VectorSubcoreMesh(core_axis_name="c", subcore_axis_name="s",
                                  num_cores=SC_CORES, num_subcores=SC_SUBCORES)


def _sc_walk(table_hbm, idx_hbm, aux_hbm, idx_v, aux_v, rows_v, sem, stage_sem, tpw, begin_group, compute, end_group):
    tok_base = (lax.axis_index("s") * SC_CORES + lax.axis_index("c")) * tpw
    ngroups = tpw // SC_GROUP
    nsteps = tpw * SC_CHUNKS

    def first_token(g):
        return pl.multiple_of(tok_base + g * SC_GROUP, SC_GROUP)

    def stage(g):
        tok0 = first_token(g)
        return (pltpu.make_async_copy(idx_hbm.at[pl.ds(tok0 * SC_CHUNKS, SC_STEPS)], idx_v.at[g % 2], stage_sem.at[0]),
                pltpu.make_async_copy(aux_hbm.at[pl.ds(tok0, SC_GROUP)],
                                      aux_v.at[g % 2, :, pl.ds(0, aux_hbm.shape[1])], stage_sem.at[1]))

    def gather(step):
        idx = idx_v.at[(step // SC_STEPS) % 2, step % SC_STEPS]
        return pltpu.make_async_copy(table_hbm.at[idx], rows_v.at[step % SC_BUFS], sem.at[step % SC_BUFS])

    for cp in stage(0):
        cp.start()
    for cp in stage(0):
        cp.wait()
    for step in range(SC_BUFS - 1):
        gather(step).start()

    def walk(step, _):
        g = step // SC_STEPS
        local = step % SC_STEPS

        @pl.when(jnp.logical_and(local == 0, g + 1 < ngroups))
        def _():
            for cp in stage(g + 1):
                cp.start()

        ahead = step + (SC_BUFS - 1)

        @pl.when(ahead < nsteps)
        def _():
            @pl.when(ahead % SC_STEPS == 0)
            def _():
                for cp in stage(ahead // SC_STEPS):
                    cp.wait()

            gather(ahead).start()

        @pl.when(local == 0)
        def _():
            begin_group()

        gather(step).wait()
        compute(rows_v.at[step % SC_BUFS], g % 2, local // SC_CHUNKS, local % SC_CHUNKS)

        @pl.when(local == SC_STEPS - 1)
        def _():
            end_group(first_token(g))

        return 0

    lax.fori_loop(0, nsteps, walk, 0)


def _sc_dots_body(u_hbm, idx_hbm, c_hbm, out_hbm, idx_v, c_v, rows_v, dots_v, sem, stage_sem, *, tpw):
    lane = lax.broadcasted_iota(jnp.int32, (SC_LANES,), 0)
    zero = jnp.zeros((SC_LANES,), F32)

    def compute(rows, slot, tt, ch):
        for half in range(SC_ROWS // SC_LANES):
            r0 = half * SC_LANES

            def vec(kk, accs):
                off = pl.multiple_of(kk * SC_LANES, SC_LANES)
                cv = c_v[slot, tt, pl.ds(off, SC_LANES)]
                return tuple(a + rows[r0 + r, pl.ds(off, SC_LANES)] * cv for r, a in enumerate(accs))

            accs = lax.fori_loop(0, SC_VECS, vec, (zero,) * SC_LANES)
            outv = zero
            for r in range(SC_LANES):
                outv = jnp.where(lane == r, jnp.sum(accs[r]), outv)
            dots_v[tt, pl.ds(pl.multiple_of(ch * SC_ROWS + half * SC_LANES, SC_LANES), SC_LANES)] = outv

    def end_group(tok0):
        pltpu.sync_copy(dots_v.at[:, pl.ds(0, PEER_SEL)], out_hbm.at[pl.ds(tok0, SC_GROUP)])

    _sc_walk(u_hbm, idx_hbm, c_hbm, idx_v, c_v, rows_v, sem, stage_sem, tpw, lambda: None, compute, end_group)


def _sc_combine_body(v_hbm, idx_hbm, coef_hbm, out_hbm, idx_v, coef_v, rows_v, acc_v, sem, stage_sem, *, tpw):
    lane = lax.broadcasted_iota(jnp.int32, (SC_LANES,), 0)
    zero = jnp.zeros((SC_LANES,), F32)

    def compute(rows, slot, tt, ch):
        for half in range(SC_ROWS // SC_LANES):
            cf = coef_v[slot, tt, pl.ds(pl.multiple_of(ch * SC_ROWS + half * SC_LANES, SC_LANES), SC_LANES)]
            splat = [jnp.full((SC_LANES,), jnp.sum(jnp.where(lane == r, cf, 0.0)), F32) for r in range(SC_LANES)]

            @plsc.parallel_loop(0, SC_VECS, unroll=2)
            def _(kk):
                off = pl.multiple_of(kk * SC_LANES, SC_LANES)
                terms = [rows[half * SC_LANES + r, pl.ds(off, SC_LANES)] * splat[r] for r in range(SC_LANES)]
                while len(terms) > 1:
                    terms = [a + b for a, b in zip(terms[0::2], terms[1::2])]
                acc_v[tt, pl.ds(off, SC_LANES)] = acc_v[tt, pl.ds(off, SC_LANES)] + terms[0]

    def begin_group():
        def clear(i, _):
            acc_v[i // SC_VECS, pl.ds(pl.multiple_of((i % SC_VECS) * SC_LANES, SC_LANES), SC_LANES)] = zero
            return 0

        lax.fori_loop(0, SC_GROUP * SC_VECS, clear, 0)

    def end_group(tok0):
        pltpu.sync_copy(acc_v, out_hbm.at[pl.ds(tok0, SC_GROUP)])

    _sc_walk(v_hbm, idx_hbm, coef_hbm, idx_v, coef_v, rows_v, sem, stage_sem, tpw, begin_group, compute, end_group)


def _sc_call(body, table, idx, per_token, out_width, name):
    t = per_token.shape[0]
    tpw = t // SC_WORKERS
    assert tpw % SC_GROUP == 0
    return pl.kernel(
        functools.partial(body, tpw=tpw),
        out_type=jax.ShapeDtypeStruct((t, out_width), F32),
        mesh=_sc_mesh(),
        scratch_types=[pltpu.VMEM((2, SC_STEPS, SC_ROWS), jnp.int32),
                       pltpu.VMEM((2, SC_GROUP, per_token.shape[1]), F32),
                       pltpu.VMEM((SC_BUFS, SC_ROWS, D_MODEL), F32),
                       pltpu.VMEM((SC_GROUP, out_width), F32),
                       pltpu.SemaphoreType.DMA((SC_BUFS,)),
                       pltpu.SemaphoreType.DMA((2,))],
        compiler_params=pltpu.CompilerParams(needs_layout_passes=False),
        name=name,
    )(table, idx.reshape(t * SC_CHUNKS, SC_ROWS), per_token)


def _sc_step_body(v_hbm, idx_a_hbm, coef_hbm, u_hbm, idx_b_hbm, c_hbm, peer_hbm, dots_hbm,
                  idx_v, aux_v, rows_v, out_v, sem, stage_sem, *, tpw_a, tpw_b):
    _sc_combine_body(v_hbm, idx_a_hbm, coef_hbm, peer_hbm, idx_v, aux_v, rows_v, out_v, sem, stage_sem, tpw=tpw_a)
    _sc_dots_body(u_hbm, idx_b_hbm, c_hbm, dots_hbm, idx_v, aux_v, rows_v, out_v, sem, stage_sem, tpw=tpw_b)


def _sc_step(expert_v, idx_a, coef, expert_u, idx_b, c):
    ta, tb = coef.shape[0], c.shape[0]
    assert ta % (SC_WORKERS * SC_GROUP) == 0 and tb % (SC_WORKERS * SC_GROUP) == 0
    return pl.kernel(
        functools.partial(_sc_step_body, tpw_a=ta // SC_WORKERS, tpw_b=tb // SC_WORKERS),
        out_type=[jax.ShapeDtypeStruct((ta, D_MODEL), F32), jax.ShapeDtypeStruct((tb, PEER_SEL), F32)],
        mesh=_sc_mesh(),
        scratch_types=[pltpu.VMEM((2, SC_STEPS, SC_ROWS), jnp.int32),
                       pltpu.VMEM((2, SC_GROUP, D_MODEL), F32),
                       pltpu.VMEM((SC_BUFS, SC_ROWS, D_MODEL), F32),
                       pltpu.VMEM((SC_GROUP, D_MODEL), F32),
                       pltpu.SemaphoreType.DMA((SC_BUFS,)),
                       pltpu.SemaphoreType.DMA((2,))],
        compiler_params=pltpu.CompilerParams(needs_layout_passes=False),
        name="peer_step",
    )(expert_v, idx_a.reshape(ta * SC_CHUNKS, SC_ROWS), coef, expert_u, idx_b.reshape(tb * SC_CHUNKS, SC_ROWS), c)


def _sc_dots(expert_u, idx, c):
    return _sc_call(_sc_dots_body, expert_u, idx, c, PEER_SEL, "peer_dots")


def _sc_combine(expert_v, idx, coef):
    return _sc_call(_sc_combine_body, expert_v, idx, coef, D_MODEL, "peer_combine")


def _ple_body(h1_ref, peer_ref, p_ref, gp_ref, wg_ref, we_ref, gfin_ref, *refs):
    y_ref = refs[-1]
    h = h1_ref[...] + peer_ref[...]
    a = _rms(h, gp_ref[...], NORM_EPS).astype(BF16)
    gate = jax.nn.sigmoid(_mm(a, wg_ref[...]))
    h = h + _mm(p_ref[...].astype(BF16), we_ref[...]) * gate
    y_ref[...] = _rms(h, gfin_ref[...], NORM_EPS)


def _ple(h1, peer, p, g_ple, w_pgate_b, w_ple_b, g_final, tok0, into=()):
    t = h1.shape[0]
    ts = min(512, t)
    first = tok0 // ts
    row = pl.BlockSpec((ts, D_MODEL), lambda i: (i, 0))
    full = lambda *shape: pl.BlockSpec(shape, lambda i: (0,) * len(shape))
    n_in = 7
    return pl.pallas_call(
        _ple_body,
        grid=(t // ts,),
        in_specs=[row, row, pl.BlockSpec((ts, PLE_DIM), lambda i: (first + i, 0)), full(1, D_MODEL),
                  full(D_MODEL, D_MODEL), full(PLE_DIM, D_MODEL), full(1, D_MODEL)]
        + [pl.BlockSpec(memory_space=pl.ANY)] * len(into),
        out_specs=pl.BlockSpec((ts, D_MODEL), lambda i: (first + i, 0)),
        out_shape=jax.ShapeDtypeStruct((p.shape[0], D_MODEL), F32),
        input_output_aliases={n_in + j: j for j in range(len(into))},
        compiler_params=_params("parallel"),
        name="ple_final",
    )(h1, peer, p, g_ple, w_pgate_b, w_ple_b, g_final, *into)


def _rope_tables(pos):
    inv = ROPE_THETA ** (-jnp.arange(0, DIFF_COMP, 2, dtype=F32) / DIFF_COMP)
    ang = pos.astype(F32)[:, None] * inv[None, :]
    cos = jnp.cos(ang)
    sin = jnp.sin(ang)
    reps = LANES // DIFF_COMP
    return (jnp.tile(jnp.concatenate([cos, cos], axis=-1), (1, reps)),
            jnp.tile(jnp.concatenate([-sin, sin], axis=-1), (1, reps)))


def kernel(x_prompt, x_sample, cache_diff_k, cache_diff_v, cache_sb_k, cache_sb_v, p_prompt, p_sample, g_mix, w_in, lambda_q1, lambda_k1, lambda_q2, lambda_k2, g_subln, w_out, g_ffn, w_query, sub_keys, expert_u, expert_v, g_ple, w_pgate, w_ple, g_final):
    assert w_in.shape[0] == 1, "single-layer encoder"
    nb, seq, _ = x_prompt.shape
    db, dq, _ = x_sample.shape
    past = cache_diff_k.shape[2]

    lam = (jnp.exp(jnp.sum(lambda_q1[0].astype(F32) * lambda_k1[0].astype(F32)))
           - jnp.exp(jnp.sum(lambda_q2[0].astype(F32) * lambda_k2[0].astype(F32))) + LAM_INIT).reshape(1)
    w_in_b = w_in[0].astype(BF16)
    w_out_b = w_out[0].astype(BF16)
    w_query_b = w_query[0].astype(BF16)
    sub_keys_b = sub_keys[0].astype(BF16)
    w_pgate_b = w_pgate[0].astype(BF16)
    w_ple_b = w_ple[0].astype(BF16)
    g_sub = g_subln[0].reshape(1, HEAD_DIM)
    g_fin = g_final.reshape(1, D_MODEL)

    def select(x, dout, sout, tok0, t):
        return _post(x, dout, sout, w_out_b, g_ffn, w_query_b, sub_keys_b, tok0, t)

    ts = db * dq
    xs = x_sample.reshape(ts, D_MODEL)
    cos_s, sin_s = _rope_tables(jnp.tile(past + jnp.arange(dq, dtype=jnp.int32), db))
    kd, vd, ks, vs, qd2, kdb, vdb, qsb, ksb, vsb = _proj(xs, g_mix, w_in_b, cos_s, sin_s, 1, ts, False)
    caches = [c[0].reshape(db, past, MIX) for c in (cache_diff_k, cache_diff_v, cache_sb_k, cache_sb_v)]
    dout_s, sout_s = _sample_attention(lam, qd2, kdb, vdb, qsb, ksb, vsb, *caches, g_sub, db, dq)
    rows_s = tuple(r.reshape(1, db, dq, HEADS, HEAD_DIM) for r in (kd, vd, ks, vs))

    xp = x_prompt.reshape(nb * seq, D_MODEL)
    pp = p_prompt[0].reshape(nb * seq, PLE_DIM)
    cos_p, sin_p = _rope_tables(jnp.arange(seq, dtype=jnp.int32))
    kd, vd, ks, vs, qd2, kdb, vdb, qsb, ksb, vsb = _proj(xp, g_mix, w_in_b, cos_p, sin_p, nb, seq, True)
    rows_p = tuple(r.reshape(nb, HEADS, HEAD_DIM, seq).transpose(0, 3, 1, 2)[None] for r in (kd, vd, ks, vs))

    cut = seq // PROMPT_ROW_BLOCKS
    spans = [(b, i * cut, cut) for b in range(nb) for i in range(PROMPT_ROW_BLOCKS)]
    for end in (0, -1):
        b, t0, n = spans.pop(end)
        spans[end if end == 0 else len(spans):0] = [(b, t0, n // 2), (b, t0 + n // 2, n // 2)]

    def prompt_block(b, t0, n):
        dout = _diff_attention(lam, qd2, kdb, vdb, g_sub, b, t0, n)
        sout = _sb_attention(qsb, ksb, vsb, b, t0, n)
        return select(xp, dout, sout, b * seq + t0, n)

    blocks = [(functools.partial(prompt_block, *span), "prompt", span[0] * seq + span[1]) for span in spans]
    blocks.insert(min(PROMPT_ROW_BLOCKS + 1, len(blocks)), (lambda: select(xs, dout_s, sout_s, 0, ts), "sample", 0))
    embeds = {"prompt": pp, "sample": p_sample[0].reshape(ts, PLE_DIM)}
    results = {"prompt": (), "sample": ()}

    def emit(h1, peer, group, tok0):
        results[group] = (_ple(h1, peer, embeds[group], g_ple, w_pgate_b, w_ple_b, g_fin, tok0, results[group]),)

    h1, c, idx, gate = blocks[0][0]()
    dots = _sc_dots(expert_u[0], idx, c)
    for k in range(1, len(blocks)):
        nh1, nc, nidx, ngate = blocks[k][0]()
        coef = _coef(gate, dots, nh1)
        peer, ndots = _sc_step(expert_v[0], idx, coef, expert_u[0], nidx, nc)
        emit(h1, peer, *blocks[k - 1][1:])
        h1, idx, gate, dots = nh1, nidx, ngate, ndots
    coef = _coef(gate, dots, lam)
    emit(h1, _sc_combine(expert_v[0], idx, coef), *blocks[-1][1:])
    y_sample = results["sample"][0].reshape(db, dq, D_MODEL)
    y_prompt = results["prompt"][0].reshape(nb, seq, D_MODEL)

    return (y_prompt, y_sample) + rows_p + rows_s
```

```python
import functools
import math

import jax
import jax.numpy as jnp
from jax import lax
from jax.experimental import pallas as pl
from jax.experimental.pallas import tpu as pltpu
from jax.experimental.pallas import tpu_sc as plsc

F32 = jnp.float32
BF16 = jnp.bfloat16

D_MODEL = 1024
HEADS = 8
HEAD_DIM = 64
DIFF_COMP = 32
MIX = HEADS * HEAD_DIM
CHUNK = 64
ROPE_THETA = 10000.0
NORM_EPS = 1e-6
SUBLN_EPS = 1e-5
PEER_HEADS = 8
PEER_KEYS = 128
PEER_TOPK = 16
PEER_HALF = 128
PEER_SEL = PEER_HEADS * PEER_TOPK
PLE_DIM = 256
LAM_INIT = 0.8 - 0.6 * math.exp(-0.3 * 0)
SB_LOG_FLOOR = -104.0
PROMPT_ROW_BLOCKS = 4

LANES = 128
VMEM_LIMIT = 48 * 1024 * 1024

NT_DIMS = (((1,), (1,)), ((), ()))


def _nt(a, b):
    return lax.dot_general(a, b, NT_DIMS, preferred_element_type=F32)


def _mm(a, b):
    return jnp.dot(a, b, preferred_element_type=F32)


def _rms(x, g, eps):
    return x * lax.rsqrt(jnp.mean(x * x, axis=-1, keepdims=True) + eps) * g


def _params(*sem):
    return pltpu.CompilerParams(dimension_semantics=sem, vmem_limit_bytes=VMEM_LIMIT)


def _with_ones(v):
    n = v.shape[0]
    ones = (lax.broadcasted_iota(jnp.int32, (n, HEAD_DIM), 1) == 0).astype(v.dtype)
    return jnp.concatenate([v, ones], axis=1)


def _proj_body(x_ref, g_ref, w_ref, cos_ref, sin_ref,
               kd_ref, vd_ref, ks_ref, vs_ref,
               qd2_ref, kdb_ref, vdb_ref, qsb_ref, ksb_ref, vsb_ref, *, feature_major):
    ts = x_ref.shape[0]
    a = _rms(x_ref[...], g_ref[...], NORM_EPS).astype(BF16)
    cos = jnp.tile(cos_ref[...], (1, MIX // LANES))
    sin = jnp.tile(sin_ref[...], (1, MIX // LANES))
    lane = lax.broadcasted_iota(jnp.int32, (ts, MIX), 1)
    first_half = (lane % DIFF_COMP) < (DIFF_COMP // 2)

    def group(i):
        return _mm(a, w_ref[:, i * MIX:(i + 1) * MIX])

    def rope(t):
        partner = jnp.where(first_half,
                            pltpu.roll(t, MIX - DIFF_COMP // 2, 1),
                            pltpu.roll(t, DIFF_COMP // 2, 1))
        return t * cos + partner * sin

    qd = rope(group(0)) * (DIFF_COMP ** -0.5)
    kd = rope(group(1))
    vd = group(2)
    qs = group(3) * (HEAD_DIM ** -0.5)
    ks = group(4)
    vs = group(5)
    for ref, rows in ((kd_ref, kd), (vd_ref, vd), (ks_ref, ks), (vs_ref, vs)):
        if feature_major:
            ref[0] = rows.T
        else:
            ref[...] = rows
    comp0 = lax.broadcasted_iota(jnp.int32, (ts, HEAD_DIM), 1) < DIFF_COMP
    for h in range(HEADS):
        sl = slice(h * HEAD_DIM, (h + 1) * HEAD_DIM)
        qh = qd[:, sl]
        qd2_ref[0, h, 0] = jnp.where(comp0, qh, 0.0).astype(BF16)
        qd2_ref[0, h, 1] = jnp.where(comp0, 0.0, qh).astype(BF16)
        kdb_ref[0, h] = kd[:, sl].astype(BF16)
        vdb_ref[0, h] = _with_ones(vd[:, sl].astype(BF16))
        qsb_ref[0, h] = qs[:, sl].astype(BF16)
        ksb_ref[0, h] = ks[:, sl].astype(BF16)
        vsb_ref[0, h] = vs[:, sl].astype(BF16)


def _proj(x, g_mix, w_in_b, cos_t, sin_t, nb, seq, feature_major):
    t = nb * seq
    ts = min(256, seq)
    nst = seq // ts
    if feature_major:
        row = pl.BlockSpec((1, MIX, ts), lambda i: (i // nst, 0, i % nst))
        rows = jax.ShapeDtypeStruct((nb, MIX, seq), F32)
    else:
        row = pl.BlockSpec((ts, MIX), lambda i: (i, 0))
        rows = jax.ShapeDtypeStruct((t, MIX), F32)
    hm = pl.BlockSpec((1, HEADS, ts, HEAD_DIM), lambda i: (i // nst, 0, i % nst, 0))
    hm2 = pl.BlockSpec((1, HEADS, 2, ts, HEAD_DIM), lambda i: (i // nst, 0, 0, i % nst, 0))
    heads = jax.ShapeDtypeStruct((nb, HEADS, seq, HEAD_DIM), BF16)
    heads2 = jax.ShapeDtypeStruct((nb, HEADS, 2, seq, HEAD_DIM), BF16)
    hm_ext = pl.BlockSpec((1, HEADS, ts, 2 * HEAD_DIM), lambda i: (i // nst, 0, i % nst, 0))
    heads_ext = jax.ShapeDtypeStruct((nb, HEADS, seq, 2 * HEAD_DIM), BF16)
    return pl.pallas_call(
        functools.partial(_proj_body, feature_major=feature_major),
        grid=(t // ts,),
        in_specs=[
            pl.BlockSpec((ts, D_MODEL), lambda i: (i, 0)),
            pl.BlockSpec((1, D_MODEL), lambda i: (0, 0)),
            pl.BlockSpec((D_MODEL, 6 * MIX), lambda i: (0, 0)),
            pl.BlockSpec((ts, LANES), lambda i: (i % nst, 0)),
            pl.BlockSpec((ts, LANES), lambda i: (i % nst, 0)),
        ],
        out_specs=[row, row, row, row, hm2, hm, hm_ext, hm, hm, hm],
        out_shape=[rows, rows, rows, rows, heads2, heads, heads_ext, heads, heads, heads],
        compiler_params=_params("parallel"),
        name="proj",
    )(x, g_mix, w_in_b, cos_t, sin_t)


def _diff_init(rows):
    return jnp.full((rows, 1), -jnp.inf, F32), jnp.zeros((rows, 2 * HEAD_DIM), F32)


def _diff_update(s, v_ext, carry):
    m, acc = carry
    m_new = jnp.maximum(m, jnp.max(s, axis=-1, keepdims=True))
    p = jnp.exp(s - m_new)
    acc = jnp.exp(m - m_new) * acc + _mm(p.astype(BF16), v_ext)
    return m_new, acc


def _diff_finish(carry, lam, g_subln, tq):
    _, acc = carry
    o = acc[:, :HEAD_DIM] / acc[:, HEAD_DIM:HEAD_DIM + 1]
    d = o[:tq] - lam * o[tq:]
    return _rms(d, g_subln, SUBLN_EPS) * (1.0 - LAM_INIT)


def _suffix_sums(lk, tri):
    hi = lk.astype(BF16)
    lo = (lk - hi.astype(F32)).astype(BF16)
    return _mm(hi, tri) + _mm(lo, tri)


def _sb_update(q, k, v, tri, earlier, carry):
    run, acc = carry
    z = _nt(q, k)
    sp = jnp.maximum(z, 0.0) + jnp.log1p(jnp.exp(-jnp.abs(z)))
    lk = -sp if earlier is None else jnp.where(earlier, -sp, 0.0)
    after = _suffix_sums(lk, tri)
    w = jnp.exp((z - sp) + after + run)
    if earlier is not None:
        w = jnp.where(earlier, w, 0.0)
    acc = acc + _mm(w.astype(BF16), v)
    run = run + after[:, 0:1] + lk[:, 0:1]
    return run, acc


def _tri(n):
    j = lax.broadcasted_iota(jnp.int32, (n, n), 0)
    s = lax.broadcasted_iota(jnp.int32, (n, n), 1)
    return (j > s).astype(BF16)


def _diff_body(lam_ref, q_ref, k_ref, v_ref, g_ref, o_ref, *, tq, q0):
    qi = q0 + pl.program_id(1)
    q2 = q_ref[0, 0].reshape(2 * tq, HEAD_DIM)

    def scores(j):
        return _nt(q2, k_ref[0, 0, pl.ds(pl.multiple_of(j * tq, tq), tq), :])

    def step(j, state):
        s, carry = state
        s_next = scores(j + 1)
        return s_next, _diff_update(s, v_ref[0, 0, pl.ds(pl.multiple_of(j * tq, tq), tq), :], carry)

    s, carry = lax.fori_loop(0, qi, step, (scores(0), _diff_init(2 * tq)))
    r = lax.broadcasted_iota(jnp.int32, (2 * tq, tq), 0) % tq
    c = lax.broadcasted_iota(jnp.int32, (2 * tq, tq), 1)
    s = jnp.where((c // CHUNK) <= (r // CHUNK), s, -jnp.inf)
    carry = _diff_update(s, v_ref[0, 0, pl.ds(pl.multiple_of(qi * tq, tq), tq), :], carry)
    o_ref[0, 0] = _diff_finish(carry, lam_ref[0], g_ref[...], tq)


def _diff_attention(lam, qd2, kdb, vdb, g_subln, b, t0, n):
    seq = qd2.shape[3]
    tq = min(512, n)
    q0 = t0 // tq
    return pl.pallas_call(
        functools.partial(_diff_body, tq=tq, q0=q0),
        grid=(HEADS, n // tq),
        in_specs=[
            pl.BlockSpec(memory_space=pltpu.SMEM),
            pl.BlockSpec((1, 1, 2, tq, HEAD_DIM), lambda h, i: (b, h, 0, q0 + i, 0)),
            pl.BlockSpec((1, 1, seq, HEAD_DIM), lambda h, i: (b, h, 0, 0)),
            pl.BlockSpec((1, 1, seq, 2 * HEAD_DIM), lambda h, i: (b, h, 0, 0)),
            pl.BlockSpec((1, HEAD_DIM), lambda h, i: (0, 0)),
        ],
        out_specs=pl.BlockSpec((1, 1, tq, HEAD_DIM), lambda h, i: (0, h, i, 0)),
        out_shape=jax.ShapeDtypeStruct((1, HEADS, n, HEAD_DIM), F32),
        compiler_params=_params("parallel", "arbitrary"),
        name="diff_attention",
    )(lam, qd2, kdb, vdb, g_subln)


def _sb_body(q_ref, k_ref, v_ref, o_ref, *, tq, q0):
    qi = q0 + pl.program_id(1)
    q = q_ref[0, 0]
    tri = _tri(tq)

    def tile(j):
        start = pl.multiple_of(j * tq, tq)
        return k_ref[0, 0, pl.ds(start, tq), :], v_ref[0, 0, pl.ds(start, tq), :]

    r = lax.broadcasted_iota(jnp.int32, (tq, tq), 0)
    c = lax.broadcasted_iota(jnp.int32, (tq, tq), 1)
    carry = (jnp.zeros((tq, 1), F32), jnp.zeros((tq, HEAD_DIM), F32))
    run, acc = _sb_update(q, *tile(qi), tri, c < r, carry)

    def live(state):
        j, run, _ = state
        return jnp.logical_and(j >= 0, jnp.max(run) > SB_LOG_FLOOR)

    def step(state):
        j, run, acc = state
        run, acc = _sb_update(q, *tile(j), tri, None, (run, acc))
        return j - 1, run, acc

    o_ref[0, 0] = lax.while_loop(live, step, (qi - 1, run, acc))[2]


def _sb_attention(qsb, ksb, vsb, b, t0, n):
    seq = qsb.shape[2]
    tq = min(256, n)
    q0 = t0 // tq
    kv = pl.BlockSpec((1, 1, seq, HEAD_DIM), lambda h, i: (b, h, 0, 0))
    return pl.pallas_call(
        functools.partial(_sb_body, tq=tq, q0=q0),
        grid=(HEADS, n // tq),
        in_specs=[pl.BlockSpec((1, 1, tq, HEAD_DIM), lambda h, i: (b, h, q0 + i, 0)), kv, kv],
        out_specs=pl.BlockSpec((1, 1, tq, HEAD_DIM), lambda h, i: (0, h, i, 0)),
        out_shape=jax.ShapeDtypeStruct((1, HEADS, n, HEAD_DIM), F32),
        compiler_params=_params("parallel", "arbitrary"),
        name="sb_attention",
    )(qsb, ksb, vsb)


SAMPLE_HEADS = 4
SAMPLE_TILE = 256


def _sample_body(lam_ref, qd2_ref, kdn_ref, vdn_ref, qs_ref, ksn_ref, vsn_ref,
                 ckd_ref, cvd_ref, cks_ref, cvs_ref, g_ref, do_ref, so_ref, *, past, nq):
    lam = lam_ref[0]
    tri_c = _tri(SAMPLE_TILE)
    tri_n = _tri(nq)
    i2 = lax.broadcasted_iota(jnp.int32, (2 * nq, nq), 0) % nq
    j2 = lax.broadcasted_iota(jnp.int32, (2 * nq, nq), 1)
    visible_new = ((past + j2) // CHUNK) <= ((past + i2) // CHUNK)
    i1 = lax.broadcasted_iota(jnp.int32, (nq, nq), 0)
    j1 = lax.broadcasted_iota(jnp.int32, (nq, nq), 1)
    earlier_new = j1 < i1
    for h in range(SAMPLE_HEADS):
        sl = slice(h * HEAD_DIM, (h + 1) * HEAD_DIM)
        q2 = qd2_ref[0, h].reshape(2 * nq, HEAD_DIM)
        carry = _diff_update(_nt(q2, ckd_ref[0, :, sl].astype(BF16)),
                             _with_ones(cvd_ref[0, :, sl].astype(BF16)), _diff_init(2 * nq))
        s_new = jnp.where(visible_new, _nt(q2, kdn_ref[0, h]), -jnp.inf)
        carry = _diff_update(s_new, vdn_ref[0, h], carry)
        do_ref[0, h] = _diff_finish(carry, lam, g_ref[...], nq)
        q = qs_ref[0, h]
        carry = (jnp.zeros((nq, 1), F32), jnp.zeros((nq, HEAD_DIM), F32))
        carry = _sb_update(q, ksn_ref[0, h], vsn_ref[0, h], tri_n, earlier_new, carry)
        for t in reversed(range(past // SAMPLE_TILE)):
            rows = slice(t * SAMPLE_TILE, (t + 1) * SAMPLE_TILE)
            carry = _sb_update(q, cks_ref[0, rows, sl].astype(BF16), cvs_ref[0, rows, sl].astype(BF16),
                               tri_c, None, carry)
        so_ref[0, h] = carry[1]


def _sample_attention(lam, qd2, kdb, vdb, qsb, ksb, vsb, ckd, cvd, cks, cvs, g_subln, nb, nq):
    past = ckd.shape[1]
    nhg = HEADS // SAMPLE_HEADS
    hm = pl.BlockSpec((1, SAMPLE_HEADS, nq, HEAD_DIM), lambda b, g: (0, g, b, 0))
    hm2 = pl.BlockSpec((1, SAMPLE_HEADS, 2, nq, HEAD_DIM), lambda b, g: (0, g, 0, b, 0))
    cache = pl.BlockSpec((1, past, SAMPLE_HEADS * HEAD_DIM), lambda b, g: (b, 0, g))
    out = jax.ShapeDtypeStruct((1, HEADS, nb * nq, HEAD_DIM), F32)
    return pl.pallas_call(
        functools.partial(_sample_body, past=past, nq=nq),
        grid=(nb, nhg),
        in_specs=[pl.BlockSpec(memory_space=pltpu.SMEM), hm2, hm,
                  pl.BlockSpec((1, SAMPLE_HEADS, nq, 2 * HEAD_DIM), lambda b, g: (0, g, b, 0)), hm, hm, hm,
                  cache, cache, cache, cache,
                  pl.BlockSpec((1, HEAD_DIM), lambda b, g: (0, 0))],
        out_specs=[hm, hm],
        out_shape=[out, out],
        compiler_params=_params("parallel", "parallel"),
        name="sample_attention",
    )(lam, qd2, kdb, vdb, qsb, ksb, vsb, ckd, cvd, cks, cvs, g_subln)


def _topk_rows(s, k):
    n = s.shape[0]
    rows = lax.broadcasted_iota(jnp.int32, s.shape, 0)
    vals, ids = [], []
    for _ in range(k):
        m = jnp.max(s, axis=0, keepdims=True)
        i = jnp.min(jnp.where(s == m, rows, n), axis=0, keepdims=True)
        vals.append(m)
        ids.append(i)
        s = jnp.where(rows == i, -jnp.inf, s)
    return jnp.concatenate(vals, axis=0), jnp.concatenate(ids, axis=0)


_STAIR = [(i, j) for i in range(PEER_TOPK) for j in range(PEER_TOPK) if (i + 1) * (j + 1) <= PEER_TOPK]


def _post_body(x_ref, do_ref, so_ref, wo_ref, gf_ref, wq_ref, sk_ref,
               h1_ref, c_ref, idx_ref, gate_ref, q_scr, idx_scr, gate_scr):
    ts = x_ref.shape[0]
    mixed = jnp.zeros((ts, D_MODEL), F32)
    for h in range(HEADS):
        mixed += _mm(do_ref[0, h].astype(BF16), wo_ref[h * HEAD_DIM:(h + 1) * HEAD_DIM, :])
        mixed += _mm(so_ref[0, h].astype(BF16), wo_ref[MIX + h * HEAD_DIM:MIX + (h + 1) * HEAD_DIM, :])
    h1 = x_ref[...] + mixed
    h1_ref[...] = h1
    c = _rms(h1, gf_ref[...], NORM_EPS)
    c_ref[...] = c
    q = _mm(c.astype(BF16), wq_ref[...])
    for hp in range(2 * PEER_HEADS):
        q_scr[hp] = q[:, hp * PEER_HALF:(hp + 1) * PEER_HALF].astype(BF16)

    npad = -len(_STAIR) % 8

    def head(h, _):
        v1, i1 = _topk_rows(_nt(sk_ref[0], q_scr[2 * h]), PEER_TOPK)
        v2, i2 = _topk_rows(_nt(sk_ref[1], q_scr[2 * h + 1]), PEER_TOPK)
        cand = jnp.concatenate([v1[i:i + 1] + v2[j:j + 1] for i, j in _STAIR]
                               + [jnp.full((npad, ts), -jnp.inf, F32)], axis=0)
        eid = jnp.concatenate([i1[i:i + 1] * PEER_KEYS + i2[j:j + 1] for i, j in _STAIR]
                              + [jnp.zeros((npad, ts), jnp.int32)], axis=0)
        top, pos = _topk_rows(cand, PEER_TOPK)
        rows = lax.broadcasted_iota(jnp.int32, cand.shape, 0)
        sel = jnp.concatenate([jnp.sum(jnp.where(rows == pos[r:r + 1], eid, 0), axis=0, keepdims=True)
                               for r in range(PEER_TOPK)], axis=0)
        e = jnp.exp(top - top[0:1])
        gate_scr[h] = e / jnp.sum(e, axis=0, keepdims=True)
        idx_scr[h] = sel
        return 0

    lax.fori_loop(0, PEER_HEADS, head, 0)
    idx_ref[...] = idx_scr[...].reshape(PEER_SEL, ts).T
    gate_ref[...] = gate_scr[...].reshape(PEER_SEL, ts).T


def _post(x, dout, sout, w_out_b, g_ffn, w_query_b, sub_keys_b, tok0, t):
    ts = min(256, t)
    first = tok0 // ts
    row = pl.BlockSpec((ts, D_MODEL), lambda i: (i, 0))
    hm = pl.BlockSpec((1, HEADS, ts, HEAD_DIM), lambda i: (0, 0, i, 0))
    sel = pl.BlockSpec((ts, PEER_SEL), lambda i: (i, 0))
    full = lambda *shape: pl.BlockSpec(shape, lambda i: (0,) * len(shape))
    return pl.pallas_call(
        _post_body,
        grid=(t // ts,),
        in_specs=[pl.BlockSpec((ts, D_MODEL), lambda i: (first + i, 0)),
                  hm, hm, full(2 * MIX, D_MODEL), full(1, D_MODEL),
                  full(D_MODEL, 2 * PEER_HEADS * PEER_HALF), full(2, PEER_KEYS, PEER_HALF)],
        out_specs=[row, row, sel, sel],
        out_shape=[jax.ShapeDtypeStruct((t, D_MODEL), F32), jax.ShapeDtypeStruct((t, D_MODEL), F32),
                   jax.ShapeDtypeStruct((t, PEER_SEL), jnp.int32), jax.ShapeDtypeStruct((t, PEER_SEL), F32)],
        scratch_shapes=[pltpu.VMEM((2 * PEER_HEADS, ts, PEER_HALF), BF16),
                        pltpu.VMEM((PEER_HEADS, PEER_TOPK, ts), jnp.int32),
                        pltpu.VMEM((PEER_HEADS, PEER_TOPK, ts), F32)],
        compiler_params=_params("parallel"),
        name="post_peer_select",
    )(x, dout, sout, w_out_b, g_ffn, w_query_b, sub_keys_b)


def _coef_body(after_ref, gate_ref, dots_ref, o_ref):
    del after_ref
    d = dots_ref[...]
    o_ref[...] = gate_ref[...] * (0.5 * d * (1.0 + lax.erf(d * (2.0 ** -0.5))))


def _coef(gate, dots, after):
    t = gate.shape[0]
    ts = min(2048, t)
    blk = pl.BlockSpec((ts, PEER_SEL), lambda i: (i, 0))
    return pl.pallas_call(
        _coef_body, grid=(t // ts,), in_specs=[pl.BlockSpec(memory_space=pl.ANY), blk, blk], out_specs=blk,
        out_shape=jax.ShapeDtypeStruct((t, PEER_SEL), F32),
        compiler_params=_params("parallel"), name="peer_coef",
    )(after, gate, dots)


SC_CORES = 2
SC_SUBCORES = 16
SC_LANES = 16
SC_WORKERS = SC_CORES * SC_SUBCORES
SC_ROWS = 16
SC_BUFS = 5
SC_GROUP = 8
SC_CHUNKS = PEER_SEL // SC_ROWS
SC_STEPS = SC_GROUP * SC_CHUNKS
SC_VECS = D_MODEL // SC_LANES


def _sc_mesh():
    return plsc.VectorSubcoreMesh(core_axis_name="c", subcore_axis_name="s",
                                  num_cores=SC_CORES, num_subcores=SC_SUBCORES)


def _sc_walk(table_hbm, idx_hbm, aux_hbm, out_hbm, idx_v, aux_v, rows_v, out_v, sem, stage_sem, out_sem, tpw,
             begin_group, compute):
    tok_base = (lax.axis_index("s") * SC_CORES + lax.axis_index("c")) * tpw
    ngroups = tpw // SC_GROUP
    nsteps = tpw * SC_CHUNKS

    def first_token(g):
        return pl.multiple_of(tok_base + g * SC_GROUP, SC_GROUP)

    def write_back(g):
        return pltpu.make_async_copy(out_v.at[g % 2, :, pl.ds(0, out_hbm.shape[1])],
                                     out_hbm.at[pl.ds(first_token(g), SC_GROUP)], out_sem.at[g % 2])

    def stage(g):
        tok0 = first_token(g)
        return (pltpu.make_async_copy(idx_hbm.at[pl.ds(tok0 * SC_CHUNKS, SC_STEPS)], idx_v.at[g % 2], stage_sem.at[0]),
                pltpu.make_async_copy(aux_hbm.at[pl.ds(tok0, SC_GROUP)],
                                      aux_v.at[g % 2, :, pl.ds(0, aux_hbm.shape[1])], stage_sem.at[1]))

    def gather(step):
        idx = idx_v.at[(step // SC_STEPS) % 2, step % SC_STEPS]
        return pltpu.make_async_copy(table_hbm.at[idx], rows_v.at[step % SC_BUFS], sem.at[step % SC_BUFS])

    for cp in stage(0):
        cp.start()
    for cp in stage(0):
        cp.wait()
    for step in range(SC_BUFS - 1):
        gather(step).start()

    def walk(step, _):
        g = step // SC_STEPS
        local = step % SC_STEPS

        @pl.when(jnp.logical_and(local == 0, g + 1 < ngroups))
        def _():
            for cp in stage(g + 1):
                cp.start()

        ahead = step + (SC_BUFS - 1)

        @pl.when(ahead < nsteps)
        def _():
            @pl.when(ahead % SC_STEPS == 0)
            def _():
                for cp in stage(ahead // SC_STEPS):
                    cp.wait()

            gather(ahead).start()

        @pl.when(local == 0)
        def _():
            @pl.when(g >= 2)
            def _():
                write_back(g - 2).wait()

            begin_group(out_v.at[g % 2])

        gather(step).wait()
        compute(rows_v.at[step % SC_BUFS], g % 2, local // SC_CHUNKS, local % SC_CHUNKS, out_v.at[g % 2])

        @pl.when(local == SC_STEPS - 1)
        def _():
            write_back(g).start()

        return 0

    lax.fori_loop(0, nsteps, walk, 0)
    for g in range(max(0, ngroups - 2), ngroups):
        write_back(g).wait()


def _sc_dots_body(u_hbm, idx_hbm, c_hbm, out_hbm, idx_v, c_v, rows_v, out_v, sem, stage_sem, out_sem, *, tpw):
    lane = lax.broadcasted_iota(jnp.int32, (SC_LANES,), 0)
    zero = jnp.zeros((SC_LANES,), F32)

    def compute(rows, slot, tt, ch, dots_v):
        for half in range(SC_ROWS // SC_LANES):
            r0 = half * SC_LANES

            def vec(kk, accs):
                off = pl.multiple_of(kk * SC_LANES, SC_LANES)
                cv = c_v[slot, tt, pl.ds(off, SC_LANES)]
                return tuple(a + rows[r0 + r, pl.ds(off, SC_LANES)] * cv for r, a in enumerate(accs))

            accs = lax.fori_loop(0, SC_VECS, vec, (zero,) * SC_LANES)
            outv = zero
            for r in range(SC_LANES):
                outv = jnp.where(lane == r, jnp.sum(accs[r]), outv)
            dots_v[tt, pl.ds(pl.multiple_of(ch * SC_ROWS + half * SC_LANES, SC_LANES), SC_LANES)] = outv

    _sc_walk(u_hbm, idx_hbm, c_hbm, out_hbm, idx_v, c_v, rows_v, out_v, sem, stage_sem, out_sem, tpw,
             lambda out: None, compute)


def _sc_combine_body(v_hbm, idx_hbm, coef_hbm, out_hbm, idx_v, coef_v, rows_v, out_v, sem, stage_sem, out_sem, *, tpw):
    lane = lax.broadcasted_iota(jnp.int32, (SC_LANES,), 0)
    zero = jnp.zeros((SC_LANES,), F32)

    def compute(rows, slot, tt, ch, acc_v):
        for half in range(SC_ROWS // SC_LANES):
            cf = coef_v[slot, tt, pl.ds(pl.multiple_of(ch * SC_ROWS + half * SC_LANES, SC_LANES), SC_LANES)]
            splat = [jnp.full((SC_LANES,), jnp.sum(jnp.where(lane == r, cf, 0.0)), F32) for r in range(SC_LANES)]

            @plsc.parallel_loop(0, SC_VECS, unroll=2)
            def _(kk):
                off = pl.multiple_of(kk * SC_LANES, SC_LANES)
                terms = [rows[half * SC_LANES + r, pl.ds(off, SC_LANES)] * splat[r] for r in range(SC_LANES)]
                while len(terms) > 1:
                    terms = [a + b for a, b in zip(terms[0::2], terms[1::2])]
                acc_v[tt, pl.ds(off, SC_LANES)] = acc_v[tt, pl.ds(off, SC_LANES)] + terms[0]

    def begin_group(acc_v):
        def clear(i, _):
            acc_v[i // SC_VECS, pl.ds(pl.multiple_of((i % SC_VECS) * SC_LANES, SC_LANES), SC_LANES)] = zero
            return 0

        lax.fori_loop(0, SC_GROUP * SC_VECS, clear, 0)

    _sc_walk(v_hbm, idx_hbm, coef_hbm, out_hbm, idx_v, coef_v, rows_v, out_v, sem, stage_sem, out_sem, tpw,
             begin_group, compute)


def _sc_call(body, table, idx, per_token, out_width, name):
    t = per_token.shape[0]
    tpw = t // SC_WORKERS
    assert tpw % SC_GROUP == 0
    return pl.kernel(
        functools.partial(body, tpw=tpw),
        out_type=jax.ShapeDtypeStruct((t, out_width), F32),
        mesh=_sc_mesh(),
        scratch_types=[pltpu.VMEM((2, SC_STEPS, SC_ROWS), jnp.int32),
                       pltpu.VMEM((2, SC_GROUP, per_token.shape[1]), F32),
                       pltpu.VMEM((SC_BUFS, SC_ROWS, D_MODEL), F32),
                       pltpu.VMEM((2, SC_GROUP, out_width), F32),
                       pltpu.SemaphoreType.DMA((SC_BUFS,)),
                       pltpu.SemaphoreType.DMA((2,)),
                       pltpu.SemaphoreType.DMA((2,))],
        compiler_params=pltpu.CompilerParams(needs_layout_passes=False),
        name=name,
    )(table, idx.reshape(t * SC_CHUNKS, SC_ROWS), per_token)


def _sc_step_body(v_hbm, idx_a_hbm, coef_hbm, u_hbm, idx_b_hbm, c_hbm, peer_hbm, dots_hbm,
                  idx_v, aux_v, rows_v, out_v, sem, stage_sem, out_sem, *, tpw_a, tpw_b):
    _sc_combine_body(v_hbm, idx_a_hbm, coef_hbm, peer_hbm, idx_v, aux_v, rows_v, out_v, sem, stage_sem, out_sem,
                     tpw=tpw_a)
    _sc_dots_body(u_hbm, idx_b_hbm, c_hbm, dots_hbm, idx_v, aux_v, rows_v, out_v, sem, stage_sem, out_sem,
                  tpw=tpw_b)


def _sc_step(expert_v, idx_a, coef, expert_u, idx_b, c):
    ta, tb = coef.shape[0], c.shape[0]
    assert ta % (SC_WORKERS * SC_GROUP) == 0 and tb % (SC_WORKERS * SC_GROUP) == 0
    return pl.kernel(
        functools.partial(_sc_step_body, tpw_a=ta // SC_WORKERS, tpw_b=tb // SC_WORKERS),
        out_type=[jax.ShapeDtypeStruct((ta, D_MODEL), F32), jax.ShapeDtypeStruct((tb, PEER_SEL), F32)],
        mesh=_sc_mesh(),
        scratch_types=[pltpu.VMEM((2, SC_STEPS, SC_ROWS), jnp.int32),
                       pltpu.VMEM((2, SC_GROUP, D_MODEL), F32),
                       pltpu.VMEM((SC_BUFS, SC_ROWS, D_MODEL), F32),
                       pltpu.VMEM((2, SC_GROUP, D_MODEL), F32),
                       pltpu.SemaphoreType.DMA((SC_BUFS,)),
                       pltpu.SemaphoreType.DMA((2,)),
                       pltpu.SemaphoreType.DMA((2,))],
        compiler_params=pltpu.CompilerParams(needs_layout_passes=False),
        name="peer_step",
    )(expert_v, idx_a.reshape(ta * SC_CHUNKS, SC_ROWS), coef, expert_u, idx_b.reshape(tb * SC_CHUNKS, SC_ROWS), c)


def _sc_dots(expert_u, idx, c):
    return _sc_call(_sc_dots_body, expert_u, idx, c, PEER_SEL, "peer_dots")


def _sc_combine(expert_v, idx, coef):
    return _sc_call(_sc_combine_body, expert_v, idx, coef, D_MODEL, "peer_combine")


def _ple_body(h1_ref, peer_ref, p_ref, gp_ref, wg_ref, we_ref, gfin_ref, *refs):
    y_ref = refs[-1]
    h = h1_ref[...] + peer_ref[...]
    a = _rms(h, gp_ref[...], NORM_EPS).astype(BF16)
    gate = jax.nn.sigmoid(_mm(a, wg_ref[...]))
    h = h + _mm(p_ref[...].astype(BF16), we_ref[...]) * gate
    y_ref[...] = _rms(h, gfin_ref[...], NORM_EPS)


def _ple(h1, peer, p, g_ple, w_pgate_b, w_ple_b, g_final, tok0, into=()):
    t = h1.shape[0]
    ts = min(512, t)
    first = tok0 // ts
    row = pl.BlockSpec((ts, D_MODEL), lambda i: (i, 0))
    full = lambda *shape: pl.BlockSpec(shape, lambda i: (0,) * len(shape))
    n_in = 7
    return pl.pallas_call(
        _ple_body,
        grid=(t // ts,),
        in_specs=[row, row, pl.BlockSpec((ts, PLE_DIM), lambda i: (first + i, 0)), full(1, D_MODEL),
                  full(D_MODEL, D_MODEL), full(PLE_DIM, D_MODEL), full(1, D_MODEL)]
        + [pl.BlockSpec(memory_space=pl.ANY)] * len(into),
        out_specs=pl.BlockSpec((ts, D_MODEL), lambda i: (first + i, 0)),
        out_shape=jax.ShapeDtypeStruct((p.shape[0], D_MODEL), F32),
        input_output_aliases={n_in + j: j for j in range(len(into))},
        compiler_params=_params("parallel"),
        name="ple_final",
    )(h1, peer, p, g_ple, w_pgate_b, w_ple_b, g_final, *into)


def _rope_tables(pos):
    inv = ROPE_THETA ** (-jnp.arange(0, DIFF_COMP, 2, dtype=F32) / DIFF_COMP)
    ang = pos.astype(F32)[:, None] * inv[None, :]
    cos = jnp.cos(ang)
    sin = jnp.sin(ang)
    reps = LANES // DIFF_COMP
    return (jnp.tile(jnp.concatenate([cos, cos], axis=-1), (1, reps)),
            jnp.tile(jnp.concatenate([-sin, sin], axis=-1), (1, reps)))


def kernel(x_prompt, x_sample, cache_diff_k, cache_diff_v, cache_sb_k, cache_sb_v, p_prompt, p_sample, g_mix, w_in, lambda_q1, lambda_k1, lambda_q2, lambda_k2, g_subln, w_out, g_ffn, w_query, sub_keys, expert_u, expert_v, g_ple, w_pgate, w_ple, g_final):
    assert w_in.shape[0] == 1, "single-layer encoder"
    nb, seq, _ = x_prompt.shape
    db, dq, _ = x_sample.shape
    past = cache_diff_k.shape[2]

    lam = (jnp.exp(jnp.sum(lambda_q1[0].astype(F32) * lambda_k1[0].astype(F32)))
           - jnp.exp(jnp.sum(lambda_q2[0].astype(F32) * lambda_k2[0].astype(F32))) + LAM_INIT).reshape(1)
    w_in_b = w_in[0].astype(BF16)
    w_out_b = w_out[0].astype(BF16)
    w_query_b = w_query[0].astype(BF16)
    sub_keys_b = sub_keys[0].astype(BF16)
    w_pgate_b = w_pgate[0].astype(BF16)
    w_ple_b = w_ple[0].astype(BF16)
    g_sub = g_subln[0].reshape(1, HEAD_DIM)
    g_fin = g_final.reshape(1, D_MODEL)

    def select(x, dout, sout, tok0, t):
        return _post(x, dout, sout, w_out_b, g_ffn, w_query_b, sub_keys_b, tok0, t)

    ts = db * dq
    xs = x_sample.reshape(ts, D_MODEL)
    cos_s, sin_s = _rope_tables(jnp.tile(past + jnp.arange(dq, dtype=jnp.int32), db))
    kd, vd, ks, vs, qd2, kdb, vdb, qsb, ksb, vsb = _proj(xs, g_mix, w_in_b, cos_s, sin_s, 1, ts, False)
    caches = [c[0].reshape(db, past, MIX) for c in (cache_diff_k, cache_diff_v, cache_sb_k, cache_sb_v)]
    dout_s, sout_s = _sample_attention(lam, qd2, kdb, vdb, qsb, ksb, vsb, *caches, g_sub, db, dq)
    rows_s = tuple(r.reshape(1, db, dq, HEADS, HEAD_DIM) for r in (kd, vd, ks, vs))

    xp = x_prompt.reshape(nb * seq, D_MODEL)
    pp = p_prompt[0].reshape(nb * seq, PLE_DIM)
    cos_p, sin_p = _rope_tables(jnp.arange(seq, dtype=jnp.int32))
    kd, vd, ks, vs, qd2, kdb, vdb, qsb, ksb, vsb = _proj(xp, g_mix, w_in_b, cos_p, sin_p, nb, seq, True)
    rows_p = tuple(r.reshape(nb, HEADS, HEAD_DIM, seq).transpose(0, 3, 1, 2)[None] for r in (kd, vd, ks, vs))

    cut = seq // PROMPT_ROW_BLOCKS
    spans = [(b, i * cut, cut) for b in range(nb) for i in range(PROMPT_ROW_BLOCKS)]
    for end in (0, -1):
        b, t0, n = spans.pop(end)
        spans[end if end == 0 else len(spans):0] = [(b, t0, n // 2), (b, t0 + n // 2, n // 2)]

    def prompt_block(b, t0, n):
        dout = _diff_attention(lam, qd2, kdb, vdb, g_sub, b, t0, n)
        sout = _sb_attention(qsb, ksb, vsb, b, t0, n)
        return select(xp, dout, sout, b * seq + t0, n)

    blocks = [(functools.partial(prompt_block, *span), "prompt", span[0] * seq + span[1]) for span in spans]
    blocks.insert(min(PROMPT_ROW_BLOCKS + 1, len(blocks)), (lambda: select(xs, dout_s, sout_s, 0, ts), "sample", 0))
    embeds = {"prompt": pp, "sample": p_sample[0].reshape(ts, PLE_DIM)}
    results = {"prompt": (), "sample": ()}

    def emit(h1, peer, group, tok0):
        results[group] = (_ple(h1, peer, embeds[group], g_ple, w_pgate_b, w_ple_b, g_fin, tok0, results[group]),)

    h1, c, idx, gate = blocks[0][0]()
    dots = _sc_dots(expert_u[0], idx, c)
    for k in range(1, len(blocks)):
        nh1, nc, nidx, ngate = blocks[k][0]()
        coef = _coef(gate, dots, nh1)
        peer, ndots = _sc_step(expert_v[0], idx, coef, expert_u[0], nidx, nc)
        emit(h1, peer, *blocks[k - 1][1:])
        h1, idx, gate, dots = nh1, nidx, ngate, ndots
    coef = _coef(gate, dots, lam)
    emit(h1, _sc_combine(expert_v[0], idx, coef), *blocks[-1][1:])
    y_sample = results["sample"][0].reshape(db, dq, D_MODEL)
    y_prompt = results["prompt"][0].reshape(nb, seq, D_MODEL)

    return (y_prompt, y_sample) + rows_p + rows_s
```

```python
import functools
import math

import jax
import jax.numpy as jnp
from jax import lax
from jax.experimental import pallas as pl
from jax.experimental.pallas import tpu as pltpu
from jax.experimental.pallas import tpu_sc as plsc

F32 = jnp.float32
BF16 = jnp.bfloat16

D_MODEL = 1024
HEADS = 8
HEAD_DIM = 64
DIFF_COMP = 32
MIX = HEADS * HEAD_DIM
CHUNK = 64
ROPE_THETA = 10000.0
NORM_EPS = 1e-6
SUBLN_EPS = 1e-5
PEER_HEADS = 8
PEER_KEYS = 128
PEER_TOPK = 16
PEER_HALF = 128
PEER_SEL = PEER_HEADS * PEER_TOPK
PLE_DIM = 256
LAM_INIT = 0.8 - 0.6 * math.exp(-0.3 * 0)
SB_LOG_FLOOR = -104.0
SB_TILE = 128
PROMPT_ROW_BLOCKS = 4

LANES = 128
VMEM_LIMIT = 48 * 1024 * 1024

NT_DIMS = (((1,), (1,)), ((), ()))


def _nt(a, b):
    return lax.dot_general(a, b, NT_DIMS, preferred_element_type=F32)


def _mm(a, b):
    return jnp.dot(a, b, preferred_element_type=F32)


def _rms(x, g, eps):
    return x * lax.rsqrt(jnp.mean(x * x, axis=-1, keepdims=True) + eps) * g


def _params(*sem):
    return pltpu.CompilerParams(dimension_semantics=sem, vmem_limit_bytes=VMEM_LIMIT)


def _with_ones(v):
    n = v.shape[0]
    ones = (lax.broadcasted_iota(jnp.int32, (n, HEAD_DIM), 1) == 0).astype(v.dtype)
    return jnp.concatenate([v, ones], axis=1)


def _proj_body(x_ref, g_ref, w_ref, cos_ref, sin_ref,
               kd_ref, vd_ref, ks_ref, vs_ref,
               qd2_ref, kdb_ref, vdb_ref, qsb_ref, ksb_ref, vsb_ref, *, feature_major):
    ts = x_ref.shape[0]
    a = _rms(x_ref[...], g_ref[...], NORM_EPS).astype(BF16)
    cos = jnp.tile(cos_ref[...], (1, MIX // LANES))
    sin = jnp.tile(sin_ref[...], (1, MIX // LANES))
    lane = lax.broadcasted_iota(jnp.int32, (ts, MIX), 1)
    first_half = (lane % DIFF_COMP) < (DIFF_COMP // 2)

    def group(i):
        return _mm(a, w_ref[:, i * MIX:(i + 1) * MIX])

    def rope(t):
        partner = jnp.where(first_half,
                            pltpu.roll(t, MIX - DIFF_COMP // 2, 1),
                            pltpu.roll(t, DIFF_COMP // 2, 1))
        return t * cos + partner * sin

    qd = rope(group(0)) * (DIFF_COMP ** -0.5)
    kd = rope(group(1))
    vd = group(2)
    qs = group(3) * (HEAD_DIM ** -0.5)
    ks = group(4)
    vs = group(5)
    for ref, rows in ((kd_ref, kd), (vd_ref, vd), (ks_ref, ks), (vs_ref, vs)):
        if feature_major:
            ref[0] = rows.T
        else:
            ref[...] = rows
    comp0 = lax.broadcasted_iota(jnp.int32, (ts, HEAD_DIM), 1) < DIFF_COMP
    for h in range(HEADS):
        sl = slice(h * HEAD_DIM, (h + 1) * HEAD_DIM)
        qh = qd[:, sl]
        qd2_ref[0, h, 0] = jnp.where(comp0, qh, 0.0).astype(BF16)
        qd2_ref[0, h, 1] = jnp.where(comp0, 0.0, qh).astype(BF16)
        kdb_ref[0, h] = kd[:, sl].astype(BF16)
        vdb_ref[0, h] = _with_ones(vd[:, sl].astype(BF16))
        qsb_ref[0, h] = qs[:, sl].astype(BF16)
        ksb_ref[0, h] = ks[:, sl].astype(BF16)
        vsb_ref[0, h] = vs[:, sl].astype(BF16)


def _proj(x, g_mix, w_in_b, cos_t, sin_t, nb, seq, feature_major):
    t = nb * seq
    ts = min(256, seq)
    nst = seq // ts
    if feature_major:
        row = pl.BlockSpec((1, MIX, ts), lambda i: (i // nst, 0, i % nst))
        rows = jax.ShapeDtypeStruct((nb, MIX, seq), F32)
    else:
        row = pl.BlockSpec((ts, MIX), lambda i: (i, 0))
        rows = jax.ShapeDtypeStruct((t, MIX), F32)
    hm = pl.BlockSpec((1, HEADS, ts, HEAD_DIM), lambda i: (i // nst, 0, i % nst, 0))
    hm2 = pl.BlockSpec((1, HEADS, 2, ts, HEAD_DIM), lambda i: (i // nst, 0, 0, i % nst, 0))
    heads = jax.ShapeDtypeStruct((nb, HEADS, seq, HEAD_DIM), BF16)
    heads2 = jax.ShapeDtypeStruct((nb, HEADS, 2, seq, HEAD_DIM), BF16)
    hm_ext = pl.BlockSpec((1, HEADS, ts, 2 * HEAD_DIM), lambda i: (i // nst, 0, i % nst, 0))
    heads_ext = jax.ShapeDtypeStruct((nb, HEADS, seq, 2 * HEAD_DIM), BF16)
    return pl.pallas_call(
        functools.partial(_proj_body, feature_major=feature_major),
        grid=(t // ts,),
        in_specs=[
            pl.BlockSpec((ts, D_MODEL), lambda i: (i, 0)),
            pl.BlockSpec((1, D_MODEL), lambda i: (0, 0)),
            pl.BlockSpec((D_MODEL, 6 * MIX), lambda i: (0, 0)),
            pl.BlockSpec((ts, LANES), lambda i: (i % nst, 0)),
            pl.BlockSpec((ts, LANES), lambda i: (i % nst, 0)),
        ],
        out_specs=[row, row, row, row, hm2, hm, hm_ext, hm, hm, hm],
        out_shape=[rows, rows, rows, rows, heads2, heads, heads_ext, heads, heads, heads],
        compiler_params=_params("parallel"),
        name="proj",
    )(x, g_mix, w_in_b, cos_t, sin_t)


def _diff_init(rows):
    return jnp.full((rows, 1), -jnp.inf, F32), jnp.zeros((rows, 2 * HEAD_DIM), F32)


def _diff_update(s, v_ext, carry):
    m, acc = carry
    m_new = jnp.maximum(m, jnp.max(s, axis=-1, keepdims=True))
    p = jnp.exp(s - m_new)
    acc = jnp.exp(m - m_new) * acc + _mm(p.astype(BF16), v_ext)
    return m_new, acc


def _diff_finish(carry, lam, g_subln, tq):
    _, acc = carry
    o = acc[:, :HEAD_DIM] / acc[:, HEAD_DIM:HEAD_DIM + 1]
    d = o[:tq] - lam * o[tq:]
    return _rms(d, g_subln, SUBLN_EPS) * (1.0 - LAM_INIT)


def _suffix_sums(lk, tri):
    hi = lk.astype(BF16)
    lo = (lk - hi.astype(F32)).astype(BF16)
    return _mm(hi, tri) + _mm(lo, tri)


def _sb_update(q, k, v, tri, earlier, carry):
    run, acc = carry
    z = _nt(q, k)
    sp = jnp.maximum(z, 0.0) + jnp.log1p(jnp.exp(-jnp.abs(z)))
    lk = -sp if earlier is None else jnp.where(earlier, -sp, 0.0)
    after = _suffix_sums(lk, tri)
    w = jnp.exp((z - sp) + after + run)
    if earlier is not None:
        w = jnp.where(earlier, w, 0.0)
    acc = acc + _mm(w.astype(BF16), v)
    run = run + after[:, 0:1] + lk[:, 0:1]
    return run, acc


def _tri(n):
    j = lax.broadcasted_iota(jnp.int32, (n, n), 0)
    s = lax.broadcasted_iota(jnp.int32, (n, n), 1)
    return (j > s).astype(BF16)


def _diff_body(lam_ref, q_ref, k_ref, v_ref, g_ref, o_ref, *, tq, q0):
    qi = q0 + pl.program_id(1)
    q2 = q_ref[0, 0].reshape(2 * tq, HEAD_DIM)

    def scores(j):
        return _nt(q2, k_ref[0, 0, pl.ds(pl.multiple_of(j * tq, tq), tq), :])

    def step(j, state):
        s, carry = state
        s_next = scores(j + 1)
        return s_next, _diff_update(s, v_ref[0, 0, pl.ds(pl.multiple_of(j * tq, tq), tq), :], carry)

    s, carry = lax.fori_loop(0, qi, step, (scores(0), _diff_init(2 * tq)))
    r = lax.broadcasted_iota(jnp.int32, (2 * tq, tq), 0) % tq
    c = lax.broadcasted_iota(jnp.int32, (2 * tq, tq), 1)
    s = jnp.where((c // CHUNK) <= (r // CHUNK), s, -jnp.inf)
    carry = _diff_update(s, v_ref[0, 0, pl.ds(pl.multiple_of(qi * tq, tq), tq), :], carry)
    o_ref[0, 0] = _diff_finish(carry, lam_ref[0], g_ref[...], tq)


def _diff_attention(lam, qd2, kdb, vdb, g_subln, b, t0, n):
    seq = qd2.shape[3]
    tq = min(512, n)
    q0 = t0 // tq
    return pl.pallas_call(
        functools.partial(_diff_body, tq=tq, q0=q0),
        grid=(HEADS, n // tq),
        in_specs=[
            pl.BlockSpec(memory_space=pltpu.SMEM),
            pl.BlockSpec((1, 1, 2, tq, HEAD_DIM), lambda h, i: (b, h, 0, q0 + i, 0)),
            pl.BlockSpec((1, 1, seq, HEAD_DIM), lambda h, i: (b, h, 0, 0)),
            pl.BlockSpec((1, 1, seq, 2 * HEAD_DIM), lambda h, i: (b, h, 0, 0)),
            pl.BlockSpec((1, HEAD_DIM), lambda h, i: (0, 0)),
        ],
        out_specs=pl.BlockSpec((1, 1, tq, HEAD_DIM), lambda h, i: (0, h, i, 0)),
        out_shape=jax.ShapeDtypeStruct((1, HEADS, n, HEAD_DIM), F32),
        compiler_params=_params("parallel", "arbitrary"),
        name="diff_attention",
    )(lam, qd2, kdb, vdb, g_subln)


def _sb_body(q_ref, k_ref, v_ref, o_ref, *, tq, q0):
    qi = q0 + pl.program_id(1)
    q = q_ref[0, 0]
    tri = _tri(tq)

    def tile(j):
        start = pl.multiple_of(j * tq, tq)
        return k_ref[0, 0, pl.ds(start, tq), :], v_ref[0, 0, pl.ds(start, tq), :]

    r = lax.broadcasted_iota(jnp.int32, (tq, tq), 0)
    c = lax.broadcasted_iota(jnp.int32, (tq, tq), 1)
    carry = (jnp.zeros((tq, 1), F32), jnp.zeros((tq, HEAD_DIM), F32))
    run, acc = _sb_update(q, *tile(qi), tri, c < r, carry)

    def live(state):
        j, run, _ = state
        return jnp.logical_and(j >= 0, jnp.max(run) > SB_LOG_FLOOR)

    def step(state):
        j, run, acc = state
        run, acc = _sb_update(q, *tile(j), tri, None, (run, acc))
        return j - 1, run, acc

    o_ref[0, 0] = lax.while_loop(live, step, (qi - 1, run, acc))[2]


def _sb_attention(qsb, ksb, vsb, b, t0, n):
    seq = qsb.shape[2]
    tq = min(SB_TILE, n)
    q0 = t0 // tq
    kv = pl.BlockSpec((1, 1, seq, HEAD_DIM), lambda h, i: (b, h, 0, 0))
    return pl.pallas_call(
        functools.partial(_sb_body, tq=tq, q0=q0),
        grid=(HEADS, n // tq),
        in_specs=[pl.BlockSpec((1, 1, tq, HEAD_DIM), lambda h, i: (b, h, q0 + i, 0)), kv, kv],
        out_specs=pl.BlockSpec((1, 1, tq, HEAD_DIM), lambda h, i: (0, h, i, 0)),
        out_shape=jax.ShapeDtypeStruct((1, HEADS, n, HEAD_DIM), F32),
        compiler_params=_params("parallel", "arbitrary"),
        name="sb_attention",
    )(qsb, ksb, vsb)


SAMPLE_HEADS = 4
SAMPLE_TILE = 256


def _sample_body(lam_ref, qd2_ref, kdn_ref, vdn_ref, qs_ref, ksn_ref, vsn_ref,
                 ckd_ref, cvd_ref, cks_ref, cvs_ref, g_ref, do_ref, so_ref, *, past, nq):
    lam = lam_ref[0]
    tri_c = _tri(SAMPLE_TILE)
    tri_n = _tri(nq)
    i2 = lax.broadcasted_iota(jnp.int32, (2 * nq, nq), 0) % nq
    j2 = lax.broadcasted_iota(jnp.int32, (2 * nq, nq), 1)
    visible_new = ((past + j2) // CHUNK) <= ((past + i2) // CHUNK)
    i1 = lax.broadcasted_iota(jnp.int32, (nq, nq), 0)
    j1 = lax.broadcasted_iota(jnp.int32, (nq, nq), 1)
    earlier_new = j1 < i1
    for h in range(SAMPLE_HEADS):
        sl = slice(h * HEAD_DIM, (h + 1) * HEAD_DIM)
        q2 = qd2_ref[0, h].reshape(2 * nq, HEAD_DIM)
        carry = _diff_update(_nt(q2, ckd_ref[0, :, sl].astype(BF16)),
                             _with_ones(cvd_ref[0, :, sl].astype(BF16)), _diff_init(2 * nq))
        s_new = jnp.where(visible_new, _nt(q2, kdn_ref[0, h]), -jnp.inf)
        carry = _diff_update(s_new, vdn_ref[0, h], carry)
        do_ref[0, h] = _diff_finish(carry, lam, g_ref[...], nq)
        q = qs_ref[0, h]
        carry = (jnp.zeros((nq, 1), F32), jnp.zeros((nq, HEAD_DIM), F32))
        carry = _sb_update(q, ksn_ref[0, h], vsn_ref[0, h], tri_n, earlier_new, carry)
        for t in reversed(range(past // SAMPLE_TILE)):
            rows = slice(t * SAMPLE_TILE, (t + 1) * SAMPLE_TILE)
            carry = _sb_update(q, cks_ref[0, rows, sl].astype(BF16), cvs_ref[0, rows, sl].astype(BF16),
                               tri_c, None, carry)
        so_ref[0, h] = carry[1]


def _sample_attention(lam, qd2, kdb, vdb, qsb, ksb, vsb, ckd, cvd, cks, cvs, g_subln, nb, nq):
    past = ckd.shape[1]
    nhg = HEADS // SAMPLE_HEADS
    hm = pl.BlockSpec((1, SAMPLE_HEADS, nq, HEAD_DIM), lambda b, g: (0, g, b, 0))
    hm2 = pl.BlockSpec((1, SAMPLE_HEADS, 2, nq, HEAD_DIM), lambda b, g: (0, g, 0, b, 0))
    cache = pl.BlockSpec((1, past, SAMPLE_HEADS * HEAD_DIM), lambda b, g: (b, 0, g))
    out = jax.ShapeDtypeStruct((1, HEADS, nb * nq, HEAD_DIM), F32)
    return pl.pallas_call(
        functools.partial(_sample_body, past=past, nq=nq),
        grid=(nb, nhg),
        in_specs=[pl.BlockSpec(memory_space=pltpu.SMEM), hm2, hm,
                  pl.BlockSpec((1, SAMPLE_HEADS, nq, 2 * HEAD_DIM), lambda b, g: (0, g, b, 0)), hm, hm, hm,
                  cache, cache, cache, cache,
                  pl.BlockSpec((1, HEAD_DIM), lambda b, g: (0, 0))],
        out_specs=[hm, hm],
        out_shape=[out, out],
        compiler_params=_params("parallel", "parallel"),
        name="sample_attention",
    )(lam, qd2, kdb, vdb, qsb, ksb, vsb, ckd, cvd, cks, cvs, g_subln)


def _topk_rows(s, k):
    n = s.shape[0]
    rows = lax.broadcasted_iota(jnp.int32, s.shape, 0)
    vals, ids = [], []
    for _ in range(k):
        m = jnp.max(s, axis=0, keepdims=True)
        i = jnp.min(jnp.where(s == m, rows, n), axis=0, keepdims=True)
        vals.append(m)
        ids.append(i)
        s = jnp.where(rows == i, -jnp.inf, s)
    return jnp.concatenate(vals, axis=0), jnp.concatenate(ids, axis=0)


_STAIR = [(i, j) for i in range(PEER_TOPK) for j in range(PEER_TOPK) if (i + 1) * (j + 1) <= PEER_TOPK]


def _post_body(x_ref, do_ref, so_ref, wo_ref, gf_ref, wq_ref, sk_ref,
               h1_ref, c_ref, idx_ref, gate_ref, q_scr, idx_scr, gate_scr):
    ts = x_ref.shape[0]
    mixed = jnp.zeros((ts, D_MODEL), F32)
    for h in range(HEADS):
        mixed += _mm(do_ref[0, h].astype(BF16), wo_ref[h * HEAD_DIM:(h + 1) * HEAD_DIM, :])
        mixed += _mm(so_ref[0, h].astype(BF16), wo_ref[MIX + h * HEAD_DIM:MIX + (h + 1) * HEAD_DIM, :])
    h1 = x_ref[...] + mixed
    h1_ref[...] = h1
    c = _rms(h1, gf_ref[...], NORM_EPS)
    c_ref[...] = c
    q = _mm(c.astype(BF16), wq_ref[...])
    for hp in range(2 * PEER_HEADS):
        q_scr[hp] = q[:, hp * PEER_HALF:(hp + 1) * PEER_HALF].astype(BF16)

    npad = -len(_STAIR) % 8

    def head(h, _):
        v1, i1 = _topk_rows(_nt(sk_ref[0], q_scr[2 * h]), PEER_TOPK)
        v2, i2 = _topk_rows(_nt(sk_ref[1], q_scr[2 * h + 1]), PEER_TOPK)
        cand = jnp.concatenate([v1[i:i + 1] + v2[j:j + 1] for i, j in _STAIR]
                               + [jnp.full((npad, ts), -jnp.inf, F32)], axis=0)
        eid = jnp.concatenate([i1[i:i + 1] * PEER_KEYS + i2[j:j + 1] for i, j in _STAIR]
                              + [jnp.zeros((npad, ts), jnp.int32)], axis=0)
        top, pos = _topk_rows(cand, PEER_TOPK)
        rows = lax.broadcasted_iota(jnp.int32, cand.shape, 0)
        sel = jnp.concatenate([jnp.sum(jnp.where(rows == pos[r:r + 1], eid, 0), axis=0, keepdims=True)
                               for r in range(PEER_TOPK)], axis=0)
        e = jnp.exp(top - top[0:1])
        gate_scr[h] = e / jnp.sum(e, axis=0, keepdims=True)
        idx_scr[h] = sel
        return 0

    lax.fori_loop(0, PEER_HEADS, head, 0)
    idx_ref[...] = idx_scr[...].reshape(PEER_SEL, ts).T
    gate_ref[...] = gate_scr[...].reshape(PEER_SEL, ts).T


def _post(x, dout, sout, w_out_b, g_ffn, w_query_b, sub_keys_b, tok0, t):
    ts = min(256, t)
    first = tok0 // ts
    row = pl.BlockSpec((ts, D_MODEL), lambda i: (i, 0))
    hm = pl.BlockSpec((1, HEADS, ts, HEAD_DIM), lambda i: (0, 0, i, 0))
    sel = pl.BlockSpec((ts, PEER_SEL), lambda i: (i, 0))
    full = lambda *shape: pl.BlockSpec(shape, lambda i: (0,) * len(shape))
    return pl.pallas_call(
        _post_body,
        grid=(t // ts,),
        in_specs=[pl.BlockSpec((ts, D_MODEL), lambda i: (first + i, 0)),
                  hm, hm, full(2 * MIX, D_MODEL), full(1, D_MODEL),
                  full(D_MODEL, 2 * PEER_HEADS * PEER_HALF), full(2, PEER_KEYS, PEER_HALF)],
        out_specs=[row, row, sel, sel],
        out_shape=[jax.ShapeDtypeStruct((t, D_MODEL), F32), jax.ShapeDtypeStruct((t, D_MODEL), F32),
                   jax.ShapeDtypeStruct((t, PEER_SEL), jnp.int32), jax.ShapeDtypeStruct((t, PEER_SEL), F32)],
        scratch_shapes=[pltpu.VMEM((2 * PEER_HEADS, ts, PEER_HALF), BF16),
                        pltpu.VMEM((PEER_HEADS, PEER_TOPK, ts), jnp.int32),
                        pltpu.VMEM((PEER_HEADS, PEER_TOPK, ts), F32)],
        compiler_params=_params("parallel"),
        name="post_peer_select",
    )(x, dout, sout, w_out_b, g_ffn, w_query_b, sub_keys_b)


def _coef_body(after_ref, gate_ref, dots_ref, o_ref):
    del after_ref
    d = dots_ref[...]
    o_ref[...] = gate_ref[...] * (0.5 * d * (1.0 + lax.erf(d * (2.0 ** -0.5))))


def _coef(gate, dots, after):
    t = gate.shape[0]
    ts = min(2048, t)
    blk = pl.BlockSpec((ts, PEER_SEL), lambda i: (i, 0))
    return pl.pallas_call(
        _coef_body, grid=(t // ts,), in_specs=[pl.BlockSpec(memory_space=pl.ANY), blk, blk], out_specs=blk,
        out_shape=jax.ShapeDtypeStruct((t, PEER_SEL), F32),
        compiler_params=_params("parallel"), name="peer_coef",
    )(after, gate, dots)


SC_CORES = 2
SC_SUBCORES = 16
SC_LANES = 16
SC_WORKERS = SC_CORES * SC_SUBCORES
SC_ROWS = 16
SC_BUFS = 5
SC_GROUP = 8
SC_CHUNKS = PEER_SEL // SC_ROWS
SC_STEPS = SC_GROUP * SC_CHUNKS
SC_VECS = D_MODEL // SC_LANES


def _sc_mesh():
    return plsc.VectorSubcoreMesh(core_axis_name="c", subcore_axis_name="s",
                                  num_cores=SC_CORES, num_subcores=SC_SUBCORES)


def _sc_walk(table_hbm, idx_hbm, aux_hbm, idx_v, aux_v, rows_v, sem, stage_sem, tpw, begin_group, compute, end_group):
    tok_base = (lax.axis_index("s") * SC_CORES + lax.axis_index("c")) * tpw
    ngroups = tpw // SC_GROUP
    nsteps = tpw * SC_CHUNKS

    def first_token(g):
        return pl.multiple_of(tok_base + g * SC_GROUP, SC_GROUP)

    def stage(g):
        tok0 = first_token(g)
        return (pltpu.make_async_copy(idx_hbm.at[pl.ds(tok0 * SC_CHUNKS, SC_STEPS)], idx_v.at[g % 2], stage_sem.at[0]),
                pltpu.make_async_copy(aux_hbm.at[pl.ds(tok0, SC_GROUP)],
                                      aux_v.at[g % 2, :, pl.ds(0, aux_hbm.shape[1])], stage_sem.at[1]))

    def gather(step):
        idx = idx_v.at[(step // SC_STEPS) % 2, step % SC_STEPS]
        return pltpu.make_async_copy(table_hbm.at[idx], rows_v.at[step % SC_BUFS], sem.at[step % SC_BUFS])

    for cp in stage(0):
        cp.start()
    for cp in stage(0):
        cp.wait()
    for step in range(SC_BUFS - 1):
        gather(step).start()

    def walk(step, _):
        g = step // SC_STEPS
        local = step % SC_STEPS

        @pl.when(jnp.logical_and(local == 0, g + 1 < ngroups))
        def _():
            for cp in stage(g + 1):
                cp.start()

        ahead = step + (SC_BUFS - 1)

        @pl.when(ahead < nsteps)
        def _():
            @pl.when(ahead % SC_STEPS == 0)
            def _():
                for cp in stage(ahead // SC_STEPS):
                    cp.wait()

            gather(ahead).start()

        @pl.when(local == 0)
        def _():
            begin_group()

        gather(step).wait()
        compute(rows_v.at[step % SC_BUFS], g % 2, local // SC_CHUNKS, local % SC_CHUNKS)

        @pl.when(local == SC_STEPS - 1)
        def _():
            end_group(first_token(g))

        return 0

    lax.fori_loop(0, nsteps, walk, 0)


def _sc_dots_body(u_hbm, idx_hbm, c_hbm, out_hbm, idx_v, c_v, rows_v, dots_v, sem, stage_sem, *, tpw):
    lane = lax.broadcasted_iota(jnp.int32, (SC_LANES,), 0)
    zero = jnp.zeros((SC_LANES,), F32)

    def compute(rows, slot, tt, ch):
        for half in range(SC_ROWS // SC_LANES):
            r0 = half * SC_LANES

            def vec(kk, accs):
                off = pl.multiple_of(kk * SC_LANES, SC_LANES)
                cv = c_v[slot, tt, pl.ds(off, SC_LANES)]
                return tuple(a + rows[r0 + r, pl.ds(off, SC_LANES)] * cv for r, a in enumerate(accs))

            accs = lax.fori_loop(0, SC_VECS, vec, (zero,) * SC_LANES)
            outv = zero
            for r in range(SC_LANES):
                outv = jnp.where(lane == r, jnp.sum(accs[r]), outv)
            dots_v[tt, pl.ds(pl.multiple_of(ch * SC_ROWS + half * SC_LANES, SC_LANES), SC_LANES)] = outv

    def end_group(tok0):
        pltpu.sync_copy(dots_v.at[:, pl.ds(0, PEER_SEL)], out_hbm.at[pl.ds(tok0, SC_GROUP)])

    _sc_walk(u_hbm, idx_hbm, c_hbm, idx_v, c_v, rows_v, sem, stage_sem, tpw, lambda: None, compute, end_group)


def _sc_combine_body(v_hbm, idx_hbm, coef_hbm, out_hbm, idx_v, coef_v, rows_v, acc_v, sem, stage_sem, *, tpw):
    lane = lax.broadcasted_iota(jnp.int32, (SC_LANES,), 0)
    zero = jnp.zeros((SC_LANES,), F32)

    def compute(rows, slot, tt, ch):
        for half in range(SC_ROWS // SC_LANES):
            cf = coef_v[slot, tt, pl.ds(pl.multiple_of(ch * SC_ROWS + half * SC_LANES, SC_LANES), SC_LANES)]
            splat = [jnp.full((SC_LANES,), jnp.sum(jnp.where(lane == r, cf, 0.0)), F32) for r in range(SC_LANES)]

            @plsc.parallel_loop(0, SC_VECS, unroll=2)
            def _(kk):
                off = pl.multiple_of(kk * SC_LANES, SC_LANES)
                terms = [rows[half * SC_LANES + r, pl.ds(off, SC_LANES)] * splat[r] for r in range(SC_LANES)]
                while len(terms) > 1:
                    terms = [a + b for a, b in zip(terms[0::2], terms[1::2])]
                acc_v[tt, pl.ds(off, SC_LANES)] = acc_v[tt, pl.ds(off, SC_LANES)] + terms[0]

    def begin_group():
        def clear(i, _):
            acc_v[i // SC_VECS, pl.ds(pl.multiple_of((i % SC_VECS) * SC_LANES, SC_LANES), SC_LANES)] = zero
            return 0

        lax.fori_loop(0, SC_GROUP * SC_VECS, clear, 0)

    def end_group(tok0):
        pltpu.sync_copy(acc_v, out_hbm.at[pl.ds(tok0, SC_GROUP)])

    _sc_walk(v_hbm, idx_hbm, coef_hbm, idx_v, coef_v, rows_v, sem, stage_sem, tpw, begin_group, compute, end_group)


def _sc_call(body, table, idx, per_token, out_width, name):
    t = per_token.shape[0]
    tpw = t // SC_WORKERS
    assert tpw % SC_GROUP == 0
    return pl.kernel(
        functools.partial(body, tpw=tpw),
        out_type=jax.ShapeDtypeStruct((t, out_width), F32),
        mesh=_sc_mesh(),
        scratch_types=[pltpu.VMEM((2, SC_STEPS, SC_ROWS), jnp.int32),
                       pltpu.VMEM((2, SC_GROUP, per_token.shape[1]), F32),
                       pltpu.VMEM((SC_BUFS, SC_ROWS, D_MODEL), F32),
                       pltpu.VMEM((SC_GROUP, out_width), F32),
                       pltpu.SemaphoreType.DMA((SC_BUFS,)),
                       pltpu.SemaphoreType.DMA((2,))],
        compiler_params=pltpu.CompilerParams(needs_layout_passes=False),
        name=name,
    )(table, idx.reshape(t * SC_CHUNKS, SC_ROWS), per_token)


def _sc_step_body(v_hbm, idx_a_hbm, coef_hbm, u_hbm, idx_b_hbm, c_hbm, peer_hbm, dots_hbm,
                  idx_v, aux_v, rows_v, out_v, sem, stage_sem, *, tpw_a, tpw_b):
    _sc_combine_body(v_hbm, idx_a_hbm, coef_hbm, peer_hbm, idx_v, aux_v, rows_v, out_v, sem, stage_sem, tpw=tpw_a)
    _sc_dots_body(u_hbm, idx_b_hbm, c_hbm, dots_hbm, idx_v, aux_v, rows_v, out_v, sem, stage_sem, tpw=tpw_b)


def _sc_step(expert_v, idx_a, coef, expert_u, idx_b, c):
    ta, tb = coef.shape[0], c.shape[0]
    assert ta % (SC_WORKERS * SC_GROUP) == 0 and tb % (SC_WORKERS * SC_GROUP) == 0
    return pl.kernel(
        functools.partial(_sc_step_body, tpw_a=ta // SC_WORKERS, tpw_b=tb // SC_WORKERS),
        out_type=[jax.ShapeDtypeStruct((ta, D_MODEL), F32), jax.ShapeDtypeStruct((tb, PEER_SEL), F32)],
        mesh=_sc_mesh(),
        scratch_types=[pltpu.VMEM((2, SC_STEPS, SC_ROWS), jnp.int32),
                       pltpu.VMEM((2, SC_GROUP, D_MODEL), F32),
                       pltpu.VMEM((SC_BUFS, SC_ROWS, D_MODEL), F32),
                       pltpu.VMEM((SC_GROUP, D_MODEL), F32),
                       pltpu.SemaphoreType.DMA((SC_BUFS,)),
                       pltpu.SemaphoreType.DMA((2,))],
        compiler_params=pltpu.CompilerParams(needs_layout_passes=False),
        name="peer_step",
    )(expert_v, idx_a.reshape(ta * SC_CHUNKS, SC_ROWS), coef, expert_u, idx_b.reshape(tb * SC_CHUNKS, SC_ROWS), c)


def _sc_dots(expert_u, idx, c):
    return _sc_call(_sc_dots_body, expert_u, idx, c, PEER_SEL, "peer_dots")


def _sc_combine(expert_v, idx, coef):
    return _sc_call(_sc_combine_body, expert_v, idx, coef, D_MODEL, "peer_combine")


def _ple_body(h1_ref, peer_ref, p_ref, gp_ref, wg_ref, we_ref, gfin_ref, *refs):
    y_ref = refs[-1]
    h = h1_ref[...] + peer_ref[...]
    a = _rms(h, gp_ref[...], NORM_EPS).astype(BF16)
    gate = jax.nn.sigmoid(_mm(a, wg_ref[...]))
    h = h + _mm(p_ref[...].astype(BF16), we_ref[...]) * gate
    y_ref[...] = _rms(h, gfin_ref[...], NORM_EPS)


def _ple(h1, peer, p, g_ple, w_pgate_b, w_ple_b, g_final, tok0, into=()):
    t = h1.shape[0]
    ts = min(512, t)
    first = tok0 // ts
    row = pl.BlockSpec((ts, D_MODEL), lambda i: (i, 0))
    full = lambda *shape: pl.BlockSpec(shape, lambda i: (0,) * len(shape))
    n_in = 7
    return pl.pallas_call(
        _ple_body,
        grid=(t // ts,),
        in_specs=[row, row, pl.BlockSpec((ts, PLE_DIM), lambda i: (first + i, 0)), full(1, D_MODEL),
                  full(D_MODEL, D_MODEL), full(PLE_DIM, D_MODEL), full(1, D_MODEL)]
        + [pl.BlockSpec(memory_space=pl.ANY)] * len(into),
        out_specs=pl.BlockSpec((ts, D_MODEL), lambda i: (first + i, 0)),
        out_shape=jax.ShapeDtypeStruct((p.shape[0], D_MODEL), F32),
        input_output_aliases={n_in + j: j for j in range(len(into))},
        compiler_params=_params("parallel"),
        name="ple_final",
    )(h1, peer, p, g_ple, w_pgate_b, w_ple_b, g_final, *into)


def _rope_tables(pos):
    inv = ROPE_THETA ** (-jnp.arange(0, DIFF_COMP, 2, dtype=F32) / DIFF_COMP)
    ang = pos.astype(F32)[:, None] * inv[None, :]
    cos = jnp.cos(ang)
    sin = jnp.sin(ang)
    reps = LANES // DIFF_COMP
    return (jnp.tile(jnp.concatenate([cos, cos], axis=-1), (1, reps)),
            jnp.tile(jnp.concatenate([-sin, sin], axis=-1), (1, reps)))


def kernel(x_prompt, x_sample, cache_diff_k, cache_diff_v, cache_sb_k, cache_sb_v, p_prompt, p_sample, g_mix, w_in, lambda_q1, lambda_k1, lambda_q2, lambda_k2, g_subln, w_out, g_ffn, w_query, sub_keys, expert_u, expert_v, g_ple, w_pgate, w_ple, g_final):
    assert w_in.shape[0] == 1, "single-layer encoder"
    nb, seq, _ = x_prompt.shape
    db, dq, _ = x_sample.shape
    past = cache_diff_k.shape[2]

    lam = (jnp.exp(jnp.sum(lambda_q1[0].astype(F32) * lambda_k1[0].astype(F32)))
           - jnp.exp(jnp.sum(lambda_q2[0].astype(F32) * lambda_k2[0].astype(F32))) + LAM_INIT).reshape(1)
    w_in_b = w_in[0].astype(BF16)
    w_out_b = w_out[0].astype(BF16)
    w_query_b = w_query[0].astype(BF16)
    sub_keys_b = sub_keys[0].astype(BF16)
    w_pgate_b = w_pgate[0].astype(BF16)
    w_ple_b = w_ple[0].astype(BF16)
    g_sub = g_subln[0].reshape(1, HEAD_DIM)
    g_fin = g_final.reshape(1, D_MODEL)

    def select(x, dout, sout, tok0, t):
        return _post(x, dout, sout, w_out_b, g_ffn, w_query_b, sub_keys_b, tok0, t)

    ts = db * dq
    xs = x_sample.reshape(ts, D_MODEL)
    cos_s, sin_s = _rope_tables(jnp.tile(past + jnp.arange(dq, dtype=jnp.int32), db))
    kd, vd, ks, vs, qd2, kdb, vdb, qsb, ksb, vsb = _proj(xs, g_mix, w_in_b, cos_s, sin_s, 1, ts, False)
    caches = [c[0].reshape(db, past, MIX) for c in (cache_diff_k, cache_diff_v, cache_sb_k, cache_sb_v)]
    dout_s, sout_s = _sample_attention(lam, qd2, kdb, vdb, qsb, ksb, vsb, *caches, g_sub, db, dq)
    rows_s = tuple(r.reshape(1, db, dq, HEADS, HEAD_DIM) for r in (kd, vd, ks, vs))

    xp = x_prompt.reshape(nb * seq, D_MODEL)
    pp = p_prompt[0].reshape(nb * seq, PLE_DIM)
    cos_p, sin_p = _rope_tables(jnp.arange(seq, dtype=jnp.int32))
    kd, vd, ks, vs, qd2, kdb, vdb, qsb, ksb, vsb = _proj(xp, g_mix, w_in_b, cos_p, sin_p, nb, seq, True)
    rows_p = tuple(r.reshape(nb, HEADS, HEAD_DIM, seq).transpose(0, 3, 1, 2)[None] for r in (kd, vd, ks, vs))

    cut = seq // PROMPT_ROW_BLOCKS
    spans = [(b, i * cut, cut) for b in range(nb) for i in range(PROMPT_ROW_BLOCKS)]
    for end in (0, -1):
        b, t0, n = spans.pop(end)
        spans[end if end == 0 else len(spans):0] = [(b, t0, n // 2), (b, t0 + n // 2, n // 2)]

    def prompt_block(b, t0, n):
        dout = _diff_attention(lam, qd2, kdb, vdb, g_sub, b, t0, n)
        sout = _sb_attention(qsb, ksb, vsb, b, t0, n)
        return select(xp, dout, sout, b * seq + t0, n)

    blocks = [(functools.partial(prompt_block, *span), "prompt", span[0] * seq + span[1]) for span in spans]
    blocks.insert(min(PROMPT_ROW_BLOCKS + 1, len(blocks)), (lambda: select(xs, dout_s, sout_s, 0, ts), "sample", 0))
    embeds = {"prompt": pp, "sample": p_sample[0].reshape(ts, PLE_DIM)}
    results = {"prompt": (), "sample": ()}

    def emit(h1, peer, group, tok0):
        results[group] = (_ple(h1, peer, embeds[group], g_ple, w_pgate_b, w_ple_b, g_fin, tok0, results[group]),)

    h1, c, idx, gate = blocks[0][0]()
    dots = _sc_dots(expert_u[0], idx, c)
    for k in range(1, len(blocks)):
        nh1, nc, nidx, ngate = blocks[k][0]()
        coef = _coef(gate, dots, nh1)
        peer, ndots = _sc_step(expert_v[0], idx, coef, expert_u[0], nidx, nc)
        emit(h1, peer, *blocks[k - 1][1:])
        h1, idx, gate, dots = nh1, nidx, ngate, ndots
    coef = _coef(gate, dots, lam)
    emit(h1, _sc_combine(expert_v[0], idx, coef), *blocks[-1][1:])
    y_sample = results["sample"][0].reshape(db, dq, D_MODEL)
    y_prompt = results["prompt"][0].reshape(nb, seq, D_MODEL)

    return (y_prompt, y_sample) + rows_p + rows_s
```

```python
import functools
import math

import jax
import jax.numpy as jnp
from jax import lax
from jax.experimental import pallas as pl
from jax.experimental.pallas import tpu as pltpu
from jax.experimental.pallas import tpu_sc as plsc

F32 = jnp.float32
BF16 = jnp.bfloat16

D_MODEL = 1024
HEADS = 8
HEAD_DIM = 64
DIFF_COMP = 32
MIX = HEADS * HEAD_DIM
CHUNK = 64
ROPE_THETA = 10000.0
NORM_EPS = 1e-6
SUBLN_EPS = 1e-5
PEER_HEADS = 8
PEER_KEYS = 128
PEER_TOPK = 16
PEER_HALF = 128
PEER_SEL = PEER_HEADS * PEER_TOPK
PLE_DIM = 256
LAM_INIT = 0.8 - 0.6 * math.exp(-0.3 * 0)
SB_LOG_FLOOR = -104.0
PROMPT_ROW_BLOCKS = 4

LANES = 128
VMEM_LIMIT = 48 * 1024 * 1024

NT_DIMS = (((1,), (1,)), ((), ()))


def _nt(a, b):
    return lax.dot_general(a, b, NT_DIMS, preferred_element_type=F32)


def _mm(a, b):
    return jnp.dot(a, b, preferred_element_type=F32)


def _rms(x, g, eps):
    return x * lax.rsqrt(jnp.mean(x * x, axis=-1, keepdims=True) + eps) * g


def _params(*sem):
    return pltpu.CompilerParams(dimension_semantics=sem, vmem_limit_bytes=VMEM_LIMIT)


def _with_ones(v):
    n = v.shape[0]
    ones = (lax.broadcasted_iota(jnp.int32, (n, HEAD_DIM), 1) == 0).astype(v.dtype)
    return jnp.concatenate([v, ones], axis=1)


def _proj_body(x_ref, g_ref, w_ref, cos_ref, sin_ref,
               kd_ref, vd_ref, ks_ref, vs_ref,
               qd2_ref, kdb_ref, vdb_ref, qsb_ref, ksb_ref, vsb_ref, *, feature_major):
    ts = x_ref.shape[0]
    a = _rms(x_ref[...], g_ref[...], NORM_EPS).astype(BF16)
    cos = jnp.tile(cos_ref[...], (1, MIX // LANES))
    sin = jnp.tile(sin_ref[...], (1, MIX // LANES))
    lane = lax.broadcasted_iota(jnp.int32, (ts, MIX), 1)
    first_half = (lane % DIFF_COMP) < (DIFF_COMP // 2)

    def group(i):
        return _mm(a, w_ref[:, i * MIX:(i + 1) * MIX])

    def rope(t):
        partner = jnp.where(first_half,
                            pltpu.roll(t, MIX - DIFF_COMP // 2, 1),
                            pltpu.roll(t, DIFF_COMP // 2, 1))
        return t * cos + partner * sin

    qd = rope(group(0)) * (DIFF_COMP ** -0.5)
    kd = rope(group(1))
    vd = group(2)
    qs = group(3) * (HEAD_DIM ** -0.5)
    ks = group(4)
    vs = group(5)
    for ref, rows in ((kd_ref, kd), (vd_ref, vd), (ks_ref, ks), (vs_ref, vs)):
        if feature_major:
            ref[0] = rows.T
        else:
            ref[...] = rows
    comp0 = lax.broadcasted_iota(jnp.int32, (ts, HEAD_DIM), 1) < DIFF_COMP
    for h in range(HEADS):
        sl = slice(h * HEAD_DIM, (h + 1) * HEAD_DIM)
        qh = qd[:, sl]
        qd2_ref[0, h, 0] = jnp.where(comp0, qh, 0.0).astype(BF16)
        qd2_ref[0, h, 1] = jnp.where(comp0, 0.0, qh).astype(BF16)
        kdb_ref[0, h] = kd[:, sl].astype(BF16)
        vdb_ref[0, h] = _with_ones(vd[:, sl].astype(BF16))
        qsb_ref[0, h] = qs[:, sl].astype(BF16)
        ksb_ref[0, h] = ks[:, sl].astype(BF16)
        vsb_ref[0, h] = vs[:, sl].astype(BF16)


def _proj(x, g_mix, w_in_b, cos_t, sin_t, nb, seq, feature_major):
    t = nb * seq
    ts = min(256, seq)
    nst = seq // ts
    if feature_major:
        row = pl.BlockSpec((1, MIX, ts), lambda i: (i // nst, 0, i % nst))
        rows = jax.ShapeDtypeStruct((nb, MIX, seq), F32)
    else:
        row = pl.BlockSpec((ts, MIX), lambda i: (i, 0))
        rows = jax.ShapeDtypeStruct((t, MIX), F32)
    hm = pl.BlockSpec((1, HEADS, ts, HEAD_DIM), lambda i: (i // nst, 0, i % nst, 0))
    hm2 = pl.BlockSpec((1, HEADS, 2, ts, HEAD_DIM), lambda i: (i // nst, 0, 0, i % nst, 0))
    heads = jax.ShapeDtypeStruct((nb, HEADS, seq, HEAD_DIM), BF16)
    heads2 = jax.ShapeDtypeStruct((nb, HEADS, 2, seq, HEAD_DIM), BF16)
    hm_ext = pl.BlockSpec((1, HEADS, ts, 2 * HEAD_DIM), lambda i: (i // nst, 0, i % nst, 0))
    heads_ext = jax.ShapeDtypeStruct((nb, HEADS, seq, 2 * HEAD_DIM), BF16)
    return pl.pallas_call(
        functools.partial(_proj_body, feature_major=feature_major),
        grid=(t // ts,),
        in_specs=[
            pl.BlockSpec((ts, D_MODEL), lambda i: (i, 0)),
            pl.BlockSpec((1, D_MODEL), lambda i: (0, 0)),
            pl.BlockSpec((D_MODEL, 6 * MIX), lambda i: (0, 0)),
            pl.BlockSpec((ts, LANES), lambda i: (i % nst, 0)),
            pl.BlockSpec((ts, LANES), lambda i: (i % nst, 0)),
        ],
        out_specs=[row, row, row, row, hm2, hm, hm_ext, hm, hm, hm],
        out_shape=[rows, rows, rows, rows, heads2, heads, heads_ext, heads, heads, heads],
        compiler_params=_params("parallel"),
        name="proj",
    )(x, g_mix, w_in_b, cos_t, sin_t)


def _diff_init(rows):
    return jnp.full((rows, 1), -jnp.inf, F32), jnp.zeros((rows, 2 * HEAD_DIM), F32)


def _diff_update(s, v_ext, carry):
    m, acc = carry
    m_new = jnp.maximum(m, jnp.max(s, axis=-1, keepdims=True))
    p = jnp.exp(s - m_new)
    acc = jnp.exp(m - m_new) * acc + _mm(p.astype(BF16), v_ext)
    return m_new, acc


def _diff_finish(carry, lam, g_subln, tq):
    _, acc = carry
    o = acc[:, :HEAD_DIM] / acc[:, HEAD_DIM:HEAD_DIM + 1]
    d = o[:tq] - lam * o[tq:]
    return _rms(d, g_subln, SUBLN_EPS) * (1.0 - LAM_INIT)


def _suffix_sums(lk, tri):
    hi = lk.astype(BF16)
    lo = (lk - hi.astype(F32)).astype(BF16)
    return _mm(hi, tri) + _mm(lo, tri)


def _sb_update(q, k, v, tri, earlier, carry):
    run, acc = carry
    z = _nt(q, k)
    sp = jnp.maximum(z, 0.0) + jnp.log1p(jnp.exp(-jnp.abs(z)))
    lk = -sp if earlier is None else jnp.where(earlier, -sp, 0.0)
    after = _suffix_sums(lk, tri)
    w = jnp.exp((z - sp) + after + run)
    if earlier is not None:
        w = jnp.where(earlier, w, 0.0)
    acc = acc + _mm(w.astype(BF16), v)
    run = run + after[:, 0:1] + lk[:, 0:1]
    return run, acc


def _tri(n):
    j = lax.broadcasted_iota(jnp.int32, (n, n), 0)
    s = lax.broadcasted_iota(jnp.int32, (n, n), 1)
    return (j > s).astype(BF16)


def _diff_body(lam_ref, q_ref, k_ref, v_ref, g_ref, o_ref, *, tq, q0):
    qi = q0 + pl.program_id(1)
    q2 = q_ref[0, 0].reshape(2 * tq, HEAD_DIM)

    def scores(j):
        return _nt(q2, k_ref[0, 0, pl.ds(pl.multiple_of(j * tq, tq), tq), :])

    def step(j, state):
        s, carry = state
        s_next = scores(j + 1)
        return s_next, _diff_update(s, v_ref[0, 0, pl.ds(pl.multiple_of(j * tq, tq), tq), :], carry)

    s, carry = lax.fori_loop(0, qi, step, (scores(0), _diff_init(2 * tq)))
    r = lax.broadcasted_iota(jnp.int32, (2 * tq, tq), 0) % tq
    c = lax.broadcasted_iota(jnp.int32, (2 * tq, tq), 1)
    s = jnp.where((c // CHUNK) <= (r // CHUNK), s, -jnp.inf)
    carry = _diff_update(s, v_ref[0, 0, pl.ds(pl.multiple_of(qi * tq, tq), tq), :], carry)
    o_ref[0, 0] = _diff_finish(carry, lam_ref[0], g_ref[...], tq)


def _diff_attention(lam, qd2, kdb, vdb, g_subln, b, t0, n):
    seq = qd2.shape[3]
    tq = min(512, n)
    q0 = t0 // tq
    return pl.pallas_call(
        functools.partial(_diff_body, tq=tq, q0=q0),
        grid=(HEADS, n // tq),
        in_specs=[
            pl.BlockSpec(memory_space=pltpu.SMEM),
            pl.BlockSpec((1, 1, 2, tq, HEAD_DIM), lambda h, i: (b, h, 0, q0 + i, 0)),
            pl.BlockSpec((1, 1, seq, HEAD_DIM), lambda h, i: (b, h, 0, 0)),
            pl.BlockSpec((1, 1, seq, 2 * HEAD_DIM), lambda h, i: (b, h, 0, 0)),
            pl.BlockSpec((1, HEAD_DIM), lambda h, i: (0, 0)),
        ],
        out_specs=pl.BlockSpec((1, 1, tq, HEAD_DIM), lambda h, i: (0, h, i, 0)),
        out_shape=jax.ShapeDtypeStruct((1, HEADS, n, HEAD_DIM), F32),
        compiler_params=_params("parallel", "arbitrary"),
        name="diff_attention",
    )(lam, qd2, kdb, vdb, g_subln)


def _sb_body(q_ref, k_ref, v_ref, o_ref, *, tq, q0):
    qi = q0 + pl.program_id(1)
    q = q_ref[0, 0]
    tri = _tri(tq)

    def tile(j):
        start = pl.multiple_of(j * tq, tq)
        return k_ref[0, 0, pl.ds(start, tq), :], v_ref[0, 0, pl.ds(start, tq), :]

    r = lax.broadcasted_iota(jnp.int32, (tq, tq), 0)
    c = lax.broadcasted_iota(jnp.int32, (tq, tq), 1)
    carry = (jnp.zeros((tq, 1), F32), jnp.zeros((tq, HEAD_DIM), F32))
    run, acc = _sb_update(q, *tile(qi), tri, c < r, carry)

    def live(state):
        j, run, _ = state
        return jnp.logical_and(j >= 0, jnp.max(run) > SB_LOG_FLOOR)

    def step(state):
        j, run, acc = state
        run, acc = _sb_update(q, *tile(j), tri, None, (run, acc))
        return j - 1, run, acc

    o_ref[0, 0] = lax.while_loop(live, step, (qi - 1, run, acc))[2]


def _sb_attention(qsb, ksb, vsb, b, t0, n):
    seq = qsb.shape[2]
    tq = min(256, n)
    q0 = t0 // tq
    kv = pl.BlockSpec((1, 1, seq, HEAD_DIM), lambda h, i: (b, h, 0, 0))
    return pl.pallas_call(
        functools.partial(_sb_body, tq=tq, q0=q0),
        grid=(HEADS, n // tq),
        in_specs=[pl.BlockSpec((1, 1, tq, HEAD_DIM), lambda h, i: (b, h, q0 + i, 0)), kv, kv],
        out_specs=pl.BlockSpec((1, 1, tq, HEAD_DIM), lambda h, i: (0, h, i, 0)),
        out_shape=jax.ShapeDtypeStruct((1, HEADS, n, HEAD_DIM), F32),
        compiler_params=_params("parallel", "arbitrary"),
        name="sb_attention",
    )(qsb, ksb, vsb)


SAMPLE_HEADS = 4
SAMPLE_TILE = 256


def _sample_body(lam_ref, qd2_ref, kdn_ref, vdn_ref, qs_ref, ksn_ref, vsn_ref,
                 ckd_ref, cvd_ref, cks_ref, cvs_ref, g_ref, do_ref, so_ref, *, past, nq):
    lam = lam_ref[0]
    tri_c = _tri(SAMPLE_TILE)
    tri_n = _tri(nq)
    i2 = lax.broadcasted_iota(jnp.int32, (2 * nq, nq), 0) % nq
    j2 = lax.broadcasted_iota(jnp.int32, (2 * nq, nq), 1)
    visible_new = ((past + j2) // CHUNK) <= ((past + i2) // CHUNK)
    i1 = lax.broadcasted_iota(jnp.int32, (nq, nq), 0)
    j1 = lax.broadcasted_iota(jnp.int32, (nq, nq), 1)
    earlier_new = j1 < i1
    for h in range(SAMPLE_HEADS):
        sl = slice(h * HEAD_DIM, (h + 1) * HEAD_DIM)
        q2 = qd2_ref[0, h].reshape(2 * nq, HEAD_DIM)
        carry = _diff_update(_nt(q2, ckd_ref[0, :, sl].astype(BF16)),
                             _with_ones(cvd_ref[0, :, sl].astype(BF16)), _diff_init(2 * nq))
        s_new = jnp.where(visible_new, _nt(q2, kdn_ref[0, h]), -jnp.inf)
        carry = _diff_update(s_new, vdn_ref[0, h], carry)
        do_ref[0, h] = _diff_finish(carry, lam, g_ref[...], nq)
        q = qs_ref[0, h]
        carry = (jnp.zeros((nq, 1), F32), jnp.zeros((nq, HEAD_DIM), F32))
        carry = _sb_update(q, ksn_ref[0, h], vsn_ref[0, h], tri_n, earlier_new, carry)
        for t in reversed(range(past // SAMPLE_TILE)):
            rows = slice(t * SAMPLE_TILE, (t + 1) * SAMPLE_TILE)
            carry = _sb_update(q, cks_ref[0, rows, sl].astype(BF16), cvs_ref[0, rows, sl].astype(BF16),
                               tri_c, None, carry)
        so_ref[0, h] = carry[1]


def _sample_attention(lam, qd2, kdb, vdb, qsb, ksb, vsb, ckd, cvd, cks, cvs, g_subln, nb, nq):
    past = ckd.shape[1]
    nhg = HEADS // SAMPLE_HEADS
    hm = pl.BlockSpec((1, SAMPLE_HEADS, nq, HEAD_DIM), lambda b, g: (0, g, b, 0))
    hm2 = pl.BlockSpec((1, SAMPLE_HEADS, 2, nq, HEAD_DIM), lambda b, g: (0, g, 0, b, 0))
    cache = pl.BlockSpec((1, past, SAMPLE_HEADS * HEAD_DIM), lambda b, g: (b, 0, g))
    out = jax.ShapeDtypeStruct((1, HEADS, nb * nq, HEAD_DIM), F32)
    return pl.pallas_call(
        functools.partial(_sample_body, past=past, nq=nq),
        grid=(nb, nhg),
        in_specs=[pl.BlockSpec(memory_space=pltpu.SMEM), hm2, hm,
                  pl.BlockSpec((1, SAMPLE_HEADS, nq, 2 * HEAD_DIM), lambda b, g: (0, g, b, 0)), hm, hm, hm,
                  cache, cache, cache, cache,
                  pl.BlockSpec((1, HEAD_DIM), lambda b, g: (0, 0))],
        out_specs=[hm, hm],
        out_shape=[out, out],
        compiler_params=_params("parallel", "parallel"),
        name="sample_attention",
    )(lam, qd2, kdb, vdb, qsb, ksb, vsb, ckd, cvd, cks, cvs, g_subln)


def _topk_rows(s, k):
    n = s.shape[0]
    rows = lax.broadcasted_iota(jnp.int32, s.shape, 0)
    vals, ids = [], []
    for _ in range(k):
        m = jnp.max(s, axis=0, keepdims=True)
        i = jnp.min(jnp.where(s == m, rows, n), axis=0, keepdims=True)
        vals.append(m)
        ids.append(i)
        s = jnp.where(rows == i, -jnp.inf, s)
    return jnp.concatenate(vals, axis=0), jnp.concatenate(ids, axis=0)


_STAIR = [(i, j) for i in range(PEER_TOPK) for j in range(PEER_TOPK) if (i + 1) * (j + 1) <= PEER_TOPK]


def _post_body(x_ref, do_ref, so_ref, wo_ref, gf_ref, wq_ref, sk_ref,
               h1_ref, c_ref, idx_ref, gate_ref, q_scr, idx_scr, gate_scr):
    ts = x_ref.shape[0]
    mixed = jnp.zeros((ts, D_MODEL), F32)
    for h in range(HEADS):
        mixed += _mm(do_ref[0, h].astype(BF16), wo_ref[h * HEAD_DIM:(h + 1) * HEAD_DIM, :])
        mixed += _mm(so_ref[0, h].astype(BF16), wo_ref[MIX + h * HEAD_DIM:MIX + (h + 1) * HEAD_DIM, :])
    h1 = x_ref[...] + mixed
    h1_ref[...] = h1
    c = _rms(h1, gf_ref[...], NORM_EPS)
    c_ref[...] = c
    q = _mm(c.astype(BF16), wq_ref[...])
    for hp in range(2 * PEER_HEADS):
        q_scr[hp] = q[:, hp * PEER_HALF:(hp + 1) * PEER_HALF].astype(BF16)

    npad = -len(_STAIR) % 8

    def head(h, _):
        v1, i1 = _topk_rows(_nt(sk_ref[0], q_scr[2 * h]), PEER_TOPK)
        v2, i2 = _topk_rows(_nt(sk_ref[1], q_scr[2 * h + 1]), PEER_TOPK)
        cand = jnp.concatenate([v1[i:i + 1] + v2[j:j + 1] for i, j in _STAIR]
                               + [jnp.full((npad, ts), -jnp.inf, F32)], axis=0)
        eid = jnp.concatenate([i1[i:i + 1] * PEER_KEYS + i2[j:j + 1] for i, j in _STAIR]
                              + [jnp.zeros((npad, ts), jnp.int32)], axis=0)
        top, pos = _topk_rows(cand, PEER_TOPK)
        rows = lax.broadcasted_iota(jnp.int32, cand.shape, 0)
        sel = jnp.concatenate([jnp.sum(jnp.where(rows == pos[r:r + 1], eid, 0), axis=0, keepdims=True)
                               for r in range(PEER_TOPK)], axis=0)
        e = jnp.exp(top - top[0:1])
        gate_scr[h] = e / jnp.sum(e, axis=0, keepdims=True)
        idx_scr[h] = sel
        return 0

    lax.fori_loop(0, PEER_HEADS, head, 0)
    idx_ref[...] = idx_scr[...].reshape(PEER_SEL, ts).T
    gate_ref[...] = gate_scr[...].reshape(PEER_SEL, ts).T


def _post(x, dout, sout, w_out_b, g_ffn, w_query_b, sub_keys_b, tok0, t):
    ts = min(256, t)
    first = tok0 // ts
    row = pl.BlockSpec((ts, D_MODEL), lambda i: (i, 0))
    hm = pl.BlockSpec((1, HEADS, ts, HEAD_DIM), lambda i: (0, 0, i, 0))
    sel = pl.BlockSpec((ts, PEER_SEL), lambda i: (i, 0))
    full = lambda *shape: pl.BlockSpec(shape, lambda i: (0,) * len(shape))
    return pl.pallas_call(
        _post_body,
        grid=(t // ts,),
        in_specs=[pl.BlockSpec((ts, D_MODEL), lambda i: (first + i, 0)),
                  hm, hm, full(2 * MIX, D_MODEL), full(1, D_MODEL),
                  full(D_MODEL, 2 * PEER_HEADS * PEER_HALF), full(2, PEER_KEYS, PEER_HALF)],
        out_specs=[row, row, sel, sel],
        out_shape=[jax.ShapeDtypeStruct((t, D_MODEL), F32), jax.ShapeDtypeStruct((t, D_MODEL), F32),
                   jax.ShapeDtypeStruct((t, PEER_SEL), jnp.int32), jax.ShapeDtypeStruct((t, PEER_SEL), F32)],
        scratch_shapes=[pltpu.VMEM((2 * PEER_HEADS, ts, PEER_HALF), BF16),
                        pltpu.VMEM((PEER_HEADS, PEER_TOPK, ts), jnp.int32),
                        pltpu.VMEM((PEER_HEADS, PEER_TOPK, ts), F32)],
        compiler_params=_params("parallel"),
        name="post_peer_select",
    )(x, dout, sout, w_out_b, g_ffn, w_query_b, sub_keys_b)


def _coef_body(after_ref, gate_ref, dots_ref, o_ref):
    del after_ref
    d = dots_ref[...]
    o_ref[...] = gate_ref[...] * (0.5 * d * (1.0 + lax.erf(d * (2.0 ** -0.5))))


def _coef(gate, dots, after):
    t = gate.shape[0]
    ts = min(2048, t)
    blk = pl.BlockSpec((ts, PEER_SEL), lambda i: (i, 0))
    return pl.pallas_call(
        _coef_body, grid=(t // ts,), in_specs=[pl.BlockSpec(memory_space=pl.ANY), blk, blk], out_specs=blk,
        out_shape=jax.ShapeDtypeStruct((t, PEER_SEL), F32),
        compiler_params=_params("parallel"), name="peer_coef",
    )(after, gate, dots)


SC_CORES = 2
SC_SUBCORES = 16
SC_LANES = 16
SC_WORKERS = SC_CORES * SC_SUBCORES
SC_ROWS = 16
SC_BUFS = 5
SC_GROUP = 8
SC_CHUNKS = PEER_SEL // SC_ROWS
SC_STEPS = SC_GROUP * SC_CHUNKS
SC_VECS = D_MODEL // SC_LANES


def _sc_mesh():
    return plsc.VectorSubcoreMesh(core_axis_name="c", subcore_axis_name="s",
                                  num_cores=SC_CORES, num_subcores=SC_SUBCORES)


def _sc_walk(table_hbm, idx_hbm, aux_hbm, idx_v, aux_v, rows_v, sem, stage_sem, tpw, begin_group, compute, end_group):
    tok_base = (lax.axis_index("s") * SC_CORES + lax.axis_index("c")) * tpw
    ngroups = tpw // SC_GROUP
    nsteps = tpw * SC_CHUNKS

    def first_token(g):
        return pl.multiple_of(tok_base + g * SC_GROUP, SC_GROUP)

    def stage(g):
        tok0 = first_token(g)
        return (pltpu.make_async_copy(idx_hbm.at[pl.ds(tok0 * SC_CHUNKS, SC_STEPS)], idx_v.at[g % 2], stage_sem.at[0]),
                pltpu.make_async_copy(aux_hbm.at[pl.ds(tok0, SC_GROUP)],
                                      aux_v.at[g % 2, :, pl.ds(0, aux_hbm.shape[1])], stage_sem.at[1]))

    def gather(step):
        idx = idx_v.at[(step // SC_STEPS) % 2, step % SC_STEPS]
        return pltpu.make_async_copy(table_hbm.at[idx], rows_v.at[step % SC_BUFS], sem.at[step % SC_BUFS])

    for cp in stage(0):
        cp.start()
    for cp in stage(0):
        cp.wait()
    for step in range(SC_BUFS - 1):
        gather(step).start()

    def walk(step, _):
        g = step // SC_STEPS
        local = step % SC_STEPS

        @pl.when(jnp.logical_and(local == 0, g + 1 < ngroups))
        def _():
            for cp in stage(g + 1):
                cp.start()

        ahead = step + (SC_BUFS - 1)

        @pl.when(ahead < nsteps)
        def _():
            @pl.when(ahead % SC_STEPS == 0)
            def _():
                for cp in stage(ahead // SC_STEPS):
                    cp.wait()

            gather(ahead).start()

        @pl.when(local == 0)
        def _():
            begin_group()

        gather(step).wait()
        compute(rows_v.at[step % SC_BUFS], g % 2, local // SC_CHUNKS, local % SC_CHUNKS)

        @pl.when(local == SC_STEPS - 1)
        def _():
            end_group(first_token(g))

        return 0

    lax.fori_loop(0, nsteps, walk, 0)


def _sc_dots_body(u_hbm, idx_hbm, c_hbm, out_hbm, idx_v, c_v, rows_v, dots_v, sem, stage_sem, *, tpw):
    lane = lax.broadcasted_iota(jnp.int32, (SC_LANES,), 0)
    zero = jnp.zeros((SC_LANES,), F32)

    def compute(rows, slot, tt, ch):
        for half in range(SC_ROWS // SC_LANES):
            r0 = half * SC_LANES

            def vec(kk, accs):
                off = pl.multiple_of(kk * SC_LANES, SC_LANES)
                cv = c_v[slot, tt, pl.ds(off, SC_LANES)]
                return tuple(a + rows[r0 + r, pl.ds(off, SC_LANES)] * cv for r, a in enumerate(accs))

            accs = lax.fori_loop(0, SC_VECS, vec, (zero,) * SC_LANES)
            outv = zero
            for r in range(SC_LANES):
                outv = jnp.where(lane == r, jnp.sum(accs[r]), outv)
            dots_v[tt, pl.ds(pl.multiple_of(ch * SC_ROWS + half * SC_LANES, SC_LANES), SC_LANES)] = outv

    def end_group(tok0):
        pltpu.sync_copy(dots_v.at[:, pl.ds(0, PEER_SEL)], out_hbm.at[pl.ds(tok0, SC_GROUP)])

    _sc_walk(u_hbm, idx_hbm, c_hbm, idx_v, c_v, rows_v, sem, stage_sem, tpw, lambda: None, compute, end_group)


def _sc_combine_body(v_hbm, idx_hbm, coef_hbm, out_hbm, idx_v, coef_v, rows_v, acc_v, sem, stage_sem, *, tpw):
    lane = lax.broadcasted_iota(jnp.int32, (SC_LANES,), 0)
    zero = jnp.zeros((SC_LANES,), F32)

    def compute(rows, slot, tt, ch):
        for half in range(SC_ROWS // SC_LANES):
            cf = coef_v[slot, tt, pl.ds(pl.multiple_of(ch * SC_ROWS + half * SC_LANES, SC_LANES), SC_LANES)]
            splat = [jnp.full((SC_LANES,), jnp.sum(jnp.where(lane == r, cf, 0.0)), F32) for r in range(SC_LANES)]

            @plsc.parallel_loop(0, SC_VECS, unroll=2)
            def _(kk):
                off = pl.multiple_of(kk * SC_LANES, SC_LANES)
                terms = [rows[half * SC_LANES + r, pl.ds(off, SC_LANES)] * splat[r] for r in range(SC_LANES)]
                while len(terms) > 1:
                    terms = [a + b for a, b in zip(terms[0::2], terms[1::2])]
                acc_v[tt, pl.ds(off, SC_LANES)] = acc_v[tt, pl.ds(off, SC_LANES)] + terms[0]

    def begin_group():
        def clear(i, _):
            acc_v[i // SC_VECS, pl.ds(pl.multiple_of((i % SC_VECS) * SC_LANES, SC_LANES), SC_LANES)] = zero
            return 0

        lax.fori_loop(0, SC_GROUP * SC_VECS, clear, 0)

    def end_group(tok0):
        pltpu.sync_copy(acc_v, out_hbm.at[pl.ds(tok0, SC_GROUP)])

    _sc_walk(v_hbm, idx_hbm, coef_hbm, idx_v, coef_v, rows_v, sem, stage_sem, tpw, begin_group, compute, end_group)


def _sc_call(body, table, idx, per_token, out_width, name):
    t = per_token.shape[0]
    tpw = t // SC_WORKERS
    assert tpw % SC_GROUP == 0
    return pl.kernel(
        functools.partial(body, tpw=tpw),
        out_type=jax.ShapeDtypeStruct((t, out_width), F32),
        mesh=_sc_mesh(),
        scratch_types=[pltpu.VMEM((2, SC_STEPS, SC_ROWS), jnp.int32),
                       pltpu.VMEM((2, SC_GROUP, per_token.shape[1]), F32),
                       pltpu.VMEM((SC_BUFS, SC_ROWS, D_MODEL), F32),
                       pltpu.VMEM((SC_GROUP, out_width), F32),
                       pltpu.SemaphoreType.DMA((SC_BUFS,)),
                       pltpu.SemaphoreType.DMA((2,))],
        compiler_params=pltpu.CompilerParams(needs_layout_passes=False),
        name=name,
    )(table, idx.reshape(t * SC_CHUNKS, SC_ROWS), per_token)


def _sc_step_body(v_hbm, idx_a_hbm, coef_hbm, u_hbm, idx_b_hbm, c_hbm, peer_hbm, dots_hbm,
                  idx_v, aux_v, rows_v, out_v, sem, stage_sem, *, tpw_a, tpw_b):
    _sc_combine_body(v_hbm, idx_a_hbm, coef_hbm, peer_hbm, idx_v, aux_v, rows_v, out_v, sem, stage_sem, tpw=tpw_a)
    _sc_dots_body(u_hbm, idx_b_hbm, c_hbm, dots_hbm, idx_v, aux_v, rows_v, out_v, sem, stage_sem, tpw=tpw_b)


def _sc_step(expert_v, idx_a, coef, expert_u, idx_b, c):
    ta, tb = coef.shape[0], c.shape[0]
    assert ta % (SC_WORKERS * SC_GROUP) == 0 and tb % (SC_WORKERS * SC_GROUP) == 0
    return pl.kernel(
        functools.partial(_sc_step_body, tpw_a=ta // SC_WORKERS, tpw_b=tb // SC_WORKERS),
        out_type=[jax.ShapeDtypeStruct((ta, D_MODEL), F32), jax.ShapeDtypeStruct((tb, PEER_SEL), F32)],
        mesh=_sc_mesh(),
        scratch_types=[pltpu.VMEM((2, SC_STEPS, SC_ROWS), jnp.int32),
                       pltpu.VMEM((2, SC_GROUP, D_MODEL), F32),
                       pltpu.VMEM((SC_BUFS, SC_ROWS, D_MODEL), F32),
                       pltpu.VMEM((SC_GROUP, D_MODEL), F32),
                       pltpu.SemaphoreType.DMA((SC_BUFS,)),
                       pltpu.SemaphoreType.DMA((2,))],
        compiler_params=pltpu.CompilerParams(needs_layout_passes=False),
        name="peer_step",
    )(expert_v, idx_a.reshape(ta * SC_CHUNKS, SC_ROWS), coef, expert_u, idx_b.reshape(tb * SC_CHUNKS, SC_ROWS), c)


def _sc_dots(expert_u, idx, c):
    return _sc_call(_sc_dots_body, expert_u, idx, c, PEER_SEL, "peer_dots")


def _sc_combine(expert_v, idx, coef):
    return _sc_call(_sc_combine_body, expert_v, idx, coef, D_MODEL, "peer_combine")


def _ple_body(h1_ref, peer_ref, p_ref, gp_ref, wg_ref, we_ref, gfin_ref, *refs):
    y_ref = refs[-1]
    h = h1_ref[...] + peer_ref[...]
    a = _rms(h, gp_ref[...], NORM_EPS).astype(BF16)
    gate = jax.nn.sigmoid(_mm(a, wg_ref[...]))
    h = h + _mm(p_ref[...].astype(BF16), we_ref[...]) * gate
    y_ref[...] = _rms(h, gfin_ref[...], NORM_EPS)


def _ple(h1, peer, p, g_ple, w_pgate_b, w_ple_b, g_final, tok0, into=()):
    t = h1.shape[0]
    ts = min(512, t)
    first = tok0 // ts
    row = pl.BlockSpec((ts, D_MODEL), lambda i: (i, 0))
    full = lambda *shape: pl.BlockSpec(shape, lambda i: (0,) * len(shape))
    n_in = 7
    return pl.pallas_call(
        _ple_body,
        grid=(t // ts,),
        in_specs=[row, row, pl.BlockSpec((ts, PLE_DIM), lambda i: (first + i, 0)), full(1, D_MODEL),
                  full(D_MODEL, D_MODEL), full(PLE_DIM, D_MODEL), full(1, D_MODEL)]
        + [pl.BlockSpec(memory_space=pl.ANY)] * len(into),
        out_specs=pl.BlockSpec((ts, D_MODEL), lambda i: (first + i, 0)),
        out_shape=jax.ShapeDtypeStruct((p.shape[0], D_MODEL), F32),
        input_output_aliases={n_in + j: j for j in range(len(into))},
        compiler_params=_params("parallel"),
        name="ple_final",
    )(h1, peer, p, g_ple, w_pgate_b, w_ple_b, g_final, *into)


def _rope_tables(pos):
    inv = ROPE_THETA ** (-jnp.arange(0, DIFF_COMP, 2, dtype=F32) / DIFF_COMP)
    ang = pos.astype(F32)[:, None] * inv[None, :]
    cos = jnp.cos(ang)
    sin = jnp.sin(ang)
    reps = LANES // DIFF_COMP
    return (jnp.tile(jnp.concatenate([cos, cos], axis=-1), (1, reps)),
            jnp.tile(jnp.concatenate([-sin, sin], axis=-1), (1, reps)))


def kernel(x_prompt, x_sample, cache_diff_k, cache_diff_v, cache_sb_k, cache_sb_v, p_prompt, p_sample, g_mix, w_in, lambda_q1, lambda_k1, lambda_q2, lambda_k2, g_subln, w_out, g_ffn, w_query, sub_keys, expert_u, expert_v, g_ple, w_pgate, w_ple, g_final):
    assert w_in.shape[0] == 1, "single-layer encoder"
    nb, seq, _ = x_prompt.shape
    db, dq, _ = x_sample.shape
    past = cache_diff_k.shape[2]

    lam = (jnp.exp(jnp.sum(lambda_q1[0].astype(F32) * lambda_k1[0].astype(F32)))
           - jnp.exp(jnp.sum(lambda_q2[0].astype(F32) * lambda_k2[0].astype(F32))) + LAM_INIT).reshape(1)
    w_in_b = w_in[0].astype(BF16)
    w_out_b = w_out[0].astype(BF16)
    w_query_b = w_query[0].astype(BF16)
    sub_keys_b = sub_keys[0].astype(BF16)
    w_pgate_b = w_pgate[0].astype(BF16)
    w_ple_b = w_ple[0].astype(BF16)
    g_sub = g_subln[0].reshape(1, HEAD_DIM)
    g_fin = g_final.reshape(1, D_MODEL)

    def select(x, dout, sout, tok0, t):
        return _post(x, dout, sout, w_out_b, g_ffn, w_query_b, sub_keys_b, tok0, t)

    ts = db * dq
    xs = x_sample.reshape(ts, D_MODEL)
    cos_s, sin_s = _rope_tables(jnp.tile(past + jnp.arange(dq, dtype=jnp.int32), db))
    kd, vd, ks, vs, qd2, kdb, vdb, qsb, ksb, vsb = _proj(xs, g_mix, w_in_b, cos_s, sin_s, 1, ts, False)
    caches = [c[0].reshape(db, past, MIX) for c in (cache_diff_k, cache_diff_v, cache_sb_k, cache_sb_v)]
    dout_s, sout_s = _sample_attention(lam, qd2, kdb, vdb, qsb, ksb, vsb, *caches, g_sub, db, dq)
    rows_s = tuple(r.reshape(1, db, dq, HEADS, HEAD_DIM) for r in (kd, vd, ks, vs))

    xp = x_prompt.reshape(nb * seq, D_MODEL)
    pp = p_prompt[0].reshape(nb * seq, PLE_DIM)
    cos_p, sin_p = _rope_tables(jnp.arange(seq, dtype=jnp.int32))
    kd, vd, ks, vs, qd2, kdb, vdb, qsb, ksb, vsb = _proj(xp, g_mix, w_in_b, cos_p, sin_p, nb, seq, True)
    rows_p = tuple(r.reshape(nb, HEADS, HEAD_DIM, seq).transpose(0, 3, 1, 2)[None] for r in (kd, vd, ks, vs))

    cut = seq // PROMPT_ROW_BLOCKS
    spans = [(b, i * cut, cut) for b in range(nb) for i in range(PROMPT_ROW_BLOCKS)]
    for end in (0, -1):
        b, t0, n = spans.pop(end)
        q = n // 4
        pieces = [(b, t0, q), (b, t0 + q, q), (b, t0 + 2 * q, 2 * q)] if end == 0 else \
                 [(b, t0, 2 * q), (b, t0 + 2 * q, q), (b, t0 + 3 * q, q)]
        spans[end if end == 0 else len(spans):0] = pieces

    def prompt_block(b, t0, n):
        dout = _diff_attention(lam, qd2, kdb, vdb, g_sub, b, t0, n)
        sout = _sb_attention(qsb, ksb, vsb, b, t0, n)
        return select(xp, dout, sout, b * seq + t0, n)

    blocks = [(functools.partial(prompt_block, *span), "prompt", span[0] * seq + span[1]) for span in spans]
    blocks.insert(min(PROMPT_ROW_BLOCKS + 1, len(blocks)), (lambda: select(xs, dout_s, sout_s, 0, ts), "sample", 0))
    embeds = {"prompt": pp, "sample": p_sample[0].reshape(ts, PLE_DIM)}
    results = {"prompt": (), "sample": ()}

    def emit(h1, peer, group, tok0):
        results[group] = (_ple(h1, peer, embeds[group], g_ple, w_pgate_b, w_ple_b, g_fin, tok0, results[group]),)

    h1, c, idx, gate = blocks[0][0]()
    dots = _sc_dots(expert_u[0], idx, c)
    for k in range(1, len(blocks)):
        nh1, nc, nidx, ngate = blocks[k][0]()
        coef = _coef(gate, dots, nh1)
        peer, ndots = _sc_step(expert_v[0], idx, coef, expert_u[0], nidx, nc)
        emit(h1, peer, *blocks[k - 1][1:])
        h1, idx, gate, dots = nh1, nidx, ngate, ndots
    coef = _coef(gate, dots, lam)
    emit(h1, _sc_combine(expert_v[0], idx, coef), *blocks[-1][1:])
    y_sample = results["sample"][0].reshape(db, dq, D_MODEL)
    y_prompt = results["prompt"][0].reshape(nb, seq, D_MODEL)

    return (y_prompt, y_sample) + rows_p + rows_s
```
